```python
import math
import jax, jax.numpy as jnp
from jax import lax
import numpy as np

D_MODEL = 1024
BATCH = 16
SEQ = 2048
DEPTH = 2
DEC_BATCH = 8
DEC_SEQ = 64
PAST_LEN = 1024

CHUNK = 64
N_AB = (DEPTH + 1) // 2
N_CD = DEPTH // 2
EPS = 1e-6
NEG_INF = -1e30
H_A = 4
DK_A = 128
DV_A = 128
ROPE_BASE = 10000.0
H_B = 4
DH_B = 64
DV_B = 2 * DH_B
T5_BUCKETS = 32
T5_MAX_DIST = 128
Q_BLOCK = 128
H_C = 8
DH_C = 64
BAND_CHUNKS = 8
C_WINDOW = BAND_CHUNKS * CHUNK
REL_CLIP = 128
H_D = 8
P_D = 64
D_INNER = H_D * P_D
G_D = 2
N_D = 128
CONV_D = 4
CONV_DIM_D = D_INNER + 2 * G_D * N_D
D_FF = 2816
CONV_F = 3
AB_SPLITS = (H_A * DK_A, H_A * DK_A, H_A * DV_A, H_A * DV_A, H_B * 2 * DH_B, H_B * 2 * DH_B, H_B * DV_B)
D_IN_AB = sum(AB_SPLITS)
D_OUT_AB = H_A * DV_A + H_B * DV_B
CD_SPLITS = (H_C * DH_C, H_C * DH_C, H_C * DH_C, D_INNER, CONV_DIM_D, H_D)
D_IN_CD = sum(CD_SPLITS)
D_OUT_CD = H_C * DH_C + D_INNER

kernel_name = 'streaming_hybrid_retention_diffattn_band_ssd'


def split_cols(y, sizes):
    out, start = [], 0
    for s in sizes:
        out.append(y[..., start:start + s])
        start += s
    return out


def to_chunks(t, size=CHUNK):
    b, l = t.shape[:2]
    return jnp.moveaxis(t.reshape((b, l // size, size) + t.shape[2:]), 1, 0)


def from_chunks(t):
    n, b, s = t.shape[:3]
    return jnp.moveaxis(t, 0, 1).reshape((b, n * s) + t.shape[3:])


def rmsnorm(x, g):
    xf = x.astype(jnp.float32)
    y = xf * lax.rsqrt(jnp.mean(xf * xf, axis=-1, keepdims=True) + EPS)
    return y.astype(x.dtype) * g


def head_groupnorm(x, g):
    xf = x.astype(jnp.float32)
    mu = jnp.mean(xf, axis=-1, keepdims=True)
    var = jnp.mean(jnp.square(xf - mu), axis=-1, keepdims=True)
    y = ((xf - mu) * lax.rsqrt(var + EPS)).astype(x.dtype)
    return y.reshape(x.shape[:-2] + (-1,)) * g


def rotary(x, pos):
    half = x.shape[-1] // 2
    inv = jnp.power(ROPE_BASE, -jnp.arange(half, dtype=jnp.float32) / half)
    ang = pos.astype(jnp.float32)[:, None] * inv[None, :]
    cos = jnp.cos(ang)[:, None, :]
    sin = jnp.sin(ang)[:, None, :]
    x1 = x[..., :half].astype(jnp.float32)
    x2 = x[..., half:].astype(jnp.float32)
    return jnp.concatenate([x1 * cos - x2 * sin, x1 * sin + x2 * cos], axis=-1).astype(x.dtype)


def causal_dwconv(x, past, w, b):
    k = w.shape[0]
    seq = x.shape[1]
    xp = jnp.concatenate([past.astype(x.dtype), x], axis=1)
    y = xp[:, 0:seq] * w[0]
    for i in range(1, k):
        y = y + xp[:, i:i + seq] * w[i]
    return y + b, xp[:, seq:]


def t5_bias(q_pos, k_pos, table):
    rel = k_pos[None, :] - q_pos[:, None]
    half = T5_BUCKETS // 2
    max_exact = half // 2
    base = jnp.where(rel > 0, half, 0)
    n = jnp.abs(rel)
    nf = jnp.maximum(n, 1).astype(jnp.float32)
    large = max_exact + (jnp.log(nf / max_exact) / math.log(T5_MAX_DIST / max_exact) * (half - max_exact)).astype(jnp.int32)
    large = jnp.minimum(large, half - 1)
    bucket = base + jnp.where(n < max_exact, n, large)
    return jnp.moveaxis(table[bucket], -1, 0)


def retention_chunk(state, q, k, v, log_gamma):
    seq = q.shape[1]
    idx = jnp.arange(seq, dtype=jnp.float32)
    diff = idx[:, None] - idx[None, :]
    decay = jnp.where(diff[None] >= 0, jnp.exp(jnp.maximum(diff, 0.0)[None] * log_gamma[:, None, None]), 0.0)
    s = jnp.einsum('blhd,bmhd->bhlm', q, k) * decay[None]
    intra = jnp.einsum('bhlm,bmhe->blhe', s, v)
    q_dec = jnp.exp((idx + 1.0)[:, None] * log_gamma[None, :])
    cross = jnp.einsum('blhd,bhde->blhe', q * q_dec[None, :, :, None], state)
    k_dec = jnp.exp((seq - 1.0 - idx)[:, None] * log_gamma[None, :])
    new_state = jnp.exp(seq * log_gamma)[None, :, None, None] * state + jnp.einsum('blhd,blhe->bhde', k * k_dec[None, :, :, None], v)
    return new_state, intra + cross


def ssd_chunk(state, x, dt, da, bm, cm):
    seq = x.shape[1]
    cum = jnp.cumsum(da, axis=1)
    causal = jnp.tril(jnp.ones((seq, seq), dtype=bool))[None, :, :, None]
    seg = cum[:, :, None, :] - cum[:, None, :, :]
    decay = jnp.exp(jnp.where(causal, seg, -jnp.inf))
    cb = jnp.einsum('blhn,bmhn->blmh', cm, bm)
    y_in = jnp.einsum('blmh,bmhp->blhp', cb * decay * dt[:, None, :, :], x)
    y_x = jnp.einsum('blhn,bhpn->blhp', cm, state) * jnp.exp(cum)[..., None]
    last = cum[:, -1:, :]
    w = jnp.exp(last - cum) * dt
    new_state = jnp.exp(last[:, 0])[:, :, None, None] * state + jnp.einsum('blh,blhp,blhn->bhpn', w, x, bm)
    return new_state, y_in + y_x


def diff_attention(q, k, v, q_pos, k_pos, t5_table, lam):
    bias = t5_bias(q_pos, k_pos, t5_table).astype(jnp.float32)
    mask = (k_pos[None, :] // CHUNK) <= (q_pos[:, None] // CHUNK)
    s = jnp.einsum('bqhid,bkhid->bhiqk', q, k).astype(jnp.float32) * (DH_B ** -0.5) + bias[None, :, None]
    p = jax.nn.softmax(jnp.where(mask, s, NEG_INF), axis=-1)
    attn = p[:, :, 0] - lam * p[:, :, 1]
    return jnp.einsum('bhqk,bkhe->bqhe', attn.astype(v.dtype), v)


def band_attention(q, k, v, q_pos, k_pos, rel_table):
    rel = q_pos[:, None] - k_pos[None, :]
    bias = jnp.moveaxis(rel_table[jnp.clip(rel, -REL_CLIP, REL_CLIP) + REL_CLIP], -1, 0).astype(jnp.float32)
    dc = q_pos[:, None] // CHUNK - k_pos[None, :] // CHUNK
    mask = (k_pos[None, :] >= 0) & (dc >= 0) & (dc <= BAND_CHUNKS)
    s = jnp.einsum('bqhd,bkhd->bhqk', q, k).astype(jnp.float32) * (DH_C ** -0.5) + bias[None]
    p = jax.nn.softmax(jnp.where(mask, s, NEG_INF), axis=-1)
    return jnp.einsum('bhqk,bkhd->bqhd', p.astype(v.dtype), v)


def mixer_ab(h, pos, layer, w_in, w_out, ret_gn, lam_q, lam_k, diff_gn, t5_table, ret_past, k_past, v_past):
    bsz, seq, _ = h.shape
    qa, ka, va, ga, qb, kb, vb = split_cols(h @ w_in, AB_SPLITS)
    qa = rotary(qa.reshape(bsz, seq, H_A, DK_A), pos)
    ka = rotary(ka.reshape(bsz, seq, H_A, DK_A), pos) * (DK_A ** -0.5)
    va = va.reshape(bsz, seq, H_A, DV_A)
    log_gamma = jnp.log1p(-jnp.exp2(-5.0 - jnp.arange(H_A, dtype=jnp.float32)))
    if ret_past is None:
        s0 = jnp.zeros((bsz, H_A, DK_A, DV_A), jnp.float32)
        s_new, o_a = lax.scan(lambda s, t: retention_chunk(s, t[0], t[1], t[2], log_gamma), s0,
                              (to_chunks(qa), to_chunks(ka), to_chunks(va)))
        o_a = from_chunks(o_a)
    else:
        s_new, o_a = retention_chunk(ret_past.astype(jnp.float32), qa, ka, va, log_gamma)
    o_a = head_groupnorm(o_a.astype(h.dtype), ret_gn) * jax.nn.silu(ga)
    qb = qb.reshape(bsz, seq, H_B, 2, DH_B)
    kb = kb.reshape(bsz, seq, H_B, 2, DH_B)
    vb = vb.reshape(bsz, seq, H_B, DV_B)
    lam_init = 0.8 - 0.6 * math.exp(-0.3 * layer)
    lq = lam_q.astype(jnp.float32)
    lk = lam_k.astype(jnp.float32)
    lam = jnp.exp(jnp.sum(lq[0] * lk[0])) - jnp.exp(jnp.sum(lq[1] * lk[1])) + lam_init
    if k_past is None:
        o_b = lax.map(lambda t: diff_attention(t[0], kb, vb, t[1], pos, t5_table, lam),
                      (to_chunks(qb, Q_BLOCK), pos.reshape(-1, Q_BLOCK)))
        o_b = from_chunks(o_b)
    else:
        past = k_past.shape[1]
        k_all = jnp.concatenate([k_past.reshape(bsz, past, H_B, 2, DH_B).astype(kb.dtype), kb], axis=1)
        v_all = jnp.concatenate([v_past.astype(vb.dtype), vb], axis=1)
        k_pos = jnp.arange(past + seq, dtype=jnp.int32)
        o_b = diff_attention(qb, k_all, v_all, pos, k_pos, t5_table, lam)
    o_b = rmsnorm(o_b, diff_gn) * (1.0 - lam_init)
    mixed = jnp.concatenate([o_a, o_b.reshape(bsz, seq, H_B * DV_B)], axis=-1)
    return mixed @ w_out, s_new, kb.reshape(bsz, seq, H_B, 2 * DH_B), vb


def mixer_cd(h, pos, w_in, w_out, rel_table, conv_w, conv_b, dt_bias, a_log, d_skip, norm_g_d, k_past, v_past, conv_past, ssm_past):
    bsz, seq, _ = h.shape
    qc, kc, vc, z, xbc, dt = split_cols(h @ w_in, CD_SPLITS)
    qc = qc.reshape(bsz, seq, H_C, DH_C)
    kc = kc.reshape(bsz, seq, H_C, DH_C)
    vc = vc.reshape(bsz, seq, H_C, DH_C)
    if k_past is None:
        band = C_WINDOW + CHUNK
        kpad = jnp.pad(kc, ((0, 0), (C_WINDOW, 0), (0, 0), (0, 0)))
        vpad = jnp.pad(vc, ((0, 0), (C_WINDOW, 0), (0, 0), (0, 0)))

        def one_chunk(t):
            q_blk, start = t
            k_blk = lax.dynamic_slice_in_dim(kpad, start, band, axis=1)
            v_blk = lax.dynamic_slice_in_dim(vpad, start, band, axis=1)
            k_pos = start - C_WINDOW + jnp.arange(band, dtype=jnp.int32)
            q_pos = start + jnp.arange(CHUNK, dtype=jnp.int32)
            return band_attention(q_blk, k_blk, v_blk, q_pos, k_pos, rel_table)

        starts = jnp.arange(seq // CHUNK, dtype=jnp.int32) * CHUNK
        o_c = from_chunks(lax.map(one_chunk, (to_chunks(qc), starts)))
        keep = min(C_WINDOW, seq)
        new_k, new_v = kc[:, seq - keep:], vc[:, seq - keep:]
    else:
        w = k_past.shape[1]
        k_all = jnp.concatenate([k_past.astype(kc.dtype), kc], axis=1)
        v_all = jnp.concatenate([v_past.astype(vc.dtype), vc], axis=1)
        k_pos = PAST_LEN - w + jnp.arange(w + seq, dtype=jnp.int32)
        o_c = band_attention(qc, k_all, v_all, pos, k_pos, rel_table)
        new_k, new_v = kc, vc
    if conv_past is None:
        conv_past = jnp.zeros((bsz, CONV_D - 1, CONV_DIM_D), h.dtype)
    xbc, new_conv = causal_dwconv(xbc, conv_past, conv_w, conv_b)
    xbc = jax.nn.silu(xbc)
    xs, bm, cm = split_cols(xbc, (D_INNER, G_D * N_D, G_D * N_D))
    xs = xs.reshape(bsz, seq, H_D, P_D)
    rep = H_D // G_D
    bm = jnp.repeat(bm.reshape(bsz, seq, G_D, N_D), rep, axis=2)
    cm = jnp.repeat(cm.reshape(bsz, seq, G_D, N_D), rep, axis=2)
    dt = jax.nn.softplus(dt.astype(jnp.float32) + dt_bias.astype(jnp.float32))
    da = dt * (-jnp.exp(a_log.astype(jnp.float32)))
    if ssm_past is None:
        h0 = jnp.zeros((bsz, H_D, P_D, N_D), jnp.float32)
        ssm_new, y_d = lax.scan(lambda s, t: ssd_chunk(s, t[0], t[1], t[2], t[3], t[4]), h0,
                                (to_chunks(xs), to_chunks(dt), to_chunks(da), to_chunks(bm), to_chunks(cm)))
        y_d = from_chunks(y_d)
    else:
        ssm_new, y_d = ssd_chunk(ssm_past.astype(jnp.float32), xs, dt, da, bm, cm)
    y_d = (y_d + d_skip.astype(jnp.float32)[:, None] * xs).astype(h.dtype).reshape(bsz, seq, D_INNER)
    y_d = y_d * jax.nn.silu(z)
    y_d = rmsnorm(y_d.reshape(bsz, seq, G_D, D_INNER // G_D), norm_g_d.reshape(G_D, D_INNER // G_D)).reshape(bsz, seq, D_INNER)
    mixed = jnp.concatenate([o_c.reshape(bsz, seq, H_C * DH_C), y_d], axis=-1)
    return mixed @ w_out, new_k, new_v, new_conv, ssm_new


def conv_ffn(h, w_up, conv_w, conv_b, w_down, past):
    bsz = h.shape[0]
    u = h @ w_up
    a, g = u[..., :D_FF], u[..., D_FF:]
    if past is None:
        past = jnp.zeros((bsz, CONV_F - 1, D_FF), h.dtype)
    g, new_past = causal_dwconv(g, past, conv_w, conv_b)
    return (a * jax.nn.gelu(g)) @ w_down, new_past


def setup_inputs(seed: int = 0) -> dict:
    key = jax.random.key(seed)
    ks = iter(jax.random.split(key, 48))

    def nrm(shape, scale):
        return jax.random.normal(next(ks), shape, jnp.float32) * scale

    c_cache = min(C_WINDOW, PAST_LEN)
    dt0 = jnp.exp(jax.random.uniform(next(ks), (N_CD, H_D), jnp.float32, math.log(1e-3), math.log(1e-1)))
    return {
        'x_prompt': nrm((BATCH, SEQ, D_MODEL), 1.0),
        'x_sample': nrm((DEC_BATCH, DEC_SEQ, D_MODEL), 1.0),
        'cache_ret_state': nrm((N_AB, DEC_BATCH, H_A, DK_A, DV_A), 0.1),
        'cache_b_k': nrm((N_AB, DEC_BATCH, PAST_LEN, H_B, 2 * DH_B), 1.0),
        'cache_b_v': nrm((N_AB, DEC_BATCH, PAST_LEN, H_B, DV_B), 1.0),
        'cache_c_k': nrm((N_CD, DEC_BATCH, c_cache, H_C, DH_C), 1.0),
        'cache_c_v': nrm((N_CD, DEC_BATCH, c_cache, H_C, DH_C), 1.0),
        'state_d_conv': nrm((N_CD, DEC_BATCH, CONV_D - 1, CONV_DIM_D), 1.0),
        'state_d_ssm': nrm((N_CD, DEC_BATCH, H_D, P_D, N_D), 0.1),
        'state_ffn_conv': nrm((DEPTH, DEC_BATCH, CONV_F - 1, D_FF), 1.0),
        'c_prompt': nrm((BATCH, D_MODEL), 1.0),
        'c_sample': nrm((DEC_BATCH, D_MODEL), 1.0),
        'w_mod': nrm((DEPTH, D_MODEL, 6 * D_MODEL), 0.5 * D_MODEL ** -0.5),
        'b_mod': nrm((DEPTH, 6 * D_MODEL), 0.02),
        'norm_g': 1.0 + nrm((DEPTH, 2, D_MODEL), 0.1),
        'final_g': 1.0 + nrm((D_MODEL,), 0.1),
        't5_table': nrm((T5_BUCKETS, H_B), 0.5),
        'w_in_ab': nrm((N_AB, D_MODEL, D_IN_AB), D_MODEL ** -0.5),
        'w_out_ab': nrm((N_AB, D_OUT_AB, D_MODEL), D_OUT_AB ** -0.5),
        'ret_gn': 1.0 + nrm((N_AB, H_A * DV_A), 0.1),
        'lam_q': nrm((N_AB, 2, DH_B), 0.1),
        'lam_k': nrm((N_AB, 2, DH_B), 0.1),
        'diff_gn': 1.0 + nrm((N_AB, DV_B), 0.1),
        'w_in_cd': nrm((N_CD, D_MODEL, D_IN_CD), D_MODEL ** -0.5),
        'w_out_cd': nrm((N_CD, D_OUT_CD, D_MODEL), D_OUT_CD ** -0.5),
        'rel_table': nrm((N_CD, 2 * REL_CLIP + 1, H_C), 0.5),
        'd_conv_w': nrm((N_CD, CONV_D, CONV_DIM_D), 0.5),
        'd_conv_b': nrm((N_CD, CONV_DIM_D), 0.02),
        'd_dt_bias': dt0 + jnp.log(-jnp.expm1(-dt0)),
        'd_a_log': jnp.log(jax.random.uniform(next(ks), (N_CD, H_D), jnp.float32, 1.0, 16.0)),
        'd_skip': 1.0 + nrm((N_CD, H_D), 0.1),
        'd_norm_g': 1.0 + nrm((N_CD, D_INNER), 0.1),
        'w_up': nrm((DEPTH, D_MODEL, 2 * D_FF), D_MODEL ** -0.5),
        'ffn_conv_w': nrm((DEPTH, CONV_F, D_FF), 0.5),
        'ffn_conv_b': nrm((DEPTH, D_FF), 0.02),
        'w_down': nrm((DEPTH, D_FF, D_MODEL), D_FF ** -0.5),
    }


def reference(x_prompt, x_sample, cache_ret_state, cache_b_k, cache_b_v, cache_c_k, cache_c_v,
              state_d_conv, state_d_ssm, state_ffn_conv, c_prompt, c_sample,
              w_mod, b_mod, norm_g, final_g, t5_table,
              w_in_ab, w_out_ab, ret_gn, lam_q, lam_k, diff_gn,
              w_in_cd, w_out_cd, rel_table, d_conv_w, d_conv_b, d_dt_bias, d_a_log, d_skip, d_norm_g,
              w_up, ffn_conv_w, ffn_conv_b, w_down):

    def trunk(x, c, pos, sample):
        s_ret, s_bk, s_bv, s_ck, s_cv, s_dconv, s_dssm, s_ffn = ([] for _ in range(8))
        c_act = jax.nn.silu(c)
        for l in range(DEPTH):
            i = l // 2
            mod = c_act @ w_mod[l] + b_mod[l]
            sh1, sc1, g1, sh2, sc2, g2 = [m[:, None, :] for m in jnp.split(mod, 6, axis=-1)]
            h = rmsnorm(x, norm_g[l, 0]) * (1.0 + sc1) + sh1
            if l % 2 == 0:
                out, st, nk, nv = mixer_ab(
                    h, pos, l, w_in_ab[i], w_out_ab[i], ret_gn[i], lam_q[i], lam_k[i], diff_gn[i], t5_table,
                    cache_ret_state[i] if sample else None,
                    cache_b_k[i] if sample else None,
                    cache_b_v[i] if sample else None)
                s_ret.append(st)
                s_bk.append(nk)
                s_bv.append(nv)
            else:
                out, nk, nv, nconv, nssm = mixer_cd(
                    h, pos, w_in_cd[i], w_out_cd[i], rel_table[i], d_conv_w[i], d_conv_b[i], d_dt_bias[i],
                    d_a_log[i], d_skip[i], d_norm_g[i],
                    cache_c_k[i] if sample else None,
                    cache_c_v[i] if sample else None,
                    state_d_conv[i] if sample else None,
                    state_d_ssm[i] if sample else None)
                s_ck.append(nk)
                s_cv.append(nv)
                s_dconv.append(nconv)
                s_dssm.append(nssm)
            x = x + g1 * out
            h = rmsnorm(x, norm_g[l, 1]) * (1.0 + sc2) + sh2
            f, nf = conv_ffn(h, w_up[l], ffn_conv_w[l], ffn_conv_b[l], w_down[l],
                             state_ffn_conv[l] if sample else None)
            x = x + g2 * f
            s_ffn.append(nf)
        y = rmsnorm(x, final_g)
        stk = lambda t: jnp.stack(t).astype(x.dtype)
        return (y, stk(s_ret), stk(s_bk), stk(s_bv), stk(s_ck), stk(s_cv), stk(s_dconv), stk(s_dssm), stk(s_ffn))

    pos_p = jnp.arange(x_prompt.shape[1], dtype=jnp.int32)
    pos_s = PAST_LEN + jnp.arange(x_sample.shape[1], dtype=jnp.int32)
    y_prompt, ret_p, bk_p, bv_p, ck_p, cv_p, dconv_p, dssm_p, ffn_p = trunk(x_prompt, c_prompt, pos_p, False)
    y_sample, ret_s, bk_s, bv_s, ck_s, cv_s, dconv_s, dssm_s, ffn_s = trunk(x_sample, c_sample, pos_s, True)
    return (y_prompt, y_sample, ret_p, ret_s, bk_p, bk_s, bv_p, bv_s, ck_p, ck_s, cv_p, cv_s,
            dconv_p, dconv_s, dssm_p, dssm_s, ffn_p, ffn_s)
```

```python
import functools
import math

import numpy as np
import jax
import jax.numpy as jnp
from jax import lax
from jax.experimental import pallas as pl
from jax.experimental.pallas import tpu as pltpu

F32 = jnp.float32
BF16 = jnp.bfloat16

CHUNK = 64
EPS = 1e-6
NEG = -1e30
H_A, DK_A, DV_A = 4, 128, 128
ROPE_BASE = 10000.0
H_B, DH_B, DV_B = 4, 64, 128
T5_BUCKETS, T5_MAX_DIST = 32, 128
H_C, DH_C = 8, 64
BAND_CHUNKS = 8
C_WINDOW = BAND_CHUNKS * CHUNK
REL_CLIP = 128
H_D, P_D, G_D, N_D = 8, 64, 2, 128
D_INNER = H_D * P_D
CONV_D = 4
CONV_DIM_D = D_INNER + 2 * G_D * N_D
CONV_F = 3
N_AB_COLS = 7 * 512
N_CD_MAIN = 6 * 512
N_CD_PAD = N_CD_MAIN + 128
FF_CHUNK = 256
BAND_TQ = 256
DIFF_TQ = 128
SCAN_CHUNK = 256
VMEM_LIMIT = 56 * 1024 * 1024


def _mm(a, b):
    return jnp.dot(a, b, preferred_element_type=F32)


def _mm_nt(a, b):
    return lax.dot_general(a, b, (((1,), (1,)), ((), ())), preferred_element_type=F32)


def _mm_tn(a, b):
    return lax.dot_general(a, b, (((0,), (0,)), ((), ())), preferred_element_type=F32)


def _mm_exact(a, b):
    return jnp.dot(a, b, preferred_element_type=F32, precision=lax.Precision.HIGHEST)


def _mm_nt_exact(a, b):
    return lax.dot_general(a, b, (((1,), (1,)), ((), ())), preferred_element_type=F32,
                           precision=lax.Precision.HIGHEST)


def _silu(x):
    return x * (1.0 / (1.0 + jnp.exp(-x)))


def _softplus(x):
    return jnp.maximum(x, 0.0) + jnp.log1p(jnp.exp(-jnp.abs(x)))


def _gelu_tanh(x):
    return x * (0.5 * (1.0 + jnp.tanh(math.sqrt(2.0 / math.pi) * (x + 0.044715 * (x * x * x)))))


def _norm_mod(x, g, sc, sh):
    ms = jnp.mean(x * x, axis=-1, keepdims=True)
    return (x * lax.rsqrt(ms + EPS)) * g * (1.0 + sc) + sh


def _params(sem):
    return pltpu.CompilerParams(dimension_semantics=sem, vmem_limit_bytes=VMEM_LIMIT)


def _resident(shape):
    nd = len(shape)
    return pl.BlockSpec(shape, lambda *_: (0,) * nd, pipeline_mode=pl.Buffered(1))


def _mod_kernel(c_ref, w_ref, b_ref, o_ref):
    c = c_ref[...]
    o_ref[0] = _mm(_silu(c).astype(BF16), w_ref[0].astype(BF16)) + b_ref[0]


def _modulation(c_all, w_mod, b_mod):
    depth, d, n = w_mod.shape
    r = c_all.shape[0]
    tn = 1536
    return pl.pallas_call(
        _mod_kernel,
        grid=(depth, n // tn),
        in_specs=[pl.BlockSpec((r, d), lambda l, j: (0, 0)),
                  pl.BlockSpec((1, d, tn), lambda l, j: (l, 0, j)),
                  pl.BlockSpec((1, 1, tn), lambda l, j: (l, 0, j))],
        out_specs=pl.BlockSpec((1, r, tn), lambda l, j: (l, 0, j)),
        out_shape=jax.ShapeDtypeStruct((depth, r, n), F32),
        compiler_params=_params(("parallel", "parallel")),
        name="modulation",
    )(c_all, w_mod, b_mod.reshape(depth, 1, n))


def _table_gather_kernel(tab_ref, idx_ref, o_ref, *, n_entries, n_heads):
    idx = idx_ref[...]

    def body(r, accs):
        m = idx == r
        return tuple(jnp.where(m, tab_ref[hh, r], a) for hh, a in enumerate(accs))

    accs = lax.fori_loop(0, n_entries, body, tuple(jnp.zeros(idx.shape, F32) for _ in range(n_heads)))
    for hh in range(n_heads):
        o_ref[hh:hh + 1, :] = accs[hh]


def _table_gather(table, idx):
    t, h = table.shape
    w = idx.shape[0]
    return pl.pallas_call(
        functools.partial(_table_gather_kernel, n_entries=t, n_heads=h),
        in_specs=[pl.BlockSpec(memory_space=pltpu.SMEM),
                  pl.BlockSpec((1, w), lambda: (0, 0))],
        out_specs=pl.BlockSpec((h, w), lambda: (0, 0)),
        out_shape=jax.ShapeDtypeStruct((h, w), F32),
        name="table_gather",
    )(table.T, idx.reshape(1, w))


def _toeplitz(vec, rows, cols):
    h, wv = vec.shape
    flat = jnp.tile(vec, (1, rows))[:, :rows * (wv - 1)]
    return flat.reshape(h, rows, wv - 1)[:, :, :cols]


def _t5_bucket(rel):
    half = T5_BUCKETS // 2
    max_exact = half // 2
    base = jnp.where(rel > 0, half, 0)
    n = jnp.abs(rel)
    nf = jnp.maximum(n, 1).astype(F32)
    large = max_exact + (jnp.log(nf / max_exact) / math.log(T5_MAX_DIST / max_exact) * (half - max_exact)).astype(jnp.int32)
    large = jnp.minimum(large, half - 1)
    return base + jnp.where(n < max_exact, n, large)


def _wrapped_offsets(n_pos, n_neg):
    p = jnp.arange(n_pos + n_neg, dtype=jnp.int32)
    return jnp.where(p < n_pos, p, p - (n_pos + n_neg))


def _t5_bias_prompt(t5_table, seq):
    nd = seq // DIFF_TQ
    u = _wrapped_offsets(seq, DIFF_TQ)
    vec = _table_gather(t5_table, _t5_bucket(u - (seq - DIFF_TQ)))
    wide = _toeplitz(vec, DIFF_TQ, seq)
    return wide.reshape(H_B, DIFF_TQ, nd, DIFF_TQ).transpose(0, 2, 1, 3)


def _t5_bias_sample(t5_table, past, lq):
    u = _wrapped_offsets(past + lq, lq)
    vec = _table_gather(t5_table, _t5_bucket(u - past))
    return _toeplitz(vec, lq, past + lq)


def _band_bias(rel_table):
    u = _wrapped_offsets(3 * BAND_TQ, BAND_TQ)
    idx = jnp.clip(2 * BAND_TQ - u, -REL_CLIP, REL_CLIP) + REL_CLIP
    return _toeplitz(_table_gather(rel_table, idx), BAND_TQ, 3 * BAND_TQ)


def _inproj_ab_kernel(x_ref, g_ref, sc_ref, sh_ref, w_ref, cos_ref, sin_ref, qkv_ref, kb_ref, vb_ref):
    h = _norm_mod(x_ref[0], g_ref[...], sc_ref[0], sh_ref[0]).astype(BF16)
    cos = cos_ref[...]
    sin = sin_ref[...]
    for j in range(7):
        y = _mm(h, w_ref[:, j * 512:(j + 1) * 512])
        if j < 2:
            for hh in range(H_A):
                yh = y[:, hh * 128:(hh + 1) * 128]
                yh = yh * cos + pltpu.roll(yh, 64, 1) * sin
                if j == 1:
                    yh = yh * (DK_A ** -0.5)
                qkv_ref[0, :, j * 512 + hh * 128:j * 512 + (hh + 1) * 128] = yh.astype(BF16)
        else:
            if j == 5:
                kb_ref[0] = y
            if j == 6:
                vb_ref[0] = y
            qkv_ref[0, :, j * 512:(j + 1) * 512] = y.astype(BF16)


def _inproj_ab(x, g, sc, sh, w, cos, sin, tl):
    nseq, seq, d = x.shape
    grid = (nseq, seq // tl)
    tok = lambda n: pl.BlockSpec((1, tl, n), lambda b, l: (b, l, 0))
    per_seq = pl.BlockSpec((1, 1, d), lambda b, l: (b, 0, 0))
    return pl.pallas_call(
        _inproj_ab_kernel,
        grid=grid,
        in_specs=[tok(d), _resident((1, d)), per_seq, per_seq, _resident((d, N_AB_COLS)),
                  pl.BlockSpec((tl, 128), lambda b, l: (l, 0)),
                  pl.BlockSpec((tl, 128), lambda b, l: (l, 0))],
        out_specs=[tok(N_AB_COLS), tok(512), tok(512)],
        out_shape=[jax.ShapeDtypeStruct((nseq, seq, N_AB_COLS), BF16),
                   jax.ShapeDtypeStruct((nseq, seq, 512), F32),
                   jax.ShapeDtypeStruct((nseq, seq, 512), F32)],
        compiler_params=_params(("parallel", "parallel")),
        name="inproj_ab",
    )(x, g, sc, sh, w, cos, sin)


def _retention_kernel(q_ref, k_ref, v_ref, g_ref, s0_ref, dec_ref, qd_ref, kd_ref, gc_ref, gn_ref,
                      o_ref, sn_ref, st_ref):
    @pl.when(pl.program_id(2) == 0)
    def _():
        st_ref[...] = s0_ref[0, 0]

    q = q_ref[0]
    k = k_ref[0]
    v = v_ref[0]
    state = st_ref[...]
    s = _mm_nt(q, k) * dec_ref[0]
    o = _mm(s.astype(BF16), v) + _mm((q.astype(F32) * qd_ref[0]).astype(BF16), state.astype(BF16))
    new_state = gc_ref[0] * state + _mm_tn((k.astype(F32) * kd_ref[0]).astype(BF16), v)
    st_ref[...] = new_state
    sn_ref[0, 0] = new_state
    mu = jnp.mean(o, axis=-1, keepdims=True)
    dlt = o - mu
    var = jnp.mean(dlt * dlt, axis=-1, keepdims=True)
    y = dlt * lax.rsqrt(var + EPS) * gn_ref[...] * _silu(g_ref[0].astype(F32))
    o_ref[0] = y.astype(BF16)


def _retention_consts(c):
    lg = np.log1p(-np.exp2(-5.0 - np.arange(H_A, dtype=np.float32))).astype(np.float32)
    idx = np.arange(c, dtype=np.float32)
    diff = idx[:, None] - idx[None, :]
    decay = np.where(diff[None] >= 0, np.exp(np.maximum(diff, 0.0)[None] * lg[:, None, None]), 0.0)
    qd = np.exp((idx + 1.0)[None, :] * lg[:, None])
    kd = np.exp((c - 1.0 - idx)[None, :] * lg[:, None])
    gc = np.exp(c * lg)
    bc = lambda a: np.ascontiguousarray(np.broadcast_to(a[..., None], a.shape + (128,))).astype(np.float32)
    return decay.astype(np.float32), bc(qd), bc(kd), bc(gc[:, None])


def _retention(qkv, s0, ret_gn, c):
    nseq, seq, _ = qkv.shape
    decay, qd, kd, gc = _retention_consts(c)
    col = lambda off: pl.BlockSpec((1, c, 128), lambda b, h, l: (b, l, off + h))
    per_head = lambda r: pl.BlockSpec((1, r, 128), lambda b, h, l: (h, 0, 0))
    state = pl.BlockSpec((1, 1, DK_A, DV_A), lambda b, h, l: (b, h, 0, 0))
    return pl.pallas_call(
        _retention_kernel,
        grid=(nseq, H_A, seq // c),
        in_specs=[col(0), col(4), col(8), col(12), state,
                  pl.BlockSpec((1, c, c), lambda b, h, l: (h, 0, 0)),
                  per_head(c), per_head(c), per_head(1),
                  pl.BlockSpec((1, 128), lambda b, h, l: (0, h))],
        out_specs=[pl.BlockSpec((1, c, 128), lambda b, h, l: (b, l, h)), state],
        out_shape=[jax.ShapeDtypeStruct((nseq, seq, H_A * DV_A), BF16),
                   jax.ShapeDtypeStruct((nseq, H_A, DK_A, DV_A), F32)],
        scratch_shapes=[pltpu.VMEM((DK_A, DV_A), F32)],
        compiler_params=_params(("parallel", "parallel", "arbitrary")),
        name="retention",
    )(qkv, qkv, qkv, qkv, s0, decay, qd, kd, gc, ret_gn.reshape(1, H_A * DV_A))


def _split_halves(q):
    lane = lax.broadcasted_iota(jnp.int32, q.shape, 1)
    zero = jnp.zeros_like(q)
    return jnp.where(lane < 64, q, zero), jnp.where(lane >= 64, q, zero)


def _lambda(lq_ref, lk_ref, lam_init):
    e = jnp.exp(jnp.sum(lq_ref[...] * lk_ref[...], axis=1, keepdims=True))
    return e[0:1, :] - e[1:2, :] + lam_init


def _diff_epilogue(o, gn, lam_init):
    ms = jnp.mean(o * o, axis=-1, keepdims=True)
    return (o * lax.rsqrt(ms + EPS)) * gn * (1.0 - lam_init)


def _diff_attn_kernel(q_ref, k_ref, v_ref, b_ref, lq_ref, lk_ref, gn_ref, o_ref, *, tq, nd, lam_init):
    t = pl.program_id(2)
    qs = _split_halves(q_ref[0])
    scale = DH_B ** -0.5
    row = lax.broadcasted_iota(jnp.int32, (tq, tq), 0) // CHUNK
    col = lax.broadcasted_iota(jnp.int32, (tq, tq), 1) // CHUNK
    chunk_mask = col <= row

    def tile(kt, carry, masked):
        start = pl.multiple_of(kt * tq, tq)
        k = k_ref[0, pl.ds(start, tq), :]
        v = v_ref[0, pl.ds(start, tq), :]
        b = b_ref[0, nd - 1 - t + kt]
        new = []
        for i in range(2):
            m, l, acc = carry[i]
            s = _mm_nt(qs[i], k) * scale + b
            if masked:
                s = jnp.where(chunk_mask, s, NEG)
            m_new = jnp.maximum(m, jnp.max(s, axis=1, keepdims=True))
            alpha = jnp.exp(m - m_new)
            p = jnp.exp(s - m_new)
            l = alpha * l + jnp.sum(p, axis=1, keepdims=True)
            acc = alpha * acc + _mm(p.astype(BF16), v)
            new.append((m_new, l, acc))
        return tuple(new)

    init = tuple((jnp.full((tq, 1), NEG, F32), jnp.zeros((tq, 1), F32), jnp.zeros((tq, DV_B), F32))
                 for _ in range(2))
    carry = lax.fori_loop(0, t, lambda kt, c: tile(kt, c, False), init)
    (_, l0, a0), (_, l1, a1) = tile(t, carry, True)
    lam = _lambda(lq_ref, lk_ref, lam_init)
    o = a0 / l0 - lam * (a1 / l1)
    o_ref[0] = _diff_epilogue(o, gn_ref[...], lam_init).astype(BF16)


def _diff_attn(qkv, bias, lam_q, lam_k, diff_gn, lam_init):
    nseq, seq, _ = qkv.shape
    tq = DIFF_TQ
    nd = seq // tq
    full = lambda off: pl.BlockSpec((1, seq, 128), lambda b, h, t: (b, 0, off + h))
    small = lambda r, c: pl.BlockSpec((r, c), lambda b, h, t: (0, 0))
    return pl.pallas_call(
        functools.partial(_diff_attn_kernel, tq=tq, nd=nd, lam_init=lam_init),
        grid=(nseq, H_B, nd),
        in_specs=[pl.BlockSpec((1, tq, 128), lambda b, h, t: (b, t, 16 + h)), full(20), full(24),
                  pl.BlockSpec((1, nd, tq, tq), lambda b, h, t: (h, 0, 0, 0)),
                  small(2, DH_B), small(2, DH_B), small(1, DV_B)],
        out_specs=pl.BlockSpec((1, tq, 128), lambda b, h, t: (b, t, h)),
        out_shape=jax.ShapeDtypeStruct((nseq, seq, H_B * DV_B), BF16),
        compiler_params=_params(("parallel", "parallel", "parallel")),
        name="diff_attn",
    )(qkv, qkv, qkv, bias, lam_q, lam_k, diff_gn.reshape(1, DV_B))


def _diff_attn_sample_kernel(q_ref, kc_ref, vc_ref, kp_ref, vp_ref, b_ref, lq_ref, lk_ref, gn_ref, o_ref,
                             *, past, lam_init):
    qs = _split_halves(q_ref[0])
    scale = DH_B ** -0.5
    kp = kp_ref[0].astype(BF16)
    vp = vp_ref[0].astype(BF16)
    kc = kc_ref[0]
    vc = vc_ref[0]
    b = b_ref[0]
    probs = []
    for i in range(2):
        sp = _mm_nt(qs[i], kp) * scale + b[:, :past]
        sc = _mm_nt(qs[i], kc) * scale + b[:, past:]
        m = jnp.maximum(jnp.max(sp, axis=1, keepdims=True), jnp.max(sc, axis=1, keepdims=True))
        pp = jnp.exp(sp - m)
        pc = jnp.exp(sc - m)
        l = jnp.sum(pp, axis=1, keepdims=True) + jnp.sum(pc, axis=1, keepdims=True)
        probs.append((pp / l, pc / l))
    lam = _lambda(lq_ref, lk_ref, lam_init)
    ap = probs[0][0] - lam * probs[1][0]
    ac = probs[0][1] - lam * probs[1][1]
    o = _mm(ap.astype(BF16), vp) + _mm(ac.astype(BF16), vc)
    o_ref[0] = _diff_epilogue(o, gn_ref[...], lam_init).astype(BF16)


def _diff_attn_sample(qkv, k_past, v_past, bias, lam_q, lam_k, diff_gn, lam_init):
    nseq, lq, _ = qkv.shape
    past = k_past.shape[1]
    cur = lambda off: pl.BlockSpec((1, lq, 128), lambda b, h: (b, 0, off + h))
    old = pl.BlockSpec((1, past, 128), lambda b, h: (b, 0, h))
    small = lambda r, c: pl.BlockSpec((r, c), lambda b, h: (0, 0))
    return pl.pallas_call(
        functools.partial(_diff_attn_sample_kernel, past=past, lam_init=lam_init),
        grid=(nseq, H_B),
        in_specs=[cur(16), cur(20), cur(24), old, old,
                  pl.BlockSpec((1, lq, past + lq), lambda b, h: (h, 0, 0)),
                  small(2, DH_B), small(2, DH_B), small(1, DV_B)],
        out_specs=pl.BlockSpec((1, lq, 128), lambda b, h: (b, 0, h)),
        out_shape=jax.ShapeDtypeStruct((nseq, lq, H_B * DV_B), BF16),
        compiler_params=_params(("parallel", "parallel")),
        name="diff_attn_sample",
    )(qkv, qkv, qkv, k_past, v_past, bias, lam_q, lam_k, diff_gn.reshape(1, DV_B))


def _inproj_cd_kernel(x_ref, g_ref, sc_ref, sh_ref, w_ref, main_ref, dt_ref, kc_ref, vc_ref, tail_ref, *, tl):
    h = _norm_mod(x_ref[0], g_ref[...], sc_ref[0], sh_ref[0]).astype(BF16)
    for j in range(6):
        y = _mm(h, w_ref[:, j * 512:(j + 1) * 512])
        main_ref[0, :, j * 512:(j + 1) * 512] = y.astype(BF16)
        if j == 1:
            kc_ref[0] = y
        if j == 2:
            vc_ref[0] = y
        if j >= 4:
            tail_ref[0, :, (j - 4) * 512:(j - 3) * 512] = y[tl - 8:tl, :]
    dt_ref[0] = _mm(h, w_ref[:, N_CD_MAIN:N_CD_PAD])


def _inproj_cd(x, g, sc, sh, w, tl):
    nseq, seq, d = x.shape
    tok = lambda n: pl.BlockSpec((1, tl, n), lambda b, l: (b, l, 0))
    per_seq = pl.BlockSpec((1, 1, d), lambda b, l: (b, 0, 0))
    return pl.pallas_call(
        functools.partial(_inproj_cd_kernel, tl=tl),
        grid=(nseq, seq // tl),
        in_specs=[tok(d), _resident((1, d)), per_seq, per_seq, _resident((d, N_CD_PAD))],
        out_specs=[tok(N_CD_MAIN), tok(128), tok(512), tok(512),
                   pl.BlockSpec((1, 8, CONV_DIM_D), lambda b, l: (b, 0, 0))],
        out_shape=[jax.ShapeDtypeStruct((nseq, seq, N_CD_MAIN), BF16),
                   jax.ShapeDtypeStruct((nseq, seq, 128), F32),
                   jax.ShapeDtypeStruct((nseq, seq, 512), F32),
                   jax.ShapeDtypeStruct((nseq, seq, 512), F32),
                   jax.ShapeDtypeStruct((nseq, 8, CONV_DIM_D), F32)],
        compiler_params=_params(("parallel", "arbitrary")),
        name="inproj_cd",
    )(x, g, sc, sh, w)


def _band_kernel(q_ref, k_ref, v_ref, b_ref, o_ref, *, tq):
    t = pl.program_id(2)
    q0, q1 = _split_halves(q_ref[0])
    scale = DH_C ** -0.5
    ri = lax.broadcasted_iota(jnp.int32, (tq, tq), 0) // CHUNK
    cj = lax.broadcasted_iota(jnp.int32, (tq, tq), 1) // CHUNK
    ks, vs, masks, offs = [], [], [], []
    for d in range(3):
        kt = t - 2 + d
        start = pl.multiple_of(jnp.maximum(kt, 0) * tq, tq)
        ks.append(k_ref[0, pl.ds(start, tq), :])
        vs.append(v_ref[0, pl.ds(start, tq), :])
        dc = ri - cj + (2 - d) * (tq // CHUNK)
        masks.append((dc >= 0) & (dc <= BAND_CHUNKS))
        offs.append(jnp.where(kt >= 0, 0.0, NEG))
    outs = []
    for hi, qh in enumerate((q0, q1)):
        ss = []
        for d in range(3):
            s = _mm_nt(qh, ks[d]) * scale + b_ref[hi, :, d * tq:(d + 1) * tq]
            ss.append(jnp.where(masks[d], s, NEG) + offs[d])
        m = jnp.maximum(jnp.maximum(jnp.max(ss[0], axis=1, keepdims=True), jnp.max(ss[1], axis=1, keepdims=True)),
                        jnp.max(ss[2], axis=1, keepdims=True))
        l = jnp.zeros((tq, 1), F32)
        o = jnp.zeros((tq, 128), F32)
        for d in range(3):
            p = jnp.exp(ss[d] - m)
            l = l + jnp.sum(p, axis=1, keepdims=True)
            o = o + _mm(p.astype(BF16), vs[d])
        outs.append(o / l)
    lane = lax.broadcasted_iota(jnp.int32, (tq, 128), 1)
    o_ref[0] = jnp.where(lane < 64, outs[0], outs[1]).astype(BF16)


def _band_attn(main, bias):
    nseq, seq, _ = main.shape
    tq = BAND_TQ
    full = lambda off: pl.BlockSpec((1, seq, 128), lambda b, p, t: (b, 0, off + p))
    return pl.pallas_call(
        functools.partial(_band_kernel, tq=tq),
        grid=(nseq, H_C // 2, seq // tq),
        in_specs=[pl.BlockSpec((1, tq, 128), lambda b, p, t: (b, t, p)), full(4), full(8),
                  pl.BlockSpec((2, tq, 3 * tq), lambda b, p, t: (p, 0, 0))],
        out_specs=pl.BlockSpec((1, tq, 128), lambda b, p, t: (b, t, p)),
        out_shape=jax.ShapeDtypeStruct((nseq, seq, H_C * DH_C), BF16),
        compiler_params=_params(("parallel", "parallel", "parallel")),
        name="band_attn",
    )(main, main, main, bias)


def _band_sample_kernel(q_ref, kc_ref, vc_ref, kp_ref, vp_ref, b_ref, o_ref, *, past):
    q0, q1 = _split_halves(q_ref[0])
    scale = DH_C ** -0.5
    kp = kp_ref[0].astype(BF16)
    vp = vp_ref[0].astype(BF16)
    kc = kc_ref[0]
    vc = vc_ref[0]
    outs = []
    for hi, qh in enumerate((q0, q1)):
        b = b_ref[hi]
        sp = _mm_nt(qh, kp) * scale + b[:, :past]
        sc = _mm_nt(qh, kc) * scale + b[:, past:]
        m = jnp.maximum(jnp.max(sp, axis=1, keepdims=True), jnp.max(sc, axis=1, keepdims=True))
        pp = jnp.exp(sp - m)
        pc = jnp.exp(sc - m)
        l = jnp.sum(pp, axis=1, keepdims=True) + jnp.sum(pc, axis=1, keepdims=True)
        outs.append(_mm((pp / l).astype(BF16), vp) + _mm((pc / l).astype(BF16), vc))
    lane = lax.broadcasted_iota(jnp.int32, outs[0].shape, 1)
    o_ref[0] = jnp.where(lane < 64, outs[0], outs[1]).astype(BF16)


def _band_attn_sample(main, k_past, v_past, bias):
    nseq, lq, _ = main.shape
    past = k_past.shape[1]
    cur = lambda off: pl.BlockSpec((1, lq, 128), lambda b, p: (b, 0, off + p))
    old = pl.BlockSpec((1, past, 128), lambda b, p: (b, 0, p))
    return pl.pallas_call(
        functools.partial(_band_sample_kernel, past=past),
        grid=(nseq, H_C // 2),
        in_specs=[cur(0), cur(4), cur(8), old, old,
                  pl.BlockSpec((2, lq, past + lq), lambda b, p: (p, 0, 0))],
        out_specs=pl.BlockSpec((1, lq, 128), lambda b, p: (b, 0, p)),
        out_shape=jax.ShapeDtypeStruct((nseq, lq, H_C * DH_C), BF16),
        compiler_params=_params(("parallel", "parallel")),
        name="band_attn_sample",
    )(main, main, main, k_past, v_past, bias)


def _ssd_kernel(z_ref, xs_ref, bm_ref, cm_ref, dt_ref, cw_ref, cb_ref, dtb_ref, alog_ref, dskip_ref, ng_ref,
                past_ref, s0_ref, o_ref, sn_ref, buf_ref, st_ref, *, c):
    @pl.when(pl.program_id(1) == 0)
    def _():
        buf_ref[0:8, :] = past_ref[0]
        st_ref[...] = s0_ref[0]

    buf_ref[8:8 + c, 0:D_INNER] = xs_ref[0].astype(F32)
    buf_ref[8:8 + c, D_INNER:D_INNER + 256] = bm_ref[0].astype(F32)
    buf_ref[8:8 + c, D_INNER + 256:CONV_DIM_D] = cm_ref[0].astype(F32)
    conv = cb_ref[...] + buf_ref[pl.ds(5, c), :] * cw_ref[0:1, :]
    for i in range(1, CONV_D):
        conv = conv + buf_ref[pl.ds(5 + i, c), :] * cw_ref[i:i + 1, :]
    buf_ref[0:8, :] = buf_ref[c:c + 8, :]
    act = _silu(conv)
    xs = act[:, 0:D_INNER]
    bm = act[:, D_INNER:D_INNER + 256].astype(BF16)
    cm = act[:, D_INNER + 256:CONV_DIM_D].astype(BF16)

    dt = _softplus(dt_ref[0] + dtb_ref[...])
    da = dt * (-jnp.exp(alog_ref[...]))
    row = lax.broadcasted_iota(jnp.int32, (c, c), 0)
    col = lax.broadcasted_iota(jnp.int32, (c, c), 1)
    tri = row >= col
    cum = _mm_exact(tri.astype(F32), da)
    eye = (lax.broadcasted_iota(jnp.int32, (8, 128), 0) == lax.broadcasted_iota(jnp.int32, (8, 128), 1)).astype(F32)
    cum_t = _mm_exact(_mm_nt_exact(eye, da), (row <= col).astype(F32))
    dt_t = _mm_nt_exact(eye, dt)
    ecum = jnp.exp(cum)
    last = cum[c - 1:c, :]
    wgt = jnp.exp(last - cum) * dt
    elast = jnp.exp(last)
    lane = lax.broadcasted_iota(jnp.int32, (c, 128), 1)
    low = lane < 64
    rlow = lax.broadcasted_iota(jnp.int32, (128, 128), 0) < 64

    for g in range(G_D):
        bm_g = bm[:, g * 128:(g + 1) * 128]
        cm_g = cm[:, g * 128:(g + 1) * 128]
        cb = _mm_nt(cm_g, bm_g)
        ys = []
        for pp in range(2):
            p = 2 * g + pp
            h0, h1 = 2 * p, 2 * p + 1
            x_f = xs[:, p * 128:(p + 1) * 128]
            x_b = x_f.astype(BF16)
            y_in = []
            for hh in (h0, h1):
                seg = cum[:, hh:hh + 1] - cum_t[hh:hh + 1, :]
                dec = jnp.exp(jnp.where(tri, seg, NEG))
                y_in.append(_mm((cb * dec * dt_t[hh:hh + 1, :]).astype(BF16), x_b))
            st = st_ref[p * 128:(p + 1) * 128, :]
            y_x = _mm_nt(cm_g, st.astype(BF16)) * jnp.where(low, ecum[:, h0:h0 + 1], ecum[:, h1:h1 + 1])
            w2 = jnp.where(low, wgt[:, h0:h0 + 1], wgt[:, h1:h1 + 1])
            new = jnp.where(rlow, elast[:, h0:h0 + 1], elast[:, h1:h1 + 1]) * st + _mm_tn((x_f * w2).astype(BF16), bm_g)
            st_ref[p * 128:(p + 1) * 128, :] = new
            sn_ref[0, p * 128:(p + 1) * 128, :] = new
            y = jnp.where(low, y_in[0], y_in[1]) + y_x + jnp.where(low, dskip_ref[h0], dskip_ref[h1]) * x_f
            ys.append(y * _silu(z_ref[0, :, p * 128:(p + 1) * 128].astype(F32)))
        ms = (jnp.sum(ys[0] * ys[0], axis=-1, keepdims=True) + jnp.sum(ys[1] * ys[1], axis=-1, keepdims=True)) / 256.0
        inv = lax.rsqrt(ms + EPS)
        for pp in range(2):
            p = 2 * g + pp
            o_ref[0, :, p * 128:(p + 1) * 128] = (ys[pp] * inv * ng_ref[:, p * 128:(p + 1) * 128]).astype(BF16)


def _ssd(main, dt, conv_w, conv_b, dt_bias, a_log, d_skip, norm_g, conv_past, s0, c):
    nseq, seq, _ = main.shape
    pad = lambda a: jnp.pad(a.reshape(1, H_D), ((0, 0), (0, 128 - H_D)))
    blk = lambda w, idx: pl.BlockSpec((1, c, w), lambda b, l: (b, l, idx))
    const = lambda r, w: pl.BlockSpec((r, w), lambda b, l: (0, 0))
    state = pl.BlockSpec((1, H_D * P_D, N_D), lambda b, l: (b, 0, 0))
    return pl.pallas_call(
        functools.partial(_ssd_kernel, c=c),
        grid=(nseq, seq // c),
        in_specs=[blk(512, 3), blk(512, 4), blk(256, 10), blk(256, 11), blk(128, 0),
                  const(CONV_D, CONV_DIM_D), const(1, CONV_DIM_D), const(1, 128), const(1, 128),
                  pl.BlockSpec(memory_space=pltpu.SMEM), const(1, D_INNER),
                  pl.BlockSpec((1, 8, CONV_DIM_D), lambda b, l: (b, 0, 0)), state],
        out_specs=[blk(512, 0), state],
        out_shape=[jax.ShapeDtypeStruct((nseq, seq, D_INNER), BF16),
                   jax.ShapeDtypeStruct((nseq, H_D * P_D, N_D), F32)],
        scratch_shapes=[pltpu.VMEM((c + 8, CONV_DIM_D), F32), pltpu.VMEM((H_D * P_D, N_D), F32)],
        compiler_params=_params(("parallel", "arbitrary")),
        name="ssd",
    )(main, main, main, main, dt, conv_w, conv_b.reshape(1, CONV_DIM_D), pad(dt_bias), pad(a_log), d_skip,
      norm_g.reshape(1, D_INNER), conv_past, s0)


def _ffn_kernel(x_ref, o1_ref, o2_ref, wo_ref, g1_ref, ng_ref, sc_ref, sh_ref, g2_ref, wup_ref, cw_ref, cb_ref,
                wdn_ref, past_ref, fg_ref, out_ref, tail_ref, acc_ref, wb_ref, gt_ref, *, tl, d_ff, final):
    @pl.when(pl.program_id(1) == 0)
    def _():
        gt_ref[...] = past_ref[0]

    half = wo_ref.shape[0] // 2
    mix = _mm(o1_ref[0], wo_ref[0:half, :]) + _mm(o2_ref[0], wo_ref[half:2 * half, :])
    x1 = x_ref[0] + g1_ref[0] * mix
    out_ref[0] = x1
    h = _norm_mod(x1, ng_ref[...], sc_ref[0], sh_ref[0]).astype(BF16)
    for j in range(d_ff // FF_CHUNK):
        c0 = j * FF_CHUNK
        a = _mm(h, wup_ref[:, c0:c0 + FF_CHUNK])
        g = _mm(h, wup_ref[:, d_ff + c0:d_ff + c0 + FF_CHUNK])
        wb_ref[0:8, :] = gt_ref[:, c0:c0 + FF_CHUNK]
        wb_ref[8:8 + tl, :] = g
        gt_ref[:, c0:c0 + FF_CHUNK] = g[tl - 8:tl, :]
        gc = (cb_ref[:, c0:c0 + FF_CHUNK] + wb_ref[pl.ds(6, tl), :] * cw_ref[0:1, c0:c0 + FF_CHUNK]
              + wb_ref[pl.ds(7, tl), :] * cw_ref[1:2, c0:c0 + FF_CHUNK] + g * cw_ref[2:3, c0:c0 + FF_CHUNK])
        act = (a * _gelu_tanh(gc)).astype(BF16)
        dn = _mm(act, wdn_ref[c0:c0 + FF_CHUNK, :])
        if j == 0:
            acc_ref[...] = dn
        else:
            acc_ref[...] += dn
    x2 = out_ref[0] + g2_ref[0] * acc_ref[...]
    if final:
        ms = jnp.mean(x2 * x2, axis=-1, keepdims=True)
        x2 = (x2 * lax.rsqrt(ms + EPS)) * fg_ref[...]
    out_ref[0] = x2
    tail_ref[0] = gt_ref[...]


def _outproj_ffn(x, o1, o2, wo, g1, ng, sc, sh, g2, wup, cw, cb, wdn, past, fg, tl, final):
    nseq, seq, d = x.shape
    d_ff = wdn.shape[0]
    tok = lambda n: pl.BlockSpec((1, tl, n), lambda b, l: (b, l, 0))
    per_seq = pl.BlockSpec((1, 1, d), lambda b, l: (b, 0, 0))
    tail = pl.BlockSpec((1, 8, d_ff), lambda b, l: (b, 0, 0))
    return pl.pallas_call(
        functools.partial(_ffn_kernel, tl=tl, d_ff=d_ff, final=final),
        grid=(nseq, seq // tl),
        in_specs=[tok(d), tok(o1.shape[-1]), tok(o2.shape[-1]), _resident(wo.shape), per_seq, _resident((1, d)),
                  per_seq, per_seq, per_seq, _resident(wup.shape), _resident(cw.shape), _resident((1, d_ff)),
                  _resident(wdn.shape), tail, _resident((1, d))],
        out_specs=[tok(d), tail],
        out_shape=[jax.ShapeDtypeStruct((nseq, seq, d), F32), jax.ShapeDtypeStruct((nseq, 8, d_ff), F32)],
        scratch_shapes=[pltpu.VMEM((tl, d), F32), pltpu.VMEM((tl + 8, FF_CHUNK), F32), pltpu.VMEM((8, d_ff), F32)],
        compiler_params=_params(("parallel", "arbitrary")),
        name="outproj_ffn",
    )(x, o1, o2, wo, g1, ng, sc, sh, g2, wup, cw, cb.reshape(1, d_ff), wdn, past, fg)


def _rope_tables(pos):
    half = DK_A // 2
    inv = jnp.power(ROPE_BASE, -jnp.arange(half, dtype=F32) / half)
    ang = pos.astype(F32)[:, None] * inv[None, :]
    cos = jnp.cos(ang)
    sin = jnp.sin(ang)
    return jnp.concatenate([cos, cos], axis=1), jnp.concatenate([-sin, sin], axis=1)


def _pad_rows(a, rows=8):
    return jnp.pad(a, ((0, 0), (rows - a.shape[1], 0), (0, 0)))


def _trunk(x, mods, pos, weights, caches, biases):
    nseq, seq, d = x.shape
    sample = caches is not None
    tl = min(seq, 512)
    c = min(seq, SCAN_CHUNK)
    depth = weights["w_up"].shape[0]
    outs = {k: [] for k in ("ret", "bk", "bv", "ck", "cv", "dconv", "dssm", "ffn")}
    cos, sin = _rope_tables(pos)
    for l in range(depth):
        i = l // 2
        sh1, sc1, g1, sh2, sc2, g2 = mods[l]
        ng1 = weights["norm_g"][l, 0].reshape(1, d)
        ng2 = weights["norm_g"][l, 1].reshape(1, d)
        if l % 2 == 0:
            qkv, kb, vb = _inproj_ab(x, ng1, sc1, sh1, weights["w_in_ab"][i], cos, sin, tl)
            s0 = caches["ret"][i] if sample else jnp.zeros((nseq, H_A, DK_A, DV_A), F32)
            o1, s_new = _retention(qkv, s0, weights["ret_gn"][i], c)
            lam_init = 0.8 - 0.6 * math.exp(-0.3 * l)
            if sample:
                o2 = _diff_attn_sample(qkv, caches["bk"][i], caches["bv"][i], biases["t5"],
                                       weights["lam_q"][i], weights["lam_k"][i], weights["diff_gn"][i], lam_init)
            else:
                o2 = _diff_attn(qkv, biases["t5"], weights["lam_q"][i], weights["lam_k"][i],
                                weights["diff_gn"][i], lam_init)
            wo = weights["w_out_ab"][i]
            outs["ret"].append(s_new)
            outs["bk"].append(kb.reshape(nseq, seq, H_B, 2 * DH_B))
            outs["bv"].append(vb.reshape(nseq, seq, H_B, DV_B))
        else:
            main, dt, kc, vc, tail = _inproj_cd(x, ng1, sc1, sh1, weights["w_in_cd"][i], tl)
            if sample:
                o1 = _band_attn_sample(main, caches["ck"][i], caches["cv"][i], biases["band"][i])
                conv_past = _pad_rows(caches["dconv"][i])
                s0 = caches["dssm"][i].reshape(nseq, H_D * P_D, N_D)
                keep = seq
            else:
                o1 = _band_attn(main, biases["band"][i])
                conv_past = jnp.zeros((nseq, 8, CONV_DIM_D), F32)
                s0 = jnp.zeros((nseq, H_D * P_D, N_D), F32)
                keep = min(C_WINDOW, seq)
            o2, ssm_new = _ssd(main, dt, weights["d_conv_w"][i], weights["d_conv_b"][i], weights["d_dt_bias"][i],
                               weights["d_a_log"][i], weights["d_skip"][i], weights["d_norm_g"][i], conv_past, s0, c)
            wo = weights["w_out_cd"][i]
            outs["ck"].append(kc[:, seq - keep:].reshape(nseq, keep, H_C, DH_C))
            outs["cv"].append(vc[:, seq - keep:].reshape(nseq, keep, H_C, DH_C))
            outs["dconv"].append(tail[:, 8 - (CONV_D - 1):])
            outs["dssm"].append(ssm_new.reshape(nseq, H_D, P_D, N_D))
        d_ff = weights["w_down"].shape[1]
        ffn_past = _pad_rows(caches["ffn"][l]) if sample else jnp.zeros((nseq, 8, d_ff), F32)
        x, ftail = _outproj_ffn(x, o1, o2, wo, g1, ng2, sc2, sh2, g2, weights["w_up"][l], weights["ffn_conv_w"][l],
                                weights["ffn_conv_b"][l], weights["w_down"][l], ffn_past,
                                weights["final_g"].reshape(1, d), tl, final=(l == depth - 1))
        outs["ffn"].append(ftail[:, 8 - (CONV_F - 1):])
    stk = lambda t: jnp.stack(t).astype(F32)
    return (x,) + tuple(stk(outs[k]) for k in ("ret", "bk", "bv", "ck", "cv", "dconv", "dssm", "ffn"))


def kernel(x_prompt, x_sample, cache_ret_state, cache_b_k, cache_b_v, cache_c_k, cache_c_v, state_d_conv, state_d_ssm, state_ffn_conv, c_prompt, c_sample, w_mod, b_mod, norm_g, final_g, t5_table, w_in_ab, w_out_ab, ret_gn, lam_q, lam_k, diff_gn, w_in_cd, w_out_cd, rel_table, d_conv_w, d_conv_b, d_dt_bias, d_a_log, d_skip, d_norm_g, w_up, ffn_conv_w, ffn_conv_b, w_down):
    batch, seq, d = x_prompt.shape
    dec_batch, dec_seq, _ = x_sample.shape
    past = cache_b_k.shape[2]
    depth = w_mod.shape[0]
    assert dec_seq <= CHUNK and past % CHUNK == 0 and cache_c_k.shape[2] == C_WINDOW and seq % BAND_TQ == 0

    w_in_cd_p = jnp.pad(w_in_cd, ((0, 0), (0, 0), (0, N_CD_PAD - w_in_cd.shape[-1]))).astype(BF16)
    weights = dict(
        norm_g=norm_g, final_g=final_g, w_in_ab=w_in_ab.astype(BF16), w_out_ab=w_out_ab.astype(BF16),
        ret_gn=ret_gn, lam_q=lam_q, lam_k=lam_k, diff_gn=diff_gn, w_in_cd=w_in_cd_p,
        w_out_cd=w_out_cd.astype(BF16), d_conv_w=d_conv_w, d_conv_b=d_conv_b, d_dt_bias=d_dt_bias, d_a_log=d_a_log,
        d_skip=d_skip, d_norm_g=d_norm_g, w_up=w_up.astype(BF16), ffn_conv_w=ffn_conv_w, ffn_conv_b=ffn_conv_b,
        w_down=w_down.astype(BF16))

    mod = _modulation(jnp.concatenate([c_prompt, c_sample], axis=0), w_mod, b_mod)

    def pieces(rows):
        return [[m[:, None, :] for m in jnp.split(mod[l, rows], 6, axis=-1)] for l in range(depth)]

    band = [_band_bias(rel_table[i]) for i in range(rel_table.shape[0])]
    biases_p = dict(t5=_t5_bias_prompt(t5_table, seq), band=band)
    biases_s = dict(t5=_t5_bias_sample(t5_table, past, dec_seq),
                    band=[b[:, :dec_seq, :C_WINDOW + dec_seq] for b in band])
    caches = dict(
        ret=cache_ret_state,
        bk=cache_b_k.reshape(cache_b_k.shape[0], dec_batch, past, H_B * 2 * DH_B),
        bv=cache_b_v.reshape(cache_b_v.shape[0], dec_batch, past, H_B * DV_B),
        ck=cache_c_k.reshape(cache_c_k.shape[0], dec_batch, C_WINDOW, H_C * DH_C),
        cv=cache_c_v.reshape(cache_c_v.shape[0], dec_batch, C_WINDOW, H_C * DH_C),
        dconv=state_d_conv, dssm=state_d_ssm, ffn=state_ffn_conv)

    pos_p = jnp.arange(seq, dtype=jnp.int32)
    pos_s = past + jnp.arange(dec_seq, dtype=jnp.int32)
    y_p, ret_p, bk_p, bv_p, ck_p, cv_p, dconv_p, dssm_p, ffn_p = _trunk(
        x_prompt, pieces(slice(0, batch)), pos_p, weights, None, biases_p)
    y_s, ret_s, bk_s, bv_s, ck_s, cv_s, dconv_s, dssm_s, ffn_s = _trunk(
        x_sample, pieces(slice(batch, batch + dec_batch)), pos_s, weights, caches, biases_s)
    return (y_p, y_s, ret_p, ret_s, bk_p, bk_s, bv_p, bv_s, ck_p, ck_s, cv_p, cv_s,
            dconv_p, dconv_s, dssm_p, dssm_s, ffn_p, ffn_s)
```

```python
import functools
import math

import numpy as np
import jax
import jax.numpy as jnp
from jax import lax
from jax.experimental import pallas as pl
from jax.experimental.pallas import tpu as pltpu

F32 = jnp.float32
BF16 = jnp.bfloat16

CHUNK = 64
EPS = 1e-6
NEG = -1e30
H_A, DK_A, DV_A = 4, 128, 128
ROPE_BASE = 10000.0
H_B, DH_B, DV_B = 4, 64, 128
T5_BUCKETS, T5_MAX_DIST = 32, 128
H_C, DH_C = 8, 64
BAND_CHUNKS = 8
C_WINDOW = BAND_CHUNKS * CHUNK
REL_CLIP = 128
H_D, P_D, G_D, N_D = 8, 64, 2, 128
D_INNER = H_D * P_D
CONV_D = 4
CONV_DIM_D = D_INNER + 2 * G_D * N_D
CONV_F = 3
N_AB_COLS = 7 * 512
N_CD_MAIN = 6 * 512
N_CD_PAD = N_CD_MAIN + 128
FF_CHUNK = 256
BAND_TQ = 256
DIFF_TQ = 256
SCAN_CHUNK = 256
VMEM_LIMIT = 56 * 1024 * 1024


def _mm(a, b):
    return jnp.dot(a, b, preferred_element_type=F32)


def _mm_nt(a, b):
    return lax.dot_general(a, b, (((1,), (1,)), ((), ())), preferred_element_type=F32)


def _mm_tn(a, b):
    return lax.dot_general(a, b, (((0,), (0,)), ((), ())), preferred_element_type=F32)


def _mm_exact(a, b):
    return jnp.dot(a, b, preferred_element_type=F32, precision=lax.Precision.HIGHEST)


def _mm_nt_exact(a, b):
    return lax.dot_general(a, b, (((1,), (1,)), ((), ())), preferred_element_type=F32,
                           precision=lax.Precision.HIGHEST)


def _silu(x):
    return x * (1.0 / (1.0 + jnp.exp(-x)))


def _softplus(x):
    return jnp.maximum(x, 0.0) + jnp.log1p(jnp.exp(-jnp.abs(x)))


def _gelu_tanh(x):
    return x * (0.5 * (1.0 + jnp.tanh(math.sqrt(2.0 / math.pi) * (x + 0.044715 * (x * x * x)))))


def _norm_mod(x, g, sc, sh):
    ms = jnp.mean(x * x, axis=-1, keepdims=True)
    return (x * lax.rsqrt(ms + EPS)) * g * (1.0 + sc) + sh


def _params(sem):
    return pltpu.CompilerParams(dimension_semantics=sem, vmem_limit_bytes=VMEM_LIMIT)


def _resident(shape):
    nd = len(shape)
    return pl.BlockSpec(shape, lambda *_: (0,) * nd, pipeline_mode=pl.Buffered(1))


def _mod_kernel(c_ref, w_ref, b_ref, o_ref):
    c = c_ref[...]
    o_ref[0] = _mm(_silu(c).astype(BF16), w_ref[0].astype(BF16)) + b_ref[0]


def _modulation(c_all, w_mod, b_mod):
    depth, d, n = w_mod.shape
    r = c_all.shape[0]
    tn = 1536
    return pl.pallas_call(
        _mod_kernel,
        grid=(depth, n // tn),
        in_specs=[pl.BlockSpec((r, d), lambda l, j: (0, 0)),
                  pl.BlockSpec((1, d, tn), lambda l, j: (l, 0, j)),
                  pl.BlockSpec((1, 1, tn), lambda l, j: (l, 0, j))],
        out_specs=pl.BlockSpec((1, r, tn), lambda l, j: (l, 0, j)),
        out_shape=jax.ShapeDtypeStruct((depth, r, n), F32),
        compiler_params=_params(("parallel", "parallel")),
        name="modulation",
    )(c_all, w_mod, b_mod.reshape(depth, 1, n))


def _table_gather_kernel(tab_ref, idx_ref, o_ref, *, n_entries, n_heads):
    idx = idx_ref[...]

    def body(r, accs):
        m = idx == r
        return tuple(jnp.where(m, tab_ref[hh, r], a) for hh, a in enumerate(accs))

    accs = lax.fori_loop(0, n_entries, body, tuple(jnp.zeros(idx.shape, F32) for _ in range(n_heads)))
    for hh in range(n_heads):
        o_ref[hh:hh + 1, :] = accs[hh]


def _table_gather(table, idx):
    t, h = table.shape
    w = idx.shape[0]
    return pl.pallas_call(
        functools.partial(_table_gather_kernel, n_entries=t, n_heads=h),
        in_specs=[pl.BlockSpec(memory_space=pltpu.SMEM),
                  pl.BlockSpec((1, w), lambda: (0, 0))],
        out_specs=pl.BlockSpec((h, w), lambda: (0, 0)),
        out_shape=jax.ShapeDtypeStruct((h, w), F32),
        name="table_gather",
    )(table.T, idx.reshape(1, w))


def _toeplitz(vec, rows, cols):
    h, wv = vec.shape
    flat = jnp.tile(vec, (1, rows))[:, :rows * (wv - 1)]
    return flat.reshape(h, rows, wv - 1)[:, :, :cols]


def _t5_bucket(rel):
    half = T5_BUCKETS // 2
    max_exact = half // 2
    base = jnp.where(rel > 0, half, 0)
    n = jnp.abs(rel)
    nf = jnp.maximum(n, 1).astype(F32)
    large = max_exact + (jnp.log(nf / max_exact) / math.log(T5_MAX_DIST / max_exact) * (half - max_exact)).astype(jnp.int32)
    large = jnp.minimum(large, half - 1)
    return base + jnp.where(n < max_exact, n, large)


def _wrapped_offsets(n_pos, n_neg):
    p = jnp.arange(n_pos + n_neg, dtype=jnp.int32)
    return jnp.where(p < n_pos, p, p - (n_pos + n_neg))


def _t5_bias_prompt(t5_table, seq):
    nd = seq // DIFF_TQ
    u = _wrapped_offsets(seq, DIFF_TQ)
    vec = _table_gather(t5_table, _t5_bucket(u - (seq - DIFF_TQ)))
    wide = _toeplitz(vec, DIFF_TQ, seq)
    return wide.reshape(H_B, DIFF_TQ, nd, DIFF_TQ).transpose(0, 2, 1, 3)


def _t5_bias_sample(t5_table, past, lq):
    u = _wrapped_offsets(past + lq, lq)
    vec = _table_gather(t5_table, _t5_bucket(u - past))
    return _toeplitz(vec, lq, past + lq)


def _band_bias(rel_table):
    u = _wrapped_offsets(3 * BAND_TQ, BAND_TQ)
    idx = jnp.clip(2 * BAND_TQ - u, -REL_CLIP, REL_CLIP) + REL_CLIP
    bias = _toeplitz(_table_gather(rel_table, idx), BAND_TQ, 3 * BAND_TQ)
    dc = (np.arange(BAND_TQ)[:, None] // CHUNK + C_WINDOW // CHUNK) - np.arange(3 * BAND_TQ)[None, :] // CHUNK
    return jnp.where(jnp.asarray((dc >= 0) & (dc <= BAND_CHUNKS))[None], bias, NEG)


def _inproj_ab_kernel(x_ref, g_ref, sc_ref, sh_ref, w_ref, cos_ref, sin_ref, qkv_ref, kb_ref, vb_ref):
    h = _norm_mod(x_ref[0], g_ref[...], sc_ref[0], sh_ref[0]).astype(BF16)
    cos = cos_ref[...]
    sin = sin_ref[...]
    for j in range(7):
        y = _mm(h, w_ref[:, j * 512:(j + 1) * 512])
        if j < 2:
            for hh in range(H_A):
                yh = y[:, hh * 128:(hh + 1) * 128]
                yh = yh * cos + pltpu.roll(yh, 64, 1) * sin
                if j == 1:
                    yh = yh * (DK_A ** -0.5)
                qkv_ref[0, :, j * 512 + hh * 128:j * 512 + (hh + 1) * 128] = yh.astype(BF16)
        else:
            if j == 5:
                kb_ref[0] = y
            if j == 6:
                vb_ref[0] = y
            qkv_ref[0, :, j * 512:(j + 1) * 512] = y.astype(BF16)


def _inproj_ab(x, g, sc, sh, w, cos, sin, tl):
    nseq, seq, d = x.shape
    grid = (nseq, seq // tl)
    tok = lambda n: pl.BlockSpec((1, tl, n), lambda b, l: (b, l, 0))
    per_seq = pl.BlockSpec((1, 1, d), lambda b, l: (b, 0, 0))
    return pl.pallas_call(
        _inproj_ab_kernel,
        grid=grid,
        in_specs=[tok(d), _resident((1, d)), per_seq, per_seq, _resident((d, N_AB_COLS)),
                  pl.BlockSpec((tl, 128), lambda b, l: (l, 0)),
                  pl.BlockSpec((tl, 128), lambda b, l: (l, 0))],
        out_specs=[tok(N_AB_COLS), tok(512), tok(512)],
        out_shape=[jax.ShapeDtypeStruct((nseq, seq, N_AB_COLS), BF16),
                   jax.ShapeDtypeStruct((nseq, seq, 512), F32),
                   jax.ShapeDtypeStruct((nseq, seq, 512), F32)],
        compiler_params=_params(("parallel", "parallel")),
        name="inproj_ab",
    )(x, g, sc, sh, w, cos, sin)


def _retention_kernel(q_ref, k_ref, v_ref, g_ref, s0_ref, dec_ref, qd_ref, kd_ref, gc_ref, gn_ref,
                      o_ref, sn_ref, st_ref):
    @pl.when(pl.program_id(1) == 0)
    def _():
        st_ref[...] = s0_ref[0]

    for h in range(H_A):
        cs = slice(h * 128, (h + 1) * 128)
        q = q_ref[0, :, cs]
        k = k_ref[0, :, cs]
        v = v_ref[0, :, cs]
        state = st_ref[h]
        s = _mm_nt(q, k) * dec_ref[h]
        o = _mm(s.astype(BF16), v) + _mm((q.astype(F32) * qd_ref[h]).astype(BF16), state.astype(BF16))
        new_state = gc_ref[h] * state + _mm_tn((k.astype(F32) * kd_ref[h]).astype(BF16), v)
        st_ref[h] = new_state
        sn_ref[0, h] = new_state
        mu = jnp.mean(o, axis=-1, keepdims=True)
        dlt = o - mu
        var = jnp.mean(dlt * dlt, axis=-1, keepdims=True)
        y = dlt * lax.rsqrt(var + EPS) * gn_ref[:, cs] * _silu(g_ref[0, :, cs].astype(F32))
        o_ref[0, :, cs] = y.astype(BF16)


def _retention_consts(c):
    lg = np.log1p(-np.exp2(-5.0 - np.arange(H_A, dtype=np.float32))).astype(np.float32)
    idx = np.arange(c, dtype=np.float32)
    diff = idx[:, None] - idx[None, :]
    decay = np.where(diff[None] >= 0, np.exp(np.maximum(diff, 0.0)[None] * lg[:, None, None]), 0.0)
    qd = np.exp((idx + 1.0)[None, :] * lg[:, None])
    kd = np.exp((c - 1.0 - idx)[None, :] * lg[:, None])
    gc = np.exp(c * lg)
    bc = lambda a: np.ascontiguousarray(np.broadcast_to(a[..., None], a.shape + (128,))).astype(np.float32)
    return decay.astype(np.float32), bc(qd), bc(kd), bc(gc[:, None])


def _retention(qkv, s0, ret_gn, c):
    nseq, seq, _ = qkv.shape
    decay, qd, kd, gc = _retention_consts(c)
    w = H_A * 128
    col = lambda idx: pl.BlockSpec((1, c, w), lambda b, l: (b, l, idx))
    const = lambda shape: pl.BlockSpec(shape, lambda b, l: (0,) * len(shape))
    state = pl.BlockSpec((1, H_A, DK_A, DV_A), lambda b, l: (b, 0, 0, 0))
    return pl.pallas_call(
        _retention_kernel,
        grid=(nseq, seq // c),
        in_specs=[col(0), col(1), col(2), col(3), state, const((H_A, c, c)), const((H_A, c, 128)),
                  const((H_A, c, 128)), const((H_A, 1, 128)), const((1, w))],
        out_specs=[col(0), state],
        out_shape=[jax.ShapeDtypeStruct((nseq, seq, w), BF16),
                   jax.ShapeDtypeStruct((nseq, H_A, DK_A, DV_A), F32)],
        scratch_shapes=[pltpu.VMEM((H_A, DK_A, DV_A), F32)],
        compiler_params=_params(("parallel", "arbitrary")),
        name="retention",
    )(qkv, qkv, qkv, qkv, s0, decay, qd, kd, gc, ret_gn.reshape(1, w))


def _split_halves(q):
    lane = lax.broadcasted_iota(jnp.int32, q.shape, 1)
    zero = jnp.zeros_like(q)
    return jnp.where(lane < 64, q, zero), jnp.where(lane >= 64, q, zero)


def _lambda(lq_ref, lk_ref, lam_init):
    e = jnp.exp(jnp.sum(lq_ref[...] * lk_ref[...], axis=1, keepdims=True))
    return e[0:1, :] - e[1:2, :] + lam_init


def _diff_epilogue(o, gn, lam_init):
    ms = jnp.mean(o * o, axis=-1, keepdims=True)
    return (o * lax.rsqrt(ms + EPS)) * gn * (1.0 - lam_init)


def _diff_attn_kernel(q_ref, k_ref, v_ref, b_ref, lq_ref, lk_ref, gn_ref, o_ref, s_ref, mx_ref, ls_ref, acc_ref,
                      *, tq, nd, lam_init):
    t = pl.program_id(2)
    qs = _split_halves((q_ref[0].astype(F32) * (DH_B ** -0.5)).astype(BF16))
    row = lax.broadcasted_iota(jnp.int32, (tq, tq), 0) // CHUNK
    col = lax.broadcasted_iota(jnp.int32, (tq, tq), 1) // CHUNK
    chunk_mask = col <= row
    mx_ref[...] = jnp.full(mx_ref.shape, NEG, F32)

    def scores(kb, masked):
        start = pl.multiple_of(kb * tq, tq)
        k = k_ref[0, pl.ds(start, tq), :]
        b = b_ref[0, nd - 1 - t + kb]
        for i in range(2):
            s = _mm_nt(qs[i], k) + b
            if masked:
                s = jnp.where(chunk_mask, s, NEG)
            s_ref[i * nd + kb] = s
            m = mx_ref[i]
            for c0 in range(0, tq, 128):
                m = jnp.maximum(m, s[:, c0:c0 + 128])
            mx_ref[i] = m

    def scores_body(kb, carry):
        scores(kb, False)
        return carry

    lax.fori_loop(0, t, scores_body, 0)
    scores(t, True)
    ms = [jnp.max(mx_ref[i], axis=1, keepdims=True) for i in range(2)]
    ls_ref[...] = jnp.zeros(ls_ref.shape, F32)
    acc_ref[...] = jnp.zeros(acc_ref.shape, F32)

    def pv_body(kb, carry):
        start = pl.multiple_of(kb * tq, tq)
        v = v_ref[0, pl.ds(start, tq), :]
        for i in range(2):
            p = jnp.exp(s_ref[i * nd + kb] - ms[i])
            l = ls_ref[i]
            for c0 in range(0, tq, 128):
                l = l + p[:, c0:c0 + 128]
            ls_ref[i] = l
            acc_ref[i] += _mm(p.astype(BF16), v)
        return carry

    lax.fori_loop(0, t + 1, pv_body, 0)
    l0 = jnp.sum(ls_ref[0], axis=1, keepdims=True)
    l1 = jnp.sum(ls_ref[1], axis=1, keepdims=True)
    lam = _lambda(lq_ref, lk_ref, lam_init)
    o = acc_ref[0] / l0 - lam * (acc_ref[1] / l1)
    o_ref[0] = _diff_epilogue(o, gn_ref[...], lam_init).astype(BF16)


def _diff_attn(qkv, bias, lam_q, lam_k, diff_gn, lam_init):
    nseq, seq, _ = qkv.shape
    tq = DIFF_TQ
    nd = seq // tq
    full = lambda off: pl.BlockSpec((1, seq, 128), lambda b, h, t: (b, 0, off + h))
    small = lambda r, c: pl.BlockSpec((r, c), lambda b, h, t: (0, 0))
    return pl.pallas_call(
        functools.partial(_diff_attn_kernel, tq=tq, nd=nd, lam_init=lam_init),
        grid=(nseq, H_B, nd),
        in_specs=[pl.BlockSpec((1, tq, 128), lambda b, h, t: (b, t, 16 + h)), full(20), full(24),
                  pl.BlockSpec((1, nd, tq, tq), lambda b, h, t: (h, 0, 0, 0)),
                  small(2, DH_B), small(2, DH_B), small(1, DV_B)],
        out_specs=pl.BlockSpec((1, tq, 128), lambda b, h, t: (b, t, h)),
        out_shape=jax.ShapeDtypeStruct((nseq, seq, H_B * DV_B), BF16),
        scratch_shapes=[pltpu.VMEM((2 * nd, tq, tq), F32), pltpu.VMEM((2, tq, 128), F32),
                        pltpu.VMEM((2, tq, 128), F32), pltpu.VMEM((2, tq, DV_B), F32)],
        compiler_params=_params(("parallel", "parallel", "arbitrary")),
        name="diff_attn",
    )(qkv, qkv, qkv, bias, lam_q, lam_k, diff_gn.reshape(1, DV_B))


def _diff_attn_sample_kernel(q_ref, kc_ref, vc_ref, kp_ref, vp_ref, b_ref, lq_ref, lk_ref, gn_ref, o_ref,
                             *, past, lam_init):
    qs = _split_halves(q_ref[0])
    scale = DH_B ** -0.5
    kp = kp_ref[0].astype(BF16)
    vp = vp_ref[0].astype(BF16)
    kc = kc_ref[0]
    vc = vc_ref[0]
    b = b_ref[0]
    probs = []
    for i in range(2):
        sp = _mm_nt(qs[i], kp) * scale + b[:, :past]
        sc = _mm_nt(qs[i], kc) * scale + b[:, past:]
        m = jnp.maximum(jnp.max(sp, axis=1, keepdims=True), jnp.max(sc, axis=1, keepdims=True))
        pp = jnp.exp(sp - m)
        pc = jnp.exp(sc - m)
        l = jnp.sum(pp, axis=1, keepdims=True) + jnp.sum(pc, axis=1, keepdims=True)
        probs.append((pp / l, pc / l))
    lam = _lambda(lq_ref, lk_ref, lam_init)
    ap = probs[0][0] - lam * probs[1][0]
    ac = probs[0][1] - lam * probs[1][1]
    o = _mm(ap.astype(BF16), vp) + _mm(ac.astype(BF16), vc)
    o_ref[0] = _diff_epilogue(o, gn_ref[...], lam_init).astype(BF16)


def _diff_attn_sample(qkv, k_past, v_past, bias, lam_q, lam_k, diff_gn, lam_init):
    nseq, lq, _ = qkv.shape
    past = k_past.shape[1]
    cur = lambda off: pl.BlockSpec((1, lq, 128), lambda b, h: (b, 0, off + h))
    old = pl.BlockSpec((1, past, 128), lambda b, h: (b, 0, h))
    small = lambda r, c: pl.BlockSpec((r, c), lambda b, h: (0, 0))
    return pl.pallas_call(
        functools.partial(_diff_attn_sample_kernel, past=past, lam_init=lam_init),
        grid=(nseq, H_B),
        in_specs=[cur(16), cur(20), cur(24), old, old,
                  pl.BlockSpec((1, lq, past + lq), lambda b, h: (h, 0, 0)),
                  small(2, DH_B), small(2, DH_B), small(1, DV_B)],
        out_specs=pl.BlockSpec((1, lq, 128), lambda b, h: (b, 0, h)),
        out_shape=jax.ShapeDtypeStruct((nseq, lq, H_B * DV_B), BF16),
        compiler_params=_params(("parallel", "parallel")),
        name="diff_attn_sample",
    )(qkv, qkv, qkv, k_past, v_past, bias, lam_q, lam_k, diff_gn.reshape(1, DV_B))


def _inproj_cd_kernel(x_ref, g_ref, sc_ref, sh_ref, w_ref, main_ref, dt_ref, kc_ref, vc_ref, tail_ref, *, tl):
    h = _norm_mod(x_ref[0], g_ref[...], sc_ref[0], sh_ref[0]).astype(BF16)
    for j in range(6):
        y = _mm(h, w_ref[:, j * 512:(j + 1) * 512])
        main_ref[0, :, j * 512:(j + 1) * 512] = y.astype(BF16)
        if j == 1:
            kc_ref[0] = y
        if j == 2:
            vc_ref[0] = y
        if j >= 4:
            tail_ref[0, :, (j - 4) * 512:(j - 3) * 512] = y[tl - 8:tl, :]
    dt_ref[0] = _mm(h, w_ref[:, N_CD_MAIN:N_CD_PAD])


def _inproj_cd(x, g, sc, sh, w, tl):
    nseq, seq, d = x.shape
    tok = lambda n: pl.BlockSpec((1, tl, n), lambda b, l: (b, l, 0))
    per_seq = pl.BlockSpec((1, 1, d), lambda b, l: (b, 0, 0))
    return pl.pallas_call(
        functools.partial(_inproj_cd_kernel, tl=tl),
        grid=(nseq, seq // tl),
        in_specs=[tok(d), _resident((1, d)), per_seq, per_seq, _resident((d, N_CD_PAD))],
        out_specs=[tok(N_CD_MAIN), tok(128), tok(512), tok(512),
                   pl.BlockSpec((1, 8, CONV_DIM_D), lambda b, l: (b, 0, 0))],
        out_shape=[jax.ShapeDtypeStruct((nseq, seq, N_CD_MAIN), BF16),
                   jax.ShapeDtypeStruct((nseq, seq, 128), F32),
                   jax.ShapeDtypeStruct((nseq, seq, 512), F32),
                   jax.ShapeDtypeStruct((nseq, seq, 512), F32),
                   jax.ShapeDtypeStruct((nseq, 8, CONV_DIM_D), F32)],
        compiler_params=_params(("parallel", "arbitrary")),
        name="inproj_cd",
    )(x, g, sc, sh, w)


def _band_kernel(q_ref, k_ref, v_ref, b_ref, o_ref, *, tq):
    t = pl.program_id(2)
    q0, q1 = _split_halves((q_ref[0].astype(F32) * (DH_C ** -0.5)).astype(BF16))

    def attend(near_start):
        ks, vs, offs = [], [], []
        for d in range(3):
            kt = t - 2 + d
            start = pl.multiple_of(jnp.maximum(kt, 0) * tq, tq)
            ks.append(k_ref[0, pl.ds(start, tq), :])
            vs.append(v_ref[0, pl.ds(start, tq), :])
            offs.append(jnp.where(kt >= 0, 0.0, NEG))
        outs = []
        for hi, qh in enumerate((q0, q1)):
            ss = []
            for d in range(3):
                s = _mm_nt(qh, ks[d]) + b_ref[hi, :, d * tq:(d + 1) * tq]
                ss.append(s + offs[d] if near_start and d < 2 else s)
            m = jnp.maximum(jnp.maximum(jnp.max(ss[0], axis=1, keepdims=True),
                                        jnp.max(ss[1], axis=1, keepdims=True)),
                            jnp.max(ss[2], axis=1, keepdims=True))
            l = jnp.zeros((tq, 1), F32)
            o = jnp.zeros((tq, 128), F32)
            for d in range(3):
                p = jnp.exp(ss[d] - m)
                l = l + jnp.sum(p, axis=1, keepdims=True)
                o = o + _mm(p.astype(BF16), vs[d])
            outs.append(o / l)
        lane = lax.broadcasted_iota(jnp.int32, (tq, 128), 1)
        o_ref[0] = jnp.where(lane < 64, outs[0], outs[1]).astype(BF16)

    @pl.when(t < 2)
    def _():
        attend(True)

    @pl.when(t >= 2)
    def _():
        attend(False)


def _band_attn(main, bias):
    nseq, seq, _ = main.shape
    tq = BAND_TQ
    full = lambda off: pl.BlockSpec((1, seq, 128), lambda b, p, t: (b, 0, off + p))
    return pl.pallas_call(
        functools.partial(_band_kernel, tq=tq),
        grid=(nseq, H_C // 2, seq // tq),
        in_specs=[pl.BlockSpec((1, tq, 128), lambda b, p, t: (b, t, p)), full(4), full(8),
                  pl.BlockSpec((2, tq, 3 * tq), lambda b, p, t: (p, 0, 0))],
        out_specs=pl.BlockSpec((1, tq, 128), lambda b, p, t: (b, t, p)),
        out_shape=jax.ShapeDtypeStruct((nseq, seq, H_C * DH_C), BF16),
        compiler_params=_params(("parallel", "parallel", "parallel")),
        name="band_attn",
    )(main, main, main, bias)


def _band_sample_kernel(q_ref, kc_ref, vc_ref, kp_ref, vp_ref, b_ref, o_ref, *, past):
    q0, q1 = _split_halves(q_ref[0])
    scale = DH_C ** -0.5
    kp = kp_ref[0].astype(BF16)
    vp = vp_ref[0].astype(BF16)
    kc = kc_ref[0]
    vc = vc_ref[0]
    outs = []
    for hi, qh in enumerate((q0, q1)):
        b = b_ref[hi]
        sp = _mm_nt(qh, kp) * scale + b[:, :past]
        sc = _mm_nt(qh, kc) * scale + b[:, past:]
        m = jnp.maximum(jnp.max(sp, axis=1, keepdims=True), jnp.max(sc, axis=1, keepdims=True))
        pp = jnp.exp(sp - m)
        pc = jnp.exp(sc - m)
        l = jnp.sum(pp, axis=1, keepdims=True) + jnp.sum(pc, axis=1, keepdims=True)
        outs.append(_mm((pp / l).astype(BF16), vp) + _mm((pc / l).astype(BF16), vc))
    lane = lax.broadcasted_iota(jnp.int32, outs[0].shape, 1)
    o_ref[0] = jnp.where(lane < 64, outs[0], outs[1]).astype(BF16)


def _band_attn_sample(main, k_past, v_past, bias):
    nseq, lq, _ = main.shape
    past = k_past.shape[1]
    cur = lambda off: pl.BlockSpec((1, lq, 128), lambda b, p: (b, 0, off + p))
    old = pl.BlockSpec((1, past, 128), lambda b, p: (b, 0, p))
    return pl.pallas_call(
        functools.partial(_band_sample_kernel, past=past),
        grid=(nseq, H_C // 2),
        in_specs=[cur(0), cur(4), cur(8), old, old,
                  pl.BlockSpec((2, lq, past + lq), lambda b, p: (p, 0, 0))],
        out_specs=pl.BlockSpec((1, lq, 128), lambda b, p: (b, 0, p)),
        out_shape=jax.ShapeDtypeStruct((nseq, lq, H_C * DH_C), BF16),
        compiler_params=_params(("parallel", "parallel")),
        name="band_attn_sample",
    )(main, main, main, k_past, v_past, bias)


def _ssd_kernel(z_ref, xs_ref, bm_ref, cm_ref, dt_ref, cw_ref, cb_ref, dtb_ref, alog_ref, dskip_ref, ng_ref,
                past_ref, s0_ref, o_ref, sn_ref, buf_ref, st_ref, *, c):
    @pl.when(pl.program_id(1) == 0)
    def _():
        buf_ref[0:8, :] = past_ref[0]
        st_ref[...] = s0_ref[0]

    buf_ref[8:8 + c, 0:D_INNER] = xs_ref[0].astype(F32)
    buf_ref[8:8 + c, D_INNER:D_INNER + 256] = bm_ref[0].astype(F32)
    buf_ref[8:8 + c, D_INNER + 256:CONV_DIM_D] = cm_ref[0].astype(F32)
    conv = cb_ref[...] + buf_ref[pl.ds(5, c), :] * cw_ref[0:1, :]
    for i in range(1, CONV_D):
        conv = conv + buf_ref[pl.ds(5 + i, c), :] * cw_ref[i:i + 1, :]
    buf_ref[0:8, :] = buf_ref[c:c + 8, :]
    act = _silu(conv)
    xs = act[:, 0:D_INNER]
    bm = act[:, D_INNER:D_INNER + 256].astype(BF16)
    cm = act[:, D_INNER + 256:CONV_DIM_D].astype(BF16)

    dt = _softplus(dt_ref[0] + dtb_ref[...])
    da = dt * (-jnp.exp(alog_ref[...]))
    row = lax.broadcasted_iota(jnp.int32, (c, c), 0)
    col = lax.broadcasted_iota(jnp.int32, (c, c), 1)
    tri = row >= col
    cum = _mm_exact(tri.astype(F32), da)
    eye = (lax.broadcasted_iota(jnp.int32, (8, 128), 0) == lax.broadcasted_iota(jnp.int32, (8, 128), 1)).astype(F32)
    cum_t = _mm_exact(_mm_nt_exact(eye, da), (row <= col).astype(F32))
    dt_t = _mm_nt_exact(eye, dt)
    ecum = jnp.exp(cum)
    last = cum[c - 1:c, :]
    wgt = jnp.exp(last - cum) * dt
    elast = jnp.exp(last)
    lane = lax.broadcasted_iota(jnp.int32, (c, 128), 1)
    low = lane < 64
    rlow = lax.broadcasted_iota(jnp.int32, (128, 128), 0) < 64

    for g in range(G_D):
        bm_g = bm[:, g * 128:(g + 1) * 128]
        cm_g = cm[:, g * 128:(g + 1) * 128]
        cb = _mm_nt(cm_g, bm_g)
        ys = []
        for pp in range(2):
            p = 2 * g + pp
            h0, h1 = 2 * p, 2 * p + 1
            x_f = xs[:, p * 128:(p + 1) * 128]
            x_b = x_f.astype(BF16)
            y_in = []
            for hh in (h0, h1):
                seg = cum[:, hh:hh + 1] - cum_t[hh:hh + 1, :]
                dec = jnp.exp(jnp.where(tri, seg, NEG))
                y_in.append(_mm((cb * dec * dt_t[hh:hh + 1, :]).astype(BF16), x_b))
            st = st_ref[p * 128:(p + 1) * 128, :]
            y_x = _mm_nt(cm_g, st.astype(BF16)) * jnp.where(low, ecum[:, h0:h0 + 1], ecum[:, h1:h1 + 1])
            w2 = jnp.where(low, wgt[:, h0:h0 + 1], wgt[:, h1:h1 + 1])
            new = jnp.where(rlow, elast[:, h0:h0 + 1], elast[:, h1:h1 + 1]) * st + _mm_tn((x_f * w2).astype(BF16), bm_g)
            st_ref[p * 128:(p + 1) * 128, :] = new
            sn_ref[0, p * 128:(p + 1) * 128, :] = new
            y = jnp.where(low, y_in[0], y_in[1]) + y_x + jnp.where(low, dskip_ref[h0], dskip_ref[h1]) * x_f
            ys.append(y * _silu(z_ref[0, :, p * 128:(p + 1) * 128].astype(F32)))
        ms = (jnp.sum(ys[0] * ys[0], axis=-1, keepdims=True) + jnp.sum(ys[1] * ys[1], axis=-1, keepdims=True)) / 256.0
        inv = lax.rsqrt(ms + EPS)
        for pp in range(2):
            p = 2 * g + pp
            o_ref[0, :, p * 128:(p + 1) * 128] = (ys[pp] * inv * ng_ref[:, p * 128:(p + 1) * 128]).astype(BF16)


def _ssd(main, dt, conv_w, conv_b, dt_bias, a_log, d_skip, norm_g, conv_past, s0, c):
    nseq, seq, _ = main.shape
    pad = lambda a: jnp.pad(a.reshape(1, H_D), ((0, 0), (0, 128 - H_D)))
    blk = lambda w, idx: pl.BlockSpec((1, c, w), lambda b, l: (b, l, idx))
    const = lambda r, w: pl.BlockSpec((r, w), lambda b, l: (0, 0))
    state = pl.BlockSpec((1, H_D * P_D, N_D), lambda b, l: (b, 0, 0))
    return pl.pallas_call(
        functools.partial(_ssd_kernel, c=c),
        grid=(nseq, seq // c),
        in_specs=[blk(512, 3), blk(512, 4), blk(256, 10), blk(256, 11), blk(128, 0),
                  const(CONV_D, CONV_DIM_D), const(1, CONV_DIM_D), const(1, 128), const(1, 128),
                  pl.BlockSpec(memory_space=pltpu.SMEM), const(1, D_INNER),
                  pl.BlockSpec((1, 8, CONV_DIM_D), lambda b, l: (b, 0, 0)), state],
        out_specs=[blk(512, 0), state],
        out_shape=[jax.ShapeDtypeStruct((nseq, seq, D_INNER), BF16),
                   jax.ShapeDtypeStruct((nseq, H_D * P_D, N_D), F32)],
        scratch_shapes=[pltpu.VMEM((c + 8, CONV_DIM_D), F32), pltpu.VMEM((H_D * P_D, N_D), F32)],
        compiler_params=_params(("parallel", "arbitrary")),
        name="ssd",
    )(main, main, main, main, dt, conv_w, conv_b.reshape(1, CONV_DIM_D), pad(dt_bias), pad(a_log), d_skip,
      norm_g.reshape(1, D_INNER), conv_past, s0)


def _ffn_kernel(x_ref, o1_ref, o2_ref, wo_ref, g1_ref, ng_ref, sc_ref, sh_ref, g2_ref, wup_ref, cw_ref, cb_ref,
                wdn_ref, past_ref, fg_ref, out_ref, tail_ref, acc_ref, wb_ref, gt_ref, *, tl, d_ff, final):
    @pl.when(pl.program_id(1) == 0)
    def _():
        gt_ref[...] = past_ref[0]

    half = wo_ref.shape[0] // 2
    mix = _mm(o1_ref[0], wo_ref[0:half, :]) + _mm(o2_ref[0], wo_ref[half:2 * half, :])
    x1 = x_ref[0] + g1_ref[0] * mix
    out_ref[0] = x1
    h = _norm_mod(x1, ng_ref[...], sc_ref[0], sh_ref[0]).astype(BF16)
    for j in range(d_ff // FF_CHUNK):
        c0 = j * FF_CHUNK
        a = _mm(h, wup_ref[:, c0:c0 + FF_CHUNK])
        g = _mm(h, wup_ref[:, d_ff + c0:d_ff + c0 + FF_CHUNK])
        wb_ref[0:8, :] = gt_ref[:, c0:c0 + FF_CHUNK]
        wb_ref[8:8 + tl, :] = g
        gt_ref[:, c0:c0 + FF_CHUNK] = g[tl - 8:tl, :]
        gc = (cb_ref[:, c0:c0 + FF_CHUNK] + wb_ref[pl.ds(6, tl), :] * cw_ref[0:1, c0:c0 + FF_CHUNK]
              + wb_ref[pl.ds(7, tl), :] * cw_ref[1:2, c0:c0 + FF_CHUNK] + g * cw_ref[2:3, c0:c0 + FF_CHUNK])
        act = (a * _gelu_tanh(gc)).astype(BF16)
        dn = _mm(act, wdn_ref[c0:c0 + FF_CHUNK, :])
        if j == 0:
            acc_ref[...] = dn
        else:
            acc_ref[...] += dn
    x2 = out_ref[0] + g2_ref[0] * acc_ref[...]
    if final:
        ms = jnp.mean(x2 * x2, axis=-1, keepdims=True)
        x2 = (x2 * lax.rsqrt(ms + EPS)) * fg_ref[...]
    out_ref[0] = x2
    tail_ref[0] = gt_ref[...]


def _outproj_ffn(x, o1, o2, wo, g1, ng, sc, sh, g2, wup, cw, cb, wdn, past, fg, tl, final):
    nseq, seq, d = x.shape
    d_ff = wdn.shape[0]
    tok = lambda n: pl.BlockSpec((1, tl, n), lambda b, l: (b, l, 0))
    per_seq = pl.BlockSpec((1, 1, d), lambda b, l: (b, 0, 0))
    tail = pl.BlockSpec((1, 8, d_ff), lambda b, l: (b, 0, 0))
    return pl.pallas_call(
        functools.partial(_ffn_kernel, tl=tl, d_ff=d_ff, final=final),
        grid=(nseq, seq // tl),
        in_specs=[tok(d), tok(o1.shape[-1]), tok(o2.shape[-1]), _resident(wo.shape), per_seq, _resident((1, d)),
                  per_seq, per_seq, per_seq, _resident(wup.shape), _resident(cw.shape), _resident((1, d_ff)),
                  _resident(wdn.shape), tail, _resident((1, d))],
        out_specs=[tok(d), tail],
        out_shape=[jax.ShapeDtypeStruct((nseq, seq, d), F32), jax.ShapeDtypeStruct((nseq, 8, d_ff), F32)],
        scratch_shapes=[pltpu.VMEM((tl, d), F32), pltpu.VMEM((tl + 8, FF_CHUNK), F32), pltpu.VMEM((8, d_ff), F32)],
        compiler_params=_params(("parallel", "arbitrary")),
        name="outproj_ffn",
    )(x, o1, o2, wo, g1, ng, sc, sh, g2, wup, cw, cb.reshape(1, d_ff), wdn, past, fg)


def _rope_tables(pos):
    half = DK_A // 2
    inv = jnp.power(ROPE_BASE, -jnp.arange(half, dtype=F32) / half)
    ang = pos.astype(F32)[:, None] * inv[None, :]
    cos = jnp.cos(ang)
    sin = jnp.sin(ang)
    return jnp.concatenate([cos, cos], axis=1), jnp.concatenate([-sin, sin], axis=1)


def _pad_rows(a, rows=8):
    return jnp.pad(a, ((0, 0), (rows - a.shape[1], 0), (0, 0)))


def _trunk(x, mods, pos, weights, caches, biases):
    nseq, seq, d = x.shape
    sample = caches is not None
    tl = min(seq, 512)
    c = min(seq, SCAN_CHUNK)
    depth = weights["w_up"].shape[0]
    outs = {k: [] for k in ("ret", "bk", "bv", "ck", "cv", "dconv", "dssm", "ffn")}
    cos, sin = _rope_tables(pos)
    for l in range(depth):
        i = l // 2
        sh1, sc1, g1, sh2, sc2, g2 = mods[l]
        ng1 = weights["norm_g"][l, 0].reshape(1, d)
        ng2 = weights["norm_g"][l, 1].reshape(1, d)
        if l % 2 == 0:
            qkv, kb, vb = _inproj_ab(x, ng1, sc1, sh1, weights["w_in_ab"][i], cos, sin, tl)
            s0 = caches["ret"][i] if sample else jnp.zeros((nseq, H_A, DK_A, DV_A), F32)
            o1, s_new = _retention(qkv, s0, weights["ret_gn"][i], c)
            lam_init = 0.8 - 0.6 * math.exp(-0.3 * l)
            if sample:
                o2 = _diff_attn_sample(qkv, caches["bk"][i], caches["bv"][i], biases["t5"],
                                       weights["lam_q"][i], weights["lam_k"][i], weights["diff_gn"][i], lam_init)
            else:
                o2 = _diff_attn(qkv, biases["t5"], weights["lam_q"][i], weights["lam_k"][i],
                                weights["diff_gn"][i], lam_init)
            wo = weights["w_out_ab"][i]
            outs["ret"].append(s_new)
            outs["bk"].append(kb.reshape(nseq, seq, H_B, 2 * DH_B))
            outs["bv"].append(vb.reshape(nseq, seq, H_B, DV_B))
        else:
            main, dt, kc, vc, tail = _inproj_cd(x, ng1, sc1, sh1, weights["w_in_cd"][i], tl)
            if sample:
                o1 = _band_attn_sample(main, caches["ck"][i], caches["cv"][i], biases["band"][i])
                conv_past = _pad_rows(caches["dconv"][i])
                s0 = caches["dssm"][i].reshape(nseq, H_D * P_D, N_D)
                keep = seq
            else:
                o1 = _band_attn(main, biases["band"][i])
                conv_past = jnp.zeros((nseq, 8, CONV_DIM_D), F32)
                s0 = jnp.zeros((nseq, H_D * P_D, N_D), F32)
                keep = min(C_WINDOW, seq)
            o2, ssm_new = _ssd(main, dt, weights["d_conv_w"][i], weights["d_conv_b"][i], weights["d_dt_bias"][i],
                               weights["d_a_log"][i], weights["d_skip"][i], weights["d_norm_g"][i], conv_past, s0, c)
            wo = weights["w_out_cd"][i]
            outs["ck"].append(kc[:, seq - keep:].reshape(nseq, keep, H_C, DH_C))
            outs["cv"].append(vc[:, seq - keep:].reshape(nseq, keep, H_C, DH_C))
            outs["dconv"].append(tail[:, 8 - (CONV_D - 1):])
            outs["dssm"].append(ssm_new.reshape(nseq, H_D, P_D, N_D))
        d_ff = weights["w_down"].shape[1]
        ffn_past = _pad_rows(caches["ffn"][l]) if sample else jnp.zeros((nseq, 8, d_ff), F32)
        x, ftail = _outproj_ffn(x, o1, o2, wo, g1, ng2, sc2, sh2, g2, weights["w_up"][l], weights["ffn_conv_w"][l],
                                weights["ffn_conv_b"][l], weights["w_down"][l], ffn_past,
                                weights["final_g"].reshape(1, d), tl, final=(l == depth - 1))
        outs["ffn"].append(ftail[:, 8 - (CONV_F - 1):])
    stk = lambda t: jnp.stack(t).astype(F32)
    return (x,) + tuple(stk(outs[k]) for k in ("ret", "bk", "bv", "ck", "cv", "dconv", "dssm", "ffn"))


def kernel(x_prompt, x_sample, cache_ret_state, cache_b_k, cache_b_v, cache_c_k, cache_c_v, state_d_conv, state_d_ssm, state_ffn_conv, c_prompt, c_sample, w_mod, b_mod, norm_g, final_g, t5_table, w_in_ab, w_out_ab, ret_gn, lam_q, lam_k, diff_gn, w_in_cd, w_out_cd, rel_table, d_conv_w, d_conv_b, d_dt_bias, d_a_log, d_skip, d_norm_g, w_up, ffn_conv_w, ffn_conv_b, w_down):
    batch, seq, d = x_prompt.shape
    dec_batch, dec_seq, _ = x_sample.shape
    past = cache_b_k.shape[2]
    depth = w_mod.shape[0]
    assert dec_seq <= CHUNK and past % CHUNK == 0 and cache_c_k.shape[2] == C_WINDOW and seq % BAND_TQ == 0

    w_in_cd_p = jnp.pad(w_in_cd, ((0, 0), (0, 0), (0, N_CD_PAD - w_in_cd.shape[-1]))).astype(BF16)
    weights = dict(
        norm_g=norm_g, final_g=final_g, w_in_ab=w_in_ab.astype(BF16), w_out_ab=w_out_ab.astype(BF16),
        ret_gn=ret_gn, lam_q=lam_q, lam_k=lam_k, diff_gn=diff_gn, w_in_cd=w_in_cd_p,
        w_out_cd=w_out_cd.astype(BF16), d_conv_w=d_conv_w, d_conv_b=d_conv_b, d_dt_bias=d_dt_bias, d_a_log=d_a_log,
        d_skip=d_skip, d_norm_g=d_norm_g, w_up=w_up.astype(BF16), ffn_conv_w=ffn_conv_w, ffn_conv_b=ffn_conv_b,
        w_down=w_down.astype(BF16))

    mod = _modulation(jnp.concatenate([c_prompt, c_sample], axis=0), w_mod, b_mod)

    def pieces(rows):
        return [[m[:, None, :] for m in jnp.split(mod[l, rows], 6, axis=-1)] for l in range(depth)]

    band = [_band_bias(rel_table[i]) for i in range(rel_table.shape[0])]
    biases_p = dict(t5=_t5_bias_prompt(t5_table, seq), band=band)
    biases_s = dict(t5=_t5_bias_sample(t5_table, past, dec_seq),
                    band=[b[:, :dec_seq, :C_WINDOW + dec_seq] for b in band])
    caches = dict(
        ret=cache_ret_state,
        bk=cache_b_k.reshape(cache_b_k.shape[0], dec_batch, past, H_B * 2 * DH_B),
        bv=cache_b_v.reshape(cache_b_v.shape[0], dec_batch, past, H_B * DV_B),
        ck=cache_c_k.reshape(cache_c_k.shape[0], dec_batch, C_WINDOW, H_C * DH_C),
        cv=cache_c_v.reshape(cache_c_v.shape[0], dec_batch, C_WINDOW, H_C * DH_C),
        dconv=state_d_conv, dssm=state_d_ssm, ffn=state_ffn_conv)

    pos_p = jnp.arange(seq, dtype=jnp.int32)
    pos_s = past + jnp.arange(dec_seq, dtype=jnp.int32)
    y_p, ret_p, bk_p, bv_p, ck_p, cv_p, dconv_p, dssm_p, ffn_p = _trunk(
        x_prompt, pieces(slice(0, batch)), pos_p, weights, None, biases_p)
    y_s, ret_s, bk_s, bv_s, ck_s, cv_s, dconv_s, dssm_s, ffn_s = _trunk(
        x_sample, pieces(slice(batch, batch + dec_batch)), pos_s, weights, caches, biases_s)
    return (y_p, y_s, ret_p, ret_s, bk_p, bk_s, bv_p, bv_s, ck_p, ck_s, cv_p, cv_s,
            dconv_p, dconv_s, dssm_p, dssm_s, ffn_p, ffn_s)
```

```python
import functools
import math

import numpy as np
import jax
import jax.numpy as jnp
from jax import lax
from jax.experimental import pallas as pl
from jax.experimental.pallas import tpu as pltpu

F32 = jnp.float32
BF16 = jnp.bfloat16

CHUNK = 64
EPS = 1e-6
NEG = -1e30
H_A, DK_A, DV_A = 4, 128, 128
ROPE_BASE = 10000.0
H_B, DH_B, DV_B = 4, 64, 128
T5_BUCKETS, T5_MAX_DIST = 32, 128
H_C, DH_C = 8, 64
BAND_CHUNKS = 8
C_WINDOW = BAND_CHUNKS * CHUNK
REL_CLIP = 128
H_D, P_D, G_D, N_D = 8, 64, 2, 128
D_INNER = H_D * P_D
CONV_D = 4
CONV_DIM_D = D_INNER + 2 * G_D * N_D
CONV_F = 3
N_AB_COLS = 7 * 512
N_CD_MAIN = 6 * 512
N_CD_PAD = N_CD_MAIN + 128
FF_CHUNK = 256
BAND_TQ = 256
DIFF_TQ = 256
SCAN_CHUNK = 256
VMEM_LIMIT = 56 * 1024 * 1024


def _mm(a, b):
    return jnp.dot(a, b, preferred_element_type=F32)


def _mm_nt(a, b):
    return lax.dot_general(a, b, (((1,), (1,)), ((), ())), preferred_element_type=F32)


def _mm_tn(a, b):
    return lax.dot_general(a, b, (((0,), (0,)), ((), ())), preferred_element_type=F32)


def _mm_exact(a, b):
    return jnp.dot(a, b, preferred_element_type=F32, precision=lax.Precision.HIGHEST)


def _mm_nt_exact(a, b):
    return lax.dot_general(a, b, (((1,), (1,)), ((), ())), preferred_element_type=F32,
                           precision=lax.Precision.HIGHEST)


def _silu(x):
    return x * (1.0 / (1.0 + jnp.exp(-x)))


def _softplus(x):
    return jnp.maximum(x, 0.0) + jnp.log1p(jnp.exp(-jnp.abs(x)))


def _gelu_tanh(x):
    return x * (0.5 * (1.0 + jnp.tanh(math.sqrt(2.0 / math.pi) * (x + 0.044715 * (x * x * x)))))


def _norm_mod(x, g, sc, sh):
    ms = jnp.mean(x * x, axis=-1, keepdims=True)
    return (x * lax.rsqrt(ms + EPS)) * g * (1.0 + sc) + sh


def _params(sem):
    return pltpu.CompilerParams(dimension_semantics=sem, vmem_limit_bytes=VMEM_LIMIT)


def _resident(shape):
    nd = len(shape)
    return pl.BlockSpec(shape, lambda *_: (0,) * nd, pipeline_mode=pl.Buffered(1))


def _mod_kernel(c_ref, w_ref, b_ref, o_ref):
    c = c_ref[...]
    o_ref[0] = _mm(_silu(c).astype(BF16), w_ref[0].astype(BF16)) + b_ref[0]


def _modulation(c_all, w_mod, b_mod):
    depth, d, n = w_mod.shape
    r = c_all.shape[0]
    tn = 1536
    return pl.pallas_call(
        _mod_kernel,
        grid=(depth, n // tn),
        in_specs=[pl.BlockSpec((r, d), lambda l, j: (0, 0)),
                  pl.BlockSpec((1, d, tn), lambda l, j: (l, 0, j)),
                  pl.BlockSpec((1, 1, tn), lambda l, j: (l, 0, j))],
        out_specs=pl.BlockSpec((1, r, tn), lambda l, j: (l, 0, j)),
        out_shape=jax.ShapeDtypeStruct((depth, r, n), F32),
        compiler_params=_params(("parallel", "parallel")),
        name="modulation",
    )(c_all, w_mod, b_mod.reshape(depth, 1, n))


def _table_gather_kernel(tab_ref, idx_ref, o_ref, *, n_entries, n_heads):
    idx = idx_ref[...]

    def body(r, accs):
        m = idx == r
        return tuple(jnp.where(m, tab_ref[hh, r], a) for hh, a in enumerate(accs))

    accs = lax.fori_loop(0, n_entries, body, tuple(jnp.zeros(idx.shape, F32) for _ in range(n_heads)))
    for hh in range(n_heads):
        o_ref[hh:hh + 1, :] = accs[hh]


def _table_gather(table, idx):
    t, h = table.shape
    w = idx.shape[0]
    return pl.pallas_call(
        functools.partial(_table_gather_kernel, n_entries=t, n_heads=h),
        in_specs=[pl.BlockSpec(memory_space=pltpu.SMEM),
                  pl.BlockSpec((1, w), lambda: (0, 0))],
        out_specs=pl.BlockSpec((h, w), lambda: (0, 0)),
        out_shape=jax.ShapeDtypeStruct((h, w), F32),
        name="table_gather",
    )(table.T, idx.reshape(1, w))


def _toeplitz(vec, rows, cols):
    h, wv = vec.shape
    flat = jnp.tile(vec, (1, rows))[:, :rows * (wv - 1)]
    return flat.reshape(h, rows, wv - 1)[:, :, :cols]


def _t5_bucket(rel):
    half = T5_BUCKETS // 2
    max_exact = half // 2
    base = jnp.where(rel > 0, half, 0)
    n = jnp.abs(rel)
    nf = jnp.maximum(n, 1).astype(F32)
    large = max_exact + (jnp.log(nf / max_exact) / math.log(T5_MAX_DIST / max_exact) * (half - max_exact)).astype(jnp.int32)
    large = jnp.minimum(large, half - 1)
    return base + jnp.where(n < max_exact, n, large)


def _wrapped_offsets(n_pos, n_neg):
    p = jnp.arange(n_pos + n_neg, dtype=jnp.int32)
    return jnp.where(p < n_pos, p, p - (n_pos + n_neg))


def _t5_bias_prompt(t5_table, seq):
    nd = seq // DIFF_TQ
    u = _wrapped_offsets(seq, DIFF_TQ)
    vec = _table_gather(t5_table, _t5_bucket(u - (seq - DIFF_TQ)))
    wide = _toeplitz(vec, DIFF_TQ, seq)
    return wide.reshape(H_B, DIFF_TQ, nd, DIFF_TQ).transpose(0, 2, 1, 3)


def _t5_bias_sample(t5_table, past, lq):
    u = _wrapped_offsets(past + lq, lq)
    vec = _table_gather(t5_table, _t5_bucket(u - past))
    return _toeplitz(vec, lq, past + lq)


def _band_bias(rel_table):
    u = _wrapped_offsets(3 * BAND_TQ, BAND_TQ)
    idx = jnp.clip(2 * BAND_TQ - u, -REL_CLIP, REL_CLIP) + REL_CLIP
    bias = _toeplitz(_table_gather(rel_table, idx), BAND_TQ, 3 * BAND_TQ)
    dc = (np.arange(BAND_TQ)[:, None] // CHUNK + C_WINDOW // CHUNK) - np.arange(3 * BAND_TQ)[None, :] // CHUNK
    return jnp.where(jnp.asarray((dc >= 0) & (dc <= BAND_CHUNKS))[None], bias, NEG)


def _inproj_ab_kernel(x_ref, g_ref, sc_ref, sh_ref, w_ref, cos_ref, sin_ref, qkv_ref, kb_ref, vb_ref):
    h = _norm_mod(x_ref[0], g_ref[...], sc_ref[0], sh_ref[0]).astype(BF16)
    cos = cos_ref[...]
    sin = sin_ref[...]
    for j in range(7):
        y = _mm(h, w_ref[:, j * 512:(j + 1) * 512])
        if j < 2:
            for hh in range(H_A):
                yh = y[:, hh * 128:(hh + 1) * 128]
                yh = yh * cos + pltpu.roll(yh, 64, 1) * sin
                if j == 1:
                    yh = yh * (DK_A ** -0.5)
                qkv_ref[0, :, j * 512 + hh * 128:j * 512 + (hh + 1) * 128] = yh.astype(BF16)
        else:
            if j == 5:
                kb_ref[0] = y
            if j == 6:
                vb_ref[0] = y
            qkv_ref[0, :, j * 512:(j + 1) * 512] = y.astype(BF16)


def _inproj_ab(x, g, sc, sh, w, cos, sin, tl):
    nseq, seq, d = x.shape
    grid = (nseq, seq // tl)
    tok = lambda n: pl.BlockSpec((1, tl, n), lambda b, l: (b, l, 0))
    per_seq = pl.BlockSpec((1, 1, d), lambda b, l: (b, 0, 0))
    return pl.pallas_call(
        _inproj_ab_kernel,
        grid=grid,
        in_specs=[tok(d), _resident((1, d)), per_seq, per_seq, _resident((d, N_AB_COLS)),
                  pl.BlockSpec((tl, 128), lambda b, l: (l, 0)),
                  pl.BlockSpec((tl, 128), lambda b, l: (l, 0))],
        out_specs=[tok(N_AB_COLS), tok(512), tok(512)],
        out_shape=[jax.ShapeDtypeStruct((nseq, seq, N_AB_COLS), BF16),
                   jax.ShapeDtypeStruct((nseq, seq, 512), F32),
                   jax.ShapeDtypeStruct((nseq, seq, 512), F32)],
        compiler_params=_params(("parallel", "parallel")),
        name="inproj_ab",
    )(x, g, sc, sh, w, cos, sin)


def _retention_kernel(q_ref, k_ref, v_ref, g_ref, s0_ref, dec_ref, qd_ref, kd_ref, gc_ref, gn_ref,
                      o_ref, sn_ref, st_ref):
    @pl.when(pl.program_id(1) == 0)
    def _():
        st_ref[...] = s0_ref[0]

    for h in range(H_A):
        cs = slice(h * 128, (h + 1) * 128)
        q = q_ref[0, :, cs]
        k = k_ref[0, :, cs]
        v = v_ref[0, :, cs]
        state = st_ref[h]
        s = _mm_nt(q, k) * dec_ref[h]
        o = _mm(s.astype(BF16), v) + _mm((q.astype(F32) * qd_ref[h]).astype(BF16), state.astype(BF16))
        new_state = gc_ref[h] * state + _mm_tn((k.astype(F32) * kd_ref[h]).astype(BF16), v)
        st_ref[h] = new_state
        sn_ref[0, h] = new_state
        mu = jnp.mean(o, axis=-1, keepdims=True)
        dlt = o - mu
        var = jnp.mean(dlt * dlt, axis=-1, keepdims=True)
        y = dlt * lax.rsqrt(var + EPS) * gn_ref[:, cs] * _silu(g_ref[0, :, cs].astype(F32))
        o_ref[0, :, cs] = y.astype(BF16)


def _retention_consts(c):
    lg = np.log1p(-np.exp2(-5.0 - np.arange(H_A, dtype=np.float32))).astype(np.float32)
    idx = np.arange(c, dtype=np.float32)
    diff = idx[:, None] - idx[None, :]
    decay = np.where(diff[None] >= 0, np.exp(np.maximum(diff, 0.0)[None] * lg[:, None, None]), 0.0)
    qd = np.exp((idx + 1.0)[None, :] * lg[:, None])
    kd = np.exp((c - 1.0 - idx)[None, :] * lg[:, None])
    gc = np.exp(c * lg)
    bc = lambda a: np.ascontiguousarray(np.broadcast_to(a[..., None], a.shape + (128,))).astype(np.float32)
    return decay.astype(np.float32), bc(qd), bc(kd), bc(gc[:, None])


def _retention(qkv, s0, ret_gn, c):
    nseq, seq, _ = qkv.shape
    decay, qd, kd, gc = _retention_consts(c)
    w = H_A * 128
    col = lambda idx: pl.BlockSpec((1, c, w), lambda b, l: (b, l, idx))
    const = lambda shape: pl.BlockSpec(shape, lambda b, l: (0,) * len(shape))
    state = pl.BlockSpec((1, H_A, DK_A, DV_A), lambda b, l: (b, 0, 0, 0))
    return pl.pallas_call(
        _retention_kernel,
        grid=(nseq, seq // c),
        in_specs=[col(0), col(1), col(2), col(3), state, const((H_A, c, c)), const((H_A, c, 128)),
                  const((H_A, c, 128)), const((H_A, 1, 128)), const((1, w))],
        out_specs=[col(0), state],
        out_shape=[jax.ShapeDtypeStruct((nseq, seq, w), BF16),
                   jax.ShapeDtypeStruct((nseq, H_A, DK_A, DV_A), F32)],
        scratch_shapes=[pltpu.VMEM((H_A, DK_A, DV_A), F32)],
        compiler_params=_params(("parallel", "arbitrary")),
        name="retention",
    )(qkv, qkv, qkv, qkv, s0, decay, qd, kd, gc, ret_gn.reshape(1, w))


def _split_halves(q):
    lane = lax.broadcasted_iota(jnp.int32, q.shape, 1)
    zero = jnp.zeros_like(q)
    return jnp.where(lane < 64, q, zero), jnp.where(lane >= 64, q, zero)


def _lambda(lq_ref, lk_ref, lam_init):
    e = jnp.exp(jnp.sum(lq_ref[...] * lk_ref[...], axis=1, keepdims=True))
    return e[0:1, :] - e[1:2, :] + lam_init


def _diff_epilogue(o, gn, lam_init):
    ms = jnp.mean(o * o, axis=-1, keepdims=True)
    return (o * lax.rsqrt(ms + EPS)) * gn * (1.0 - lam_init)


def _diff_attn_kernel(q_ref, k_ref, v_ref, b_ref, lq_ref, lk_ref, gn_ref, o_ref, s_ref, qs_ref, mx_ref, ls_ref,
                      acc_ref, *, tq, nd, lam_init):
    tp = pl.program_id(2)
    tiles = (tp, nd - 1 - tp)
    row = lax.broadcasted_iota(jnp.int32, (tq, tq), 0) // CHUNK
    col = lax.broadcasted_iota(jnp.int32, (tq, tq), 1) // CHUNK
    chunk_mask = col <= row
    for side in range(2):
        q = q_ref[0, pl.ds(pl.multiple_of(tiles[side] * tq, tq), tq), :]
        q0, q1 = _split_halves((q.astype(F32) * (DH_B ** -0.5)).astype(BF16))
        qs_ref[2 * side] = q0
        qs_ref[2 * side + 1] = q1
    mx_ref[...] = jnp.full(mx_ref.shape, NEG, F32)
    ls_ref[...] = jnp.zeros(ls_ref.shape, F32)
    acc_ref[...] = jnp.zeros(acc_ref.shape, F32)

    slots = [(0, tiles[0], True), (1, tiles[1], True)]
    for j in range(nd - 1):
        side = jnp.where(j < tp, 0, 1)
        slots.append((side, jnp.where(j < tp, j, j - tp), False))

    for side, kb, masked in slots:
        t = tiles[side] if masked else jnp.where(side == 0, tiles[0], tiles[1])
        k = k_ref[0, pl.ds(pl.multiple_of(kb * tq, tq), tq), :]
        b = b_ref[0, nd - 1 - t + kb]
        for i in range(2):
            r = 2 * side + i
            s = _mm_nt(qs_ref[r], k) + b
            if masked:
                s = jnp.where(chunk_mask, s, NEG)
            s_ref[r * nd + kb] = s
            m = mx_ref[r]
            for c0 in range(0, tq, 128):
                m = jnp.maximum(m, s[:, c0:c0 + 128])
            mx_ref[r] = m

    for r in range(4):
        mx_ref[r] = jnp.broadcast_to(jnp.max(mx_ref[r], axis=1, keepdims=True), (tq, 128))

    for side, kb, _ in slots:
        v = v_ref[0, pl.ds(pl.multiple_of(kb * tq, tq), tq), :]
        for i in range(2):
            r = 2 * side + i
            m = mx_ref[r]
            l = ls_ref[r]
            ps = []
            for c0 in range(0, tq, 128):
                p = jnp.exp(s_ref[r * nd + kb, :, c0:c0 + 128] - m)
                l = l + p
                ps.append(p.astype(BF16))
            ls_ref[r] = l
            acc_ref[r] += _mm(jnp.concatenate(ps, axis=1), v)

    lam = _lambda(lq_ref, lk_ref, lam_init)
    for side in range(2):
        l0 = jnp.sum(ls_ref[2 * side], axis=1, keepdims=True)
        l1 = jnp.sum(ls_ref[2 * side + 1], axis=1, keepdims=True)
        o = acc_ref[2 * side] / l0 - lam * (acc_ref[2 * side + 1] / l1)
        o_ref[0, pl.ds(pl.multiple_of(tiles[side] * tq, tq), tq), :] = (
            _diff_epilogue(o, gn_ref[...], lam_init).astype(BF16))


def _diff_attn(qkv, bias, lam_q, lam_k, diff_gn, lam_init):
    nseq, seq, _ = qkv.shape
    tq = DIFF_TQ
    nd = seq // tq
    full = lambda off: pl.BlockSpec((1, seq, 128), lambda b, h, t: (b, 0, off + h))
    small = lambda r, c: pl.BlockSpec((r, c), lambda b, h, t: (0, 0))
    return pl.pallas_call(
        functools.partial(_diff_attn_kernel, tq=tq, nd=nd, lam_init=lam_init),
        grid=(nseq, H_B, nd // 2),
        in_specs=[full(16), full(20), full(24),
                  pl.BlockSpec((1, nd, tq, tq), lambda b, h, t: (h, 0, 0, 0)),
                  small(2, DH_B), small(2, DH_B), small(1, DV_B)],
        out_specs=full(0),
        out_shape=jax.ShapeDtypeStruct((nseq, seq, H_B * DV_B), BF16),
        scratch_shapes=[pltpu.VMEM((4 * nd, tq, tq), F32), pltpu.VMEM((4, tq, 128), BF16),
                        pltpu.VMEM((4, tq, 128), F32), pltpu.VMEM((4, tq, 128), F32),
                        pltpu.VMEM((4, tq, DV_B), F32)],
        compiler_params=_params(("parallel", "parallel", "arbitrary")),
        name="diff_attn",
    )(qkv, qkv, qkv, bias, lam_q, lam_k, diff_gn.reshape(1, DV_B))


def _diff_attn_sample_kernel(q_ref, kc_ref, vc_ref, kp_ref, vp_ref, b_ref, lq_ref, lk_ref, gn_ref, o_ref,
                             *, past, lam_init):
    qs = _split_halves(q_ref[0])
    scale = DH_B ** -0.5
    kp = kp_ref[0].astype(BF16)
    vp = vp_ref[0].astype(BF16)
    kc = kc_ref[0]
    vc = vc_ref[0]
    b = b_ref[0]
    probs = []
    for i in range(2):
        sp = _mm_nt(qs[i], kp) * scale + b[:, :past]
        sc = _mm_nt(qs[i], kc) * scale + b[:, past:]
        m = jnp.maximum(jnp.max(sp, axis=1, keepdims=True), jnp.max(sc, axis=1, keepdims=True))
        pp = jnp.exp(sp - m)
        pc = jnp.exp(sc - m)
        l = jnp.sum(pp, axis=1, keepdims=True) + jnp.sum(pc, axis=1, keepdims=True)
        probs.append((pp / l, pc / l))
    lam = _lambda(lq_ref, lk_ref, lam_init)
    ap = probs[0][0] - lam * probs[1][0]
    ac = probs[0][1] - lam * probs[1][1]
    o = _mm(ap.astype(BF16), vp) + _mm(ac.astype(BF16), vc)
    o_ref[0] = _diff_epilogue(o, gn_ref[...], lam_init).astype(BF16)


def _diff_attn_sample(qkv, k_past, v_past, bias, lam_q, lam_k, diff_gn, lam_init):
    nseq, lq, _ = qkv.shape
    past = k_past.shape[1]
    cur = lambda off: pl.BlockSpec((1, lq, 128), lambda b, h: (b, 0, off + h))
    old = pl.BlockSpec((1, past, 128), lambda b, h: (b, 0, h))
    small = lambda r, c: pl.BlockSpec((r, c), lambda b, h: (0, 0))
    return pl.pallas_call(
        functools.partial(_diff_attn_sample_kernel, past=past, lam_init=lam_init),
        grid=(nseq, H_B),
        in_specs=[cur(16), cur(20), cur(24), old, old,
                  pl.BlockSpec((1, lq, past + lq), lambda b, h: (h, 0, 0)),
                  small(2, DH_B), small(2, DH_B), small(1, DV_B)],
        out_specs=pl.BlockSpec((1, lq, 128), lambda b, h: (b, 0, h)),
        out_shape=jax.ShapeDtypeStruct((nseq, lq, H_B * DV_B), BF16),
        compiler_params=_params(("parallel", "parallel")),
        name="diff_attn_sample",
    )(qkv, qkv, qkv, k_past, v_past, bias, lam_q, lam_k, diff_gn.reshape(1, DV_B))


def _inproj_cd_kernel(x_ref, g_ref, sc_ref, sh_ref, w_ref, main_ref, dt_ref, kc_ref, vc_ref, tail_ref, *, tl):
    h = _norm_mod(x_ref[0], g_ref[...], sc_ref[0], sh_ref[0]).astype(BF16)
    for j in range(6):
        y = _mm(h, w_ref[:, j * 512:(j + 1) * 512])
        main_ref[0, :, j * 512:(j + 1) * 512] = y.astype(BF16)
        if j == 1:
            kc_ref[0] = y
        if j == 2:
            vc_ref[0] = y
        if j >= 4:
            tail_ref[0, :, (j - 4) * 512:(j - 3) * 512] = y[tl - 8:tl, :]
    dt_ref[0] = _mm(h, w_ref[:, N_CD_MAIN:N_CD_PAD])


def _inproj_cd(x, g, sc, sh, w, tl):
    nseq, seq, d = x.shape
    tok = lambda n: pl.BlockSpec((1, tl, n), lambda b, l: (b, l, 0))
    per_seq = pl.BlockSpec((1, 1, d), lambda b, l: (b, 0, 0))
    return pl.pallas_call(
        functools.partial(_inproj_cd_kernel, tl=tl),
        grid=(nseq, seq // tl),
        in_specs=[tok(d), _resident((1, d)), per_seq, per_seq, _resident((d, N_CD_PAD))],
        out_specs=[tok(N_CD_MAIN), tok(128), tok(512), tok(512),
                   pl.BlockSpec((1, 8, CONV_DIM_D), lambda b, l: (b, 0, 0))],
        out_shape=[jax.ShapeDtypeStruct((nseq, seq, N_CD_MAIN), BF16),
                   jax.ShapeDtypeStruct((nseq, seq, 128), F32),
                   jax.ShapeDtypeStruct((nseq, seq, 512), F32),
                   jax.ShapeDtypeStruct((nseq, seq, 512), F32),
                   jax.ShapeDtypeStruct((nseq, 8, CONV_DIM_D), F32)],
        compiler_params=_params(("parallel", "arbitrary")),
        name="inproj_cd",
    )(x, g, sc, sh, w)


def _band_kernel(q_ref, k_ref, v_ref, b_ref, o_ref, *, tq):
    t = pl.program_id(2)
    q0, q1 = _split_halves((q_ref[0].astype(F32) * (DH_C ** -0.5)).astype(BF16))

    def attend(near_start):
        ks, vs, offs = [], [], []
        for d in range(3):
            kt = t - 2 + d
            start = pl.multiple_of(jnp.maximum(kt, 0) * tq, tq)
            ks.append(k_ref[0, pl.ds(start, tq), :])
            vs.append(v_ref[0, pl.ds(start, tq), :])
            offs.append(jnp.where(kt >= 0, 0.0, NEG))
        outs = []
        for hi, qh in enumerate((q0, q1)):
            ss = []
            for d in range(3):
                s = _mm_nt(qh, ks[d]) + b_ref[hi, :, d * tq:(d + 1) * tq]
                ss.append(s + offs[d] if near_start and d < 2 else s)
            m = jnp.maximum(jnp.maximum(jnp.max(ss[0], axis=1, keepdims=True),
                                        jnp.max(ss[1], axis=1, keepdims=True)),
                            jnp.max(ss[2], axis=1, keepdims=True))
            l = jnp.zeros((tq, 1), F32)
            o = jnp.zeros((tq, 128), F32)
            for d in range(3):
                p = jnp.exp(ss[d] - m)
                l = l + jnp.sum(p, axis=1, keepdims=True)
                o = o + _mm(p.astype(BF16), vs[d])
            outs.append(o / l)
        lane = lax.broadcasted_iota(jnp.int32, (tq, 128), 1)
        o_ref[0] = jnp.where(lane < 64, outs[0], outs[1]).astype(BF16)

    @pl.when(t < 2)
    def _():
        attend(True)

    @pl.when(t >= 2)
    def _():
        attend(False)


def _band_attn(main, bias):
    nseq, seq, _ = main.shape
    tq = BAND_TQ
    full = lambda off: pl.BlockSpec((1, seq, 128), lambda b, p, t: (b, 0, off + p))
    return pl.pallas_call(
        functools.partial(_band_kernel, tq=tq),
        grid=(nseq, H_C // 2, seq // tq),
        in_specs=[pl.BlockSpec((1, tq, 128), lambda b, p, t: (b, t, p)), full(4), full(8),
                  pl.BlockSpec((2, tq, 3 * tq), lambda b, p, t: (p, 0, 0))],
        out_specs=pl.BlockSpec((1, tq, 128), lambda b, p, t: (b, t, p)),
        out_shape=jax.ShapeDtypeStruct((nseq, seq, H_C * DH_C), BF16),
        compiler_params=_params(("parallel", "parallel", "parallel")),
        name="band_attn",
    )(main, main, main, bias)


def _band_sample_kernel(q_ref, kc_ref, vc_ref, kp_ref, vp_ref, b_ref, o_ref, *, past):
    q0, q1 = _split_halves(q_ref[0])
    scale = DH_C ** -0.5
    kp = kp_ref[0].astype(BF16)
    vp = vp_ref[0].astype(BF16)
    kc = kc_ref[0]
    vc = vc_ref[0]
    outs = []
    for hi, qh in enumerate((q0, q1)):
        b = b_ref[hi]
        sp = _mm_nt(qh, kp) * scale + b[:, :past]
        sc = _mm_nt(qh, kc) * scale + b[:, past:]
        m = jnp.maximum(jnp.max(sp, axis=1, keepdims=True), jnp.max(sc, axis=1, keepdims=True))
        pp = jnp.exp(sp - m)
        pc = jnp.exp(sc - m)
        l = jnp.sum(pp, axis=1, keepdims=True) + jnp.sum(pc, axis=1, keepdims=True)
        outs.append(_mm((pp / l).astype(BF16), vp) + _mm((pc / l).astype(BF16), vc))
    lane = lax.broadcasted_iota(jnp.int32, outs[0].shape, 1)
    o_ref[0] = jnp.where(lane < 64, outs[0], outs[1]).astype(BF16)


def _band_attn_sample(main, k_past, v_past, bias):
    nseq, lq, _ = main.shape
    past = k_past.shape[1]
    cur = lambda off: pl.BlockSpec((1, lq, 128), lambda b, p: (b, 0, off + p))
    old = pl.BlockSpec((1, past, 128), lambda b, p: (b, 0, p))
    return pl.pallas_call(
        functools.partial(_band_sample_kernel, past=past),
        grid=(nseq, H_C // 2),
        in_specs=[cur(0), cur(4), cur(8), old, old,
                  pl.BlockSpec((2, lq, past + lq), lambda b, p: (p, 0, 0))],
        out_specs=pl.BlockSpec((1, lq, 128), lambda b, p: (b, 0, p)),
        out_shape=jax.ShapeDtypeStruct((nseq, lq, H_C * DH_C), BF16),
        compiler_params=_params(("parallel", "parallel")),
        name="band_attn_sample",
    )(main, main, main, k_past, v_past, bias)


def _ssd_kernel(z_ref, xs_ref, bm_ref, cm_ref, dt_ref, cw_ref, cb_ref, dtb_ref, alog_ref, dskip_ref, ng_ref,
                past_ref, s0_ref, o_ref, sn_ref, buf_ref, st_ref, *, c):
    @pl.when(pl.program_id(1) == 0)
    def _():
        buf_ref[0:8, :] = past_ref[0]
        st_ref[...] = s0_ref[0]

    buf_ref[8:8 + c, 0:D_INNER] = xs_ref[0].astype(F32)
    buf_ref[8:8 + c, D_INNER:D_INNER + 256] = bm_ref[0].astype(F32)
    buf_ref[8:8 + c, D_INNER + 256:CONV_DIM_D] = cm_ref[0].astype(F32)
    conv = cb_ref[...] + buf_ref[pl.ds(5, c), :] * cw_ref[0:1, :]
    for i in range(1, CONV_D):
        conv = conv + buf_ref[pl.ds(5 + i, c), :] * cw_ref[i:i + 1, :]
    buf_ref[0:8, :] = buf_ref[c:c + 8, :]
    act = _silu(conv)
    xs = act[:, 0:D_INNER]
    bm = act[:, D_INNER:D_INNER + 256].astype(BF16)
    cm = act[:, D_INNER + 256:CONV_DIM_D].astype(BF16)

    dt = _softplus(dt_ref[0] + dtb_ref[...])
    da = dt * (-jnp.exp(alog_ref[...]))
    row = lax.broadcasted_iota(jnp.int32, (c, c), 0)
    col = lax.broadcasted_iota(jnp.int32, (c, c), 1)
    tri = row >= col
    cum = _mm_exact(tri.astype(F32), da)
    eye = (lax.broadcasted_iota(jnp.int32, (8, 128), 0) == lax.broadcasted_iota(jnp.int32, (8, 128), 1)).astype(F32)
    cum_t = _mm_exact(_mm_nt_exact(eye, da), (row <= col).astype(F32))
    dt_t = _mm_nt_exact(eye, dt)
    ecum = jnp.exp(cum)
    last = cum[c - 1:c, :]
    wgt = jnp.exp(last - cum) * dt
    elast = jnp.exp(last)
    lane = lax.broadcasted_iota(jnp.int32, (c, 128), 1)
    low = lane < 64
    rlow = lax.broadcasted_iota(jnp.int32, (128, 128), 0) < 64

    for g in range(G_D):
        bm_g = bm[:, g * 128:(g + 1) * 128]
        cm_g = cm[:, g * 128:(g + 1) * 128]
        cb = _mm_nt(cm_g, bm_g)
        ys = []
        for pp in range(2):
            p = 2 * g + pp
            h0, h1 = 2 * p, 2 * p + 1
            x_f = xs[:, p * 128:(p + 1) * 128]
            x_b = x_f.astype(BF16)
            y_in = []
            for hh in (h0, h1):
                seg = cum[:, hh:hh + 1] - cum_t[hh:hh + 1, :]
                dec = jnp.exp(jnp.where(tri, seg, NEG))
                y_in.append(_mm((cb * dec * dt_t[hh:hh + 1, :]).astype(BF16), x_b))
            st = st_ref[p * 128:(p + 1) * 128, :]
            y_x = _mm_nt(cm_g, st.astype(BF16)) * jnp.where(low, ecum[:, h0:h0 + 1], ecum[:, h1:h1 + 1])
            w2 = jnp.where(low, wgt[:, h0:h0 + 1], wgt[:, h1:h1 + 1])
            new = jnp.where(rlow, elast[:, h0:h0 + 1], elast[:, h1:h1 + 1]) * st + _mm_tn((x_f * w2).astype(BF16), bm_g)
            st_ref[p * 128:(p + 1) * 128, :] = new
            sn_ref[0, p * 128:(p + 1) * 128, :] = new
            y = jnp.where(low, y_in[0], y_in[1]) + y_x + jnp.where(low, dskip_ref[h0], dskip_ref[h1]) * x_f
            ys.append(y * _silu(z_ref[0, :, p * 128:(p + 1) * 128].astype(F32)))
        ms = (jnp.sum(ys[0] * ys[0], axis=-1, keepdims=True) + jnp.sum(ys[1] * ys[1], axis=-1, keepdims=True)) / 256.0
        inv = lax.rsqrt(ms + EPS)
        for pp in range(2):
            p = 2 * g + pp
            o_ref[0, :, p * 128:(p + 1) * 128] = (ys[pp] * inv * ng_ref[:, p * 128:(p + 1) * 128]).astype(BF16)


def _ssd(main, dt, conv_w, conv_b, dt_bias, a_log, d_skip, norm_g, conv_past, s0, c):
    nseq, seq, _ = main.shape
    pad = lambda a: jnp.pad(a.reshape(1, H_D), ((0, 0), (0, 128 - H_D)))
    blk = lambda w, idx: pl.BlockSpec((1, c, w), lambda b, l: (b, l, idx))
    const = lambda r, w: pl.BlockSpec((r, w), lambda b, l: (0, 0))
    state = pl.BlockSpec((1, H_D * P_D, N_D), lambda b, l: (b, 0, 0))
    return pl.pallas_call(
        functools.partial(_ssd_kernel, c=c),
        grid=(nseq, seq // c),
        in_specs=[blk(512, 3), blk(512, 4), blk(256, 10), blk(256, 11), blk(128, 0),
                  const(CONV_D, CONV_DIM_D), const(1, CONV_DIM_D), const(1, 128), const(1, 128),
                  pl.BlockSpec(memory_space=pltpu.SMEM), const(1, D_INNER),
                  pl.BlockSpec((1, 8, CONV_DIM_D), lambda b, l: (b, 0, 0)), state],
        out_specs=[blk(512, 0), state],
        out_shape=[jax.ShapeDtypeStruct((nseq, seq, D_INNER), BF16),
                   jax.ShapeDtypeStruct((nseq, H_D * P_D, N_D), F32)],
        scratch_shapes=[pltpu.VMEM((c + 8, CONV_DIM_D), F32), pltpu.VMEM((H_D * P_D, N_D), F32)],
        compiler_params=_params(("parallel", "arbitrary")),
        name="ssd",
    )(main, main, main, main, dt, conv_w, conv_b.reshape(1, CONV_DIM_D), pad(dt_bias), pad(a_log), d_skip,
      norm_g.reshape(1, D_INNER), conv_past, s0)


def _ffn_kernel(x_ref, o1_ref, o2_ref, wo_ref, g1_ref, ng_ref, sc_ref, sh_ref, g2_ref, wup_ref, cw_ref, cb_ref,
                wdn_ref, past_ref, fg_ref, out_ref, tail_ref, act_ref, wb_ref, gt_ref, *, tl, d_ff, final):
    @pl.when(pl.program_id(1) == 0)
    def _():
        gt_ref[...] = past_ref[0]

    half = wo_ref.shape[0] // 2
    mix = _mm(o1_ref[0], wo_ref[0:half, :]) + _mm(o2_ref[0], wo_ref[half:2 * half, :])
    x1 = x_ref[0] + g1_ref[0] * mix
    out_ref[0] = x1
    h = _norm_mod(x1, ng_ref[...], sc_ref[0], sh_ref[0]).astype(BF16)
    for j in range(d_ff // FF_CHUNK):
        c0 = j * FF_CHUNK
        a = _mm(h, wup_ref[:, c0:c0 + FF_CHUNK])
        g = _mm(h, wup_ref[:, d_ff + c0:d_ff + c0 + FF_CHUNK])
        wb_ref[0:8, :] = gt_ref[:, c0:c0 + FF_CHUNK]
        wb_ref[8:8 + tl, :] = g
        gt_ref[:, c0:c0 + FF_CHUNK] = g[tl - 8:tl, :]
        gc = (cb_ref[:, c0:c0 + FF_CHUNK] + wb_ref[pl.ds(6, tl), :] * cw_ref[0:1, c0:c0 + FF_CHUNK]
              + wb_ref[pl.ds(7, tl), :] * cw_ref[1:2, c0:c0 + FF_CHUNK] + g * cw_ref[2:3, c0:c0 + FF_CHUNK])
        act_ref[:, c0:c0 + FF_CHUNK] = (a * _gelu_tanh(gc)).astype(BF16)
    split = (d_ff // FF_CHUNK + 1) // 2 * FF_CHUNK
    dn = _mm(act_ref[:, 0:split], wdn_ref[0:split, :]) + _mm(act_ref[:, split:d_ff], wdn_ref[split:d_ff, :])
    x2 = out_ref[0] + g2_ref[0] * dn
    if final:
        ms = jnp.mean(x2 * x2, axis=-1, keepdims=True)
        x2 = (x2 * lax.rsqrt(ms + EPS)) * fg_ref[...]
    out_ref[0] = x2
    tail_ref[0] = gt_ref[...]


def _outproj_ffn(x, o1, o2, wo, g1, ng, sc, sh, g2, wup, cw, cb, wdn, past, fg, tl, final):
    nseq, seq, d = x.shape
    d_ff = wdn.shape[0]
    tok = lambda n: pl.BlockSpec((1, tl, n), lambda b, l: (b, l, 0))
    per_seq = pl.BlockSpec((1, 1, d), lambda b, l: (b, 0, 0))
    tail = pl.BlockSpec((1, 8, d_ff), lambda b, l: (b, 0, 0))
    return pl.pallas_call(
        functools.partial(_ffn_kernel, tl=tl, d_ff=d_ff, final=final),
        grid=(nseq, seq // tl),
        in_specs=[tok(d), tok(o1.shape[-1]), tok(o2.shape[-1]), _resident(wo.shape), per_seq, _resident((1, d)),
                  per_seq, per_seq, per_seq, _resident(wup.shape), _resident(cw.shape), _resident((1, d_ff)),
                  _resident(wdn.shape), tail, _resident((1, d))],
        out_specs=[tok(d), tail],
        out_shape=[jax.ShapeDtypeStruct((nseq, seq, d), F32), jax.ShapeDtypeStruct((nseq, 8, d_ff), F32)],
        scratch_shapes=[pltpu.VMEM((tl, d_ff), BF16), pltpu.VMEM((tl + 8, FF_CHUNK), F32), pltpu.VMEM((8, d_ff), F32)],
        compiler_params=_params(("parallel", "arbitrary")),
        name="outproj_ffn",
    )(x, o1, o2, wo, g1, ng, sc, sh, g2, wup, cw, cb.reshape(1, d_ff), wdn, past, fg)


def _rope_tables(pos):
    half = DK_A // 2
    inv = jnp.power(ROPE_BASE, -jnp.arange(half, dtype=F32) / half)
    ang = pos.astype(F32)[:, None] * inv[None, :]
    cos = jnp.cos(ang)
    sin = jnp.sin(ang)
    return jnp.concatenate([cos, cos], axis=1), jnp.concatenate([-sin, sin], axis=1)


def _pad_rows(a, rows=8):
    return jnp.pad(a, ((0, 0), (rows - a.shape[1], 0), (0, 0)))


def _trunk(x, mods, pos, weights, caches, biases):
    nseq, seq, d = x.shape
    sample = caches is not None
    tl = min(seq, 512)
    c = min(seq, SCAN_CHUNK)
    depth = weights["w_up"].shape[0]
    outs = {k: [] for k in ("ret", "bk", "bv", "ck", "cv", "dconv", "dssm", "ffn")}
    cos, sin = _rope_tables(pos)
    for l in range(depth):
        i = l // 2
        sh1, sc1, g1, sh2, sc2, g2 = mods[l]
        ng1 = weights["norm_g"][l, 0].reshape(1, d)
        ng2 = weights["norm_g"][l, 1].reshape(1, d)
        if l % 2 == 0:
            qkv, kb, vb = _inproj_ab(x, ng1, sc1, sh1, weights["w_in_ab"][i], cos, sin, tl)
            s0 = caches["ret"][i] if sample else jnp.zeros((nseq, H_A, DK_A, DV_A), F32)
            o1, s_new = _retention(qkv, s0, weights["ret_gn"][i], c)
            lam_init = 0.8 - 0.6 * math.exp(-0.3 * l)
            if sample:
                o2 = _diff_attn_sample(qkv, caches["bk"][i], caches["bv"][i], biases["t5"],
                                       weights["lam_q"][i], weights["lam_k"][i], weights["diff_gn"][i], lam_init)
            else:
                o2 = _diff_attn(qkv, biases["t5"], weights["lam_q"][i], weights["lam_k"][i],
                                weights["diff_gn"][i], lam_init)
            wo = weights["w_out_ab"][i]
            outs["ret"].append(s_new)
            outs["bk"].append(kb.reshape(nseq, seq, H_B, 2 * DH_B))
            outs["bv"].append(vb.reshape(nseq, seq, H_B, DV_B))
        else:
            main, dt, kc, vc, tail = _inproj_cd(x, ng1, sc1, sh1, weights["w_in_cd"][i], tl)
            if sample:
                o1 = _band_attn_sample(main, caches["ck"][i], caches["cv"][i], biases["band"][i])
                conv_past = _pad_rows(caches["dconv"][i])
                s0 = caches["dssm"][i].reshape(nseq, H_D * P_D, N_D)
                keep = seq
            else:
                o1 = _band_attn(main, biases["band"][i])
                conv_past = jnp.zeros((nseq, 8, CONV_DIM_D), F32)
                s0 = jnp.zeros((nseq, H_D * P_D, N_D), F32)
                keep = min(C_WINDOW, seq)
            o2, ssm_new = _ssd(main, dt, weights["d_conv_w"][i], weights["d_conv_b"][i], weights["d_dt_bias"][i],
                               weights["d_a_log"][i], weights["d_skip"][i], weights["d_norm_g"][i], conv_past, s0, c)
            wo = weights["w_out_cd"][i]
            outs["ck"].append(kc[:, seq - keep:].reshape(nseq, keep, H_C, DH_C))
            outs["cv"].append(vc[:, seq - keep:].reshape(nseq, keep, H_C, DH_C))
            outs["dconv"].append(tail[:, 8 - (CONV_D - 1):])
            outs["dssm"].append(ssm_new.reshape(nseq, H_D, P_D, N_D))
        d_ff = weights["w_down"].shape[1]
        ffn_past = _pad_rows(caches["ffn"][l]) if sample else jnp.zeros((nseq, 8, d_ff), F32)
        x, ftail = _outproj_ffn(x, o1, o2, wo, g1, ng2, sc2, sh2, g2, weights["w_up"][l], weights["ffn_conv_w"][l],
                                weights["ffn_conv_b"][l], weights["w_down"][l], ffn_past,
                                weights["final_g"].reshape(1, d), tl, final=(l == depth - 1))
        outs["ffn"].append(ftail[:, 8 - (CONV_F - 1):])
    stk = lambda t: jnp.stack(t).astype(F32)
    return (x,) + tuple(stk(outs[k]) for k in ("ret", "bk", "bv", "ck", "cv", "dconv", "dssm", "ffn"))


def kernel(x_prompt, x_sample, cache_ret_state, cache_b_k, cache_b_v, cache_c_k, cache_c_v, state_d_conv, state_d_ssm, state_ffn_conv, c_prompt, c_sample, w_mod, b_mod, norm_g, final_g, t5_table, w_in_ab, w_out_ab, ret_gn, lam_q, lam_k, diff_gn, w_in_cd, w_out_cd, rel_table, d_conv_w, d_conv_b, d_dt_bias, d_a_log, d_skip, d_norm_g, w_up, ffn_conv_w, ffn_conv_b, w_down):
    batch, seq, d = x_prompt.shape
    dec_batch, dec_seq, _ = x_sample.shape
    past = cache_b_k.shape[2]
    depth = w_mod.shape[0]
    assert dec_seq <= CHUNK and past % CHUNK == 0 and cache_c_k.shape[2] == C_WINDOW
    assert seq % BAND_TQ == 0 and seq % (2 * DIFF_TQ) == 0

    w_in_cd_p = jnp.pad(w_in_cd, ((0, 0), (0, 0), (0, N_CD_PAD - w_in_cd.shape[-1]))).astype(BF16)
    weights = dict(
        norm_g=norm_g, final_g=final_g, w_in_ab=w_in_ab.astype(BF16), w_out_ab=w_out_ab.astype(BF16),
        ret_gn=ret_gn, lam_q=lam_q, lam_k=lam_k, diff_gn=diff_gn, w_in_cd=w_in_cd_p,
        w_out_cd=w_out_cd.astype(BF16), d_conv_w=d_conv_w, d_conv_b=d_conv_b, d_dt_bias=d_dt_bias, d_a_log=d_a_log,
        d_skip=d_skip, d_norm_g=d_norm_g, w_up=w_up.astype(BF16), ffn_conv_w=ffn_conv_w, ffn_conv_b=ffn_conv_b,
        w_down=w_down.astype(BF16))

    mod = _modulation(jnp.concatenate([c_prompt, c_sample], axis=0), w_mod, b_mod)

    def pieces(rows):
        return [[m[:, None, :] for m in jnp.split(mod[l, rows], 6, axis=-1)] for l in range(depth)]

    band = [_band_bias(rel_table[i]) for i in range(rel_table.shape[0])]
    biases_p = dict(t5=_t5_bias_prompt(t5_table, seq), band=band)
    biases_s = dict(t5=_t5_bias_sample(t5_table, past, dec_seq),
                    band=[b[:, :dec_seq, :C_WINDOW + dec_seq] for b in band])
    caches = dict(
        ret=cache_ret_state,
        bk=cache_b_k.reshape(cache_b_k.shape[0], dec_batch, past, H_B * 2 * DH_B),
        bv=cache_b_v.reshape(cache_b_v.shape[0], dec_batch, past, H_B * DV_B),
        ck=cache_c_k.reshape(cache_c_k.shape[0], dec_batch, C_WINDOW, H_C * DH_C),
        cv=cache_c_v.reshape(cache_c_v.shape[0], dec_batch, C_WINDOW, H_C * DH_C),
        dconv=state_d_conv, dssm=state_d_ssm, ffn=state_ffn_conv)

    pos_p = jnp.arange(seq, dtype=jnp.int32)
    pos_s = past + jnp.arange(dec_seq, dtype=jnp.int32)
    y_p, ret_p, bk_p, bv_p, ck_p, cv_p, dconv_p, dssm_p, ffn_p = _trunk(
        x_prompt, pieces(slice(0, batch)), pos_p, weights, None, biases_p)
    y_s, ret_s, bk_s, bv_s, ck_s, cv_s, dconv_s, dssm_s, ffn_s = _trunk(
        x_sample, pieces(slice(batch, batch + dec_batch)), pos_s, weights, caches, biases_s)
    return (y_p, y_s, ret_p, ret_s, bk_p, bk_s, bv_p, bv_s, ck_p, ck_s, cv_p, cv_s,
            dconv_p, dconv_s, dssm_p, dssm_s, ffn_p, ffn_s)
```

```python
import functools
import math

import numpy as np
import jax
import jax.numpy as jnp
from jax import lax
from jax.experimental import pallas as pl
from jax.experimental.pallas import tpu as pltpu

F32 = jnp.float32
BF16 = jnp.bfloat16

CHUNK = 64
EPS = 1e-6
NEG = -1e30
H_A, DK_A, DV_A = 4, 128, 128
ROPE_BASE = 10000.0
H_B, DH_B, DV_B = 4, 64, 128
T5_BUCKETS, T5_MAX_DIST = 32, 128
H_C, DH_C = 8, 64
BAND_CHUNKS = 8
C_WINDOW = BAND_CHUNKS * CHUNK
REL_CLIP = 128
H_D, P_D, G_D, N_D = 8, 64, 2, 128
D_INNER = H_D * P_D
CONV_D = 4
CONV_DIM_D = D_INNER + 2 * G_D * N_D
CONV_F = 3
N_AB_COLS = 7 * 512
N_CD_MAIN = 6 * 512
N_CD_PAD = N_CD_MAIN + 128
FF_CHUNK = 256
BAND_TQ = 256
DIFF_TQ = 256
SCAN_CHUNK = 256
VMEM_LIMIT = 56 * 1024 * 1024


def _mm(a, b):
    return jnp.dot(a, b, preferred_element_type=F32)


def _mm_nt(a, b):
    return lax.dot_general(a, b, (((1,), (1,)), ((), ())), preferred_element_type=F32)


def _mm_tn(a, b):
    return lax.dot_general(a, b, (((0,), (0,)), ((), ())), preferred_element_type=F32)


def _mm_exact(a, b):
    return jnp.dot(a, b, preferred_element_type=F32, precision=lax.Precision.HIGHEST)


def _mm_nt_exact(a, b):
    return lax.dot_general(a, b, (((1,), (1,)), ((), ())), preferred_element_type=F32,
                           precision=lax.Precision.HIGHEST)


def _silu(x):
    return x * (1.0 / (1.0 + jnp.exp(-x)))


def _softplus(x):
    return jnp.maximum(x, 0.0) + jnp.log1p(jnp.exp(-jnp.abs(x)))


def _gelu_tanh(x):
    return x * (0.5 * (1.0 + jnp.tanh(math.sqrt(2.0 / math.pi) * (x + 0.044715 * (x * x * x)))))


def _norm_mod(x, g, sc, sh):
    ms = jnp.mean(x * x, axis=-1, keepdims=True)
    return (x * lax.rsqrt(ms + EPS)) * g * (1.0 + sc) + sh


def _params(sem):
    return pltpu.CompilerParams(dimension_semantics=sem, vmem_limit_bytes=VMEM_LIMIT)


def _resident(shape):
    nd = len(shape)
    return pl.BlockSpec(shape, lambda *_: (0,) * nd, pipeline_mode=pl.Buffered(1))


def _mod_kernel(c_ref, w_ref, b_ref, o_ref):
    c = c_ref[...]
    o_ref[0] = _mm(_silu(c).astype(BF16), w_ref[0].astype(BF16)) + b_ref[0]


def _modulation(c_all, w_mod, b_mod):
    depth, d, n = w_mod.shape
    r = c_all.shape[0]
    tn = 1536
    return pl.pallas_call(
        _mod_kernel,
        grid=(depth, n // tn),
        in_specs=[pl.BlockSpec((r, d), lambda l, j: (0, 0)),
                  pl.BlockSpec((1, d, tn), lambda l, j: (l, 0, j)),
                  pl.BlockSpec((1, 1, tn), lambda l, j: (l, 0, j))],
        out_specs=pl.BlockSpec((1, r, tn), lambda l, j: (l, 0, j)),
        out_shape=jax.ShapeDtypeStruct((depth, r, n), F32),
        compiler_params=_params(("parallel", "parallel")),
        name="modulation",
    )(c_all, w_mod, b_mod.reshape(depth, 1, n))


def _table_gather_kernel(tab_ref, idx_ref, o_ref, *, n_entries, n_heads):
    idx = idx_ref[...]

    def body(r, accs):
        m = idx == r
        return tuple(jnp.where(m, tab_ref[hh, r], a) for hh, a in enumerate(accs))

    accs = lax.fori_loop(0, n_entries, body, tuple(jnp.zeros(idx.shape, F32) for _ in range(n_heads)))
    for hh in range(n_heads):
        o_ref[hh:hh + 1, :] = accs[hh]


def _table_gather(table, idx):
    t, h = table.shape
    w = idx.shape[0]
    return pl.pallas_call(
        functools.partial(_table_gather_kernel, n_entries=t, n_heads=h),
        in_specs=[pl.BlockSpec(memory_space=pltpu.SMEM),
                  pl.BlockSpec((1, w), lambda: (0, 0))],
        out_specs=pl.BlockSpec((h, w), lambda: (0, 0)),
        out_shape=jax.ShapeDtypeStruct((h, w), F32),
        name="table_gather",
    )(table.T, idx.reshape(1, w))


def _toeplitz(vec, rows, cols):
    h, wv = vec.shape
    flat = jnp.tile(vec, (1, rows))[:, :rows * (wv - 1)]
    return flat.reshape(h, rows, wv - 1)[:, :, :cols]


def _t5_bucket(rel):
    half = T5_BUCKETS // 2
    max_exact = half // 2
    base = jnp.where(rel > 0, half, 0)
    n = jnp.abs(rel)
    nf = jnp.maximum(n, 1).astype(F32)
    large = max_exact + (jnp.log(nf / max_exact) / math.log(T5_MAX_DIST / max_exact) * (half - max_exact)).astype(jnp.int32)
    large = jnp.minimum(large, half - 1)
    return base + jnp.where(n < max_exact, n, large)


def _wrapped_offsets(n_pos, n_neg):
    p = jnp.arange(n_pos + n_neg, dtype=jnp.int32)
    return jnp.where(p < n_pos, p, p - (n_pos + n_neg))


def _t5_bias_prompt(t5_table, seq):
    nd = seq // DIFF_TQ
    u = _wrapped_offsets(seq, DIFF_TQ)
    vec = _table_gather(t5_table, _t5_bucket(u - (seq - DIFF_TQ)))
    wide = _toeplitz(vec, DIFF_TQ, seq)
    return wide.reshape(H_B, DIFF_TQ, nd, DIFF_TQ).transpose(0, 2, 1, 3)


def _t5_bias_sample(t5_table, past, lq):
    u = _wrapped_offsets(past + lq, lq)
    vec = _table_gather(t5_table, _t5_bucket(u - past))
    return _toeplitz(vec, lq, past + lq)


def _band_bias(rel_table):
    u = _wrapped_offsets(3 * BAND_TQ, BAND_TQ)
    idx = jnp.clip(2 * BAND_TQ - u, -REL_CLIP, REL_CLIP) + REL_CLIP
    bias = _toeplitz(_table_gather(rel_table, idx), BAND_TQ, 3 * BAND_TQ)
    dc = (np.arange(BAND_TQ)[:, None] // CHUNK + C_WINDOW // CHUNK) - np.arange(3 * BAND_TQ)[None, :] // CHUNK
    return jnp.where(jnp.asarray((dc >= 0) & (dc <= BAND_CHUNKS))[None], bias, NEG)


def _inproj_ab_kernel(x_ref, g_ref, sc_ref, sh_ref, w_ref, cos_ref, sin_ref, qkv_ref, kb_ref, vb_ref, *, tl):
    h = _norm_mod(x_ref[0], g_ref[...], sc_ref[0], sh_ref[0]).astype(BF16)
    cos = cos_ref[...]
    sin = sin_ref[...]
    for j in range(7):
        y = _mm(h, w_ref[:, j * 512:(j + 1) * 512])
        if j < 2:
            for hh in range(H_A):
                yh = y[:, hh * 128:(hh + 1) * 128]
                yh = yh * cos + pltpu.roll(yh, 64, 1) * sin
                if j == 1:
                    yh = yh * (DK_A ** -0.5)
                qkv_ref[0, :, j * 512 + hh * 128:j * 512 + (hh + 1) * 128] = yh.astype(BF16)
        else:
            if j >= 5:
                dst = kb_ref if j == 5 else vb_ref
                for hh in range(H_B):
                    dst[0, pl.ds(hh, tl, stride=H_B), :] = y[:, hh * 128:(hh + 1) * 128]
            qkv_ref[0, :, j * 512:(j + 1) * 512] = y.astype(BF16)


def _inproj_ab(x, g, sc, sh, w, cos, sin, tl):
    nseq, seq, d = x.shape
    grid = (nseq, seq // tl)
    tok = lambda n: pl.BlockSpec((1, tl, n), lambda b, l: (b, l, 0))
    per_seq = pl.BlockSpec((1, 1, d), lambda b, l: (b, 0, 0))
    cache = pl.BlockSpec((1, tl * H_B, 128), lambda b, l: (b, l, 0))
    return pl.pallas_call(
        functools.partial(_inproj_ab_kernel, tl=tl),
        grid=grid,
        in_specs=[tok(d), _resident((1, d)), per_seq, per_seq, _resident((d, N_AB_COLS)),
                  pl.BlockSpec((tl, 128), lambda b, l: (l, 0)),
                  pl.BlockSpec((tl, 128), lambda b, l: (l, 0))],
        out_specs=[tok(N_AB_COLS), cache, cache],
        out_shape=[jax.ShapeDtypeStruct((nseq, seq, N_AB_COLS), BF16),
                   jax.ShapeDtypeStruct((nseq, seq * H_B, 128), F32),
                   jax.ShapeDtypeStruct((nseq, seq * H_B, 128), F32)],
        compiler_params=_params(("parallel", "parallel")),
        name="inproj_ab",
    )(x, g, sc, sh, w, cos, sin)


def _retention_kernel(q_ref, k_ref, v_ref, g_ref, s0_ref, dec_ref, qd_ref, kd_ref, gc_ref, gn_ref,
                      o_ref, sn_ref, st_ref):
    @pl.when(pl.program_id(1) == 0)
    def _():
        st_ref[...] = s0_ref[0]

    for h in range(H_A):
        cs = slice(h * 128, (h + 1) * 128)
        q = q_ref[0, :, cs]
        k = k_ref[0, :, cs]
        v = v_ref[0, :, cs]
        state = st_ref[h]
        s = _mm_nt(q, k) * dec_ref[h]
        o = _mm(s.astype(BF16), v) + _mm((q.astype(F32) * qd_ref[h]).astype(BF16), state.astype(BF16))
        new_state = gc_ref[h] * state + _mm_tn((k.astype(F32) * kd_ref[h]).astype(BF16), v)
        st_ref[h] = new_state
        sn_ref[0, h] = new_state
        mu = jnp.mean(o, axis=-1, keepdims=True)
        dlt = o - mu
        var = jnp.mean(dlt * dlt, axis=-1, keepdims=True)
        y = dlt * lax.rsqrt(var + EPS) * gn_ref[:, cs] * _silu(g_ref[0, :, cs].astype(F32))
        o_ref[0, :, cs] = y.astype(BF16)


def _retention_consts(c):
    lg = np.log1p(-np.exp2(-5.0 - np.arange(H_A, dtype=np.float32))).astype(np.float32)
    idx = np.arange(c, dtype=np.float32)
    diff = idx[:, None] - idx[None, :]
    decay = np.where(diff[None] >= 0, np.exp(np.maximum(diff, 0.0)[None] * lg[:, None, None]), 0.0)
    qd = np.exp((idx + 1.0)[None, :] * lg[:, None])
    kd = np.exp((c - 1.0 - idx)[None, :] * lg[:, None])
    gc = np.exp(c * lg)
    bc = lambda a: np.ascontiguousarray(np.broadcast_to(a[..., None], a.shape + (128,))).astype(np.float32)
    return decay.astype(np.float32), bc(qd), bc(kd), bc(gc[:, None])


def _retention(qkv, s0, ret_gn, c):
    nseq, seq, _ = qkv.shape
    decay, qd, kd, gc = _retention_consts(c)
    w = H_A * 128
    col = lambda idx: pl.BlockSpec((1, c, w), lambda b, l: (b, l, idx))
    const = lambda shape: pl.BlockSpec(shape, lambda b, l: (0,) * len(shape))
    state = pl.BlockSpec((1, H_A, DK_A, DV_A), lambda b, l: (b, 0, 0, 0))
    return pl.pallas_call(
        _retention_kernel,
        grid=(nseq, seq // c),
        in_specs=[col(0), col(1), col(2), col(3), state, const((H_A, c, c)), const((H_A, c, 128)),
                  const((H_A, c, 128)), const((H_A, 1, 128)), const((1, w))],
        out_specs=[col(0), state],
        out_shape=[jax.ShapeDtypeStruct((nseq, seq, w), BF16),
                   jax.ShapeDtypeStruct((nseq, H_A, DK_A, DV_A), F32)],
        scratch_shapes=[pltpu.VMEM((H_A, DK_A, DV_A), F32)],
        compiler_params=_params(("parallel", "arbitrary")),
        name="retention",
    )(qkv, qkv, qkv, qkv, s0, decay, qd, kd, gc, ret_gn.reshape(1, w))


def _split_halves(q):
    lane = lax.broadcasted_iota(jnp.int32, q.shape, 1)
    zero = jnp.zeros_like(q)
    return jnp.where(lane < 64, q, zero), jnp.where(lane >= 64, q, zero)


def _lambda(lq_ref, lk_ref, lam_init):
    e = jnp.exp(jnp.sum(lq_ref[...] * lk_ref[...], axis=1, keepdims=True))
    return e[0:1, :] - e[1:2, :] + lam_init


def _diff_epilogue(o, gn, lam_init):
    ms = jnp.mean(o * o, axis=-1, keepdims=True)
    return (o * lax.rsqrt(ms + EPS)) * gn * (1.0 - lam_init)


def _diff_attn_kernel(q_ref, k_ref, v_ref, b_ref, lq_ref, lk_ref, gn_ref, o_ref, s_ref, qs_ref, mx_ref, ls_ref,
                      acc_ref, *, tq, nd, lam_init):
    tp = pl.program_id(2)
    tiles = (tp, nd - 1 - tp)
    row = lax.broadcasted_iota(jnp.int32, (tq, tq), 0) // CHUNK
    col = lax.broadcasted_iota(jnp.int32, (tq, tq), 1) // CHUNK
    chunk_mask = col <= row
    for side in range(2):
        q = q_ref[0, pl.ds(pl.multiple_of(tiles[side] * tq, tq), tq), :]
        q0, q1 = _split_halves((q.astype(F32) * (DH_B ** -0.5)).astype(BF16))
        qs_ref[2 * side] = q0
        qs_ref[2 * side + 1] = q1
    mx_ref[...] = jnp.full(mx_ref.shape, NEG, F32)
    ls_ref[...] = jnp.zeros(ls_ref.shape, F32)
    acc_ref[...] = jnp.zeros(acc_ref.shape, F32)

    slots = [(0, tiles[0], True), (1, tiles[1], True)]
    for j in range(nd - 1):
        side = jnp.where(j < tp, 0, 1)
        slots.append((side, jnp.where(j < tp, j, j - tp), False))

    for side, kb, masked in slots:
        t = tiles[side] if masked else jnp.where(side == 0, tiles[0], tiles[1])
        k = k_ref[0, pl.ds(pl.multiple_of(kb * tq, tq), tq), :]
        b = b_ref[0, nd - 1 - t + kb]
        for i in range(2):
            r = 2 * side + i
            s = _mm_nt(qs_ref[r], k) + b
            if masked:
                s = jnp.where(chunk_mask, s, NEG)
            s_ref[r * nd + kb] = s
            m = mx_ref[r]
            for c0 in range(0, tq, 128):
                m = jnp.maximum(m, s[:, c0:c0 + 128])
            mx_ref[r] = m

    for r in range(4):
        mx_ref[r] = jnp.broadcast_to(jnp.max(mx_ref[r], axis=1, keepdims=True), (tq, 128))

    for side, kb, _ in slots:
        v = v_ref[0, pl.ds(pl.multiple_of(kb * tq, tq), tq), :]
        for i in range(2):
            r = 2 * side + i
            m = mx_ref[r]
            l = ls_ref[r]
            ps = []
            for c0 in range(0, tq, 128):
                p = jnp.exp(s_ref[r * nd + kb, :, c0:c0 + 128] - m)
                l = l + p
                ps.append(p.astype(BF16))
            ls_ref[r] = l
            acc_ref[r] += _mm(jnp.concatenate(ps, axis=1), v)

    lam = _lambda(lq_ref, lk_ref, lam_init)
    for side in range(2):
        l0 = jnp.sum(ls_ref[2 * side], axis=1, keepdims=True)
        l1 = jnp.sum(ls_ref[2 * side + 1], axis=1, keepdims=True)
        o = acc_ref[2 * side] / l0 - lam * (acc_ref[2 * side + 1] / l1)
        o_ref[0, pl.ds(pl.multiple_of(tiles[side] * tq, tq), tq), :] = (
            _diff_epilogue(o, gn_ref[...], lam_init).astype(BF16))


def _diff_attn(qkv, bias, lam_q, lam_k, diff_gn, lam_init):
    nseq, seq, _ = qkv.shape
    tq = DIFF_TQ
    nd = seq // tq
    full = lambda off: pl.BlockSpec((1, seq, 128), lambda b, h, t: (b, 0, off + h))
    small = lambda r, c: pl.BlockSpec((r, c), lambda b, h, t: (0, 0))
    return pl.pallas_call(
        functools.partial(_diff_attn_kernel, tq=tq, nd=nd, lam_init=lam_init),
        grid=(nseq, H_B, nd // 2),
        in_specs=[full(16), full(20), full(24),
                  pl.BlockSpec((1, nd, tq, tq), lambda b, h, t: (h, 0, 0, 0)),
                  small(2, DH_B), small(2, DH_B), small(1, DV_B)],
        out_specs=full(0),
        out_shape=jax.ShapeDtypeStruct((nseq, seq, H_B * DV_B), BF16),
        scratch_shapes=[pltpu.VMEM((4 * nd, tq, tq), F32), pltpu.VMEM((4, tq, 128), BF16),
                        pltpu.VMEM((4, tq, 128), F32), pltpu.VMEM((4, tq, 128), F32),
                        pltpu.VMEM((4, tq, DV_B), F32)],
        compiler_params=_params(("parallel", "parallel", "arbitrary")),
        name="diff_attn",
    )(qkv, qkv, qkv, bias, lam_q, lam_k, diff_gn.reshape(1, DV_B))


def _diff_attn_sample_kernel(q_ref, kc_ref, vc_ref, kp_ref, vp_ref, b_ref, lq_ref, lk_ref, gn_ref, o_ref,
                             *, past, lam_init):
    scale = DH_B ** -0.5
    lam = _lambda(lq_ref, lk_ref, lam_init)
    for h in range(H_B):
        cs = slice(h * 128, (h + 1) * 128)
        qs = _split_halves(q_ref[0, :, cs])
        kp = kp_ref[0, :, cs].astype(BF16)
        vp = vp_ref[0, :, cs].astype(BF16)
        kc = kc_ref[0, :, cs]
        vc = vc_ref[0, :, cs]
        b = b_ref[h]
        probs = []
        for i in range(2):
            sp = _mm_nt(qs[i], kp) * scale + b[:, :past]
            sc = _mm_nt(qs[i], kc) * scale + b[:, past:]
            m = jnp.maximum(jnp.max(sp, axis=1, keepdims=True), jnp.max(sc, axis=1, keepdims=True))
            pp = jnp.exp(sp - m)
            pc = jnp.exp(sc - m)
            l = jnp.sum(pp, axis=1, keepdims=True) + jnp.sum(pc, axis=1, keepdims=True)
            probs.append((pp / l, pc / l))
        ap = probs[0][0] - lam * probs[1][0]
        ac = probs[0][1] - lam * probs[1][1]
        o = _mm(ap.astype(BF16), vp) + _mm(ac.astype(BF16), vc)
        o_ref[0, :, cs] = _diff_epilogue(o, gn_ref[...], lam_init).astype(BF16)


def _diff_attn_sample(qkv, k_past, v_past, bias, lam_q, lam_k, diff_gn, lam_init):
    nseq, lq, _ = qkv.shape
    past = k_past.shape[1]
    w = H_B * DV_B
    cur = lambda idx: pl.BlockSpec((1, lq, w), lambda b: (b, 0, idx))
    old = pl.BlockSpec((1, past, w), lambda b: (b, 0, 0))
    small = lambda r, c: pl.BlockSpec((r, c), lambda b: (0, 0))
    return pl.pallas_call(
        functools.partial(_diff_attn_sample_kernel, past=past, lam_init=lam_init),
        grid=(nseq,),
        in_specs=[cur(4), cur(5), cur(6), old, old, _resident(bias.shape),
                  small(2, DH_B), small(2, DH_B), small(1, DV_B)],
        out_specs=cur(0),
        out_shape=jax.ShapeDtypeStruct((nseq, lq, w), BF16),
        compiler_params=_params(("parallel",)),
        name="diff_attn_sample",
    )(qkv, qkv, qkv, k_past, v_past, bias, lam_q, lam_k, diff_gn.reshape(1, DV_B))


def _inproj_cd_kernel(x_ref, g_ref, sc_ref, sh_ref, w_ref, main_ref, dt_ref, kc_ref, vc_ref, tail_ref, *, tl):
    h = _norm_mod(x_ref[0], g_ref[...], sc_ref[0], sh_ref[0]).astype(BF16)
    for j in range(6):
        y = _mm(h, w_ref[:, j * 512:(j + 1) * 512])
        main_ref[0, :, j * 512:(j + 1) * 512] = y.astype(BF16)
        if j == 1:
            kc_ref[0] = y
        if j == 2:
            vc_ref[0] = y
        if j >= 4:
            tail_ref[0, :, (j - 4) * 512:(j - 3) * 512] = y[tl - 8:tl, :]
    dt_ref[0] = _mm(h, w_ref[:, N_CD_MAIN:N_CD_PAD])


def _inproj_cd(x, g, sc, sh, w, tl, keep):
    nseq, seq, d = x.shape
    assert keep % tl == 0
    skip = (seq - keep) // tl
    tok = lambda n: pl.BlockSpec((1, tl, n), lambda b, l: (b, l, 0))
    kept = pl.BlockSpec((1, tl, 512), lambda b, l: (b, jnp.maximum(l - skip, 0), 0))
    per_seq = pl.BlockSpec((1, 1, d), lambda b, l: (b, 0, 0))
    return pl.pallas_call(
        functools.partial(_inproj_cd_kernel, tl=tl),
        grid=(nseq, seq // tl),
        in_specs=[tok(d), _resident((1, d)), per_seq, per_seq, _resident((d, N_CD_PAD))],
        out_specs=[tok(N_CD_MAIN), tok(128), kept, kept,
                   pl.BlockSpec((1, 8, CONV_DIM_D), lambda b, l: (b, 0, 0))],
        out_shape=[jax.ShapeDtypeStruct((nseq, seq, N_CD_MAIN), BF16),
                   jax.ShapeDtypeStruct((nseq, seq, 128), F32),
                   jax.ShapeDtypeStruct((nseq, keep, 512), F32),
                   jax.ShapeDtypeStruct((nseq, keep, 512), F32),
                   jax.ShapeDtypeStruct((nseq, 8, CONV_DIM_D), F32)],
        compiler_params=_params(("parallel", "arbitrary")),
        name="inproj_cd",
    )(x, g, sc, sh, w)


def _band_kernel(q_ref, k_ref, v_ref, b_ref, o_ref, *, tq):
    t = pl.program_id(1)
    lane = lax.broadcasted_iota(jnp.int32, (tq, 128), 1)

    def attend(near_start):
        starts, offs = [], []
        for d in range(3):
            kt = t - 2 + d
            starts.append(pl.multiple_of(jnp.maximum(kt, 0) * tq, tq))
            offs.append(jnp.where(kt >= 0, 0.0, NEG))
        for pr in range(H_C // 2):
            cs = slice(pr * 128, (pr + 1) * 128)
            halves = _split_halves((q_ref[0, :, cs].astype(F32) * (DH_C ** -0.5)).astype(BF16))
            ks = [k_ref[0, pl.ds(starts[d], tq), cs] for d in range(3)]
            vs = [v_ref[0, pl.ds(starts[d], tq), cs] for d in range(3)]
            outs = []
            for hi, qh in enumerate(halves):
                ss = []
                for d in range(3):
                    s = _mm_nt(qh, ks[d]) + b_ref[2 * pr + hi, :, d * tq:(d + 1) * tq]
                    ss.append(s + offs[d] if near_start and d < 2 else s)
                mm = ss[2][:, 0:128]
                for d in range(3):
                    for c0 in range(0, tq, 128):
                        mm = jnp.maximum(mm, ss[d][:, c0:c0 + 128])
                m = jnp.max(mm, axis=1, keepdims=True)
                ls = jnp.zeros((tq, 128), F32)
                o = jnp.zeros((tq, 128), F32)
                for d in range(3):
                    p = jnp.exp(ss[d] - m)
                    for c0 in range(0, tq, 128):
                        ls = ls + p[:, c0:c0 + 128]
                    o = o + _mm(p.astype(BF16), vs[d])
                outs.append(o / jnp.sum(ls, axis=1, keepdims=True))
            o_ref[0, :, cs] = jnp.where(lane < 64, outs[0], outs[1]).astype(BF16)

    @pl.when(t < 2)
    def _():
        attend(True)

    @pl.when(t >= 2)
    def _():
        attend(False)


def _band_attn(main, bias):
    nseq, seq, _ = main.shape
    tq = BAND_TQ
    w = H_C * DH_C
    full = lambda idx: pl.BlockSpec((1, seq, w), lambda b, t: (b, 0, idx))
    return pl.pallas_call(
        functools.partial(_band_kernel, tq=tq),
        grid=(nseq, seq // tq),
        in_specs=[pl.BlockSpec((1, tq, w), lambda b, t: (b, t, 0)), full(1), full(2), _resident(bias.shape)],
        out_specs=pl.BlockSpec((1, tq, w), lambda b, t: (b, t, 0)),
        out_shape=jax.ShapeDtypeStruct((nseq, seq, w), BF16),
        compiler_params=_params(("parallel", "parallel")),
        name="band_attn",
    )(main, main, main, bias)


def _band_sample_kernel(q_ref, kc_ref, vc_ref, kp_ref, vp_ref, b_ref, o_ref, *, past):
    scale = DH_C ** -0.5
    for pr in range(H_C // 2):
        cs = slice(pr * 128, (pr + 1) * 128)
        halves = _split_halves(q_ref[0, :, cs])
        kp = kp_ref[0, :, cs].astype(BF16)
        vp = vp_ref[0, :, cs].astype(BF16)
        kc = kc_ref[0, :, cs]
        vc = vc_ref[0, :, cs]
        outs = []
        for hi, qh in enumerate(halves):
            b = b_ref[2 * pr + hi]
            sp = _mm_nt(qh, kp) * scale + b[:, :past]
            sc = _mm_nt(qh, kc) * scale + b[:, past:]
            m = jnp.maximum(jnp.max(sp, axis=1, keepdims=True), jnp.max(sc, axis=1, keepdims=True))
            pp = jnp.exp(sp - m)
            pc = jnp.exp(sc - m)
            l = jnp.sum(pp, axis=1, keepdims=True) + jnp.sum(pc, axis=1, keepdims=True)
            outs.append(_mm((pp / l).astype(BF16), vp) + _mm((pc / l).astype(BF16), vc))
        lane = lax.broadcasted_iota(jnp.int32, outs[0].shape, 1)
        o_ref[0, :, cs] = jnp.where(lane < 64, outs[0], outs[1]).astype(BF16)


def _band_attn_sample(main, k_past, v_past, bias):
    nseq, lq, _ = main.shape
    past = k_past.shape[1]
    w = H_C * DH_C
    cur = lambda idx: pl.BlockSpec((1, lq, w), lambda b: (b, 0, idx))
    old = pl.BlockSpec((1, past, w), lambda b: (b, 0, 0))
    return pl.pallas_call(
        functools.partial(_band_sample_kernel, past=past),
        grid=(nseq,),
        in_specs=[cur(0), cur(1), cur(2), old, old, _resident(bias.shape)],
        out_specs=cur(0),
        out_shape=jax.ShapeDtypeStruct((nseq, lq, w), BF16),
        compiler_params=_params(("parallel",)),
        name="band_attn_sample",
    )(main, main, main, k_past, v_past, bias)


def _ssd_kernel(z_ref, xs_ref, bm_ref, cm_ref, dt_ref, cw_ref, cb_ref, dtb_ref, alog_ref, dskip_ref, ng_ref,
                past_ref, s0_ref, o_ref, sn_ref, buf_ref, st_ref, *, c):
    @pl.when(pl.program_id(1) == 0)
    def _():
        buf_ref[0:8, :] = past_ref[0]
        st_ref[...] = s0_ref[0]

    buf_ref[8:8 + c, 0:D_INNER] = xs_ref[0].astype(F32)
    buf_ref[8:8 + c, D_INNER:D_INNER + 256] = bm_ref[0].astype(F32)
    buf_ref[8:8 + c, D_INNER + 256:CONV_DIM_D] = cm_ref[0].astype(F32)
    conv = cb_ref[...] + buf_ref[pl.ds(5, c), :] * cw_ref[0:1, :]
    for i in range(1, CONV_D):
        conv = conv + buf_ref[pl.ds(5 + i, c), :] * cw_ref[i:i + 1, :]
    buf_ref[0:8, :] = buf_ref[c:c + 8, :]
    act = _silu(conv)
    xs = act[:, 0:D_INNER]
    bm = act[:, D_INNER:D_INNER + 256].astype(BF16)
    cm = act[:, D_INNER + 256:CONV_DIM_D].astype(BF16)

    dt = _softplus(dt_ref[0] + dtb_ref[...])
    da = dt * (-jnp.exp(alog_ref[...]))
    row = lax.broadcasted_iota(jnp.int32, (c, c), 0)
    col = lax.broadcasted_iota(jnp.int32, (c, c), 1)
    tri = row >= col
    cum = _mm_exact(tri.astype(F32), da)
    eye = (lax.broadcasted_iota(jnp.int32, (8, 128), 0) == lax.broadcasted_iota(jnp.int32, (8, 128), 1)).astype(F32)
    cum_t = _mm_exact(_mm_nt_exact(eye, da), (row <= col).astype(F32))
    dt_t = _mm_nt_exact(eye, dt)
    ecum = jnp.exp(cum)
    last = cum[c - 1:c, :]
    wgt = jnp.exp(last - cum) * dt
    elast = jnp.exp(last)
    lane = lax.broadcasted_iota(jnp.int32, (c, 128), 1)
    low = lane < 64
    rlow = lax.broadcasted_iota(jnp.int32, (128, 128), 0) < 64

    for g in range(G_D):
        bm_g = bm[:, g * 128:(g + 1) * 128]
        cm_g = cm[:, g * 128:(g + 1) * 128]
        cb = _mm_nt(cm_g, bm_g)
        ys = []
        for pp in range(2):
            p = 2 * g + pp
            h0, h1 = 2 * p, 2 * p + 1
            x_f = xs[:, p * 128:(p + 1) * 128]
            x_b = x_f.astype(BF16)
            y_in = []
            for hh in (h0, h1):
                seg = cum[:, hh:hh + 1] - cum_t[hh:hh + 1, :]
                dec = jnp.exp(jnp.where(tri, seg, NEG))
                y_in.append(_mm((cb * dec * dt_t[hh:hh + 1, :]).astype(BF16), x_b))
            st = st_ref[p * 128:(p + 1) * 128, :]
            y_x = _mm_nt(cm_g, st.astype(BF16)) * jnp.where(low, ecum[:, h0:h0 + 1], ecum[:, h1:h1 + 1])
            w2 = jnp.where(low, wgt[:, h0:h0 + 1], wgt[:, h1:h1 + 1])
            new = jnp.where(rlow, elast[:, h0:h0 + 1], elast[:, h1:h1 + 1]) * st + _mm_tn((x_f * w2).astype(BF16), bm_g)
            st_ref[p * 128:(p + 1) * 128, :] = new
            sn_ref[0, p * 128:(p + 1) * 128, :] = new
            y = jnp.where(low, y_in[0], y_in[1]) + y_x + jnp.where(low, dskip_ref[h0], dskip_ref[h1]) * x_f
            ys.append(y * _silu(z_ref[0, :, p * 128:(p + 1) * 128].astype(F32)))
        ms = (jnp.sum(ys[0] * ys[0], axis=-1, keepdims=True) + jnp.sum(ys[1] * ys[1], axis=-1, keepdims=True)) / 256.0
        inv = lax.rsqrt(ms + EPS)
        for pp in range(2):
            p = 2 * g + pp
            o_ref[0, :, p * 128:(p + 1) * 128] = (ys[pp] * inv * ng_ref[:, p * 128:(p + 1) * 128]).astype(BF16)


def _ssd(main, dt, conv_w, conv_b, dt_bias, a_log, d_skip, norm_g, conv_past, s0, c):
    nseq, seq, _ = main.shape
    pad = lambda a: jnp.pad(a.reshape(1, H_D), ((0, 0), (0, 128 - H_D)))
    blk = lambda w, idx: pl.BlockSpec((1, c, w), lambda b, l: (b, l, idx))
    const = lambda r, w: pl.BlockSpec((r, w), lambda b, l: (0, 0))
    state = pl.BlockSpec((1, H_D * P_D, N_D), lambda b, l: (b, 0, 0))
    return pl.pallas_call(
        functools.partial(_ssd_kernel, c=c),
        grid=(nseq, seq // c),
        in_specs=[blk(512, 3), blk(512, 4), blk(256, 10), blk(256, 11), blk(128, 0),
                  const(CONV_D, CONV_DIM_D), const(1, CONV_DIM_D), const(1, 128), const(1, 128),
                  pl.BlockSpec(memory_space=pltpu.SMEM), const(1, D_INNER),
                  pl.BlockSpec((1, 8, CONV_DIM_D), lambda b, l: (b, 0, 0)), state],
        out_specs=[blk(512, 0), state],
        out_shape=[jax.ShapeDtypeStruct((nseq, seq, D_INNER), BF16),
                   jax.ShapeDtypeStruct((nseq, H_D * P_D, N_D), F32)],
        scratch_shapes=[pltpu.VMEM((c + 8, CONV_DIM_D), F32), pltpu.VMEM((H_D * P_D, N_D), F32)],
        compiler_params=_params(("parallel", "arbitrary")),
        name="ssd",
    )(main, main, main, main, dt, conv_w, conv_b.reshape(1, CONV_DIM_D), pad(dt_bias), pad(a_log), d_skip,
      norm_g.reshape(1, D_INNER), conv_past, s0)


def _ffn_kernel(x_ref, o1_ref, o2_ref, wo_ref, g1_ref, ng_ref, sc_ref, sh_ref, g2_ref, wup_ref, cw_ref, cb_ref,
                wdn_ref, past_ref, fg_ref, out_ref, tail_ref, act_ref, wb_ref, gt_ref, *, tl, d_ff, final):
    @pl.when(pl.program_id(1) == 0)
    def _():
        gt_ref[...] = past_ref[0]

    half = wo_ref.shape[0] // 2
    mix = _mm(o1_ref[0], wo_ref[0:half, :]) + _mm(o2_ref[0], wo_ref[half:2 * half, :])
    x1 = x_ref[0] + g1_ref[0] * mix
    out_ref[0] = x1
    h = _norm_mod(x1, ng_ref[...], sc_ref[0], sh_ref[0]).astype(BF16)
    for j in range(d_ff // FF_CHUNK):
        c0 = j * FF_CHUNK
        a = _mm(h, wup_ref[:, c0:c0 + FF_CHUNK])
        g = _mm(h, wup_ref[:, d_ff + c0:d_ff + c0 + FF_CHUNK])
        wb_ref[0:8, :] = gt_ref[:, c0:c0 + FF_CHUNK]
        wb_ref[8:8 + tl, :] = g
        gt_ref[:, c0:c0 + FF_CHUNK] = g[tl - 8:tl, :]
        gc = (cb_ref[:, c0:c0 + FF_CHUNK] + wb_ref[pl.ds(6, tl), :] * cw_ref[0:1, c0:c0 + FF_CHUNK]
              + wb_ref[pl.ds(7, tl), :] * cw_ref[1:2, c0:c0 + FF_CHUNK] + g * cw_ref[2:3, c0:c0 + FF_CHUNK])
        act_ref[:, c0:c0 + FF_CHUNK] = (a * _gelu_tanh(gc)).astype(BF16)
    split = (d_ff // FF_CHUNK + 1) // 2 * FF_CHUNK
    dn = _mm(act_ref[:, 0:split], wdn_ref[0:split, :]) + _mm(act_ref[:, split:d_ff], wdn_ref[split:d_ff, :])
    x2 = out_ref[0] + g2_ref[0] * dn
    if final:
        ms = jnp.mean(x2 * x2, axis=-1, keepdims=True)
        x2 = (x2 * lax.rsqrt(ms + EPS)) * fg_ref[...]
    out_ref[0] = x2
    tail_ref[0] = gt_ref[...]


def _outproj_ffn(x, o1, o2, wo, g1, ng, sc, sh, g2, wup, cw, cb, wdn, past, fg, tl, final):
    nseq, seq, d = x.shape
    d_ff = wdn.shape[0]
    tok = lambda n: pl.BlockSpec((1, tl, n), lambda b, l: (b, l, 0))
    per_seq = pl.BlockSpec((1, 1, d), lambda b, l: (b, 0, 0))
    tail = pl.BlockSpec((1, 8, d_ff), lambda b, l: (b, 0, 0))
    return pl.pallas_call(
        functools.partial(_ffn_kernel, tl=tl, d_ff=d_ff, final=final),
        grid=(nseq, seq // tl),
        in_specs=[tok(d), tok(o1.shape[-1]), tok(o2.shape[-1]), _resident(wo.shape), per_seq, _resident((1, d)),
                  per_seq, per_seq, per_seq, _resident(wup.shape), _resident(cw.shape), _resident((1, d_ff)),
                  _resident(wdn.shape), tail, _resident((1, d))],
        out_specs=[tok(d), tail],
        out_shape=[jax.ShapeDtypeStruct((nseq, seq, d), F32), jax.ShapeDtypeStruct((nseq, 8, d_ff), F32)],
        scratch_shapes=[pltpu.VMEM((tl, d_ff), BF16), pltpu.VMEM((tl + 8, FF_CHUNK), F32), pltpu.VMEM((8, d_ff), F32)],
        compiler_params=_params(("parallel", "arbitrary")),
        name="outproj_ffn",
    )(x, o1, o2, wo, g1, ng, sc, sh, g2, wup, cw, cb.reshape(1, d_ff), wdn, past, fg)


def _rope_tables(pos):
    half = DK_A // 2
    inv = jnp.power(ROPE_BASE, -jnp.arange(half, dtype=F32) / half)
    ang = pos.astype(F32)[:, None] * inv[None, :]
    cos = jnp.cos(ang)
    sin = jnp.sin(ang)
    return jnp.concatenate([cos, cos], axis=1), jnp.concatenate([-sin, sin], axis=1)


def _pad_rows(a, rows=8):
    return jnp.pad(a, ((0, 0), (rows - a.shape[1], 0), (0, 0)))


def _trunk(x, mods, pos, weights, caches, biases):
    nseq, seq, d = x.shape
    sample = caches is not None
    tl = min(seq, 512)
    c = min(seq, SCAN_CHUNK)
    depth = weights["w_up"].shape[0]
    outs = {k: [] for k in ("ret", "bk", "bv", "ck", "cv", "dconv", "dssm", "ffn")}
    cos, sin = _rope_tables(pos)
    for l in range(depth):
        i = l // 2
        sh1, sc1, g1, sh2, sc2, g2 = mods[l]
        ng1 = weights["norm_g"][l, 0].reshape(1, d)
        ng2 = weights["norm_g"][l, 1].reshape(1, d)
        if l % 2 == 0:
            qkv, kb, vb = _inproj_ab(x, ng1, sc1, sh1, weights["w_in_ab"][i], cos, sin, tl)
            s0 = caches["ret"][i] if sample else jnp.zeros((nseq, H_A, DK_A, DV_A), F32)
            o1, s_new = _retention(qkv, s0, weights["ret_gn"][i], c)
            lam_init = 0.8 - 0.6 * math.exp(-0.3 * l)
            if sample:
                o2 = _diff_attn_sample(qkv, caches["bk"][i], caches["bv"][i], biases["t5"],
                                       weights["lam_q"][i], weights["lam_k"][i], weights["diff_gn"][i], lam_init)
            else:
                o2 = _diff_attn(qkv, biases["t5"], weights["lam_q"][i], weights["lam_k"][i],
                                weights["diff_gn"][i], lam_init)
            wo = weights["w_out_ab"][i]
            outs["ret"].append(s_new)
            outs["bk"].append(kb.reshape(nseq, seq, H_B, 2 * DH_B))
            outs["bv"].append(vb.reshape(nseq, seq, H_B, DV_B))
        else:
            keep = seq if sample else min(C_WINDOW, seq)
            main, dt, kc, vc, tail = _inproj_cd(x, ng1, sc1, sh1, weights["w_in_cd"][i], tl, keep)
            if sample:
                o1 = _band_attn_sample(main, caches["ck"][i], caches["cv"][i], biases["band"][i])
                conv_past = _pad_rows(caches["dconv"][i])
                s0 = caches["dssm"][i].reshape(nseq, H_D * P_D, N_D)
            else:
                o1 = _band_attn(main, biases["band"][i])
                conv_past = jnp.zeros((nseq, 8, CONV_DIM_D), F32)
                s0 = jnp.zeros((nseq, H_D * P_D, N_D), F32)
            o2, ssm_new = _ssd(main, dt, weights["d_conv_w"][i], weights["d_conv_b"][i], weights["d_dt_bias"][i],
                               weights["d_a_log"][i], weights["d_skip"][i], weights["d_norm_g"][i], conv_past, s0, c)
            wo = weights["w_out_cd"][i]
            outs["ck"].append(kc.reshape(nseq, keep, H_C, DH_C))
            outs["cv"].append(vc.reshape(nseq, keep, H_C, DH_C))
            outs["dconv"].append(tail[:, 8 - (CONV_D - 1):])
            outs["dssm"].append(ssm_new.reshape(nseq, H_D, P_D, N_D))
        d_ff = weights["w_down"].shape[1]
        ffn_past = _pad_rows(caches["ffn"][l]) if sample else jnp.zeros((nseq, 8, d_ff), F32)
        x, ftail = _outproj_ffn(x, o1, o2, wo, g1, ng2, sc2, sh2, g2, weights["w_up"][l], weights["ffn_conv_w"][l],
                                weights["ffn_conv_b"][l], weights["w_down"][l], ffn_past,
                                weights["final_g"].reshape(1, d), tl, final=(l == depth - 1))
        outs["ffn"].append(ftail[:, 8 - (CONV_F - 1):])
    stk = lambda t: jnp.stack(t).astype(F32)
    return (x,) + tuple(stk(outs[k]) for k in ("ret", "bk", "bv", "ck", "cv", "dconv", "dssm", "ffn"))


def kernel(x_prompt, x_sample, cache_ret_state, cache_b_k, cache_b_v, cache_c_k, cache_c_v, state_d_conv, state_d_ssm, state_ffn_conv, c_prompt, c_sample, w_mod, b_mod, norm_g, final_g, t5_table, w_in_ab, w_out_ab, ret_gn, lam_q, lam_k, diff_gn, w_in_cd, w_out_cd, rel_table, d_conv_w, d_conv_b, d_dt_bias, d_a_log, d_skip, d_norm_g, w_up, ffn_conv_w, ffn_conv_b, w_down):
    batch, seq, d = x_prompt.shape
    dec_batch, dec_seq, _ = x_sample.shape
    past = cache_b_k.shape[2]
    depth = w_mod.shape[0]
    assert dec_seq <= CHUNK and past % CHUNK == 0 and cache_c_k.shape[2] == C_WINDOW
    assert seq % BAND_TQ == 0 and seq % (2 * DIFF_TQ) == 0

    w_in_cd_p = jnp.pad(w_in_cd, ((0, 0), (0, 0), (0, N_CD_PAD - w_in_cd.shape[-1]))).astype(BF16)
    weights = dict(
        norm_g=norm_g, final_g=final_g, w_in_ab=w_in_ab.astype(BF16), w_out_ab=w_out_ab.astype(BF16),
        ret_gn=ret_gn, lam_q=lam_q, lam_k=lam_k, diff_gn=diff_gn, w_in_cd=w_in_cd_p,
        w_out_cd=w_out_cd.astype(BF16), d_conv_w=d_conv_w, d_conv_b=d_conv_b, d_dt_bias=d_dt_bias, d_a_log=d_a_log,
        d_skip=d_skip, d_norm_g=d_norm_g, w_up=w_up.astype(BF16), ffn_conv_w=ffn_conv_w, ffn_conv_b=ffn_conv_b,
        w_down=w_down.astype(BF16))

    mod = _modulation(jnp.concatenate([c_prompt, c_sample], axis=0), w_mod, b_mod)

    def pieces(rows):
        return [[m[:, None, :] for m in jnp.split(mod[l, rows], 6, axis=-1)] for l in range(depth)]

    band = [_band_bias(rel_table[i]) for i in range(rel_table.shape[0])]
    biases_p = dict(t5=_t5_bias_prompt(t5_table, seq), band=band)
    biases_s = dict(t5=_t5_bias_sample(t5_table, past, dec_seq),
                    band=[b[:, :dec_seq, :C_WINDOW + dec_seq] for b in band])
    caches = dict(
        ret=cache_ret_state,
        bk=cache_b_k.reshape(cache_b_k.shape[0], dec_batch, past, H_B * 2 * DH_B),
        bv=cache_b_v.reshape(cache_b_v.shape[0], dec_batch, past, H_B * DV_B),
        ck=cache_c_k.reshape(cache_c_k.shape[0], dec_batch, C_WINDOW, H_C * DH_C),
        cv=cache_c_v.reshape(cache_c_v.shape[0], dec_batch, C_WINDOW, H_C * DH_C),
        dconv=state_d_conv, dssm=state_d_ssm, ffn=state_ffn_conv)

    pos_p = jnp.arange(seq, dtype=jnp.int32)
    pos_s = past + jnp.arange(dec_seq, dtype=jnp.int32)
    y_p, ret_p, bk_p, bv_p, ck_p, cv_p, dconv_p, dssm_p, ffn_p = _trunk(
        x_prompt, pieces(slice(0, batch)), pos_p, weights, None, biases_p)
    y_s, ret_s, bk_s, bv_s, ck_s, cv_s, dconv_s, dssm_s, ffn_s = _trunk(
        x_sample, pieces(slice(batch, batch + dec_batch)), pos_s, weights, caches, biases_s)
    return (y_p, y_s, ret_p, ret_s, bk_p, bk_s, bv_p, bv_s, ck_p, ck_s, cv_p, cv_s,
            dconv_p, dconv_s, dssm_p, dssm_s, ffn_p, ffn_s)
```

```python
import functools
import math

import numpy as np
import jax
import jax.numpy as jnp
from jax import lax
from jax.experimental import pallas as pl
from jax.experimental.pallas import tpu as pltpu

F32 = jnp.float32
BF16 = jnp.bfloat16

CHUNK = 64
EPS = 1e-6
NEG = -1e30
H_A, DK_A, DV_A = 4, 128, 128
ROPE_BASE = 10000.0
H_B, DH_B, DV_B = 4, 64, 128
T5_BUCKETS, T5_MAX_DIST = 32, 128
H_C, DH_C = 8, 64
BAND_CHUNKS = 8
C_WINDOW = BAND_CHUNKS * CHUNK
REL_CLIP = 128
H_D, P_D, G_D, N_D = 8, 64, 2, 128
D_INNER = H_D * P_D
CONV_D = 4
CONV_DIM_D = D_INNER + 2 * G_D * N_D
CONV_F = 3
N_AB_COLS = 7 * 512
N_CD_MAIN = 6 * 512
N_CD_PAD = N_CD_MAIN + 128
FF_CHUNK = 256
BAND_TQ = 256
DIFF_TQ = 256
SCAN_CHUNK = 256
VMEM_LIMIT = 56 * 1024 * 1024


def _mm(a, b):
    return jnp.dot(a, b, preferred_element_type=F32)


def _mm_nt(a, b):
    return lax.dot_general(a, b, (((1,), (1,)), ((), ())), preferred_element_type=F32)


def _mm_tn(a, b):
    return lax.dot_general(a, b, (((0,), (0,)), ((), ())), preferred_element_type=F32)


def _mm_exact(a, b):
    return jnp.dot(a, b, preferred_element_type=F32, precision=lax.Precision.HIGHEST)


def _mm_nt_exact(a, b):
    return lax.dot_general(a, b, (((1,), (1,)), ((), ())), preferred_element_type=F32,
                           precision=lax.Precision.HIGHEST)


def _silu(x):
    return x * (1.0 / (1.0 + jnp.exp(-x)))


def _softplus(x):
    return jnp.maximum(x, 0.0) + jnp.log1p(jnp.exp(-jnp.abs(x)))


def _gelu_tanh(x):
    return x * (0.5 * (1.0 + jnp.tanh(math.sqrt(2.0 / math.pi) * (x + 0.044715 * (x * x * x)))))


def _norm_mod(x, g, sc, sh):
    ms = jnp.mean(x * x, axis=-1, keepdims=True)
    return (x * lax.rsqrt(ms + EPS)) * g * (1.0 + sc) + sh


def _params(sem):
    return pltpu.CompilerParams(dimension_semantics=sem, vmem_limit_bytes=VMEM_LIMIT)


def _resident(shape):
    nd = len(shape)
    return pl.BlockSpec(shape, lambda *_: (0,) * nd, pipeline_mode=pl.Buffered(1))


def _mod_kernel(c_ref, w_ref, b_ref, o_ref):
    c = c_ref[...]
    o_ref[0] = _mm(_silu(c).astype(BF16), w_ref[0].astype(BF16)) + b_ref[0]


def _modulation(c_all, w_mod, b_mod):
    depth, d, n = w_mod.shape
    r = c_all.shape[0]
    tn = 1536
    return pl.pallas_call(
        _mod_kernel,
        grid=(depth, n // tn),
        in_specs=[pl.BlockSpec((r, d), lambda l, j: (0, 0)),
                  pl.BlockSpec((1, d, tn), lambda l, j: (l, 0, j)),
                  pl.BlockSpec((1, 1, tn), lambda l, j: (l, 0, j))],
        out_specs=pl.BlockSpec((1, r, tn), lambda l, j: (l, 0, j)),
        out_shape=jax.ShapeDtypeStruct((depth, r, n), F32),
        compiler_params=_params(("parallel", "parallel")),
        name="modulation",
    )(c_all, w_mod, b_mod.reshape(depth, 1, n))


def _table_gather_kernel(tab_ref, idx_ref, o_ref, *, n_entries, n_heads):
    idx = idx_ref[...]

    def body(r, accs):
        m = idx == r
        return tuple(jnp.where(m, tab_ref[hh, r], a) for hh, a in enumerate(accs))

    accs = lax.fori_loop(0, n_entries, body, tuple(jnp.zeros(idx.shape, F32) for _ in range(n_heads)))
    for hh in range(n_heads):
        o_ref[hh:hh + 1, :] = accs[hh]


def _table_gather(table, idx):
    t, h = table.shape
    w = idx.shape[0]
    return pl.pallas_call(
        functools.partial(_table_gather_kernel, n_entries=t, n_heads=h),
        in_specs=[pl.BlockSpec(memory_space=pltpu.SMEM),
                  pl.BlockSpec((1, w), lambda: (0, 0))],
        out_specs=pl.BlockSpec((h, w), lambda: (0, 0)),
        out_shape=jax.ShapeDtypeStruct((h, w), F32),
        name="table_gather",
    )(table.T, idx.reshape(1, w))


def _toeplitz(vec, rows, cols):
    h, wv = vec.shape
    flat = jnp.tile(vec, (1, rows))[:, :rows * (wv - 1)]
    return flat.reshape(h, rows, wv - 1)[:, :, :cols]


def _t5_bucket(rel):
    half = T5_BUCKETS // 2
    max_exact = half // 2
    base = jnp.where(rel > 0, half, 0)
    n = jnp.abs(rel)
    nf = jnp.maximum(n, 1).astype(F32)
    large = max_exact + (jnp.log(nf / max_exact) / math.log(T5_MAX_DIST / max_exact) * (half - max_exact)).astype(jnp.int32)
    large = jnp.minimum(large, half - 1)
    return base + jnp.where(n < max_exact, n, large)


def _wrapped_offsets(n_pos, n_neg):
    p = jnp.arange(n_pos + n_neg, dtype=jnp.int32)
    return jnp.where(p < n_pos, p, p - (n_pos + n_neg))


def _t5_bias_prompt(t5_table, seq):
    nd = seq // DIFF_TQ
    u = _wrapped_offsets(seq, DIFF_TQ)
    vec = _table_gather(t5_table, _t5_bucket(u - (seq - DIFF_TQ)))
    wide = _toeplitz(vec, DIFF_TQ, seq)
    return wide.reshape(H_B, DIFF_TQ, nd, DIFF_TQ).transpose(0, 2, 1, 3)


def _t5_bias_sample(t5_table, past, lq):
    u = _wrapped_offsets(past + lq, lq)
    vec = _table_gather(t5_table, _t5_bucket(u - past))
    return _toeplitz(vec, lq, past + lq)


def _band_bias(rel_table):
    u = _wrapped_offsets(3 * BAND_TQ, BAND_TQ)
    idx = jnp.clip(2 * BAND_TQ - u, -REL_CLIP, REL_CLIP) + REL_CLIP
    bias = _toeplitz(_table_gather(rel_table, idx), BAND_TQ, 3 * BAND_TQ)
    dc = (np.arange(BAND_TQ)[:, None] // CHUNK + C_WINDOW // CHUNK) - np.arange(3 * BAND_TQ)[None, :] // CHUNK
    return jnp.where(jnp.asarray((dc >= 0) & (dc <= BAND_CHUNKS))[None], bias, NEG)


def _inproj_ab_kernel(x_ref, g_ref, sc_ref, sh_ref, w_ref, cos_ref, sin_ref, qkv_ref, kb_ref, vb_ref, *, tl):
    h = _norm_mod(x_ref[0], g_ref[...], sc_ref[0], sh_ref[0]).astype(BF16)
    cos = cos_ref[...]
    sin = sin_ref[...]
    for j in range(7):
        y = _mm(h, w_ref[:, j * 512:(j + 1) * 512])
        if j < 2:
            for hh in range(H_A):
                yh = y[:, hh * 128:(hh + 1) * 128]
                yh = yh * cos + pltpu.roll(yh, 64, 1) * sin
                if j == 1:
                    yh = yh * (DK_A ** -0.5)
                qkv_ref[0, :, j * 512 + hh * 128:j * 512 + (hh + 1) * 128] = yh.astype(BF16)
        else:
            if j >= 5:
                dst = kb_ref if j == 5 else vb_ref
                for hh in range(H_B):
                    dst[0, pl.ds(hh, tl, stride=H_B), :] = y[:, hh * 128:(hh + 1) * 128]
            qkv_ref[0, :, j * 512:(j + 1) * 512] = y.astype(BF16)


def _inproj_ab(x, g, sc, sh, w, cos, sin, tl):
    nseq, seq, d = x.shape
    grid = (nseq, seq // tl)
    tok = lambda n: pl.BlockSpec((1, tl, n), lambda b, l: (b, l, 0))
    per_seq = pl.BlockSpec((1, 1, d), lambda b, l: (b, 0, 0))
    cache = pl.BlockSpec((1, tl * H_B, 128), lambda b, l: (b, l, 0))
    return pl.pallas_call(
        functools.partial(_inproj_ab_kernel, tl=tl),
        grid=grid,
        in_specs=[tok(d), _resident((1, d)), per_seq, per_seq, _resident((d, N_AB_COLS)),
                  pl.BlockSpec((tl, 128), lambda b, l: (l, 0)),
                  pl.BlockSpec((tl, 128), lambda b, l: (l, 0))],
        out_specs=[tok(N_AB_COLS), cache, cache],
        out_shape=[jax.ShapeDtypeStruct((nseq, seq, N_AB_COLS), BF16),
                   jax.ShapeDtypeStruct((nseq, seq * H_B, 128), F32),
                   jax.ShapeDtypeStruct((nseq, seq * H_B, 128), F32)],
        compiler_params=_params(("parallel", "parallel")),
        name="inproj_ab",
    )(x, g, sc, sh, w, cos, sin)


def _retention_kernel(q_ref, k_ref, v_ref, g_ref, s0_ref, dec_ref, qd_ref, kd_ref, gc_ref, gn_ref,
                      o_ref, sn_ref, st_ref):
    @pl.when(pl.program_id(1) == 0)
    def _():
        st_ref[...] = s0_ref[0]

    for h in range(H_A):
        cs = slice(h * 128, (h + 1) * 128)
        q = q_ref[0, :, cs]
        k = k_ref[0, :, cs]
        v = v_ref[0, :, cs]
        state = st_ref[h]
        s = _mm_nt(q, k) * dec_ref[h]
        o = _mm(s.astype(BF16), v) + _mm((q.astype(F32) * qd_ref[h]).astype(BF16), state.astype(BF16))
        new_state = gc_ref[h] * state + _mm_tn((k.astype(F32) * kd_ref[h]).astype(BF16), v)
        st_ref[h] = new_state
        sn_ref[0, h] = new_state
        mu = jnp.mean(o, axis=-1, keepdims=True)
        dlt = o - mu
        var = jnp.mean(dlt * dlt, axis=-1, keepdims=True)
        y = dlt * lax.rsqrt(var + EPS) * gn_ref[:, cs] * _silu(g_ref[0, :, cs].astype(F32))
        o_ref[0, :, cs] = y.astype(BF16)


def _retention_consts(c):
    lg = np.log1p(-np.exp2(-5.0 - np.arange(H_A, dtype=np.float32))).astype(np.float32)
    idx = np.arange(c, dtype=np.float32)
    diff = idx[:, None] - idx[None, :]
    decay = np.where(diff[None] >= 0, np.exp(np.maximum(diff, 0.0)[None] * lg[:, None, None]), 0.0)
    qd = np.exp((idx + 1.0)[None, :] * lg[:, None])
    kd = np.exp((c - 1.0 - idx)[None, :] * lg[:, None])
    gc = np.exp(c * lg)
    bc = lambda a: np.ascontiguousarray(np.broadcast_to(a[..., None], a.shape + (128,))).astype(np.float32)
    return decay.astype(np.float32), bc(qd), bc(kd), bc(gc[:, None])


def _retention(qkv, s0, ret_gn, c):
    nseq, seq, _ = qkv.shape
    decay, qd, kd, gc = _retention_consts(c)
    w = H_A * 128
    col = lambda idx: pl.BlockSpec((1, c, w), lambda b, l: (b, l, idx))
    const = lambda shape: pl.BlockSpec(shape, lambda b, l: (0,) * len(shape))
    state = pl.BlockSpec((1, H_A, DK_A, DV_A), lambda b, l: (b, 0, 0, 0))
    return pl.pallas_call(
        _retention_kernel,
        grid=(nseq, seq // c),
        in_specs=[col(0), col(1), col(2), col(3), state, const((H_A, c, c)), const((H_A, c, 128)),
                  const((H_A, c, 128)), const((H_A, 1, 128)), const((1, w))],
        out_specs=[col(0), state],
        out_shape=[jax.ShapeDtypeStruct((nseq, seq, w), BF16),
                   jax.ShapeDtypeStruct((nseq, H_A, DK_A, DV_A), F32)],
        scratch_shapes=[pltpu.VMEM((H_A, DK_A, DV_A), F32)],
        compiler_params=_params(("parallel", "arbitrary")),
        name="retention",
    )(qkv, qkv, qkv, qkv, s0, decay, qd, kd, gc, ret_gn.reshape(1, w))


def _split_halves(q):
    lane = lax.broadcasted_iota(jnp.int32, q.shape, 1)
    zero = jnp.zeros_like(q)
    return jnp.where(lane < 64, q, zero), jnp.where(lane >= 64, q, zero)


def _lambda(lq_ref, lk_ref, lam_init):
    e = jnp.exp(jnp.sum(lq_ref[...] * lk_ref[...], axis=1, keepdims=True))
    return e[0:1, :] - e[1:2, :] + lam_init


def _diff_epilogue(o, gn, lam_init):
    ms = jnp.mean(o * o, axis=-1, keepdims=True)
    return (o * lax.rsqrt(ms + EPS)) * gn * (1.0 - lam_init)


def _diff_attn_kernel(q_ref, k_ref, v_ref, b_ref, lq_ref, lk_ref, gn_ref, o_ref, s_ref, qs_ref, mx_ref, ls_ref,
                      acc_ref, *, tq, nd, lam_init):
    row = lax.broadcasted_iota(jnp.int32, (tq, tq), 0) // CHUNK
    col = lax.broadcasted_iota(jnp.int32, (tq, tq), 1) // CHUNK
    chunk_mask = col <= row
    lam = _lambda(lq_ref, lk_ref, lam_init)

    def prepare(t):
        q = q_ref[0, t * tq:(t + 1) * tq, :]
        halves = _split_halves((q.astype(F32) * (DH_B ** -0.5)).astype(BF16))
        for i in range(2):
            r = 2 * (t % 2) + i
            qs_ref[r] = halves[i]
            mx_ref[r] = jnp.full((tq, 128), NEG, F32)
            ls_ref[r] = jnp.zeros((tq, 128), F32)
            acc_ref[r] = jnp.zeros((tq, DV_B), F32)

    def scores(t, kb):
        k = k_ref[0, kb * tq:(kb + 1) * tq, :]
        b = b_ref[0, nd - 1 - t + kb]
        for i in range(2):
            r = 2 * (t % 2) + i
            s = _mm_nt(qs_ref[r], k) + b
            if kb == t:
                s = jnp.where(chunk_mask, s, NEG)
            s_ref[r * nd + kb] = s
            m = mx_ref[r]
            for c0 in range(0, tq, 128):
                m = jnp.maximum(m, s[:, c0:c0 + 128])
            mx_ref[r] = m

    def row_max(t):
        for i in range(2):
            r = 2 * (t % 2) + i
            mx_ref[r] = jnp.broadcast_to(jnp.max(mx_ref[r], axis=1, keepdims=True), (tq, 128))

    def weigh(t, kb):
        v = v_ref[0, kb * tq:(kb + 1) * tq, :]
        for i in range(2):
            r = 2 * (t % 2) + i
            m = mx_ref[r]
            l = ls_ref[r]
            ps = []
            for c0 in range(0, tq, 128):
                p = jnp.exp(s_ref[r * nd + kb, :, c0:c0 + 128] - m)
                l = l + p
                ps.append(p.astype(BF16))
            ls_ref[r] = l
            acc_ref[r] += _mm(jnp.concatenate(ps, axis=1), v)

    def finish(t):
        r = 2 * (t % 2)
        l0 = jnp.sum(ls_ref[r], axis=1, keepdims=True)
        l1 = jnp.sum(ls_ref[r + 1], axis=1, keepdims=True)
        o = acc_ref[r] / l0 - lam * (acc_ref[r + 1] / l1)
        o_ref[0, t * tq:(t + 1) * tq, :] = _diff_epilogue(o, gn_ref[...], lam_init).astype(BF16)

    last = nd - 1
    prepare(last)
    for kb in range(last + 1):
        scores(last, kb)
    row_max(last)
    for t in range(last - 1, -1, -1):
        prepare(t)
        for kb in range(t + 2):
            if kb <= t:
                scores(t, kb)
            weigh(t + 1, kb)
        finish(t + 1)
        row_max(t)
    weigh(0, 0)
    finish(0)


def _diff_attn(qkv, bias, lam_q, lam_k, diff_gn, lam_init):
    nseq, seq, _ = qkv.shape
    tq = DIFF_TQ
    nd = seq // tq
    full = lambda off: pl.BlockSpec((1, seq, 128), lambda b, h: (b, 0, off + h))
    small = lambda r, c: pl.BlockSpec((r, c), lambda b, h: (0, 0))
    return pl.pallas_call(
        functools.partial(_diff_attn_kernel, tq=tq, nd=nd, lam_init=lam_init),
        grid=(nseq, H_B),
        in_specs=[full(16), full(20), full(24),
                  pl.BlockSpec((1, nd, tq, tq), lambda b, h: (h, 0, 0, 0)),
                  small(2, DH_B), small(2, DH_B), small(1, DV_B)],
        out_specs=full(0),
        out_shape=jax.ShapeDtypeStruct((nseq, seq, H_B * DV_B), BF16),
        scratch_shapes=[pltpu.VMEM((4 * nd, tq, tq), F32), pltpu.VMEM((4, tq, 128), BF16),
                        pltpu.VMEM((4, tq, 128), F32), pltpu.VMEM((4, tq, 128), F32),
                        pltpu.VMEM((4, tq, DV_B), F32)],
        compiler_params=_params(("parallel", "parallel")),
        name="diff_attn",
    )(qkv, qkv, qkv, bias, lam_q, lam_k, diff_gn.reshape(1, DV_B))


def _diff_attn_sample_kernel(q_ref, kc_ref, vc_ref, kp_ref, vp_ref, b_ref, lq_ref, lk_ref, gn_ref, o_ref,
                             *, past, lam_init):
    scale = DH_B ** -0.5
    lam = _lambda(lq_ref, lk_ref, lam_init)
    for h in range(H_B):
        cs = slice(h * 128, (h + 1) * 128)
        qs = _split_halves(q_ref[0, :, cs])
        kp = kp_ref[0, :, cs].astype(BF16)
        vp = vp_ref[0, :, cs].astype(BF16)
        kc = kc_ref[0, :, cs]
        vc = vc_ref[0, :, cs]
        b = b_ref[h]
        probs = []
        for i in range(2):
            sp = _mm_nt(qs[i], kp) * scale + b[:, :past]
            sc = _mm_nt(qs[i], kc) * scale + b[:, past:]
            m = jnp.maximum(jnp.max(sp, axis=1, keepdims=True), jnp.max(sc, axis=1, keepdims=True))
            pp = jnp.exp(sp - m)
            pc = jnp.exp(sc - m)
            l = jnp.sum(pp, axis=1, keepdims=True) + jnp.sum(pc, axis=1, keepdims=True)
            probs.append((pp / l, pc / l))
        ap = probs[0][0] - lam * probs[1][0]
        ac = probs[0][1] - lam * probs[1][1]
        o = _mm(ap.astype(BF16), vp) + _mm(ac.astype(BF16), vc)
        o_ref[0, :, cs] = _diff_epilogue(o, gn_ref[...], lam_init).astype(BF16)


def _diff_attn_sample(qkv, k_past, v_past, bias, lam_q, lam_k, diff_gn, lam_init):
    nseq, lq, _ = qkv.shape
    past = k_past.shape[1]
    w = H_B * DV_B
    cur = lambda idx: pl.BlockSpec((1, lq, w), lambda b: (b, 0, idx))
    old = pl.BlockSpec((1, past, w), lambda b: (b, 0, 0))
    small = lambda r, c: pl.BlockSpec((r, c), lambda b: (0, 0))
    return pl.pallas_call(
        functools.partial(_diff_attn_sample_kernel, past=past, lam_init=lam_init),
        grid=(nseq,),
        in_specs=[cur(4), cur(5), cur(6), old, old, _resident(bias.shape),
                  small(2, DH_B), small(2, DH_B), small(1, DV_B)],
        out_specs=cur(0),
        out_shape=jax.ShapeDtypeStruct((nseq, lq, w), BF16),
        compiler_params=_params(("parallel",)),
        name="diff_attn_sample",
    )(qkv, qkv, qkv, k_past, v_past, bias, lam_q, lam_k, diff_gn.reshape(1, DV_B))


def _inproj_cd_kernel(x_ref, g_ref, sc_ref, sh_ref, w_ref, main_ref, dt_ref, kc_ref, vc_ref, tail_ref, *, tl):
    h = _norm_mod(x_ref[0], g_ref[...], sc_ref[0], sh_ref[0]).astype(BF16)
    for j in range(6):
        y = _mm(h, w_ref[:, j * 512:(j + 1) * 512])
        main_ref[0, :, j * 512:(j + 1) * 512] = y.astype(BF16)
        if j == 1:
            kc_ref[0] = y
        if j == 2:
            vc_ref[0] = y
        if j >= 4:
            tail_ref[0, :, (j - 4) * 512:(j - 3) * 512] = y[tl - 8:tl, :]
    dt_ref[0] = _mm(h, w_ref[:, N_CD_MAIN:N_CD_PAD])


def _inproj_cd(x, g, sc, sh, w, tl, keep):
    nseq, seq, d = x.shape
    assert keep % tl == 0
    skip = (seq - keep) // tl
    tok = lambda n: pl.BlockSpec((1, tl, n), lambda b, l: (b, l, 0))
    kept = pl.BlockSpec((1, tl, 512), lambda b, l: (b, jnp.maximum(l - skip, 0), 0))
    per_seq = pl.BlockSpec((1, 1, d), lambda b, l: (b, 0, 0))
    return pl.pallas_call(
        functools.partial(_inproj_cd_kernel, tl=tl),
        grid=(nseq, seq // tl),
        in_specs=[tok(d), _resident((1, d)), per_seq, per_seq, _resident((d, N_CD_PAD))],
        out_specs=[tok(N_CD_MAIN), tok(128), kept, kept,
                   pl.BlockSpec((1, 8, CONV_DIM_D), lambda b, l: (b, 0, 0))],
        out_shape=[jax.ShapeDtypeStruct((nseq, seq, N_CD_MAIN), BF16),
                   jax.ShapeDtypeStruct((nseq, seq, 128), F32),
                   jax.ShapeDtypeStruct((nseq, keep, 512), F32),
                   jax.ShapeDtypeStruct((nseq, keep, 512), F32),
                   jax.ShapeDtypeStruct((nseq, 8, CONV_DIM_D), F32)],
        compiler_params=_params(("parallel", "arbitrary")),
        name="inproj_cd",
    )(x, g, sc, sh, w)


def _band_kernel(q_ref, k_ref, v_ref, b_ref, o_ref, *, tq):
    t = pl.program_id(1)
    lane = lax.broadcasted_iota(jnp.int32, (tq, 128), 1)

    def attend(near_start):
        starts, offs = [], []
        for d in range(3):
            kt = t - 2 + d
            starts.append(pl.multiple_of(jnp.maximum(kt, 0) * tq, tq))
            offs.append(jnp.where(kt >= 0, 0.0, NEG))
        for pr in range(H_C // 2):
            cs = slice(pr * 128, (pr + 1) * 128)
            halves = _split_halves((q_ref[0, :, cs].astype(F32) * (DH_C ** -0.5)).astype(BF16))
            ks = [k_ref[0, pl.ds(starts[d], tq), cs] for d in range(3)]
            vs = [v_ref[0, pl.ds(starts[d], tq), cs] for d in range(3)]
            outs = []
            for hi, qh in enumerate(halves):
                ss = []
                for d in range(3):
                    s = _mm_nt(qh, ks[d]) + b_ref[2 * pr + hi, :, d * tq:(d + 1) * tq]
                    ss.append(s + offs[d] if near_start and d < 2 else s)
                mm = ss[2][:, 0:128]
                for d in range(3):
                    for c0 in range(0, tq, 128):
                        mm = jnp.maximum(mm, ss[d][:, c0:c0 + 128])
                m = jnp.max(mm, axis=1, keepdims=True)
                ls = jnp.zeros((tq, 128), F32)
                o = jnp.zeros((tq, 128), F32)
                for d in range(3):
                    p = jnp.exp(ss[d] - m)
                    for c0 in range(0, tq, 128):
                        ls = ls + p[:, c0:c0 + 128]
                    o = o + _mm(p.astype(BF16), vs[d])
                outs.append(o / jnp.sum(ls, axis=1, keepdims=True))
            o_ref[0, :, cs] = jnp.where(lane < 64, outs[0], outs[1]).astype(BF16)

    @pl.when(t < 2)
    def _():
        attend(True)

    @pl.when(t >= 2)
    def _():
        attend(False)


def _band_attn(main, bias):
    nseq, seq, _ = main.shape
    tq = BAND_TQ
    w = H_C * DH_C
    full = lambda idx: pl.BlockSpec((1, seq, w), lambda b, t: (b, 0, idx))
    return pl.pallas_call(
        functools.partial(_band_kernel, tq=tq),
        grid=(nseq, seq // tq),
        in_specs=[pl.BlockSpec((1, tq, w), lambda b, t: (b, t, 0)), full(1), full(2), _resident(bias.shape)],
        out_specs=pl.BlockSpec((1, tq, w), lambda b, t: (b, t, 0)),
        out_shape=jax.ShapeDtypeStruct((nseq, seq, w), BF16),
        compiler_params=_params(("parallel", "parallel")),
        name="band_attn",
    )(main, main, main, bias)


def _band_sample_kernel(q_ref, kc_ref, vc_ref, kp_ref, vp_ref, b_ref, o_ref, *, past):
    scale = DH_C ** -0.5
    for pr in range(H_C // 2):
        cs = slice(pr * 128, (pr + 1) * 128)
        halves = _split_halves(q_ref[0, :, cs])
        kp = kp_ref[0, :, cs].astype(BF16)
        vp = vp_ref[0, :, cs].astype(BF16)
        kc = kc_ref[0, :, cs]
        vc = vc_ref[0, :, cs]
        outs = []
        for hi, qh in enumerate(halves):
            b = b_ref[2 * pr + hi]
            sp = _mm_nt(qh, kp) * scale + b[:, :past]
            sc = _mm_nt(qh, kc) * scale + b[:, past:]
            m = jnp.maximum(jnp.max(sp, axis=1, keepdims=True), jnp.max(sc, axis=1, keepdims=True))
            pp = jnp.exp(sp - m)
            pc = jnp.exp(sc - m)
            l = jnp.sum(pp, axis=1, keepdims=True) + jnp.sum(pc, axis=1, keepdims=True)
            outs.append(_mm((pp / l).astype(BF16), vp) + _mm((pc / l).astype(BF16), vc))
        lane = lax.broadcasted_iota(jnp.int32, outs[0].shape, 1)
        o_ref[0, :, cs] = jnp.where(lane < 64, outs[0], outs[1]).astype(BF16)


def _band_attn_sample(main, k_past, v_past, bias):
    nseq, lq, _ = main.shape
    past = k_past.shape[1]
    w = H_C * DH_C
    cur = lambda idx: pl.BlockSpec((1, lq, w), lambda b: (b, 0, idx))
    old = pl.BlockSpec((1, past, w), lambda b: (b, 0, 0))
    return pl.pallas_call(
        functools.partial(_band_sample_kernel, past=past),
        grid=(nseq,),
        in_specs=[cur(0), cur(1), cur(2), old, old, _resident(bias.shape)],
        out_specs=cur(0),
        out_shape=jax.ShapeDtypeStruct((nseq, lq, w), BF16),
        compiler_params=_params(("parallel",)),
        name="band_attn_sample",
    )(main, main, main, k_past, v_past, bias)


def _ssd_kernel(z_ref, xs_ref, bm_ref, cm_ref, dt_ref, cw_ref, cb_ref, dtb_ref, alog_ref, dskip_ref, ng_ref,
                past_ref, s0_ref, o_ref, sn_ref, buf_ref, st_ref, *, c):
    @pl.when(pl.program_id(1) == 0)
    def _():
        buf_ref[0:8, :] = past_ref[0]
        st_ref[...] = s0_ref[0]

    buf_ref[8:8 + c, 0:D_INNER] = xs_ref[0].astype(F32)
    buf_ref[8:8 + c, D_INNER:D_INNER + 256] = bm_ref[0].astype(F32)
    buf_ref[8:8 + c, D_INNER + 256:CONV_DIM_D] = cm_ref[0].astype(F32)
    conv = cb_ref[...] + buf_ref[pl.ds(5, c), :] * cw_ref[0:1, :]
    for i in range(1, CONV_D):
        conv = conv + buf_ref[pl.ds(5 + i, c), :] * cw_ref[i:i + 1, :]
    buf_ref[0:8, :] = buf_ref[c:c + 8, :]
    act = _silu(conv)
    xs = act[:, 0:D_INNER]
    bm = act[:, D_INNER:D_INNER + 256].astype(BF16)
    cm = act[:, D_INNER + 256:CONV_DIM_D].astype(BF16)

    dt = _softplus(dt_ref[0] + dtb_ref[...])
    da = dt * (-jnp.exp(alog_ref[...]))
    row = lax.broadcasted_iota(jnp.int32, (c, c), 0)
    col = lax.broadcasted_iota(jnp.int32, (c, c), 1)
    tri = row >= col
    cum = _mm_exact(tri.astype(F32), da)
    eye = (lax.broadcasted_iota(jnp.int32, (8, 128), 0) == lax.broadcasted_iota(jnp.int32, (8, 128), 1)).astype(F32)
    cum_t = _mm_exact(_mm_nt_exact(eye, da), (row <= col).astype(F32))
    dt_t = _mm_nt_exact(eye, dt)
    ecum = jnp.exp(cum)
    last = cum[c - 1:c, :]
    wgt = jnp.exp(last - cum) * dt
    elast = jnp.exp(last)
    lane = lax.broadcasted_iota(jnp.int32, (c, 128), 1)
    low = lane < 64
    rlow = lax.broadcasted_iota(jnp.int32, (128, 128), 0) < 64

    for g in range(G_D):
        bm_g = bm[:, g * 128:(g + 1) * 128]
        cm_g = cm[:, g * 128:(g + 1) * 128]
        cb = _mm_nt(cm_g, bm_g)
        ys = []
        for pp in range(2):
            p = 2 * g + pp
            h0, h1 = 2 * p, 2 * p + 1
            x_f = xs[:, p * 128:(p + 1) * 128]
            x_b = x_f.astype(BF16)
            y_in = []
            for hh in (h0, h1):
                seg = cum[:, hh:hh + 1] - cum_t[hh:hh + 1, :]
                dec = jnp.exp(jnp.where(tri, seg, NEG))
                y_in.append(_mm((cb * dec * dt_t[hh:hh + 1, :]).astype(BF16), x_b))
            st = st_ref[p * 128:(p + 1) * 128, :]
            y_x = _mm_nt(cm_g, st.astype(BF16)) * jnp.where(low, ecum[:, h0:h0 + 1], ecum[:, h1:h1 + 1])
            w2 = jnp.where(low, wgt[:, h0:h0 + 1], wgt[:, h1:h1 + 1])
            new = jnp.where(rlow, elast[:, h0:h0 + 1], elast[:, h1:h1 + 1]) * st + _mm_tn((x_f * w2).astype(BF16), bm_g)
            st_ref[p * 128:(p + 1) * 128, :] = new
            sn_ref[0, p * 128:(p + 1) * 128, :] = new
            y = jnp.where(low, y_in[0], y_in[1]) + y_x + jnp.where(low, dskip_ref[h0], dskip_ref[h1]) * x_f
            ys.append(y * _silu(z_ref[0, :, p * 128:(p + 1) * 128].astype(F32)))
        ms = (jnp.sum(ys[0] * ys[0], axis=-1, keepdims=True) + jnp.sum(ys[1] * ys[1], axis=-1, keepdims=True)) / 256.0
        inv = lax.rsqrt(ms + EPS)
        for pp in range(2):
            p = 2 * g + pp
            o_ref[0, :, p * 128:(p + 1) * 128] = (ys[pp] * inv * ng_ref[:, p * 128:(p + 1) * 128]).astype(BF16)


def _ssd(main, dt, conv_w, conv_b, dt_bias, a_log, d_skip, norm_g, conv_past, s0, c):
    nseq, seq, _ = main.shape
    pad = lambda a: jnp.pad(a.reshape(1, H_D), ((0, 0), (0, 128 - H_D)))
    blk = lambda w, idx: pl.BlockSpec((1, c, w), lambda b, l: (b, l, idx))
    const = lambda r, w: pl.BlockSpec((r, w), lambda b, l: (0, 0))
    state = pl.BlockSpec((1, H_D * P_D, N_D), lambda b, l: (b, 0, 0))
    return pl.pallas_call(
        functools.partial(_ssd_kernel, c=c),
        grid=(nseq, seq // c),
        in_specs=[blk(512, 3), blk(512, 4), blk(256, 10), blk(256, 11), blk(128, 0),
                  const(CONV_D, CONV_DIM_D), const(1, CONV_DIM_D), const(1, 128), const(1, 128),
                  pl.BlockSpec(memory_space=pltpu.SMEM), const(1, D_INNER),
                  pl.BlockSpec((1, 8, CONV_DIM_D), lambda b, l: (b, 0, 0)), state],
        out_specs=[blk(512, 0), state],
        out_shape=[jax.ShapeDtypeStruct((nseq, seq, D_INNER), BF16),
                   jax.ShapeDtypeStruct((nseq, H_D * P_D, N_D), F32)],
        scratch_shapes=[pltpu.VMEM((c + 8, CONV_DIM_D), F32), pltpu.VMEM((H_D * P_D, N_D), F32)],
        compiler_params=_params(("parallel", "arbitrary")),
        name="ssd",
    )(main, main, main, main, dt, conv_w, conv_b.reshape(1, CONV_DIM_D), pad(dt_bias), pad(a_log), d_skip,
      norm_g.reshape(1, D_INNER), conv_past, s0)


def _ffn_kernel(x_ref, o1_ref, o2_ref, wo_ref, g1_ref, ng_ref, sc_ref, sh_ref, g2_ref, wup_ref, cw_ref, cb_ref,
                wdn_ref, past_ref, fg_ref, out_ref, tail_ref, act_ref, wb_ref, gt_ref, *, tl, d_ff, final):
    @pl.when(pl.program_id(1) == 0)
    def _():
        gt_ref[...] = past_ref[0]

    half = wo_ref.shape[0] // 2
    mix = _mm(o1_ref[0], wo_ref[0:half, :]) + _mm(o2_ref[0], wo_ref[half:2 * half, :])
    x1 = x_ref[0] + g1_ref[0] * mix
    out_ref[0] = x1
    h = _norm_mod(x1, ng_ref[...], sc_ref[0], sh_ref[0]).astype(BF16)
    for j in range(d_ff // FF_CHUNK):
        c0 = j * FF_CHUNK
        a = _mm(h, wup_ref[:, c0:c0 + FF_CHUNK])
        g = _mm(h, wup_ref[:, d_ff + c0:d_ff + c0 + FF_CHUNK])
        wb_ref[0:8, :] = gt_ref[:, c0:c0 + FF_CHUNK]
        wb_ref[8:8 + tl, :] = g
        gt_ref[:, c0:c0 + FF_CHUNK] = g[tl - 8:tl, :]
        gc = (cb_ref[:, c0:c0 + FF_CHUNK] + wb_ref[pl.ds(6, tl), :] * cw_ref[0:1, c0:c0 + FF_CHUNK]
              + wb_ref[pl.ds(7, tl), :] * cw_ref[1:2, c0:c0 + FF_CHUNK] + g * cw_ref[2:3, c0:c0 + FF_CHUNK])
        act_ref[:, c0:c0 + FF_CHUNK] = (a * _gelu_tanh(gc)).astype(BF16)
    split = (d_ff // FF_CHUNK + 1) // 2 * FF_CHUNK
    dn = _mm(act_ref[:, 0:split], wdn_ref[0:split, :]) + _mm(act_ref[:, split:d_ff], wdn_ref[split:d_ff, :])
    x2 = out_ref[0] + g2_ref[0] * dn
    if final:
        ms = jnp.mean(x2 * x2, axis=-1, keepdims=True)
        x2 = (x2 * lax.rsqrt(ms + EPS)) * fg_ref[...]
    out_ref[0] = x2
    tail_ref[0] = gt_ref[...]


def _outproj_ffn(x, o1, o2, wo, g1, ng, sc, sh, g2, wup, cw, cb, wdn, past, fg, tl, final):
    nseq, seq, d = x.shape
    d_ff = wdn.shape[0]
    tok = lambda n: pl.BlockSpec((1, tl, n), lambda b, l: (b, l, 0))
    per_seq = pl.BlockSpec((1, 1, d), lambda b, l: (b, 0, 0))
    tail = pl.BlockSpec((1, 8, d_ff), lambda b, l: (b, 0, 0))
    return pl.pallas_call(
        functools.partial(_ffn_kernel, tl=tl, d_ff=d_ff, final=final),
        grid=(nseq, seq // tl),
        in_specs=[tok(d), tok(o1.shape[-1]), tok(o2.shape[-1]), _resident(wo.shape), per_seq, _resident((1, d)),
                  per_seq, per_seq, per_seq, _resident(wup.shape), _resident(cw.shape), _resident((1, d_ff)),
                  _resident(wdn.shape), tail, _resident((1, d))],
        out_specs=[tok(d), tail],
        out_shape=[jax.ShapeDtypeStruct((nseq, seq, d), F32), jax.ShapeDtypeStruct((nseq, 8, d_ff), F32)],
        scratch_shapes=[pltpu.VMEM((tl, d_ff), BF16), pltpu.VMEM((tl + 8, FF_CHUNK), F32), pltpu.VMEM((8, d_ff), F32)],
        compiler_params=_params(("parallel", "arbitrary")),
        name="outproj_ffn",
    )(x, o1, o2, wo, g1, ng, sc, sh, g2, wup, cw, cb.reshape(1, d_ff), wdn, past, fg)


def _rope_tables(pos):
    half = DK_A // 2
    inv = jnp.power(ROPE_BASE, -jnp.arange(half, dtype=F32) / half)
    ang = pos.astype(F32)[:, None] * inv[None, :]
    cos = jnp.cos(ang)
    sin = jnp.sin(ang)
    return jnp.concatenate([cos, cos], axis=1), jnp.concatenate([-sin, sin], axis=1)


def _pad_rows(a, rows=8):
    return jnp.pad(a, ((0, 0), (rows - a.shape[1], 0), (0, 0)))


def _trunk(x, mods, pos, weights, caches, biases):
    nseq, seq, d = x.shape
    sample = caches is not None
    tl = min(seq, 512)
    c = min(seq, SCAN_CHUNK)
    depth = weights["w_up"].shape[0]
    outs = {k: [] for k in ("ret", "bk", "bv", "ck", "cv", "dconv", "dssm", "ffn")}
    cos, sin = _rope_tables(pos)
    for l in range(depth):
        i = l // 2
        sh1, sc1, g1, sh2, sc2, g2 = mods[l]
        ng1 = weights["norm_g"][l, 0].reshape(1, d)
        ng2 = weights["norm_g"][l, 1].reshape(1, d)
        if l % 2 == 0:
            qkv, kb, vb = _inproj_ab(x, ng1, sc1, sh1, weights["w_in_ab"][i], cos, sin, tl)
            s0 = caches["ret"][i] if sample else jnp.zeros((nseq, H_A, DK_A, DV_A), F32)
            o1, s_new = _retention(qkv, s0, weights["ret_gn"][i], c)
            lam_init = 0.8 - 0.6 * math.exp(-0.3 * l)
            if sample:
                o2 = _diff_attn_sample(qkv, caches["bk"][i], caches["bv"][i], biases["t5"],
                                       weights["lam_q"][i], weights["lam_k"][i], weights["diff_gn"][i], lam_init)
            else:
                o2 = _diff_attn(qkv, biases["t5"], weights["lam_q"][i], weights["lam_k"][i],
                                weights["diff_gn"][i], lam_init)
            wo = weights["w_out_ab"][i]
            outs["ret"].append(s_new)
            outs["bk"].append(kb.reshape(nseq, seq, H_B, 2 * DH_B))
            outs["bv"].append(vb.reshape(nseq, seq, H_B, DV_B))
        else:
            keep = seq if sample else min(C_WINDOW, seq)
            main, dt, kc, vc, tail = _inproj_cd(x, ng1, sc1, sh1, weights["w_in_cd"][i], tl, keep)
            if sample:
                o1 = _band_attn_sample(main, caches["ck"][i], caches["cv"][i], biases["band"][i])
                conv_past = _pad_rows(caches["dconv"][i])
                s0 = caches["dssm"][i].reshape(nseq, H_D * P_D, N_D)
            else:
                o1 = _band_attn(main, biases["band"][i])
                conv_past = jnp.zeros((nseq, 8, CONV_DIM_D), F32)
                s0 = jnp.zeros((nseq, H_D * P_D, N_D), F32)
            o2, ssm_new = _ssd(main, dt, weights["d_conv_w"][i], weights["d_conv_b"][i], weights["d_dt_bias"][i],
                               weights["d_a_log"][i], weights["d_skip"][i], weights["d_norm_g"][i], conv_past, s0, c)
            wo = weights["w_out_cd"][i]
            outs["ck"].append(kc.reshape(nseq, keep, H_C, DH_C))
            outs["cv"].append(vc.reshape(nseq, keep, H_C, DH_C))
            outs["dconv"].append(tail[:, 8 - (CONV_D - 1):])
            outs["dssm"].append(ssm_new.reshape(nseq, H_D, P_D, N_D))
        d_ff = weights["w_down"].shape[1]
        ffn_past = _pad_rows(caches["ffn"][l]) if sample else jnp.zeros((nseq, 8, d_ff), F32)
        x, ftail = _outproj_ffn(x, o1, o2, wo, g1, ng2, sc2, sh2, g2, weights["w_up"][l], weights["ffn_conv_w"][l],
                                weights["ffn_conv_b"][l], weights["w_down"][l], ffn_past,
                                weights["final_g"].reshape(1, d), tl, final=(l == depth - 1))
        outs["ffn"].append(ftail[:, 8 - (CONV_F - 1):])
    stk = lambda t: jnp.stack(t).astype(F32)
    return (x,) + tuple(stk(outs[k]) for k in ("ret", "bk", "bv", "ck", "cv", "dconv", "dssm", "ffn"))


def kernel(x_prompt, x_sample, cache_ret_state, cache_b_k, cache_b_v, cache_c_k, cache_c_v, state_d_conv, state_d_ssm, state_ffn_conv, c_prompt, c_sample, w_mod, b_mod, norm_g, final_g, t5_table, w_in_ab, w_out_ab, ret_gn, lam_q, lam_k, diff_gn, w_in_cd, w_out_cd, rel_table, d_conv_w, d_conv_b, d_dt_bias, d_a_log, d_skip, d_norm_g, w_up, ffn_conv_w, ffn_conv_b, w_down):
    batch, seq, d = x_prompt.shape
    dec_batch, dec_seq, _ = x_sample.shape
    past = cache_b_k.shape[2]
    depth = w_mod.shape[0]
    assert dec_seq <= CHUNK and past % CHUNK == 0 and cache_c_k.shape[2] == C_WINDOW
    assert seq % BAND_TQ == 0 and seq % DIFF_TQ == 0

    w_in_cd_p = jnp.pad(w_in_cd, ((0, 0), (0, 0), (0, N_CD_PAD - w_in_cd.shape[-1]))).astype(BF16)
    weights = dict(
        norm_g=norm_g, final_g=final_g, w_in_ab=w_in_ab.astype(BF16), w_out_ab=w_out_ab.astype(BF16),
        ret_gn=ret_gn, lam_q=lam_q, lam_k=lam_k, diff_gn=diff_gn, w_in_cd=w_in_cd_p,
        w_out_cd=w_out_cd.astype(BF16), d_conv_w=d_conv_w, d_conv_b=d_conv_b, d_dt_bias=d_dt_bias, d_a_log=d_a_log,
        d_skip=d_skip, d_norm_g=d_norm_g, w_up=w_up.astype(BF16), ffn_conv_w=ffn_conv_w, ffn_conv_b=ffn_conv_b,
        w_down=w_down.astype(BF16))

    mod = _modulation(jnp.concatenate([c_prompt, c_sample], axis=0), w_mod, b_mod)

    def pieces(rows):
        return [[m[:, None, :] for m in jnp.split(mod[l, rows], 6, axis=-1)] for l in range(depth)]

    band = [_band_bias(rel_table[i]) for i in range(rel_table.shape[0])]
    biases_p = dict(t5=_t5_bias_prompt(t5_table, seq), band=band)
    biases_s = dict(t5=_t5_bias_sample(t5_table, past, dec_seq),
                    band=[b[:, :dec_seq, :C_WINDOW + dec_seq] for b in band])
    caches = dict(
        ret=cache_ret_state,
        bk=cache_b_k.reshape(cache_b_k.shape[0], dec_batch, past, H_B * 2 * DH_B),
        bv=cache_b_v.reshape(cache_b_v.shape[0], dec_batch, past, H_B * DV_B),
        ck=cache_c_k.reshape(cache_c_k.shape[0], dec_batch, C_WINDOW, H_C * DH_C),
        cv=cache_c_v.reshape(cache_c_v.shape[0], dec_batch, C_WINDOW, H_C * DH_C),
        dconv=state_d_conv, dssm=state_d_ssm, ffn=state_ffn_conv)

    pos_p = jnp.arange(seq, dtype=jnp.int32)
    pos_s = past + jnp.arange(dec_seq, dtype=jnp.int32)
    y_p, ret_p, bk_p, bv_p, ck_p, cv_p, dconv_p, dssm_p, ffn_p = _trunk(
        x_prompt, pieces(slice(0, batch)), pos_p, weights, None, biases_p)
    y_s, ret_s, bk_s, bv_s, ck_s, cv_s, dconv_s, dssm_s, ffn_s = _trunk(
        x_sample, pieces(slice(batch, batch + dec_batch)), pos_s, weights, caches, biases_s)
    return (y_p, y_s, ret_p, ret_s, bk_p, bk_s, bv_p, bv_s, ck_p, ck_s, cv_p, cv_s,
            dconv_p, dconv_s, dssm_p, dssm_s, ffn_p, ffn_s)
```

```python
import functools
import math

import numpy as np
import jax
import jax.numpy as jnp
from jax import lax
from jax.experimental import pallas as pl
from jax.experimental.pallas import tpu as pltpu

F32 = jnp.float32
BF16 = jnp.bfloat16

CHUNK = 64
EPS = 1e-6
NEG = -1e30
H_A, DK_A, DV_A = 4, 128, 128
ROPE_BASE = 10000.0
H_B, DH_B, DV_B = 4, 64, 128
T5_BUCKETS, T5_MAX_DIST = 32, 128
H_C, DH_C = 8, 64
BAND_CHUNKS = 8
C_WINDOW = BAND_CHUNKS * CHUNK
REL_CLIP = 128
H_D, P_D, G_D, N_D = 8, 64, 2, 128
D_INNER = H_D * P_D
CONV_D = 4
CONV_DIM_D = D_INNER + 2 * G_D * N_D
CONV_F = 3
N_AB_COLS = 7 * 512
N_CD_MAIN = 6 * 512
N_CD_PAD = N_CD_MAIN + 128
FF_CHUNK = 256
BAND_TQ = 256
DIFF_TQ = 256
SCAN_CHUNK = 256
VMEM_LIMIT = 56 * 1024 * 1024


def _mm(a, b):
    return jnp.dot(a, b, preferred_element_type=F32)


def _mm_nt(a, b):
    return lax.dot_general(a, b, (((1,), (1,)), ((), ())), preferred_element_type=F32)


def _mm_tn(a, b):
    return lax.dot_general(a, b, (((0,), (0,)), ((), ())), preferred_element_type=F32)


def _mm_exact(a, b):
    return jnp.dot(a, b, preferred_element_type=F32, precision=lax.Precision.HIGHEST)


def _mm_nt_exact(a, b):
    return lax.dot_general(a, b, (((1,), (1,)), ((), ())), preferred_element_type=F32,
                           precision=lax.Precision.HIGHEST)


def _silu(x):
    return x * (1.0 / (1.0 + jnp.exp(-x)))


def _softplus(x):
    return jnp.maximum(x, 0.0) + jnp.log1p(jnp.exp(-jnp.abs(x)))


def _gelu_tanh(x):
    return x * (0.5 * (1.0 + jnp.tanh(math.sqrt(2.0 / math.pi) * (x + 0.044715 * (x * x * x)))))


def _norm_mod(x, g, sc, sh):
    ms = jnp.mean(x * x, axis=-1, keepdims=True)
    return (x * lax.rsqrt(ms + EPS)) * g * (1.0 + sc) + sh


def _params(sem):
    return pltpu.CompilerParams(dimension_semantics=sem, vmem_limit_bytes=VMEM_LIMIT)


def _resident(shape):
    nd = len(shape)
    return pl.BlockSpec(shape, lambda *_: (0,) * nd, pipeline_mode=pl.Buffered(1))


def _mod_kernel(c_ref, w_ref, b_ref, o_ref):
    c = c_ref[...]
    o_ref[0] = _mm(_silu(c).astype(BF16), w_ref[0].astype(BF16)) + b_ref[0]


def _modulation(c_all, w_mod, b_mod):
    depth, d, n = w_mod.shape
    r = c_all.shape[0]
    tn = 1536
    return pl.pallas_call(
        _mod_kernel,
        grid=(depth, n // tn),
        in_specs=[pl.BlockSpec((r, d), lambda l, j: (0, 0)),
                  pl.BlockSpec((1, d, tn), lambda l, j: (l, 0, j)),
                  pl.BlockSpec((1, 1, tn), lambda l, j: (l, 0, j))],
        out_specs=pl.BlockSpec((1, r, tn), lambda l, j: (l, 0, j)),
        out_shape=jax.ShapeDtypeStruct((depth, r, n), F32),
        compiler_params=_params(("parallel", "parallel")),
        name="modulation",
    )(c_all, w_mod, b_mod.reshape(depth, 1, n))


def _table_gather_kernel(tab_ref, idx_ref, o_ref, *, n_entries, n_heads):
    idx = idx_ref[...]

    def body(r, accs):
        m = idx == r
        return tuple(jnp.where(m, tab_ref[hh, r], a) for hh, a in enumerate(accs))

    accs = lax.fori_loop(0, n_entries, body, tuple(jnp.zeros(idx.shape, F32) for _ in range(n_heads)))
    for hh in range(n_heads):
        o_ref[hh:hh + 1, :] = accs[hh]


def _table_gather(table, idx):
    t, h = table.shape
    w = idx.shape[0]
    return pl.pallas_call(
        functools.partial(_table_gather_kernel, n_entries=t, n_heads=h),
        in_specs=[pl.BlockSpec(memory_space=pltpu.SMEM),
                  pl.BlockSpec((1, w), lambda: (0, 0))],
        out_specs=pl.BlockSpec((h, w), lambda: (0, 0)),
        out_shape=jax.ShapeDtypeStruct((h, w), F32),
        name="table_gather",
    )(table.T, idx.reshape(1, w))


def _toeplitz(vec, rows, cols):
    h, wv = vec.shape
    flat = jnp.tile(vec, (1, rows))[:, :rows * (wv - 1)]
    return flat.reshape(h, rows, wv - 1)[:, :, :cols]


def _t5_bucket(rel):
    half = T5_BUCKETS // 2
    max_exact = half // 2
    base = jnp.where(rel > 0, half, 0)
    n = jnp.abs(rel)
    nf = jnp.maximum(n, 1).astype(F32)
    large = max_exact + (jnp.log(nf / max_exact) / math.log(T5_MAX_DIST / max_exact) * (half - max_exact)).astype(jnp.int32)
    large = jnp.minimum(large, half - 1)
    return base + jnp.where(n < max_exact, n, large)


def _wrapped_offsets(n_pos, n_neg):
    p = jnp.arange(n_pos + n_neg, dtype=jnp.int32)
    return jnp.where(p < n_pos, p, p - (n_pos + n_neg))


def _t5_bias_prompt(t5_table, seq):
    nd = seq // DIFF_TQ
    u = _wrapped_offsets(seq, DIFF_TQ)
    vec = _table_gather(t5_table, _t5_bucket(u - (seq - DIFF_TQ)))
    wide = _toeplitz(vec, DIFF_TQ, seq)
    return wide.reshape(H_B, DIFF_TQ, nd, DIFF_TQ).transpose(0, 2, 1, 3)


def _t5_bias_sample(t5_table, past, lq):
    u = _wrapped_offsets(past + lq, lq)
    vec = _table_gather(t5_table, _t5_bucket(u - past))
    return _toeplitz(vec, lq, past + lq)


def _band_bias(rel_table):
    u = _wrapped_offsets(3 * BAND_TQ, BAND_TQ)
    idx = jnp.clip(2 * BAND_TQ - u, -REL_CLIP, REL_CLIP) + REL_CLIP
    bias = _toeplitz(_table_gather(rel_table, idx), BAND_TQ, 3 * BAND_TQ)
    dc = (np.arange(BAND_TQ)[:, None] // CHUNK + C_WINDOW // CHUNK) - np.arange(3 * BAND_TQ)[None, :] // CHUNK
    return jnp.where(jnp.asarray((dc >= 0) & (dc <= BAND_CHUNKS))[None], bias, NEG)


def _inproj_ab_kernel(x_ref, g_ref, sc_ref, sh_ref, w_ref, cos_ref, sin_ref, qkv_ref, kb_ref, vb_ref, *, tl):
    h = _norm_mod(x_ref[0], g_ref[...], sc_ref[0], sh_ref[0]).astype(BF16)
    cos = cos_ref[...]
    sin = sin_ref[...]
    for j in range(7):
        y = _mm(h, w_ref[:, j * 512:(j + 1) * 512])
        if j < 2:
            for hh in range(H_A):
                yh = y[:, hh * 128:(hh + 1) * 128]
                yh = yh * cos + pltpu.roll(yh, 64, 1) * sin
                if j == 1:
                    yh = yh * (DK_A ** -0.5)
                qkv_ref[0, :, j * 512 + hh * 128:j * 512 + (hh + 1) * 128] = yh.astype(BF16)
        else:
            if j >= 5:
                dst = kb_ref if j == 5 else vb_ref
                for hh in range(H_B):
                    dst[0, pl.ds(hh, tl, stride=H_B), :] = y[:, hh * 128:(hh + 1) * 128]
            qkv_ref[0, :, j * 512:(j + 1) * 512] = y.astype(BF16)


def _inproj_ab(x, g, sc, sh, w, cos, sin, tl):
    nseq, seq, d = x.shape
    grid = (nseq, seq // tl)
    tok = lambda n: pl.BlockSpec((1, tl, n), lambda b, l: (b, l, 0))
    per_seq = pl.BlockSpec((1, 1, d), lambda b, l: (b, 0, 0))
    cache = pl.BlockSpec((1, tl * H_B, 128), lambda b, l: (b, l, 0))
    return pl.pallas_call(
        functools.partial(_inproj_ab_kernel, tl=tl),
        grid=grid,
        in_specs=[tok(d), _resident((1, d)), per_seq, per_seq, _resident((d, N_AB_COLS)),
                  pl.BlockSpec((tl, 128), lambda b, l: (l, 0)),
                  pl.BlockSpec((tl, 128), lambda b, l: (l, 0))],
        out_specs=[tok(N_AB_COLS), cache, cache],
        out_shape=[jax.ShapeDtypeStruct((nseq, seq, N_AB_COLS), BF16),
                   jax.ShapeDtypeStruct((nseq, seq * H_B, 128), F32),
                   jax.ShapeDtypeStruct((nseq, seq * H_B, 128), F32)],
        compiler_params=_params(("parallel", "parallel")),
        name="inproj_ab",
    )(x, g, sc, sh, w, cos, sin)


def _retention_kernel(q_ref, k_ref, v_ref, g_ref, s0_ref, dec_ref, qd_ref, kd_ref, gc_ref, gn_ref,
                      o_ref, sn_ref, *, c, n_chunks):
    states = [s0_ref[0, h] for h in range(H_A)]
    for ci in range(n_chunks):
        rows = slice(ci * c, (ci + 1) * c)
        for h in range(H_A):
            cs = slice(h * 128, (h + 1) * 128)
            q = q_ref[0, rows, cs]
            k = k_ref[0, rows, cs]
            v = v_ref[0, rows, cs]
            state = states[h]
            s = _mm_nt(q, k) * dec_ref[h]
            o = _mm(s.astype(BF16), v) + _mm((q.astype(F32) * qd_ref[h]).astype(BF16), state.astype(BF16))
            states[h] = gc_ref[h] * state + _mm_tn((k.astype(F32) * kd_ref[h]).astype(BF16), v)
            mu = jnp.mean(o, axis=-1, keepdims=True)
            dlt = o - mu
            var = jnp.mean(dlt * dlt, axis=-1, keepdims=True)
            y = dlt * lax.rsqrt(var + EPS) * gn_ref[:, cs] * _silu(g_ref[0, rows, cs].astype(F32))
            o_ref[0, rows, cs] = y.astype(BF16)
    for h in range(H_A):
        sn_ref[0, h] = states[h]


def _retention_consts(c):
    lg = np.log1p(-np.exp2(-5.0 - np.arange(H_A, dtype=np.float32))).astype(np.float32)
    idx = np.arange(c, dtype=np.float32)
    diff = idx[:, None] - idx[None, :]
    decay = np.where(diff[None] >= 0, np.exp(np.maximum(diff, 0.0)[None] * lg[:, None, None]), 0.0)
    qd = np.exp((idx + 1.0)[None, :] * lg[:, None])
    kd = np.exp((c - 1.0 - idx)[None, :] * lg[:, None])
    gc = np.exp(c * lg)
    bc = lambda a: np.ascontiguousarray(np.broadcast_to(a[..., None], a.shape + (128,))).astype(np.float32)
    return decay.astype(np.float32), bc(qd), bc(kd), bc(gc[:, None])


def _retention(qkv, s0, ret_gn, c):
    nseq, seq, _ = qkv.shape
    decay, qd, kd, gc = _retention_consts(c)
    w = H_A * 128
    col = lambda idx: pl.BlockSpec((1, seq, w), lambda b: (b, 0, idx))
    const = lambda shape: pl.BlockSpec(shape, lambda b: (0,) * len(shape))
    state = pl.BlockSpec((1, H_A, DK_A, DV_A), lambda b: (b, 0, 0, 0))
    return pl.pallas_call(
        functools.partial(_retention_kernel, c=c, n_chunks=seq // c),
        grid=(nseq,),
        in_specs=[col(0), col(1), col(2), col(3), state, const((H_A, c, c)), const((H_A, c, 128)),
                  const((H_A, c, 128)), const((H_A, 1, 128)), const((1, w))],
        out_specs=[col(0), state],
        out_shape=[jax.ShapeDtypeStruct((nseq, seq, w), BF16),
                   jax.ShapeDtypeStruct((nseq, H_A, DK_A, DV_A), F32)],
        compiler_params=_params(("parallel",)),
        name="retention",
    )(qkv, qkv, qkv, qkv, s0, decay, qd, kd, gc, ret_gn.reshape(1, w))


def _split_halves(q):
    lane = lax.broadcasted_iota(jnp.int32, q.shape, 1)
    zero = jnp.zeros_like(q)
    return jnp.where(lane < 64, q, zero), jnp.where(lane >= 64, q, zero)


def _lambda(lq_ref, lk_ref, lam_init):
    e = jnp.exp(jnp.sum(lq_ref[...] * lk_ref[...], axis=1, keepdims=True))
    return e[0:1, :] - e[1:2, :] + lam_init


def _diff_epilogue(o, gn, lam_init):
    ms = jnp.mean(o * o, axis=-1, keepdims=True)
    return (o * lax.rsqrt(ms + EPS)) * gn * (1.0 - lam_init)


def _diff_attn_kernel(q_ref, k_ref, v_ref, b_ref, lq_ref, lk_ref, gn_ref, o_ref, s_ref, qs_ref, mx_ref, ls_ref,
                      acc_ref, *, tq, nd, lam_init):
    row = lax.broadcasted_iota(jnp.int32, (tq, tq), 0) // CHUNK
    col = lax.broadcasted_iota(jnp.int32, (tq, tq), 1) // CHUNK
    chunk_mask = col <= row
    lam = _lambda(lq_ref, lk_ref, lam_init)

    def prepare(t):
        q = q_ref[0, t * tq:(t + 1) * tq, :]
        halves = _split_halves((q.astype(F32) * (DH_B ** -0.5)).astype(BF16))
        for i in range(2):
            r = 2 * (t % 2) + i
            qs_ref[r] = halves[i]
            mx_ref[r] = jnp.full((tq, 128), NEG, F32)
            ls_ref[r] = jnp.zeros((tq, 128), F32)
            acc_ref[r] = jnp.zeros((tq, DV_B), F32)

    def scores(t, kb):
        k = k_ref[0, kb * tq:(kb + 1) * tq, :]
        b = b_ref[0, nd - 1 - t + kb]
        for i in range(2):
            r = 2 * (t % 2) + i
            s = _mm_nt(qs_ref[r], k) + b
            if kb == t:
                s = jnp.where(chunk_mask, s, NEG)
            s_ref[r * nd + kb] = s
            m = mx_ref[r]
            for c0 in range(0, tq, 128):
                m = jnp.maximum(m, s[:, c0:c0 + 128])
            mx_ref[r] = m

    def row_max(t):
        for i in range(2):
            r = 2 * (t % 2) + i
            mx_ref[r] = jnp.broadcast_to(jnp.max(mx_ref[r], axis=1, keepdims=True), (tq, 128))

    def weigh(t, kb):
        v = v_ref[0, kb * tq:(kb + 1) * tq, :]
        for i in range(2):
            r = 2 * (t % 2) + i
            m = mx_ref[r]
            l = ls_ref[r]
            ps = []
            for c0 in range(0, tq, 128):
                p = jnp.exp(s_ref[r * nd + kb, :, c0:c0 + 128] - m)
                l = l + p
                ps.append(p.astype(BF16))
            ls_ref[r] = l
            acc_ref[r] += _mm(jnp.concatenate(ps, axis=1), v)

    def finish(t):
        r = 2 * (t % 2)
        l0 = jnp.sum(ls_ref[r], axis=1, keepdims=True)
        l1 = jnp.sum(ls_ref[r + 1], axis=1, keepdims=True)
        o = acc_ref[r] / l0 - lam * (acc_ref[r + 1] / l1)
        o_ref[0, t * tq:(t + 1) * tq, :] = _diff_epilogue(o, gn_ref[...], lam_init).astype(BF16)

    last = nd - 1
    prepare(last)
    for kb in range(last + 1):
        scores(last, kb)
    row_max(last)
    for t in range(last - 1, -1, -1):
        prepare(t)
        for kb in range(t + 2):
            if kb <= t:
                scores(t, kb)
            weigh(t + 1, kb)
        finish(t + 1)
        row_max(t)
    weigh(0, 0)
    finish(0)


def _diff_attn(qkv, bias, lam_q, lam_k, diff_gn, lam_init):
    nseq, seq, _ = qkv.shape
    tq = DIFF_TQ
    nd = seq // tq
    full = lambda off: pl.BlockSpec((1, seq, 128), lambda b, h: (b, 0, off + h))
    small = lambda r, c: pl.BlockSpec((r, c), lambda b, h: (0, 0))
    return pl.pallas_call(
        functools.partial(_diff_attn_kernel, tq=tq, nd=nd, lam_init=lam_init),
        grid=(nseq, H_B),
        in_specs=[full(16), full(20), full(24),
                  pl.BlockSpec((1, nd, tq, tq), lambda b, h: (h, 0, 0, 0)),
                  small(2, DH_B), small(2, DH_B), small(1, DV_B)],
        out_specs=full(0),
        out_shape=jax.ShapeDtypeStruct((nseq, seq, H_B * DV_B), BF16),
        scratch_shapes=[pltpu.VMEM((4 * nd, tq, tq), F32), pltpu.VMEM((4, tq, 128), BF16),
                        pltpu.VMEM((4, tq, 128), F32), pltpu.VMEM((4, tq, 128), F32),
                        pltpu.VMEM((4, tq, DV_B), F32)],
        compiler_params=_params(("parallel", "parallel")),
        name="diff_attn",
    )(qkv, qkv, qkv, bias, lam_q, lam_k, diff_gn.reshape(1, DV_B))


def _diff_attn_sample_kernel(q_ref, kc_ref, vc_ref, kp_ref, vp_ref, b_ref, lq_ref, lk_ref, gn_ref, o_ref,
                             *, past, lam_init):
    scale = DH_B ** -0.5
    lam = _lambda(lq_ref, lk_ref, lam_init)
    for h in range(H_B):
        cs = slice(h * 128, (h + 1) * 128)
        qs = _split_halves(q_ref[0, :, cs])
        kp = kp_ref[0, :, cs].astype(BF16)
        vp = vp_ref[0, :, cs].astype(BF16)
        kc = kc_ref[0, :, cs]
        vc = vc_ref[0, :, cs]
        b = b_ref[h]
        probs = []
        for i in range(2):
            sp = _mm_nt(qs[i], kp) * scale + b[:, :past]
            sc = _mm_nt(qs[i], kc) * scale + b[:, past:]
            m = jnp.maximum(jnp.max(sp, axis=1, keepdims=True), jnp.max(sc, axis=1, keepdims=True))
            pp = jnp.exp(sp - m)
            pc = jnp.exp(sc - m)
            l = jnp.sum(pp, axis=1, keepdims=True) + jnp.sum(pc, axis=1, keepdims=True)
            probs.append((pp / l, pc / l))
        ap = probs[0][0] - lam * probs[1][0]
        ac = probs[0][1] - lam * probs[1][1]
        o = _mm(ap.astype(BF16), vp) + _mm(ac.astype(BF16), vc)
        o_ref[0, :, cs] = _diff_epilogue(o, gn_ref[...], lam_init).astype(BF16)


def _diff_attn_sample(qkv, k_past, v_past, bias, lam_q, lam_k, diff_gn, lam_init):
    nseq, lq, _ = qkv.shape
    past = k_past.shape[1]
    w = H_B * DV_B
    cur = lambda idx: pl.BlockSpec((1, lq, w), lambda b: (b, 0, idx))
    old = pl.BlockSpec((1, past, w), lambda b: (b, 0, 0))
    small = lambda r, c: pl.BlockSpec((r, c), lambda b: (0, 0))
    return pl.pallas_call(
        functools.partial(_diff_attn_sample_kernel, past=past, lam_init=lam_init),
        grid=(nseq,),
        in_specs=[cur(4), cur(5), cur(6), old, old, _resident(bias.shape),
                  small(2, DH_B), small(2, DH_B), small(1, DV_B)],
        out_specs=cur(0),
        out_shape=jax.ShapeDtypeStruct((nseq, lq, w), BF16),
        compiler_params=_params(("parallel",)),
        name="diff_attn_sample",
    )(qkv, qkv, qkv, k_past, v_past, bias, lam_q, lam_k, diff_gn.reshape(1, DV_B))


def _inproj_cd_kernel(x_ref, g_ref, sc_ref, sh_ref, w_ref, main_ref, dt_ref, kc_ref, vc_ref, tail_ref, *, tl):
    h = _norm_mod(x_ref[0], g_ref[...], sc_ref[0], sh_ref[0]).astype(BF16)
    for j in range(6):
        y = _mm(h, w_ref[:, j * 512:(j + 1) * 512])
        main_ref[0, :, j * 512:(j + 1) * 512] = y.astype(BF16)
        if j == 1:
            kc_ref[0] = y
        if j == 2:
            vc_ref[0] = y
        if j >= 4:
            tail_ref[0, :, (j - 4) * 512:(j - 3) * 512] = y[tl - 8:tl, :]
    dt_ref[0] = _mm(h, w_ref[:, N_CD_MAIN:N_CD_PAD])


def _inproj_cd(x, g, sc, sh, w, tl, keep):
    nseq, seq, d = x.shape
    assert keep % tl == 0
    skip = (seq - keep) // tl
    tok = lambda n: pl.BlockSpec((1, tl, n), lambda b, l: (b, l, 0))
    kept = pl.BlockSpec((1, tl, 512), lambda b, l: (b, jnp.maximum(l - skip, 0), 0))
    per_seq = pl.BlockSpec((1, 1, d), lambda b, l: (b, 0, 0))
    return pl.pallas_call(
        functools.partial(_inproj_cd_kernel, tl=tl),
        grid=(nseq, seq // tl),
        in_specs=[tok(d), _resident((1, d)), per_seq, per_seq, _resident((d, N_CD_PAD))],
        out_specs=[tok(N_CD_MAIN), tok(128), kept, kept,
                   pl.BlockSpec((1, 8, CONV_DIM_D), lambda b, l: (b, 0, 0))],
        out_shape=[jax.ShapeDtypeStruct((nseq, seq, N_CD_MAIN), BF16),
                   jax.ShapeDtypeStruct((nseq, seq, 128), F32),
                   jax.ShapeDtypeStruct((nseq, keep, 512), F32),
                   jax.ShapeDtypeStruct((nseq, keep, 512), F32),
                   jax.ShapeDtypeStruct((nseq, 8, CONV_DIM_D), F32)],
        compiler_params=_params(("parallel", "arbitrary")),
        name="inproj_cd",
    )(x, g, sc, sh, w)


def _band_kernel(q_ref, k_ref, v_ref, b_ref, o_ref, *, tq):
    t = pl.program_id(1)
    lane = lax.broadcasted_iota(jnp.int32, (tq, 128), 1)
    n_heads = H_C

    def attend(near_start):
        starts, offs = [], []
        for d in range(3):
            kt = t - 2 + d
            starts.append(pl.multiple_of(jnp.maximum(kt, 0) * tq, tq))
            offs.append(jnp.where(kt >= 0, 0.0, NEG))

        def scores(head):
            pr, hi = divmod(head, 2)
            cs = slice(pr * 128, (pr + 1) * 128)
            qh = _split_halves((q_ref[0, :, cs].astype(F32) * (DH_C ** -0.5)).astype(BF16))[hi]
            ss = []
            for d in range(3):
                s = _mm_nt(qh, k_ref[0, pl.ds(starts[d], tq), cs]) + b_ref[head, :, d * tq:(d + 1) * tq]
                ss.append(s + offs[d] if near_start and d < 2 else s)
            mm = ss[2][:, 0:128]
            for d in range(3):
                for c0 in range(0, tq, 128):
                    mm = jnp.maximum(mm, ss[d][:, c0:c0 + 128])
            return ss, jnp.max(mm, axis=1, keepdims=True)

        def weigh(head, ss, m):
            cs = slice((head // 2) * 128, (head // 2 + 1) * 128)
            ls = jnp.zeros((tq, 128), F32)
            o = jnp.zeros((tq, 128), F32)
            for d in range(3):
                p = jnp.exp(ss[d] - m)
                for c0 in range(0, tq, 128):
                    ls = ls + p[:, c0:c0 + 128]
                o = o + _mm(p.astype(BF16), v_ref[0, pl.ds(starts[d], tq), cs])
            return o / jnp.sum(ls, axis=1, keepdims=True)

        outs = []
        pending = scores(0)
        for head in range(n_heads):
            nxt = scores(head + 1) if head + 1 < n_heads else None
            outs.append(weigh(head, *pending))
            pending = nxt
            if head % 2 == 1:
                cs = slice((head // 2) * 128, (head // 2 + 1) * 128)
                o_ref[0, :, cs] = jnp.where(lane < 64, outs[head - 1], outs[head]).astype(BF16)

    @pl.when(t < 2)
    def _():
        attend(True)

    @pl.when(t >= 2)
    def _():
        attend(False)


def _band_attn(main, bias):
    nseq, seq, _ = main.shape
    tq = BAND_TQ
    w = H_C * DH_C
    full = lambda idx: pl.BlockSpec((1, seq, w), lambda b, t: (b, 0, idx))
    return pl.pallas_call(
        functools.partial(_band_kernel, tq=tq),
        grid=(nseq, seq // tq),
        in_specs=[pl.BlockSpec((1, tq, w), lambda b, t: (b, t, 0)), full(1), full(2), _resident(bias.shape)],
        out_specs=pl.BlockSpec((1, tq, w), lambda b, t: (b, t, 0)),
        out_shape=jax.ShapeDtypeStruct((nseq, seq, w), BF16),
        compiler_params=_params(("parallel", "parallel")),
        name="band_attn",
    )(main, main, main, bias)


def _band_sample_kernel(q_ref, kc_ref, vc_ref, kp_ref, vp_ref, b_ref, o_ref, *, past):
    scale = DH_C ** -0.5
    for pr in range(H_C // 2):
        cs = slice(pr * 128, (pr + 1) * 128)
        halves = _split_halves(q_ref[0, :, cs])
        kp = kp_ref[0, :, cs].astype(BF16)
        vp = vp_ref[0, :, cs].astype(BF16)
        kc = kc_ref[0, :, cs]
        vc = vc_ref[0, :, cs]
        outs = []
        for hi, qh in enumerate(halves):
            b = b_ref[2 * pr + hi]
            sp = _mm_nt(qh, kp) * scale + b[:, :past]
            sc = _mm_nt(qh, kc) * scale + b[:, past:]
            m = jnp.maximum(jnp.max(sp, axis=1, keepdims=True), jnp.max(sc, axis=1, keepdims=True))
            pp = jnp.exp(sp - m)
            pc = jnp.exp(sc - m)
            l = jnp.sum(pp, axis=1, keepdims=True) + jnp.sum(pc, axis=1, keepdims=True)
            outs.append(_mm((pp / l).astype(BF16), vp) + _mm((pc / l).astype(BF16), vc))
        lane = lax.broadcasted_iota(jnp.int32, outs[0].shape, 1)
        o_ref[0, :, cs] = jnp.where(lane < 64, outs[0], outs[1]).astype(BF16)


def _band_attn_sample(main, k_past, v_past, bias):
    nseq, lq, _ = main.shape
    past = k_past.shape[1]
    w = H_C * DH_C
    cur = lambda idx: pl.BlockSpec((1, lq, w), lambda b: (b, 0, idx))
    old = pl.BlockSpec((1, past, w), lambda b: (b, 0, 0))
    return pl.pallas_call(
        functools.partial(_band_sample_kernel, past=past),
        grid=(nseq,),
        in_specs=[cur(0), cur(1), cur(2), old, old, _resident(bias.shape)],
        out_specs=cur(0),
        out_shape=jax.ShapeDtypeStruct((nseq, lq, w), BF16),
        compiler_params=_params(("parallel",)),
        name="band_attn_sample",
    )(main, main, main, k_past, v_past, bias)


def _ssd_kernel(z_ref, xs_ref, bm_ref, cm_ref, dt_ref, cw_ref, cb_ref, dtb_ref, alog_ref, dskip_ref, ng_ref,
                past_ref, s0_ref, o_ref, sn_ref, buf_ref, st_ref, *, c):
    @pl.when(pl.program_id(1) == 0)
    def _():
        buf_ref[0:8, :] = past_ref[0]
        st_ref[...] = s0_ref[0]

    buf_ref[8:8 + c, 0:D_INNER] = xs_ref[0].astype(F32)
    buf_ref[8:8 + c, D_INNER:D_INNER + 256] = bm_ref[0].astype(F32)
    buf_ref[8:8 + c, D_INNER + 256:CONV_DIM_D] = cm_ref[0].astype(F32)
    conv = cb_ref[...] + buf_ref[pl.ds(5, c), :] * cw_ref[0:1, :]
    for i in range(1, CONV_D):
        conv = conv + buf_ref[pl.ds(5 + i, c), :] * cw_ref[i:i + 1, :]
    buf_ref[0:8, :] = buf_ref[c:c + 8, :]
    act = _silu(conv)
    xs = act[:, 0:D_INNER]
    bm = act[:, D_INNER:D_INNER + 256].astype(BF16)
    cm = act[:, D_INNER + 256:CONV_DIM_D].astype(BF16)

    dt = _softplus(dt_ref[0] + dtb_ref[...])
    da = dt * (-jnp.exp(alog_ref[...]))
    row = lax.broadcasted_iota(jnp.int32, (c, c), 0)
    col = lax.broadcasted_iota(jnp.int32, (c, c), 1)
    tri = row >= col
    cum = _mm_exact(tri.astype(F32), da)
    eye = (lax.broadcasted_iota(jnp.int32, (8, 128), 0) == lax.broadcasted_iota(jnp.int32, (8, 128), 1)).astype(F32)
    cum_t = _mm_exact(_mm_nt_exact(eye, da), (row <= col).astype(F32))
    dt_t = _mm_nt_exact(eye, dt)
    ecum = jnp.exp(cum)
    last = cum[c - 1:c, :]
    wgt = jnp.exp(last - cum) * dt
    elast = jnp.exp(last)
    lane = lax.broadcasted_iota(jnp.int32, (c, 128), 1)
    low = lane < 64
    rlow = lax.broadcasted_iota(jnp.int32, (128, 128), 0) < 64

    for g in range(G_D):
        bm_g = bm[:, g * 128:(g + 1) * 128]
        cm_g = cm[:, g * 128:(g + 1) * 128]
        cb = _mm_nt(cm_g, bm_g)
        ys = []
        for pp in range(2):
            p = 2 * g + pp
            h0, h1 = 2 * p, 2 * p + 1
            x_f = xs[:, p * 128:(p + 1) * 128]
            x_b = x_f.astype(BF16)
            y_in = []
            for hh in (h0, h1):
                seg = cum[:, hh:hh + 1] - cum_t[hh:hh + 1, :]
                dec = jnp.exp(jnp.where(tri, seg, NEG))
                y_in.append(_mm((cb * dec * dt_t[hh:hh + 1, :]).astype(BF16), x_b))
            st = st_ref[p * 128:(p + 1) * 128, :]
            y_x = _mm_nt(cm_g, st.astype(BF16)) * jnp.where(low, ecum[:, h0:h0 + 1], ecum[:, h1:h1 + 1])
            w2 = jnp.where(low, wgt[:, h0:h0 + 1], wgt[:, h1:h1 + 1])
            new = jnp.where(rlow, elast[:, h0:h0 + 1], elast[:, h1:h1 + 1]) * st + _mm_tn((x_f * w2).astype(BF16), bm_g)
            st_ref[p * 128:(p + 1) * 128, :] = new
            sn_ref[0, p * 128:(p + 1) * 128, :] = new
            y = jnp.where(low, y_in[0], y_in[1]) + y_x + jnp.where(low, dskip_ref[h0], dskip_ref[h1]) * x_f
            ys.append(y * _silu(z_ref[0, :, p * 128:(p + 1) * 128].astype(F32)))
        ms = (jnp.sum(ys[0] * ys[0], axis=-1, keepdims=True) + jnp.sum(ys[1] * ys[1], axis=-1, keepdims=True)) / 256.0
        inv = lax.rsqrt(ms + EPS)
        for pp in range(2):
            p = 2 * g + pp
            o_ref[0, :, p * 128:(p + 1) * 128] = (ys[pp] * inv * ng_ref[:, p * 128:(p + 1) * 128]).astype(BF16)


def _ssd(main, dt, conv_w, conv_b, dt_bias, a_log, d_skip, norm_g, conv_past, s0, c):
    nseq, seq, _ = main.shape
    pad = lambda a: jnp.pad(a.reshape(1, H_D), ((0, 0), (0, 128 - H_D)))
    blk = lambda w, idx: pl.BlockSpec((1, c, w), lambda b, l: (b, l, idx))
    const = lambda r, w: pl.BlockSpec((r, w), lambda b, l: (0, 0))
    state = pl.BlockSpec((1, H_D * P_D, N_D), lambda b, l: (b, 0, 0))
    return pl.pallas_call(
        functools.partial(_ssd_kernel, c=c),
        grid=(nseq, seq // c),
        in_specs=[blk(512, 3), blk(512, 4), blk(256, 10), blk(256, 11), blk(128, 0),
                  const(CONV_D, CONV_DIM_D), const(1, CONV_DIM_D), const(1, 128), const(1, 128),
                  pl.BlockSpec(memory_space=pltpu.SMEM), const(1, D_INNER),
                  pl.BlockSpec((1, 8, CONV_DIM_D), lambda b, l: (b, 0, 0)), state],
        out_specs=[blk(512, 0), state],
        out_shape=[jax.ShapeDtypeStruct((nseq, seq, D_INNER), BF16),
                   jax.ShapeDtypeStruct((nseq, H_D * P_D, N_D), F32)],
        scratch_shapes=[pltpu.VMEM((c + 8, CONV_DIM_D), F32), pltpu.VMEM((H_D * P_D, N_D), F32)],
        compiler_params=_params(("parallel", "arbitrary")),
        name="ssd",
    )(main, main, main, main, dt, conv_w, conv_b.reshape(1, CONV_DIM_D), pad(dt_bias), pad(a_log), d_skip,
      norm_g.reshape(1, D_INNER), conv_past, s0)


def _ffn_kernel(x_ref, o1_ref, o2_ref, wo_ref, g1_ref, ng_ref, sc_ref, sh_ref, g2_ref, wup_ref, cw_ref, cb_ref,
                wdn_ref, past_ref, fg_ref, out_ref, tail_ref, act_ref, wb_ref, gt_ref, *, tl, d_ff, final):
    @pl.when(pl.program_id(1) == 0)
    def _():
        gt_ref[...] = past_ref[0]

    half = wo_ref.shape[0] // 2
    mix = _mm(o1_ref[0], wo_ref[0:half, :]) + _mm(o2_ref[0], wo_ref[half:2 * half, :])
    x1 = x_ref[0] + g1_ref[0] * mix
    out_ref[0] = x1
    h = _norm_mod(x1, ng_ref[...], sc_ref[0], sh_ref[0]).astype(BF16)
    for j in range(d_ff // FF_CHUNK):
        c0 = j * FF_CHUNK
        a = _mm(h, wup_ref[:, c0:c0 + FF_CHUNK])
        g = _mm(h, wup_ref[:, d_ff + c0:d_ff + c0 + FF_CHUNK])
        wb_ref[0:8, :] = gt_ref[:, c0:c0 + FF_CHUNK]
        wb_ref[8:8 + tl, :] = g
        gt_ref[:, c0:c0 + FF_CHUNK] = g[tl - 8:tl, :]
        gc = (cb_ref[:, c0:c0 + FF_CHUNK] + wb_ref[pl.ds(6, tl), :] * cw_ref[0:1, c0:c0 + FF_CHUNK]
              + wb_ref[pl.ds(7, tl), :] * cw_ref[1:2, c0:c0 + FF_CHUNK] + g * cw_ref[2:3, c0:c0 + FF_CHUNK])
        act_ref[:, c0:c0 + FF_CHUNK] = (a * _gelu_tanh(gc)).astype(BF16)
    split = (d_ff // FF_CHUNK + 1) // 2 * FF_CHUNK
    dn = _mm(act_ref[:, 0:split], wdn_ref[0:split, :]) + _mm(act_ref[:, split:d_ff], wdn_ref[split:d_ff, :])
    x2 = out_ref[0] + g2_ref[0] * dn
    if final:
        ms = jnp.mean(x2 * x2, axis=-1, keepdims=True)
        x2 = (x2 * lax.rsqrt(ms + EPS)) * fg_ref[...]
    out_ref[0] = x2
    tail_ref[0] = gt_ref[...]


def _outproj_ffn(x, o1, o2, wo, g1, ng, sc, sh, g2, wup, cw, cb, wdn, past, fg, tl, final):
    nseq, seq, d = x.shape
    d_ff = wdn.shape[0]
    tok = lambda n: pl.BlockSpec((1, tl, n), lambda b, l: (b, l, 0))
    per_seq = pl.BlockSpec((1, 1, d), lambda b, l: (b, 0, 0))
    tail = pl.BlockSpec((1, 8, d_ff), lambda b, l: (b, 0, 0))
    return pl.pallas_call(
        functools.partial(_ffn_kernel, tl=tl, d_ff=d_ff, final=final),
        grid=(nseq, seq // tl),
        in_specs=[tok(d), tok(o1.shape[-1]), tok(o2.shape[-1]), _resident(wo.shape), per_seq, _resident((1, d)),
                  per_seq, per_seq, per_seq, _resident(wup.shape), _resident(cw.shape), _resident((1, d_ff)),
                  _resident(wdn.shape), tail, _resident((1, d))],
        out_specs=[tok(d), tail],
        out_shape=[jax.ShapeDtypeStruct((nseq, seq, d), F32), jax.ShapeDtypeStruct((nseq, 8, d_ff), F32)],
        scratch_shapes=[pltpu.VMEM((tl, d_ff), BF16), pltpu.VMEM((tl + 8, FF_CHUNK), F32), pltpu.VMEM((8, d_ff), F32)],
        compiler_params=_params(("parallel", "arbitrary")),
        name="outproj_ffn",
    )(x, o1, o2, wo, g1, ng, sc, sh, g2, wup, cw, cb.reshape(1, d_ff), wdn, past, fg)


def _rope_tables(pos):
    half = DK_A // 2
    inv = jnp.power(ROPE_BASE, -jnp.arange(half, dtype=F32) / half)
    ang = pos.astype(F32)[:, None] * inv[None, :]
    cos = jnp.cos(ang)
    sin = jnp.sin(ang)
    return jnp.concatenate([cos, cos], axis=1), jnp.concatenate([-sin, sin], axis=1)


def _pad_rows(a, rows=8):
    return jnp.pad(a, ((0, 0), (rows - a.shape[1], 0), (0, 0)))


def _trunk(x, mods, pos, weights, caches, biases):
    nseq, seq, d = x.shape
    sample = caches is not None
    tl = min(seq, 512)
    c = min(seq, SCAN_CHUNK)
    depth = weights["w_up"].shape[0]
    outs = {k: [] for k in ("ret", "bk", "bv", "ck", "cv", "dconv", "dssm", "ffn")}
    cos, sin = _rope_tables(pos)
    for l in range(depth):
        i = l // 2
        sh1, sc1, g1, sh2, sc2, g2 = mods[l]
        ng1 = weights["norm_g"][l, 0].reshape(1, d)
        ng2 = weights["norm_g"][l, 1].reshape(1, d)
        if l % 2 == 0:
            qkv, kb, vb = _inproj_ab(x, ng1, sc1, sh1, weights["w_in_ab"][i], cos, sin, tl)
            s0 = caches["ret"][i] if sample else jnp.zeros((nseq, H_A, DK_A, DV_A), F32)
            o1, s_new = _retention(qkv, s0, weights["ret_gn"][i], c)
            lam_init = 0.8 - 0.6 * math.exp(-0.3 * l)
            if sample:
                o2 = _diff_attn_sample(qkv, caches["bk"][i], caches["bv"][i], biases["t5"],
                                       weights["lam_q"][i], weights["lam_k"][i], weights["diff_gn"][i], lam_init)
            else:
                o2 = _diff_attn(qkv, biases["t5"], weights["lam_q"][i], weights["lam_k"][i],
                                weights["diff_gn"][i], lam_init)
            wo = weights["w_out_ab"][i]
            outs["ret"].append(s_new)
            outs["bk"].append(kb.reshape(nseq, seq, H_B, 2 * DH_B))
            outs["bv"].append(vb.reshape(nseq, seq, H_B, DV_B))
        else:
            keep = seq if sample else min(C_WINDOW, seq)
            main, dt, kc, vc, tail = _inproj_cd(x, ng1, sc1, sh1, weights["w_in_cd"][i], tl, keep)
            if sample:
                o1 = _band_attn_sample(main, caches["ck"][i], caches["cv"][i], biases["band"][i])
                conv_past = _pad_rows(caches["dconv"][i])
                s0 = caches["dssm"][i].reshape(nseq, H_D * P_D, N_D)
            else:
                o1 = _band_attn(main, biases["band"][i])
                conv_past = jnp.zeros((nseq, 8, CONV_DIM_D), F32)
                s0 = jnp.zeros((nseq, H_D * P_D, N_D), F32)
            o2, ssm_new = _ssd(main, dt, weights["d_conv_w"][i], weights["d_conv_b"][i], weights["d_dt_bias"][i],
                               weights["d_a_log"][i], weights["d_skip"][i], weights["d_norm_g"][i], conv_past, s0, c)
            wo = weights["w_out_cd"][i]
            outs["ck"].append(kc.reshape(nseq, keep, H_C, DH_C))
            outs["cv"].append(vc.reshape(nseq, keep, H_C, DH_C))
            outs["dconv"].append(tail[:, 8 - (CONV_D - 1):])
            outs["dssm"].append(ssm_new.reshape(nseq, H_D, P_D, N_D))
        d_ff = weights["w_down"].shape[1]
        ffn_past = _pad_rows(caches["ffn"][l]) if sample else jnp.zeros((nseq, 8, d_ff), F32)
        x, ftail = _outproj_ffn(x, o1, o2, wo, g1, ng2, sc2, sh2, g2, weights["w_up"][l], weights["ffn_conv_w"][l],
                                weights["ffn_conv_b"][l], weights["w_down"][l], ffn_past,
                                weights["final_g"].reshape(1, d), tl, final=(l == depth - 1))
        outs["ffn"].append(ftail[:, 8 - (CONV_F - 1):])
    stk = lambda t: jnp.stack(t).astype(F32)
    return (x,) + tuple(stk(outs[k]) for k in ("ret", "bk", "bv", "ck", "cv", "dconv", "dssm", "ffn"))


def kernel(x_prompt, x_sample, cache_ret_state, cache_b_k, cache_b_v, cache_c_k, cache_c_v, state_d_conv, state_d_ssm, state_ffn_conv, c_prompt, c_sample, w_mod, b_mod, norm_g, final_g, t5_table, w_in_ab, w_out_ab, ret_gn, lam_q, lam_k, diff_gn, w_in_cd, w_out_cd, rel_table, d_conv_w, d_conv_b, d_dt_bias, d_a_log, d_skip, d_norm_g, w_up, ffn_conv_w, ffn_conv_b, w_down):
    batch, seq, d = x_prompt.shape
    dec_batch, dec_seq, _ = x_sample.shape
    past = cache_b_k.shape[2]
    depth = w_mod.shape[0]
    assert dec_seq <= CHUNK and past % CHUNK == 0 and cache_c_k.shape[2] == C_WINDOW
    assert seq % BAND_TQ == 0 and seq % DIFF_TQ == 0

    w_in_cd_p = jnp.pad(w_in_cd, ((0, 0), (0, 0), (0, N_CD_PAD - w_in_cd.shape[-1]))).astype(BF16)
    weights = dict(
        norm_g=norm_g, final_g=final_g, w_in_ab=w_in_ab.astype(BF16), w_out_ab=w_out_ab.astype(BF16),
        ret_gn=ret_gn, lam_q=lam_q, lam_k=lam_k, diff_gn=diff_gn, w_in_cd=w_in_cd_p,
        w_out_cd=w_out_cd.astype(BF16), d_conv_w=d_conv_w, d_conv_b=d_conv_b, d_dt_bias=d_dt_bias, d_a_log=d_a_log,
        d_skip=d_skip, d_norm_g=d_norm_g, w_up=w_up.astype(BF16), ffn_conv_w=ffn_conv_w, ffn_conv_b=ffn_conv_b,
        w_down=w_down.astype(BF16))

    mod = _modulation(jnp.concatenate([c_prompt, c_sample], axis=0), w_mod, b_mod)

    def pieces(rows):
        return [[m[:, None, :] for m in jnp.split(mod[l, rows], 6, axis=-1)] for l in range(depth)]

    band = [_band_bias(rel_table[i]) for i in range(rel_table.shape[0])]
    biases_p = dict(t5=_t5_bias_prompt(t5_table, seq), band=band)
    biases_s = dict(t5=_t5_bias_sample(t5_table, past, dec_seq),
                    band=[b[:, :dec_seq, :C_WINDOW + dec_seq] for b in band])
    caches = dict(
        ret=cache_ret_state,
        bk=cache_b_k.reshape(cache_b_k.shape[0], dec_batch, past, H_B * 2 * DH_B),
        bv=cache_b_v.reshape(cache_b_v.shape[0], dec_batch, past, H_B * DV_B),
        ck=cache_c_k.reshape(cache_c_k.shape[0], dec_batch, C_WINDOW, H_C * DH_C),
        cv=cache_c_v.reshape(cache_c_v.shape[0], dec_batch, C_WINDOW, H_C * DH_C),
        dconv=state_d_conv, dssm=state_d_ssm, ffn=state_ffn_conv)

    pos_p = jnp.arange(seq, dtype=jnp.int32)
    pos_s = past + jnp.arange(dec_seq, dtype=jnp.int32)
    y_p, ret_p, bk_p, bv_p, ck_p, cv_p, dconv_p, dssm_p, ffn_p = _trunk(
        x_prompt, pieces(slice(0, batch)), pos_p, weights, None, biases_p)
    y_s, ret_s, bk_s, bv_s, ck_s, cv_s, dconv_s, dssm_s, ffn_s = _trunk(
        x_sample, pieces(slice(batch, batch + dec_batch)), pos_s, weights, caches, biases_s)
    return (y_p, y_s, ret_p, ret_s, bk_p, bk_s, bv_p, bv_s, ck_p, ck_s, cv_p, cv_s,
            dconv_p, dconv_s, dssm_p, dssm_s, ffn_p, ffn_s)
```

```python
import functools
import math

import numpy as np
import jax
import jax.numpy as jnp
from jax import lax
from jax.experimental import pallas as pl
from jax.experimental.pallas import tpu as pltpu

F32 = jnp.float32
BF16 = jnp.bfloat16

CHUNK = 64
EPS = 1e-6
NEG = -1e30
H_A, DK_A, DV_A = 4, 128, 128
ROPE_BASE = 10000.0
H_B, DH_B, DV_B = 4, 64, 128
T5_BUCKETS, T5_MAX_DIST = 32, 128
H_C, DH_C = 8, 64
BAND_CHUNKS = 8
C_WINDOW = BAND_CHUNKS * CHUNK
REL_CLIP = 128
H_D, P_D, G_D, N_D = 8, 64, 2, 128
D_INNER = H_D * P_D
CONV_D = 4
CONV_DIM_D = D_INNER + 2 * G_D * N_D
CONV_F = 3
N_AB_COLS = 7 * 512
N_CD_MAIN = 6 * 512
N_CD_PAD = N_CD_MAIN + 128
FF_CHUNK = 256
BAND_TQ = 256
DIFF_TQ = 256
SCAN_CHUNK = 256
FUSED_TL = 256
VMEM_LIMIT = 56 * 1024 * 1024


def _mm(a, b):
    return jnp.dot(a, b, preferred_element_type=F32)


def _mm_nt(a, b):
    return lax.dot_general(a, b, (((1,), (1,)), ((), ())), preferred_element_type=F32)


def _mm_tn(a, b):
    return lax.dot_general(a, b, (((0,), (0,)), ((), ())), preferred_element_type=F32)


def _mm_exact(a, b):
    return jnp.dot(a, b, preferred_element_type=F32, precision=lax.Precision.HIGHEST)


def _mm_nt_exact(a, b):
    return lax.dot_general(a, b, (((1,), (1,)), ((), ())), preferred_element_type=F32,
                           precision=lax.Precision.HIGHEST)


def _silu(x):
    return x * (1.0 / (1.0 + jnp.exp(-x)))


def _softplus(x):
    return jnp.maximum(x, 0.0) + jnp.log1p(jnp.exp(-jnp.abs(x)))


def _gelu_tanh(x):
    return x * (0.5 * (1.0 + jnp.tanh(math.sqrt(2.0 / math.pi) * (x + 0.044715 * (x * x * x)))))


def _norm_mod(x, g, sc, sh):
    ms = jnp.mean(x * x, axis=-1, keepdims=True)
    return (x * lax.rsqrt(ms + EPS)) * g * (1.0 + sc) + sh


def _params(sem):
    return pltpu.CompilerParams(dimension_semantics=sem, vmem_limit_bytes=VMEM_LIMIT)


def _resident(shape):
    nd = len(shape)
    return pl.BlockSpec(shape, lambda *_: (0,) * nd, pipeline_mode=pl.Buffered(1))


def _mod_kernel(c_ref, w_ref, b_ref, o_ref):
    c = c_ref[...]
    o_ref[0] = _mm(_silu(c).astype(BF16), w_ref[0].astype(BF16)) + b_ref[0]


def _modulation(c_all, w_mod, b_mod):
    depth, d, n = w_mod.shape
    r = c_all.shape[0]
    tn = 1536
    return pl.pallas_call(
        _mod_kernel,
        grid=(depth, n // tn),
        in_specs=[pl.BlockSpec((r, d), lambda l, j: (0, 0)),
                  pl.BlockSpec((1, d, tn), lambda l, j: (l, 0, j)),
                  pl.BlockSpec((1, 1, tn), lambda l, j: (l, 0, j))],
        out_specs=pl.BlockSpec((1, r, tn), lambda l, j: (l, 0, j)),
        out_shape=jax.ShapeDtypeStruct((depth, r, n), F32),
        compiler_params=_params(("parallel", "parallel")),
        name="modulation",
    )(c_all, w_mod, b_mod.reshape(depth, 1, n))


def _table_gather_kernel(tab_ref, idx_ref, o_ref, *, n_entries, n_heads):
    idx = idx_ref[...]

    def body(r, accs):
        m = idx == r
        return tuple(jnp.where(m, tab_ref[hh, r], a) for hh, a in enumerate(accs))

    accs = lax.fori_loop(0, n_entries, body, tuple(jnp.zeros(idx.shape, F32) for _ in range(n_heads)))
    for hh in range(n_heads):
        o_ref[hh:hh + 1, :] = accs[hh]


def _table_gather(table, idx):
    t, h = table.shape
    w = idx.shape[0]
    return pl.pallas_call(
        functools.partial(_table_gather_kernel, n_entries=t, n_heads=h),
        in_specs=[pl.BlockSpec(memory_space=pltpu.SMEM),
                  pl.BlockSpec((1, w), lambda: (0, 0))],
        out_specs=pl.BlockSpec((h, w), lambda: (0, 0)),
        out_shape=jax.ShapeDtypeStruct((h, w), F32),
        name="table_gather",
    )(table.T, idx.reshape(1, w))


def _toeplitz(vec, rows, cols):
    h, wv = vec.shape
    flat = jnp.tile(vec, (1, rows))[:, :rows * (wv - 1)]
    return flat.reshape(h, rows, wv - 1)[:, :, :cols]


def _t5_bucket(rel):
    half = T5_BUCKETS // 2
    max_exact = half // 2
    base = jnp.where(rel > 0, half, 0)
    n = jnp.abs(rel)
    nf = jnp.maximum(n, 1).astype(F32)
    large = max_exact + (jnp.log(nf / max_exact) / math.log(T5_MAX_DIST / max_exact) * (half - max_exact)).astype(jnp.int32)
    large = jnp.minimum(large, half - 1)
    return base + jnp.where(n < max_exact, n, large)


def _wrapped_offsets(n_pos, n_neg):
    p = jnp.arange(n_pos + n_neg, dtype=jnp.int32)
    return jnp.where(p < n_pos, p, p - (n_pos + n_neg))


def _t5_bias_prompt(t5_table, seq):
    nd = seq // DIFF_TQ
    u = _wrapped_offsets(seq, DIFF_TQ)
    vec = _table_gather(t5_table, _t5_bucket(u - (seq - DIFF_TQ)))
    wide = _toeplitz(vec, DIFF_TQ, seq)
    return wide.reshape(H_B, DIFF_TQ, nd, DIFF_TQ).transpose(0, 2, 1, 3)


def _t5_bias_sample(t5_table, past, lq):
    u = _wrapped_offsets(past + lq, lq)
    vec = _table_gather(t5_table, _t5_bucket(u - past))
    return _toeplitz(vec, lq, past + lq)


def _band_bias(rel_table):
    u = _wrapped_offsets(3 * BAND_TQ, BAND_TQ)
    idx = jnp.clip(2 * BAND_TQ - u, -REL_CLIP, REL_CLIP) + REL_CLIP
    bias = _toeplitz(_table_gather(rel_table, idx), BAND_TQ, 3 * BAND_TQ)
    dc = (np.arange(BAND_TQ)[:, None] // CHUNK + C_WINDOW // CHUNK) - np.arange(3 * BAND_TQ)[None, :] // CHUNK
    return jnp.where(jnp.asarray((dc >= 0) & (dc <= BAND_CHUNKS))[None], bias, NEG)


def _inproj_ab_kernel(x_ref, g_ref, sc_ref, sh_ref, w_ref, cos_ref, sin_ref, qkv_ref, kb_ref, vb_ref, *, tl):
    h = _norm_mod(x_ref[0], g_ref[...], sc_ref[0], sh_ref[0]).astype(BF16)
    cos = cos_ref[...]
    sin = sin_ref[...]
    for j in range(7):
        y = _mm(h, w_ref[:, j * 512:(j + 1) * 512])
        if j < 2:
            for hh in range(H_A):
                yh = y[:, hh * 128:(hh + 1) * 128]
                yh = yh * cos + pltpu.roll(yh, 64, 1) * sin
                if j == 1:
                    yh = yh * (DK_A ** -0.5)
                qkv_ref[0, :, j * 512 + hh * 128:j * 512 + (hh + 1) * 128] = yh.astype(BF16)
        else:
            if j >= 5:
                dst = kb_ref if j == 5 else vb_ref
                for hh in range(H_B):
                    dst[0, pl.ds(hh, tl, stride=H_B), :] = y[:, hh * 128:(hh + 1) * 128]
            qkv_ref[0, :, j * 512:(j + 1) * 512] = y.astype(BF16)


def _inproj_ab(x, g, sc, sh, w, cos, sin, tl):
    nseq, seq, d = x.shape
    grid = (nseq, seq // tl)
    tok = lambda n: pl.BlockSpec((1, tl, n), lambda b, l: (b, l, 0))
    per_seq = pl.BlockSpec((1, 1, d), lambda b, l: (b, 0, 0))
    cache = pl.BlockSpec((1, tl * H_B, 128), lambda b, l: (b, l, 0))
    return pl.pallas_call(
        functools.partial(_inproj_ab_kernel, tl=tl),
        grid=grid,
        in_specs=[tok(d), _resident((1, d)), per_seq, per_seq, _resident((d, N_AB_COLS)),
                  pl.BlockSpec((tl, 128), lambda b, l: (l, 0)),
                  pl.BlockSpec((tl, 128), lambda b, l: (l, 0))],
        out_specs=[tok(N_AB_COLS), cache, cache],
        out_shape=[jax.ShapeDtypeStruct((nseq, seq, N_AB_COLS), BF16),
                   jax.ShapeDtypeStruct((nseq, seq * H_B, 128), F32),
                   jax.ShapeDtypeStruct((nseq, seq * H_B, 128), F32)],
        compiler_params=_params(("parallel", "parallel")),
        name="inproj_ab",
    )(x, g, sc, sh, w, cos, sin)


def _retention_kernel(q_ref, k_ref, v_ref, g_ref, s0_ref, dec_ref, qd_ref, kd_ref, gc_ref, gn_ref,
                      o_ref, sn_ref, *, c, n_chunks):
    states = [s0_ref[0, h] for h in range(H_A)]
    for ci in range(n_chunks):
        rows = slice(ci * c, (ci + 1) * c)
        for h in range(H_A):
            cs = slice(h * 128, (h + 1) * 128)
            q = q_ref[0, rows, cs]
            k = k_ref[0, rows, cs]
            v = v_ref[0, rows, cs]
            state = states[h]
            s = _mm_nt(q, k) * dec_ref[h]
            o = _mm(s.astype(BF16), v) + _mm((q.astype(F32) * qd_ref[h]).astype(BF16), state.astype(BF16))
            states[h] = gc_ref[h] * state + _mm_tn((k.astype(F32) * kd_ref[h]).astype(BF16), v)
            mu = jnp.mean(o, axis=-1, keepdims=True)
            dlt = o - mu
            var = jnp.mean(dlt * dlt, axis=-1, keepdims=True)
            y = dlt * lax.rsqrt(var + EPS) * gn_ref[:, cs] * _silu(g_ref[0, rows, cs].astype(F32))
            o_ref[0, rows, cs] = y.astype(BF16)
    for h in range(H_A):
        sn_ref[0, h] = states[h]


def _retention_consts(c):
    lg = np.log1p(-np.exp2(-5.0 - np.arange(H_A, dtype=np.float32))).astype(np.float32)
    idx = np.arange(c, dtype=np.float32)
    diff = idx[:, None] - idx[None, :]
    decay = np.where(diff[None] >= 0, np.exp(np.maximum(diff, 0.0)[None] * lg[:, None, None]), 0.0)
    qd = np.exp((idx + 1.0)[None, :] * lg[:, None])
    kd = np.exp((c - 1.0 - idx)[None, :] * lg[:, None])
    gc = np.exp(c * lg)
    bc = lambda a: np.ascontiguousarray(np.broadcast_to(a[..., None], a.shape + (128,))).astype(np.float32)
    return decay.astype(np.float32), bc(qd), bc(kd), bc(gc[:, None])


def _retention(qkv, s0, ret_gn, c):
    nseq, seq, _ = qkv.shape
    decay, qd, kd, gc = _retention_consts(c)
    w = H_A * 128
    col = lambda idx: pl.BlockSpec((1, seq, w), lambda b: (b, 0, idx))
    const = lambda shape: pl.BlockSpec(shape, lambda b: (0,) * len(shape))
    state = pl.BlockSpec((1, H_A, DK_A, DV_A), lambda b: (b, 0, 0, 0))
    return pl.pallas_call(
        functools.partial(_retention_kernel, c=c, n_chunks=seq // c),
        grid=(nseq,),
        in_specs=[col(0), col(1), col(2), col(3), state, const((H_A, c, c)), const((H_A, c, 128)),
                  const((H_A, c, 128)), const((H_A, 1, 128)), const((1, w))],
        out_specs=[col(0), state],
        out_shape=[jax.ShapeDtypeStruct((nseq, seq, w), BF16),
                   jax.ShapeDtypeStruct((nseq, H_A, DK_A, DV_A), F32)],
        compiler_params=_params(("parallel",)),
        name="retention",
    )(qkv, qkv, qkv, qkv, s0, decay, qd, kd, gc, ret_gn.reshape(1, w))


def _split_halves(q):
    lane = lax.broadcasted_iota(jnp.int32, q.shape, 1)
    zero = jnp.zeros_like(q)
    return jnp.where(lane < 64, q, zero), jnp.where(lane >= 64, q, zero)


def _lambda(lq_ref, lk_ref, lam_init):
    e = jnp.exp(jnp.sum(lq_ref[...] * lk_ref[...], axis=1, keepdims=True))
    return e[0:1, :] - e[1:2, :] + lam_init


def _diff_epilogue(o, gn, lam_init):
    ms = jnp.mean(o * o, axis=-1, keepdims=True)
    return (o * lax.rsqrt(ms + EPS)) * gn * (1.0 - lam_init)


def _diff_attn_kernel(q_ref, k_ref, v_ref, b_ref, lq_ref, lk_ref, gn_ref, o_ref, s_ref, qs_ref, mx_ref, ls_ref,
                      acc_ref, *, tq, nd, lam_init):
    row = lax.broadcasted_iota(jnp.int32, (tq, tq), 0) // CHUNK
    col = lax.broadcasted_iota(jnp.int32, (tq, tq), 1) // CHUNK
    chunk_mask = col <= row
    lam = _lambda(lq_ref, lk_ref, lam_init)

    def prepare(t):
        q = q_ref[0, t * tq:(t + 1) * tq, :]
        halves = _split_halves((q.astype(F32) * (DH_B ** -0.5)).astype(BF16))
        for i in range(2):
            r = 2 * (t % 2) + i
            qs_ref[r] = halves[i]
            mx_ref[r] = jnp.full((tq, 128), NEG, F32)
            ls_ref[r] = jnp.zeros((tq, 128), F32)
            acc_ref[r] = jnp.zeros((tq, DV_B), F32)

    def scores(t, kb):
        k = k_ref[0, kb * tq:(kb + 1) * tq, :]
        b = b_ref[0, nd - 1 - t + kb]
        for i in range(2):
            r = 2 * (t % 2) + i
            s = _mm_nt(qs_ref[r], k) + b
            if kb == t:
                s = jnp.where(chunk_mask, s, NEG)
            s_ref[r * nd + kb] = s
            m = mx_ref[r]
            for c0 in range(0, tq, 128):
                m = jnp.maximum(m, s[:, c0:c0 + 128])
            mx_ref[r] = m

    def row_max(t):
        for i in range(2):
            r = 2 * (t % 2) + i
            mx_ref[r] = jnp.broadcast_to(jnp.max(mx_ref[r], axis=1, keepdims=True), (tq, 128))

    def weigh(t, kb):
        v = v_ref[0, kb * tq:(kb + 1) * tq, :]
        for i in range(2):
            r = 2 * (t % 2) + i
            m = mx_ref[r]
            l = ls_ref[r]
            ps = []
            for c0 in range(0, tq, 128):
                p = jnp.exp(s_ref[r * nd + kb, :, c0:c0 + 128] - m)
                l = l + p
                ps.append(p.astype(BF16))
            ls_ref[r] = l
            acc_ref[r] += _mm(jnp.concatenate(ps, axis=1), v)

    def finish(t):
        r = 2 * (t % 2)
        l0 = jnp.sum(ls_ref[r], axis=1, keepdims=True)
        l1 = jnp.sum(ls_ref[r + 1], axis=1, keepdims=True)
        o = acc_ref[r] / l0 - lam * (acc_ref[r + 1] / l1)
        o_ref[0, t * tq:(t + 1) * tq, :] = _diff_epilogue(o, gn_ref[...], lam_init).astype(BF16)

    last = nd - 1
    prepare(last)
    for kb in range(last + 1):
        scores(last, kb)
    row_max(last)
    for t in range(last - 1, -1, -1):
        prepare(t)
        for kb in range(t + 2):
            if kb <= t:
                scores(t, kb)
            weigh(t + 1, kb)
        finish(t + 1)
        row_max(t)
    weigh(0, 0)
    finish(0)


def _diff_attn(qkv, bias, lam_q, lam_k, diff_gn, lam_init):
    nseq, seq, _ = qkv.shape
    tq = DIFF_TQ
    nd = seq // tq
    full = lambda off: pl.BlockSpec((1, seq, 128), lambda b, h: (b, 0, off + h))
    small = lambda r, c: pl.BlockSpec((r, c), lambda b, h: (0, 0))
    return pl.pallas_call(
        functools.partial(_diff_attn_kernel, tq=tq, nd=nd, lam_init=lam_init),
        grid=(nseq, H_B),
        in_specs=[full(16), full(20), full(24),
                  pl.BlockSpec((1, nd, tq, tq), lambda b, h: (h, 0, 0, 0)),
                  small(2, DH_B), small(2, DH_B), small(1, DV_B)],
        out_specs=full(0),
        out_shape=jax.ShapeDtypeStruct((nseq, seq, H_B * DV_B), BF16),
        scratch_shapes=[pltpu.VMEM((4 * nd, tq, tq), F32), pltpu.VMEM((4, tq, 128), BF16),
                        pltpu.VMEM((4, tq, 128), F32), pltpu.VMEM((4, tq, 128), F32),
                        pltpu.VMEM((4, tq, DV_B), F32)],
        compiler_params=_params(("parallel", "parallel")),
        name="diff_attn",
    )(qkv, qkv, qkv, bias, lam_q, lam_k, diff_gn.reshape(1, DV_B))


def _diff_attn_sample_kernel(q_ref, kc_ref, vc_ref, kp_ref, vp_ref, b_ref, lq_ref, lk_ref, gn_ref, o_ref,
                             *, past, lam_init):
    scale = DH_B ** -0.5
    lam = _lambda(lq_ref, lk_ref, lam_init)
    for h in range(H_B):
        cs = slice(h * 128, (h + 1) * 128)
        qs = _split_halves(q_ref[0, :, cs])
        kp = kp_ref[0, :, cs].astype(BF16)
        vp = vp_ref[0, :, cs].astype(BF16)
        kc = kc_ref[0, :, cs]
        vc = vc_ref[0, :, cs]
        b = b_ref[h]
        probs = []
        for i in range(2):
            sp = _mm_nt(qs[i], kp) * scale + b[:, :past]
            sc = _mm_nt(qs[i], kc) * scale + b[:, past:]
            m = jnp.maximum(jnp.max(sp, axis=1, keepdims=True), jnp.max(sc, axis=1, keepdims=True))
            pp = jnp.exp(sp - m)
            pc = jnp.exp(sc - m)
            l = jnp.sum(pp, axis=1, keepdims=True) + jnp.sum(pc, axis=1, keepdims=True)
            probs.append((pp / l, pc / l))
        ap = probs[0][0] - lam * probs[1][0]
        ac = probs[0][1] - lam * probs[1][1]
        o = _mm(ap.astype(BF16), vp) + _mm(ac.astype(BF16), vc)
        o_ref[0, :, cs] = _diff_epilogue(o, gn_ref[...], lam_init).astype(BF16)


def _diff_attn_sample(qkv, k_past, v_past, bias, lam_q, lam_k, diff_gn, lam_init):
    nseq, lq, _ = qkv.shape
    past = k_past.shape[1]
    w = H_B * DV_B
    cur = lambda idx: pl.BlockSpec((1, lq, w), lambda b: (b, 0, idx))
    old = pl.BlockSpec((1, past, w), lambda b: (b, 0, 0))
    small = lambda r, c: pl.BlockSpec((r, c), lambda b: (0, 0))
    return pl.pallas_call(
        functools.partial(_diff_attn_sample_kernel, past=past, lam_init=lam_init),
        grid=(nseq,),
        in_specs=[cur(4), cur(5), cur(6), old, old, _resident(bias.shape),
                  small(2, DH_B), small(2, DH_B), small(1, DV_B)],
        out_specs=cur(0),
        out_shape=jax.ShapeDtypeStruct((nseq, lq, w), BF16),
        compiler_params=_params(("parallel",)),
        name="diff_attn_sample",
    )(qkv, qkv, qkv, k_past, v_past, bias, lam_q, lam_k, diff_gn.reshape(1, DV_B))


def _inproj_cd_kernel(x_ref, g_ref, sc_ref, sh_ref, w_ref, main_ref, dt_ref, kc_ref, vc_ref, tail_ref, *, tl):
    h = _norm_mod(x_ref[0], g_ref[...], sc_ref[0], sh_ref[0]).astype(BF16)
    for j in range(6):
        y = _mm(h, w_ref[:, j * 512:(j + 1) * 512])
        main_ref[0, :, j * 512:(j + 1) * 512] = y.astype(BF16)
        if j == 1:
            kc_ref[0] = y
        if j == 2:
            vc_ref[0] = y
        if j >= 4:
            tail_ref[0, :, (j - 4) * 512:(j - 3) * 512] = y[tl - 8:tl, :]
    dt_ref[0] = _mm(h, w_ref[:, N_CD_MAIN:N_CD_PAD])


def _inproj_cd(x, g, sc, sh, w, tl, keep):
    nseq, seq, d = x.shape
    assert keep % tl == 0
    skip = (seq - keep) // tl
    tok = lambda n: pl.BlockSpec((1, tl, n), lambda b, l: (b, l, 0))
    kept = pl.BlockSpec((1, tl, 512), lambda b, l: (b, jnp.maximum(l - skip, 0), 0))
    per_seq = pl.BlockSpec((1, 1, d), lambda b, l: (b, 0, 0))
    return pl.pallas_call(
        functools.partial(_inproj_cd_kernel, tl=tl),
        grid=(nseq, seq // tl),
        in_specs=[tok(d), _resident((1, d)), per_seq, per_seq, _resident((d, N_CD_PAD))],
        out_specs=[tok(N_CD_MAIN), tok(128), kept, kept,
                   pl.BlockSpec((1, 8, CONV_DIM_D), lambda b, l: (b, 0, 0))],
        out_shape=[jax.ShapeDtypeStruct((nseq, seq, N_CD_MAIN), BF16),
                   jax.ShapeDtypeStruct((nseq, seq, 128), F32),
                   jax.ShapeDtypeStruct((nseq, keep, 512), F32),
                   jax.ShapeDtypeStruct((nseq, keep, 512), F32),
                   jax.ShapeDtypeStruct((nseq, 8, CONV_DIM_D), F32)],
        compiler_params=_params(("parallel", "arbitrary")),
        name="inproj_cd",
    )(x, g, sc, sh, w)


def _band_kernel(q_ref, k_ref, v_ref, b_ref, o_ref, *, tq):
    t = pl.program_id(1)
    lane = lax.broadcasted_iota(jnp.int32, (tq, 128), 1)
    n_heads = H_C

    def attend(near_start):
        starts, offs = [], []
        for d in range(3):
            kt = t - 2 + d
            starts.append(pl.multiple_of(jnp.maximum(kt, 0) * tq, tq))
            offs.append(jnp.where(kt >= 0, 0.0, NEG))

        def scores(head):
            pr, hi = divmod(head, 2)
            cs = slice(pr * 128, (pr + 1) * 128)
            qh = _split_halves((q_ref[0, :, cs].astype(F32) * (DH_C ** -0.5)).astype(BF16))[hi]
            ss = []
            for d in range(3):
                s = _mm_nt(qh, k_ref[0, pl.ds(starts[d], tq), cs]) + b_ref[head, :, d * tq:(d + 1) * tq]
                ss.append(s + offs[d] if near_start and d < 2 else s)
            mm = ss[2][:, 0:128]
            for d in range(3):
                for c0 in range(0, tq, 128):
                    mm = jnp.maximum(mm, ss[d][:, c0:c0 + 128])
            return ss, jnp.max(mm, axis=1, keepdims=True)

        def weigh(head, ss, m):
            cs = slice((head // 2) * 128, (head // 2 + 1) * 128)
            ls = jnp.zeros((tq, 128), F32)
            o = jnp.zeros((tq, 128), F32)
            for d in range(3):
                p = jnp.exp(ss[d] - m)
                for c0 in range(0, tq, 128):
                    ls = ls + p[:, c0:c0 + 128]
                o = o + _mm(p.astype(BF16), v_ref[0, pl.ds(starts[d], tq), cs])
            return o / jnp.sum(ls, axis=1, keepdims=True)

        outs = []
        pending = scores(0)
        for head in range(n_heads):
            nxt = scores(head + 1) if head + 1 < n_heads else None
            outs.append(weigh(head, *pending))
            pending = nxt
            if head % 2 == 1:
                cs = slice((head // 2) * 128, (head // 2 + 1) * 128)
                o_ref[0, :, cs] = jnp.where(lane < 64, outs[head - 1], outs[head]).astype(BF16)

    @pl.when(t < 2)
    def _():
        attend(True)

    @pl.when(t >= 2)
    def _():
        attend(False)


def _band_attn(main, bias):
    nseq, seq, _ = main.shape
    tq = BAND_TQ
    w = H_C * DH_C
    full = lambda idx: pl.BlockSpec((1, seq, w), lambda b, t: (b, 0, idx))
    return pl.pallas_call(
        functools.partial(_band_kernel, tq=tq),
        grid=(nseq, seq // tq),
        in_specs=[pl.BlockSpec((1, tq, w), lambda b, t: (b, t, 0)), full(1), full(2), _resident(bias.shape)],
        out_specs=pl.BlockSpec((1, tq, w), lambda b, t: (b, t, 0)),
        out_shape=jax.ShapeDtypeStruct((nseq, seq, w), BF16),
        compiler_params=_params(("parallel", "parallel")),
        name="band_attn",
    )(main, main, main, bias)


def _band_sample_kernel(q_ref, kc_ref, vc_ref, kp_ref, vp_ref, b_ref, o_ref, *, past):
    scale = DH_C ** -0.5
    for pr in range(H_C // 2):
        cs = slice(pr * 128, (pr + 1) * 128)
        halves = _split_halves(q_ref[0, :, cs])
        kp = kp_ref[0, :, cs].astype(BF16)
        vp = vp_ref[0, :, cs].astype(BF16)
        kc = kc_ref[0, :, cs]
        vc = vc_ref[0, :, cs]
        outs = []
        for hi, qh in enumerate(halves):
            b = b_ref[2 * pr + hi]
            sp = _mm_nt(qh, kp) * scale + b[:, :past]
            sc = _mm_nt(qh, kc) * scale + b[:, past:]
            m = jnp.maximum(jnp.max(sp, axis=1, keepdims=True), jnp.max(sc, axis=1, keepdims=True))
            pp = jnp.exp(sp - m)
            pc = jnp.exp(sc - m)
            l = jnp.sum(pp, axis=1, keepdims=True) + jnp.sum(pc, axis=1, keepdims=True)
            outs.append(_mm((pp / l).astype(BF16), vp) + _mm((pc / l).astype(BF16), vc))
        lane = lax.broadcasted_iota(jnp.int32, outs[0].shape, 1)
        o_ref[0, :, cs] = jnp.where(lane < 64, outs[0], outs[1]).astype(BF16)


def _band_attn_sample(main, k_past, v_past, bias):
    nseq, lq, _ = main.shape
    past = k_past.shape[1]
    w = H_C * DH_C
    cur = lambda idx: pl.BlockSpec((1, lq, w), lambda b: (b, 0, idx))
    old = pl.BlockSpec((1, past, w), lambda b: (b, 0, 0))
    return pl.pallas_call(
        functools.partial(_band_sample_kernel, past=past),
        grid=(nseq,),
        in_specs=[cur(0), cur(1), cur(2), old, old, _resident(bias.shape)],
        out_specs=cur(0),
        out_shape=jax.ShapeDtypeStruct((nseq, lq, w), BF16),
        compiler_params=_params(("parallel",)),
        name="band_attn_sample",
    )(main, main, main, k_past, v_past, bias)


def _ssd_kernel(z_ref, xs_ref, bm_ref, cm_ref, dt_ref, cw_ref, cb_ref, dtb_ref, alog_ref, dskip_ref, ng_ref,
                past_ref, s0_ref, o_ref, sn_ref, buf_ref, st_ref, *, c):
    @pl.when(pl.program_id(1) == 0)
    def _():
        buf_ref[0:8, :] = past_ref[0]
        st_ref[...] = s0_ref[0]

    buf_ref[8:8 + c, 0:D_INNER] = xs_ref[0].astype(F32)
    buf_ref[8:8 + c, D_INNER:D_INNER + 256] = bm_ref[0].astype(F32)
    buf_ref[8:8 + c, D_INNER + 256:CONV_DIM_D] = cm_ref[0].astype(F32)
    conv = cb_ref[...] + buf_ref[pl.ds(5, c), :] * cw_ref[0:1, :]
    for i in range(1, CONV_D):
        conv = conv + buf_ref[pl.ds(5 + i, c), :] * cw_ref[i:i + 1, :]
    buf_ref[0:8, :] = buf_ref[c:c + 8, :]
    act = _silu(conv)
    xs = act[:, 0:D_INNER]
    bm = act[:, D_INNER:D_INNER + 256].astype(BF16)
    cm = act[:, D_INNER + 256:CONV_DIM_D].astype(BF16)

    dt = _softplus(dt_ref[0] + dtb_ref[...])
    da = dt * (-jnp.exp(alog_ref[...]))
    row = lax.broadcasted_iota(jnp.int32, (c, c), 0)
    col = lax.broadcasted_iota(jnp.int32, (c, c), 1)
    tri = row >= col
    cum = _mm_exact(tri.astype(F32), da)
    eye = (lax.broadcasted_iota(jnp.int32, (8, 128), 0) == lax.broadcasted_iota(jnp.int32, (8, 128), 1)).astype(F32)
    cum_t = _mm_exact(_mm_nt_exact(eye, da), (row <= col).astype(F32))
    dt_t = _mm_nt_exact(eye, dt)
    ecum = jnp.exp(cum)
    last = cum[c - 1:c, :]
    wgt = jnp.exp(last - cum) * dt
    elast = jnp.exp(last)
    lane = lax.broadcasted_iota(jnp.int32, (c, 128), 1)
    low = lane < 64
    rlow = lax.broadcasted_iota(jnp.int32, (128, 128), 0) < 64

    for g in range(G_D):
        bm_g = bm[:, g * 128:(g + 1) * 128]
        cm_g = cm[:, g * 128:(g + 1) * 128]
        cb = _mm_nt(cm_g, bm_g)
        ys = []
        for pp in range(2):
            p = 2 * g + pp
            h0, h1 = 2 * p, 2 * p + 1
            x_f = xs[:, p * 128:(p + 1) * 128]
            x_b = x_f.astype(BF16)
            y_in = []
            for hh in (h0, h1):
                seg = cum[:, hh:hh + 1] - cum_t[hh:hh + 1, :]
                dec = jnp.exp(jnp.where(tri, seg, NEG))
                y_in.append(_mm((cb * dec * dt_t[hh:hh + 1, :]).astype(BF16), x_b))
            st = st_ref[p * 128:(p + 1) * 128, :]
            y_x = _mm_nt(cm_g, st.astype(BF16)) * jnp.where(low, ecum[:, h0:h0 + 1], ecum[:, h1:h1 + 1])
            w2 = jnp.where(low, wgt[:, h0:h0 + 1], wgt[:, h1:h1 + 1])
            new = jnp.where(rlow, elast[:, h0:h0 + 1], elast[:, h1:h1 + 1]) * st + _mm_tn((x_f * w2).astype(BF16), bm_g)
            st_ref[p * 128:(p + 1) * 128, :] = new
            sn_ref[0, p * 128:(p + 1) * 128, :] = new
            y = jnp.where(low, y_in[0], y_in[1]) + y_x + jnp.where(low, dskip_ref[h0], dskip_ref[h1]) * x_f
            ys.append(y * _silu(z_ref[0, :, p * 128:(p + 1) * 128].astype(F32)))
        ms = (jnp.sum(ys[0] * ys[0], axis=-1, keepdims=True) + jnp.sum(ys[1] * ys[1], axis=-1, keepdims=True)) / 256.0
        inv = lax.rsqrt(ms + EPS)
        for pp in range(2):
            p = 2 * g + pp
            o_ref[0, :, p * 128:(p + 1) * 128] = (ys[pp] * inv * ng_ref[:, p * 128:(p + 1) * 128]).astype(BF16)


def _ssd(main, dt, conv_w, conv_b, dt_bias, a_log, d_skip, norm_g, conv_past, s0, c):
    nseq, seq, _ = main.shape
    pad = lambda a: jnp.pad(a.reshape(1, H_D), ((0, 0), (0, 128 - H_D)))
    blk = lambda w, idx: pl.BlockSpec((1, c, w), lambda b, l: (b, l, idx))
    const = lambda r, w: pl.BlockSpec((r, w), lambda b, l: (0, 0))
    state = pl.BlockSpec((1, H_D * P_D, N_D), lambda b, l: (b, 0, 0))
    return pl.pallas_call(
        functools.partial(_ssd_kernel, c=c),
        grid=(nseq, seq // c),
        in_specs=[blk(512, 3), blk(512, 4), blk(256, 10), blk(256, 11), blk(128, 0),
                  const(CONV_D, CONV_DIM_D), const(1, CONV_DIM_D), const(1, 128), const(1, 128),
                  pl.BlockSpec(memory_space=pltpu.SMEM), const(1, D_INNER),
                  pl.BlockSpec((1, 8, CONV_DIM_D), lambda b, l: (b, 0, 0)), state],
        out_specs=[blk(512, 0), state],
        out_shape=[jax.ShapeDtypeStruct((nseq, seq, D_INNER), BF16),
                   jax.ShapeDtypeStruct((nseq, H_D * P_D, N_D), F32)],
        scratch_shapes=[pltpu.VMEM((c + 8, CONV_DIM_D), F32), pltpu.VMEM((H_D * P_D, N_D), F32)],
        compiler_params=_params(("parallel", "arbitrary")),
        name="ssd",
    )(main, main, main, main, dt, conv_w, conv_b.reshape(1, CONV_DIM_D), pad(dt_bias), pad(a_log), d_skip,
      norm_g.reshape(1, D_INNER), conv_past, s0)


def _ffn_kernel(x_ref, o1_ref, o2_ref, wo_ref, g1_ref, ng_ref, sc_ref, sh_ref, g2_ref, wup_ref, cw_ref, cb_ref,
                wdn_ref, past_ref, fg_ref, out_ref, tail_ref, act_ref, wb_ref, gt_ref, *, tl, d_ff, final):
    @pl.when(pl.program_id(1) == 0)
    def _():
        gt_ref[...] = past_ref[0]

    half = wo_ref.shape[0] // 2
    mix = _mm(o1_ref[0], wo_ref[0:half, :]) + _mm(o2_ref[0], wo_ref[half:2 * half, :])
    x1 = x_ref[0] + g1_ref[0] * mix
    out_ref[0] = x1
    h = _norm_mod(x1, ng_ref[...], sc_ref[0], sh_ref[0]).astype(BF16)
    for j in range(d_ff // FF_CHUNK):
        c0 = j * FF_CHUNK
        a = _mm(h, wup_ref[:, c0:c0 + FF_CHUNK])
        g = _mm(h, wup_ref[:, d_ff + c0:d_ff + c0 + FF_CHUNK])
        wb_ref[0:8, :] = gt_ref[:, c0:c0 + FF_CHUNK]
        wb_ref[8:8 + tl, :] = g
        gt_ref[:, c0:c0 + FF_CHUNK] = g[tl - 8:tl, :]
        gc = (cb_ref[:, c0:c0 + FF_CHUNK] + wb_ref[pl.ds(6, tl), :] * cw_ref[0:1, c0:c0 + FF_CHUNK]
              + wb_ref[pl.ds(7, tl), :] * cw_ref[1:2, c0:c0 + FF_CHUNK] + g * cw_ref[2:3, c0:c0 + FF_CHUNK])
        act_ref[:, c0:c0 + FF_CHUNK] = (a * _gelu_tanh(gc)).astype(BF16)
    split = (d_ff // FF_CHUNK + 1) // 2 * FF_CHUNK
    dn = _mm(act_ref[:, 0:split], wdn_ref[0:split, :]) + _mm(act_ref[:, split:d_ff], wdn_ref[split:d_ff, :])
    x2 = out_ref[0] + g2_ref[0] * dn
    if final:
        ms = jnp.mean(x2 * x2, axis=-1, keepdims=True)
        x2 = (x2 * lax.rsqrt(ms + EPS)) * fg_ref[...]
    out_ref[0] = x2
    tail_ref[0] = gt_ref[...]


def _outproj_ffn(x, o1, o2, wo, g1, ng, sc, sh, g2, wup, cw, cb, wdn, past, fg, tl, final):
    nseq, seq, d = x.shape
    d_ff = wdn.shape[0]
    tok = lambda n: pl.BlockSpec((1, tl, n), lambda b, l: (b, l, 0))
    per_seq = pl.BlockSpec((1, 1, d), lambda b, l: (b, 0, 0))
    tail = pl.BlockSpec((1, 8, d_ff), lambda b, l: (b, 0, 0))
    return pl.pallas_call(
        functools.partial(_ffn_kernel, tl=tl, d_ff=d_ff, final=final),
        grid=(nseq, seq // tl),
        in_specs=[tok(d), tok(o1.shape[-1]), tok(o2.shape[-1]), _resident(wo.shape), per_seq, _resident((1, d)),
                  per_seq, per_seq, per_seq, _resident(wup.shape), _resident(cw.shape), _resident((1, d_ff)),
                  _resident(wdn.shape), tail, _resident((1, d))],
        out_specs=[tok(d), tail],
        out_shape=[jax.ShapeDtypeStruct((nseq, seq, d), F32), jax.ShapeDtypeStruct((nseq, 8, d_ff), F32)],
        scratch_shapes=[pltpu.VMEM((tl, d_ff), BF16), pltpu.VMEM((tl + 8, FF_CHUNK), F32), pltpu.VMEM((8, d_ff), F32)],
        compiler_params=_params(("parallel", "arbitrary")),
        name="outproj_ffn",
    )(x, o1, o2, wo, g1, ng, sc, sh, g2, wup, cw, cb.reshape(1, d_ff), wdn, past, fg)


def _interleave(gens, strides):
    live = list(zip(gens, strides))
    while live:
        for item in list(live):
            for _ in range(item[1]):
                try:
                    next(item[0])
                except StopIteration:
                    live.remove(item)
                    break


def _chain(*gens):
    for g in gens:
        yield from g


def _ffn_steps(x_ref, o1, o2, wo_ref, g1_ref, ng_ref, sc_ref, sh_ref, g2_ref, wup_ref, cw_ref, cb_ref, wdn_ref,
               fg_ref, out_ref, tail_ref, act_ref, wb_ref, gt_ref, *, tl, d_ff, final):
    half = wo_ref.shape[0] // 2
    mix = _mm(o1[...], wo_ref[0:half, :]) + _mm(o2[...], wo_ref[half:2 * half, :])
    x1 = x_ref[0] + g1_ref[0] * mix
    out_ref[0] = x1
    h = _norm_mod(x1, ng_ref[...], sc_ref[0], sh_ref[0]).astype(BF16)
    yield
    for j in range(d_ff // FF_CHUNK):
        c0 = j * FF_CHUNK
        a = _mm(h, wup_ref[:, c0:c0 + FF_CHUNK])
        g = _mm(h, wup_ref[:, d_ff + c0:d_ff + c0 + FF_CHUNK])
        wb_ref[0:8, :] = gt_ref[:, c0:c0 + FF_CHUNK]
        wb_ref[8:8 + tl, :] = g
        gt_ref[:, c0:c0 + FF_CHUNK] = g[tl - 8:tl, :]
        gc = (cb_ref[:, c0:c0 + FF_CHUNK] + wb_ref[pl.ds(6, tl), :] * cw_ref[0:1, c0:c0 + FF_CHUNK]
              + wb_ref[pl.ds(7, tl), :] * cw_ref[1:2, c0:c0 + FF_CHUNK] + g * cw_ref[2:3, c0:c0 + FF_CHUNK])
        act_ref[:, c0:c0 + FF_CHUNK] = (a * _gelu_tanh(gc)).astype(BF16)
        yield
    split = (d_ff // FF_CHUNK + 1) // 2 * FF_CHUNK
    dn = _mm(act_ref[:, 0:split], wdn_ref[0:split, :])
    yield
    dn = dn + _mm(act_ref[:, split:d_ff], wdn_ref[split:d_ff, :])
    x2 = out_ref[0] + g2_ref[0] * dn
    if final:
        ms = jnp.mean(x2 * x2, axis=-1, keepdims=True)
        x2 = (x2 * lax.rsqrt(ms + EPS)) * fg_ref[...]
    out_ref[0] = x2
    tail_ref[0] = gt_ref[...]
    yield


def _band_steps(q, k_ref, v_ref, b_ref, o, t, *, tq):
    lane = lax.broadcasted_iota(jnp.int32, (tq, 128), 1)
    starts, offs = [], []
    for d in range(3):
        kt = t - 2 + d
        starts.append(pl.multiple_of(jnp.maximum(kt, 0) * tq, tq))
        offs.append(jnp.where(kt >= 0, 0.0, NEG))

    def scores(head):
        pr, hi = divmod(head, 2)
        cs = slice(pr * 128, (pr + 1) * 128)
        qh = _split_halves((q[:, cs].astype(F32) * (DH_C ** -0.5)).astype(BF16))[hi]
        ss = []
        for d in range(3):
            s = _mm_nt(qh, k_ref[0, pl.ds(starts[d], tq), cs]) + b_ref[head, :, d * tq:(d + 1) * tq]
            ss.append(s + offs[d] if d < 2 else s)
        mm = ss[2][:, 0:128]
        for d in range(3):
            for c0 in range(0, tq, 128):
                mm = jnp.maximum(mm, ss[d][:, c0:c0 + 128])
        return ss, jnp.max(mm, axis=1, keepdims=True)

    def weigh(head, ss, m):
        cs = slice((head // 2) * 128, (head // 2 + 1) * 128)
        ls = jnp.zeros((tq, 128), F32)
        acc = jnp.zeros((tq, 128), F32)
        for d in range(3):
            p = jnp.exp(ss[d] - m)
            for c0 in range(0, tq, 128):
                ls = ls + p[:, c0:c0 + 128]
            acc = acc + _mm(p.astype(BF16), v_ref[0, pl.ds(starts[d], tq), cs])
        return acc / jnp.sum(ls, axis=1, keepdims=True)

    outs = []
    pending = scores(0)
    yield
    for head in range(H_C):
        nxt = scores(head + 1) if head + 1 < H_C else None
        outs.append(weigh(head, *pending))
        pending = nxt
        if head % 2 == 1:
            cs = slice((head // 2) * 128, (head // 2 + 1) * 128)
            o[:, cs] = jnp.where(lane < 64, outs[head - 1], outs[head]).astype(BF16)
        yield


def _ssd_steps(z, xs_in, bm_in, cm_in, dt_in, cw_ref, cb_ref, dtb_ref, alog_ref, dskip_ref, ng_ref, o,
               buf_ref, st_ref, *, c):
    buf_ref[8:8 + c, 0:D_INNER] = xs_in[...].astype(F32)
    buf_ref[8:8 + c, D_INNER:D_INNER + 256] = bm_in[...].astype(F32)
    buf_ref[8:8 + c, D_INNER + 256:CONV_DIM_D] = cm_in[...].astype(F32)
    conv = cb_ref[...] + buf_ref[pl.ds(5, c), :] * cw_ref[0:1, :]
    for i in range(1, CONV_D):
        conv = conv + buf_ref[pl.ds(5 + i, c), :] * cw_ref[i:i + 1, :]
    buf_ref[0:8, :] = buf_ref[c:c + 8, :]
    act = _silu(conv)
    xs = act[:, 0:D_INNER]
    bm = act[:, D_INNER:D_INNER + 256].astype(BF16)
    cm = act[:, D_INNER + 256:CONV_DIM_D].astype(BF16)
    yield

    dt = _softplus(dt_in[...] + dtb_ref[...])
    da = dt * (-jnp.exp(alog_ref[...]))
    row = lax.broadcasted_iota(jnp.int32, (c, c), 0)
    col = lax.broadcasted_iota(jnp.int32, (c, c), 1)
    tri = row >= col
    cum = _mm_exact(tri.astype(F32), da)
    eye = (lax.broadcasted_iota(jnp.int32, (8, 128), 0) == lax.broadcasted_iota(jnp.int32, (8, 128), 1)).astype(F32)
    cum_t = _mm_exact(_mm_nt_exact(eye, da), (row <= col).astype(F32))
    dt_t = _mm_nt_exact(eye, dt)
    ecum = jnp.exp(cum)
    last = cum[c - 1:c, :]
    wgt = jnp.exp(last - cum) * dt
    elast = jnp.exp(last)
    lane = lax.broadcasted_iota(jnp.int32, (c, 128), 1)
    low = lane < 64
    rlow = lax.broadcasted_iota(jnp.int32, (128, 128), 0) < 64
    yield

    for g in range(G_D):
        bm_g = bm[:, g * 128:(g + 1) * 128]
        cm_g = cm[:, g * 128:(g + 1) * 128]
        cb = _mm_nt(cm_g, bm_g)
        ys = []
        for pp in range(2):
            p = 2 * g + pp
            h0, h1 = 2 * p, 2 * p + 1
            x_f = xs[:, p * 128:(p + 1) * 128]
            x_b = x_f.astype(BF16)
            y_in = []
            for hh in (h0, h1):
                seg = cum[:, hh:hh + 1] - cum_t[hh:hh + 1, :]
                dec = jnp.exp(jnp.where(tri, seg, NEG))
                y_in.append(_mm((cb * dec * dt_t[hh:hh + 1, :]).astype(BF16), x_b))
            st = st_ref[p * 128:(p + 1) * 128, :]
            y_x = _mm_nt(cm_g, st.astype(BF16)) * jnp.where(low, ecum[:, h0:h0 + 1], ecum[:, h1:h1 + 1])
            w2 = jnp.where(low, wgt[:, h0:h0 + 1], wgt[:, h1:h1 + 1])
            st_ref[p * 128:(p + 1) * 128, :] = (jnp.where(rlow, elast[:, h0:h0 + 1], elast[:, h1:h1 + 1]) * st
                                                + _mm_tn((x_f * w2).astype(BF16), bm_g))
            y = jnp.where(low, y_in[0], y_in[1]) + y_x + jnp.where(low, dskip_ref[h0], dskip_ref[h1]) * x_f
            ys.append(y * _silu(z[:, p * 128:(p + 1) * 128].astype(F32)))
            yield
        ms = (jnp.sum(ys[0] * ys[0], axis=-1, keepdims=True) + jnp.sum(ys[1] * ys[1], axis=-1, keepdims=True)) / 256.0
        inv = lax.rsqrt(ms + EPS)
        for pp in range(2):
            p = 2 * g + pp
            o[:, p * 128:(p + 1) * 128] = (ys[pp] * inv * ng_ref[:, p * 128:(p + 1) * 128]).astype(BF16)
    yield


def _cd_ffn_kernel(x_ref, wo_ref, g1_ref, ng_ref, sc_ref, sh_ref, g2_ref, wup_ref, cwf_ref, cbf_ref, wdn_ref, fg_ref,
                   q_ref, k_ref, v_ref, bb_ref,
                   z_ref, xs_ref, bm_ref, cm_ref, dt_ref, cws_ref, cbs_ref, dtb_ref, alog_ref, dskip_ref, ngs_ref,
                   out_ref, tail_ref, sn_ref,
                   obuf_ref, act_ref, wb_ref, gt_ref, buf_ref, st_ref,
                   *, tl, d_ff, final, n_tiles, tiles_per_seq, c, tq):
    g = pl.program_id(0)
    gm = jnp.minimum(g, n_tiles - 1)
    gf = jnp.maximum(g - 1, 0)
    jm = gm % tiles_per_seq
    jf = gf % tiles_per_seq
    slot = g % 2
    prev = (g + 1) % 2

    @pl.when(g == 0)
    def _():
        obuf_ref[...] = jnp.zeros(obuf_ref.shape, BF16)

    @pl.when(jm == 0)
    def _():
        buf_ref[0:8, :] = jnp.zeros((8, CONV_DIM_D), F32)
        st_ref[...] = jnp.zeros(st_ref.shape, F32)

    @pl.when(jf == 0)
    def _():
        gt_ref[...] = jnp.zeros(gt_ref.shape, F32)

    ffn = _ffn_steps(x_ref, obuf_ref.at[prev, :, 0:512], obuf_ref.at[prev, :, 512:1024], wo_ref, g1_ref, ng_ref,
                     sc_ref, sh_ref, g2_ref, wup_ref, cwf_ref, cbf_ref, wdn_ref, fg_ref, out_ref, tail_ref, act_ref,
                     wb_ref, gt_ref, tl=tl, d_ff=d_ff, final=final)
    band = _chain(*[_band_steps(q_ref.at[0, ti * tq:(ti + 1) * tq, :], k_ref, v_ref, bb_ref,
                                obuf_ref.at[slot, ti * tq:(ti + 1) * tq, 0:512],
                                jm * (tl // tq) + ti, tq=tq) for ti in range(tl // tq)])
    ssd = _chain(*[_ssd_steps(z_ref.at[0, ci * c:(ci + 1) * c, :], xs_ref.at[0, ci * c:(ci + 1) * c, :],
                              bm_ref.at[0, ci * c:(ci + 1) * c, :], cm_ref.at[0, ci * c:(ci + 1) * c, :],
                              dt_ref.at[0, ci * c:(ci + 1) * c, :], cws_ref, cbs_ref, dtb_ref, alog_ref, dskip_ref,
                              ngs_ref, obuf_ref.at[slot, ci * c:(ci + 1) * c, 512:1024], buf_ref, st_ref, c=c)
                   for ci in range(tl // c)])
    _interleave([ffn, _chain(band, ssd)], (1, 1))

    @pl.when(g < n_tiles)
    def _():
        sn_ref[0] = st_ref[...]


def _cd_mixers_ffn(x, main, dt, bias, wo, g1, ng, sc, sh, g2, wup, cwf, cbf, wdn, fg,
                   conv_w, conv_b, dt_bias, a_log, d_skip, norm_g, tl, final):
    nseq, seq, d = x.shape
    d_ff = wdn.shape[0]
    tps = seq // tl
    n_tiles = nseq * tps
    c = SCAN_CHUNK
    tq = BAND_TQ
    pad = lambda a: jnp.pad(a.reshape(1, H_D), ((0, 0), (0, 128 - H_D)))
    mix = lambda g: jnp.minimum(g, n_tiles - 1)
    ffn = lambda g: jnp.maximum(g - 1, 0)
    tok_f = lambda n: pl.BlockSpec((1, tl, n), lambda g: (ffn(g) // tps, ffn(g) % tps, 0))
    seq_f = lambda n: pl.BlockSpec((1, 1, n), lambda g: (ffn(g) // tps, 0, 0))
    tok_m = lambda w, idx: pl.BlockSpec((1, tl, w), lambda g: (mix(g) // tps, mix(g) % tps, idx))
    full_m = lambda idx: pl.BlockSpec((1, seq, 512), lambda g: (mix(g) // tps, 0, idx),
                                      pipeline_mode=pl.Buffered(1))
    const = lambda r, w: pl.BlockSpec((r, w), lambda g: (0, 0))
    return pl.pallas_call(
        functools.partial(_cd_ffn_kernel, tl=tl, d_ff=d_ff, final=final, n_tiles=n_tiles, tiles_per_seq=tps,
                          c=c, tq=tq),
        grid=(n_tiles + 1,),
        in_specs=[tok_f(d), _resident(wo.shape), seq_f(d), _resident((1, d)), seq_f(d), seq_f(d), seq_f(d),
                  _resident(wup.shape), _resident(cwf.shape), _resident((1, d_ff)), _resident(wdn.shape),
                  _resident((1, d)),
                  tok_m(512, 0), full_m(1), full_m(2), _resident(bias.shape),
                  tok_m(512, 3), tok_m(512, 4), tok_m(256, 10), tok_m(256, 11), tok_m(128, 0),
                  const(CONV_D, CONV_DIM_D), const(1, CONV_DIM_D), const(1, 128), const(1, 128),
                  pl.BlockSpec(memory_space=pltpu.SMEM), const(1, D_INNER)],
        out_specs=[tok_f(d),
                   pl.BlockSpec((1, 8, d_ff), lambda g: (ffn(g) // tps, 0, 0)),
                   pl.BlockSpec((1, H_D * P_D, N_D), lambda g: (mix(g) // tps, 0, 0))],
        out_shape=[jax.ShapeDtypeStruct((nseq, seq, d), F32), jax.ShapeDtypeStruct((nseq, 8, d_ff), F32),
                   jax.ShapeDtypeStruct((nseq, H_D * P_D, N_D), F32)],
        scratch_shapes=[pltpu.VMEM((2, tl, 1024), BF16), pltpu.VMEM((tl, d_ff), BF16),
                        pltpu.VMEM((tl + 8, FF_CHUNK), F32), pltpu.VMEM((8, d_ff), F32),
                        pltpu.VMEM((c + 8, CONV_DIM_D), F32), pltpu.VMEM((H_D * P_D, N_D), F32)],
        compiler_params=_params(("arbitrary",)),
        name="cd_mixers_ffn",
    )(x, wo, g1, ng, sc, sh, g2, wup, cwf, cbf.reshape(1, d_ff), wdn, fg,
      main, main, main, bias,
      main, main, main, main, dt, conv_w, conv_b.reshape(1, CONV_DIM_D), pad(dt_bias), pad(a_log), d_skip,
      norm_g.reshape(1, D_INNER))


def _rope_tables(pos):
    half = DK_A // 2
    inv = jnp.power(ROPE_BASE, -jnp.arange(half, dtype=F32) / half)
    ang = pos.astype(F32)[:, None] * inv[None, :]
    cos = jnp.cos(ang)
    sin = jnp.sin(ang)
    return jnp.concatenate([cos, cos], axis=1), jnp.concatenate([-sin, sin], axis=1)


def _pad_rows(a, rows=8):
    return jnp.pad(a, ((0, 0), (rows - a.shape[1], 0), (0, 0)))


def _trunk(x, mods, pos, weights, caches, biases):
    nseq, seq, d = x.shape
    sample = caches is not None
    tl = min(seq, 512)
    c = min(seq, SCAN_CHUNK)
    depth = weights["w_up"].shape[0]
    outs = {k: [] for k in ("ret", "bk", "bv", "ck", "cv", "dconv", "dssm", "ffn")}
    cos, sin = _rope_tables(pos)
    for l in range(depth):
        i = l // 2
        sh1, sc1, g1, sh2, sc2, g2 = mods[l]
        ng1 = weights["norm_g"][l, 0].reshape(1, d)
        ng2 = weights["norm_g"][l, 1].reshape(1, d)
        fused = None
        if l % 2 == 0:
            qkv, kb, vb = _inproj_ab(x, ng1, sc1, sh1, weights["w_in_ab"][i], cos, sin, tl)
            s0 = caches["ret"][i] if sample else jnp.zeros((nseq, H_A, DK_A, DV_A), F32)
            o1, s_new = _retention(qkv, s0, weights["ret_gn"][i], c)
            lam_init = 0.8 - 0.6 * math.exp(-0.3 * l)
            if sample:
                o2 = _diff_attn_sample(qkv, caches["bk"][i], caches["bv"][i], biases["t5"],
                                       weights["lam_q"][i], weights["lam_k"][i], weights["diff_gn"][i], lam_init)
            else:
                o2 = _diff_attn(qkv, biases["t5"], weights["lam_q"][i], weights["lam_k"][i],
                                weights["diff_gn"][i], lam_init)
            wo = weights["w_out_ab"][i]
            outs["ret"].append(s_new)
            outs["bk"].append(kb.reshape(nseq, seq, H_B, 2 * DH_B))
            outs["bv"].append(vb.reshape(nseq, seq, H_B, DV_B))
        else:
            keep = seq if sample else min(C_WINDOW, seq)
            main, dt, kc, vc, tail = _inproj_cd(x, ng1, sc1, sh1, weights["w_in_cd"][i], tl, keep)
            if sample:
                o1 = _band_attn_sample(main, caches["ck"][i], caches["cv"][i], biases["band"][i])
                conv_past = _pad_rows(caches["dconv"][i])
                s0 = caches["dssm"][i].reshape(nseq, H_D * P_D, N_D)
                o2, ssm_new = _ssd(main, dt, weights["d_conv_w"][i], weights["d_conv_b"][i], weights["d_dt_bias"][i],
                                   weights["d_a_log"][i], weights["d_skip"][i], weights["d_norm_g"][i], conv_past, s0,
                                   c)
            else:
                fused = _cd_mixers_ffn(
                    x, main, dt, biases["band"][i], weights["w_out_cd"][i], g1, ng2, sc2, sh2, g2, weights["w_up"][l],
                    weights["ffn_conv_w"][l], weights["ffn_conv_b"][l], weights["w_down"][l],
                    weights["final_g"].reshape(1, d), weights["d_conv_w"][i], weights["d_conv_b"][i],
                    weights["d_dt_bias"][i], weights["d_a_log"][i], weights["d_skip"][i], weights["d_norm_g"][i],
                    FUSED_TL, final=(l == depth - 1))
                ssm_new = fused[2]
            wo = weights["w_out_cd"][i]
            outs["ck"].append(kc.reshape(nseq, keep, H_C, DH_C))
            outs["cv"].append(vc.reshape(nseq, keep, H_C, DH_C))
            outs["dconv"].append(tail[:, 8 - (CONV_D - 1):])
            outs["dssm"].append(ssm_new.reshape(nseq, H_D, P_D, N_D))
        d_ff = weights["w_down"].shape[1]
        if fused is not None:
            x, ftail = fused[0], fused[1]
        else:
            ffn_past = _pad_rows(caches["ffn"][l]) if sample else jnp.zeros((nseq, 8, d_ff), F32)
            x, ftail = _outproj_ffn(x, o1, o2, wo, g1, ng2, sc2, sh2, g2, weights["w_up"][l],
                                    weights["ffn_conv_w"][l], weights["ffn_conv_b"][l], weights["w_down"][l], ffn_past,
                                    weights["final_g"].reshape(1, d), tl, final=(l == depth - 1))
        outs["ffn"].append(ftail[:, 8 - (CONV_F - 1):])
    stk = lambda t: jnp.stack(t).astype(F32)
    return (x,) + tuple(stk(outs[k]) for k in ("ret", "bk", "bv", "ck", "cv", "dconv", "dssm", "ffn"))


def kernel(x_prompt, x_sample, cache_ret_state, cache_b_k, cache_b_v, cache_c_k, cache_c_v, state_d_conv, state_d_ssm, state_ffn_conv, c_prompt, c_sample, w_mod, b_mod, norm_g, final_g, t5_table, w_in_ab, w_out_ab, ret_gn, lam_q, lam_k, diff_gn, w_in_cd, w_out_cd, rel_table, d_conv_w, d_conv_b, d_dt_bias, d_a_log, d_skip, d_norm_g, w_up, ffn_conv_w, ffn_conv_b, w_down):
    batch, seq, d = x_prompt.shape
    dec_batch, dec_seq, _ = x_sample.shape
    past = cache_b_k.shape[2]
    depth = w_mod.shape[0]
    assert dec_seq <= CHUNK and past % CHUNK == 0 and cache_c_k.shape[2] == C_WINDOW
    assert seq % BAND_TQ == 0 and seq % DIFF_TQ == 0

    w_in_cd_p = jnp.pad(w_in_cd, ((0, 0), (0, 0), (0, N_CD_PAD - w_in_cd.shape[-1]))).astype(BF16)
    weights = dict(
        norm_g=norm_g, final_g=final_g, w_in_ab=w_in_ab.astype(BF16), w_out_ab=w_out_ab.astype(BF16),
        ret_gn=ret_gn, lam_q=lam_q, lam_k=lam_k, diff_gn=diff_gn, w_in_cd=w_in_cd_p,
        w_out_cd=w_out_cd.astype(BF16), d_conv_w=d_conv_w, d_conv_b=d_conv_b, d_dt_bias=d_dt_bias, d_a_log=d_a_log,
        d_skip=d_skip, d_norm_g=d_norm_g, w_up=w_up.astype(BF16), ffn_conv_w=ffn_conv_w, ffn_conv_b=ffn_conv_b,
        w_down=w_down.astype(BF16))

    mod = _modulation(jnp.concatenate([c_prompt, c_sample], axis=0), w_mod, b_mod)

    def pieces(rows):
        return [[m[:, None, :] for m in jnp.split(mod[l, rows], 6, axis=-1)] for l in range(depth)]

    band = [_band_bias(rel_table[i]) for i in range(rel_table.shape[0])]
    biases_p = dict(t5=_t5_bias_prompt(t5_table, seq), band=band)
    biases_s = dict(t5=_t5_bias_sample(t5_table, past, dec_seq),
                    band=[b[:, :dec_seq, :C_WINDOW + dec_seq] for b in band])
    caches = dict(
        ret=cache_ret_state,
        bk=cache_b_k.reshape(cache_b_k.shape[0], dec_batch, past, H_B * 2 * DH_B),
        bv=cache_b_v.reshape(cache_b_v.shape[0], dec_batch, past, H_B * DV_B),
        ck=cache_c_k.reshape(cache_c_k.shape[0], dec_batch, C_WINDOW, H_C * DH_C),
        cv=cache_c_v.reshape(cache_c_v.shape[0], dec_batch, C_WINDOW, H_C * DH_C),
        dconv=state_d_conv, dssm=state_d_ssm, ffn=state_ffn_conv)

    pos_p = jnp.arange(seq, dtype=jnp.int32)
    pos_s = past + jnp.arange(dec_seq, dtype=jnp.int32)
    y_p, ret_p, bk_p, bv_p, ck_p, cv_p, dconv_p, dssm_p, ffn_p = _trunk(
        x_prompt, pieces(slice(0, batch)), pos_p, weights, None, biases_p)
    y_s, ret_s, bk_s, bv_s, ck_s, cv_s, dconv_s, dssm_s, ffn_s = _trunk(
        x_sample, pieces(slice(batch, batch + dec_batch)), pos_s, weights, caches, biases_s)
    return (y_p, y_s, ret_p, ret_s, bk_p, bk_s, bv_p, bv_s, ck_p, ck_s, cv_p, cv_s,
            dconv_p, dconv_s, dssm_p, dssm_s, ffn_p, ffn_s)
```

```python
import functools
import math

import numpy as np
import jax
import jax.numpy as jnp
from jax import lax
from jax.experimental import pallas as pl
from jax.experimental.pallas import tpu as pltpu

F32 = jnp.float32
BF16 = jnp.bfloat16

CHUNK = 64
EPS = 1e-6
NEG = -1e30
H_A, DK_A, DV_A = 4, 128, 128
ROPE_BASE = 10000.0
H_B, DH_B, DV_B = 4, 64, 128
T5_BUCKETS, T5_MAX_DIST = 32, 128
H_C, DH_C = 8, 64
BAND_CHUNKS = 8
C_WINDOW = BAND_CHUNKS * CHUNK
REL_CLIP = 128
H_D, P_D, G_D, N_D = 8, 64, 2, 128
D_INNER = H_D * P_D
CONV_D = 4
CONV_DIM_D = D_INNER + 2 * G_D * N_D
CONV_F = 3
N_AB_COLS = 7 * 512
N_CD_MAIN = 6 * 512
N_CD_PAD = N_CD_MAIN + 128
FF_CHUNK = 256
BAND_TQ = 256
DIFF_TQ = 256
SCAN_CHUNK = 256
FUSED_TL = 256
VMEM_LIMIT = 56 * 1024 * 1024


def _mm(a, b):
    return jnp.dot(a, b, preferred_element_type=F32)


def _mm_nt(a, b):
    return lax.dot_general(a, b, (((1,), (1,)), ((), ())), preferred_element_type=F32)


def _mm_tn(a, b):
    return lax.dot_general(a, b, (((0,), (0,)), ((), ())), preferred_element_type=F32)


def _mm_exact(a, b):
    return jnp.dot(a, b, preferred_element_type=F32, precision=lax.Precision.HIGHEST)


def _mm_nt_exact(a, b):
    return lax.dot_general(a, b, (((1,), (1,)), ((), ())), preferred_element_type=F32,
                           precision=lax.Precision.HIGHEST)


def _silu(x):
    return x * (1.0 / (1.0 + jnp.exp(-x)))


def _softplus(x):
    return jnp.maximum(x, 0.0) + jnp.log1p(jnp.exp(-jnp.abs(x)))


def _gelu_tanh(x):
    k1 = -2.0 * math.sqrt(2.0 / math.pi) * math.log2(math.e)
    return x * (1.0 / (1.0 + jnp.exp2(x * (k1 + (k1 * 0.044715) * (x * x)))))


def _norm_mod(x, g, sc, sh):
    ms = jnp.mean(x * x, axis=-1, keepdims=True)
    return (x * lax.rsqrt(ms + EPS)) * g * (1.0 + sc) + sh


def _params(sem):
    return pltpu.CompilerParams(dimension_semantics=sem, vmem_limit_bytes=VMEM_LIMIT)


def _resident(shape):
    nd = len(shape)
    return pl.BlockSpec(shape, lambda *_: (0,) * nd, pipeline_mode=pl.Buffered(1))


def _mod_kernel(c_ref, w_ref, b_ref, o_ref):
    c = c_ref[...]
    o_ref[0] = _mm(_silu(c).astype(BF16), w_ref[0].astype(BF16)) + b_ref[0]


def _modulation(c_all, w_mod, b_mod):
    depth, d, n = w_mod.shape
    r = c_all.shape[0]
    tn = 1536
    return pl.pallas_call(
        _mod_kernel,
        grid=(depth, n // tn),
        in_specs=[pl.BlockSpec((r, d), lambda l, j: (0, 0)),
                  pl.BlockSpec((1, d, tn), lambda l, j: (l, 0, j)),
                  pl.BlockSpec((1, 1, tn), lambda l, j: (l, 0, j))],
        out_specs=pl.BlockSpec((1, r, tn), lambda l, j: (l, 0, j)),
        out_shape=jax.ShapeDtypeStruct((depth, r, n), F32),
        compiler_params=_params(("parallel", "parallel")),
        name="modulation",
    )(c_all, w_mod, b_mod.reshape(depth, 1, n))


def _table_gather_kernel(tab_ref, idx_ref, o_ref, *, n_entries, n_heads):
    idx = idx_ref[...]

    def body(r, accs):
        m = idx == r
        return tuple(jnp.where(m, tab_ref[hh, r], a) for hh, a in enumerate(accs))

    accs = lax.fori_loop(0, n_entries, body, tuple(jnp.zeros(idx.shape, F32) for _ in range(n_heads)))
    for hh in range(n_heads):
        o_ref[hh:hh + 1, :] = accs[hh]


def _table_gather(table, idx):
    t, h = table.shape
    w = idx.shape[0]
    return pl.pallas_call(
        functools.partial(_table_gather_kernel, n_entries=t, n_heads=h),
        in_specs=[pl.BlockSpec(memory_space=pltpu.SMEM),
                  pl.BlockSpec((1, w), lambda: (0, 0))],
        out_specs=pl.BlockSpec((h, w), lambda: (0, 0)),
        out_shape=jax.ShapeDtypeStruct((h, w), F32),
        name="table_gather",
    )(table.T, idx.reshape(1, w))


def _toeplitz_kernel(v_ref, o_ref, *, rows, cols, tile):
    x = pltpu.roll(jnp.broadcast_to(v_ref[0], (rows, v_ref.shape[2])), 0, 1, stride=1, stride_axis=0)
    if tile is None:
        o_ref[0] = x[:, :cols]
    else:
        for jb in range(cols // tile):
            o_ref[0, jb] = x[:, jb * tile:(jb + 1) * tile]


def _toeplitz(vec, rows, cols, tile=None):
    h, wv = vec.shape
    if tile is None:
        out_shape, block, imap = (h, rows, cols), (1, rows, cols), lambda i: (i, 0, 0)
    else:
        out_shape, block, imap = (h, cols // tile, rows, tile), (1, cols // tile, rows, tile), lambda i: (i, 0, 0, 0)
    return pl.pallas_call(
        functools.partial(_toeplitz_kernel, rows=rows, cols=cols, tile=tile),
        grid=(h,),
        in_specs=[pl.BlockSpec((1, 1, wv), lambda i: (i, 0, 0))],
        out_specs=pl.BlockSpec(block, imap),
        out_shape=jax.ShapeDtypeStruct(out_shape, F32),
        compiler_params=_params(("parallel",)),
        name="toeplitz",
    )(vec.reshape(h, 1, wv))


def _t5_bucket(rel):
    half = T5_BUCKETS // 2
    max_exact = half // 2
    base = jnp.where(rel > 0, half, 0)
    n = jnp.abs(rel)
    nf = jnp.maximum(n, 1).astype(F32)
    large = max_exact + (jnp.log(nf / max_exact) / math.log(T5_MAX_DIST / max_exact) * (half - max_exact)).astype(jnp.int32)
    large = jnp.minimum(large, half - 1)
    return base + jnp.where(n < max_exact, n, large)


def _wrapped_offsets(n_pos, n_neg):
    p = jnp.arange(n_pos + n_neg, dtype=jnp.int32)
    return jnp.where(p < n_pos, p, p - (n_pos + n_neg))


def _t5_bias_prompt(t5_table, seq):
    nd = seq // DIFF_TQ
    u = _wrapped_offsets(seq, DIFF_TQ)
    vec = _table_gather(t5_table, _t5_bucket(u - (seq - DIFF_TQ)))
    return _toeplitz(vec, DIFF_TQ, seq, tile=DIFF_TQ)


def _t5_bias_sample(t5_table, past, lq):
    u = _wrapped_offsets(past + lq, lq)
    vec = _table_gather(t5_table, _t5_bucket(u - past))
    return _toeplitz(vec, lq, past + 2 * lq)[:, :, :past + lq]


def _band_bias(rel_table):
    u = _wrapped_offsets(3 * BAND_TQ, BAND_TQ)
    idx = jnp.clip(2 * BAND_TQ - u, -REL_CLIP, REL_CLIP) + REL_CLIP
    bias = _toeplitz(_table_gather(rel_table, idx), BAND_TQ, 3 * BAND_TQ)
    dc = (np.arange(BAND_TQ)[:, None] // CHUNK + C_WINDOW // CHUNK) - np.arange(3 * BAND_TQ)[None, :] // CHUNK
    return jnp.where(jnp.asarray((dc >= 0) & (dc <= BAND_CHUNKS))[None], bias, NEG)


def _inproj_ab_kernel(x_ref, g_ref, sc_ref, sh_ref, w_ref, cos_ref, sin_ref, qkv_ref, kb_ref, vb_ref, *, tl):
    h = _norm_mod(x_ref[0], g_ref[...], sc_ref[0], sh_ref[0]).astype(BF16)
    cos = cos_ref[...]
    sin = sin_ref[...]
    for j in range(7):
        y = _mm(h, w_ref[:, j * 512:(j + 1) * 512])
        if j < 2:
            for hh in range(H_A):
                yh = y[:, hh * 128:(hh + 1) * 128]
                yh = yh * cos + pltpu.roll(yh, 64, 1) * sin
                if j == 1:
                    yh = yh * (DK_A ** -0.5)
                qkv_ref[0, :, j * 512 + hh * 128:j * 512 + (hh + 1) * 128] = yh.astype(BF16)
        else:
            if j >= 5:
                dst = kb_ref if j == 5 else vb_ref
                for hh in range(H_B):
                    dst[0, pl.ds(hh, tl, stride=H_B), :] = y[:, hh * 128:(hh + 1) * 128]
            qkv_ref[0, :, j * 512:(j + 1) * 512] = y.astype(BF16)


def _inproj_ab(x, g, sc, sh, w, cos, sin, tl):
    nseq, seq, d = x.shape
    grid = (nseq, seq // tl)
    tok = lambda n: pl.BlockSpec((1, tl, n), lambda b, l: (b, l, 0))
    per_seq = pl.BlockSpec((1, 1, d), lambda b, l: (b, 0, 0))
    cache = pl.BlockSpec((1, tl * H_B, 128), lambda b, l: (b, l, 0))
    return pl.pallas_call(
        functools.partial(_inproj_ab_kernel, tl=tl),
        grid=grid,
        in_specs=[tok(d), _resident((1, d)), per_seq, per_seq, _resident((d, N_AB_COLS)),
                  pl.BlockSpec((tl, 128), lambda b, l: (l, 0)),
                  pl.BlockSpec((tl, 128), lambda b, l: (l, 0))],
        out_specs=[tok(N_AB_COLS), cache, cache],
        out_shape=[jax.ShapeDtypeStruct((nseq, seq, N_AB_COLS), BF16),
                   jax.ShapeDtypeStruct((nseq, seq * H_B, 128), F32),
                   jax.ShapeDtypeStruct((nseq, seq * H_B, 128), F32)],
        compiler_params=_params(("parallel", "parallel")),
        name="inproj_ab",
    )(x, g, sc, sh, w, cos, sin)


def _retention_kernel(q_ref, k_ref, v_ref, g_ref, s0_ref, dec_ref, qd_ref, kd_ref, gc_ref, gn_ref,
                      o_ref, sn_ref, *, c, n_chunks):
    states = [s0_ref[0, h] for h in range(H_A)]
    for ci in range(n_chunks):
        rows = slice(ci * c, (ci + 1) * c)
        for h in range(H_A):
            cs = slice(h * 128, (h + 1) * 128)
            q = q_ref[0, rows, cs]
            k = k_ref[0, rows, cs]
            v = v_ref[0, rows, cs]
            state = states[h]
            s = _mm_nt(q, k) * dec_ref[h]
            o = _mm(s.astype(BF16), v) + _mm((q.astype(F32) * qd_ref[h]).astype(BF16), state.astype(BF16))
            states[h] = gc_ref[h] * state + _mm_tn((k.astype(F32) * kd_ref[h]).astype(BF16), v)
            mu = jnp.mean(o, axis=-1, keepdims=True)
            dlt = o - mu
            var = jnp.mean(dlt * dlt, axis=-1, keepdims=True)
            y = dlt * lax.rsqrt(var + EPS) * gn_ref[:, cs] * _silu(g_ref[0, rows, cs].astype(F32))
            o_ref[0, rows, cs] = y.astype(BF16)
    for h in range(H_A):
        sn_ref[0, h] = states[h]


def _retention_consts(c):
    lg = np.log1p(-np.exp2(-5.0 - np.arange(H_A, dtype=np.float32))).astype(np.float32)
    idx = np.arange(c, dtype=np.float32)
    diff = idx[:, None] - idx[None, :]
    decay = np.where(diff[None] >= 0, np.exp(np.maximum(diff, 0.0)[None] * lg[:, None, None]), 0.0)
    qd = np.exp((idx + 1.0)[None, :] * lg[:, None])
    kd = np.exp((c - 1.0 - idx)[None, :] * lg[:, None])
    gc = np.exp(c * lg)
    bc = lambda a: np.ascontiguousarray(np.broadcast_to(a[..., None], a.shape + (128,))).astype(np.float32)
    return decay.astype(np.float32), bc(qd), bc(kd), bc(gc[:, None])


def _retention(qkv, s0, ret_gn, c):
    nseq, seq, _ = qkv.shape
    decay, qd, kd, gc = _retention_consts(c)
    w = H_A * 128
    col = lambda idx: pl.BlockSpec((1, seq, w), lambda b: (b, 0, idx))
    const = lambda shape: pl.BlockSpec(shape, lambda b: (0,) * len(shape))
    state = pl.BlockSpec((1, H_A, DK_A, DV_A), lambda b: (b, 0, 0, 0))
    return pl.pallas_call(
        functools.partial(_retention_kernel, c=c, n_chunks=seq // c),
        grid=(nseq,),
        in_specs=[col(0), col(1), col(2), col(3), state, const((H_A, c, c)), const((H_A, c, 128)),
                  const((H_A, c, 128)), const((H_A, 1, 128)), const((1, w))],
        out_specs=[col(0), state],
        out_shape=[jax.ShapeDtypeStruct((nseq, seq, w), BF16),
                   jax.ShapeDtypeStruct((nseq, H_A, DK_A, DV_A), F32)],
        compiler_params=_params(("parallel",)),
        name="retention",
    )(qkv, qkv, qkv, qkv, s0, decay, qd, kd, gc, ret_gn.reshape(1, w))


def _split_halves(q):
    lane = lax.broadcasted_iota(jnp.int32, q.shape, 1)
    zero = jnp.zeros_like(q)
    return jnp.where(lane < 64, q, zero), jnp.where(lane >= 64, q, zero)


def _lambda(lq_ref, lk_ref, lam_init):
    e = jnp.exp(jnp.sum(lq_ref[...] * lk_ref[...], axis=1, keepdims=True))
    return e[0:1, :] - e[1:2, :] + lam_init


def _diff_epilogue(o, gn, lam_init):
    ms = jnp.mean(o * o, axis=-1, keepdims=True)
    return (o * lax.rsqrt(ms + EPS)) * gn * (1.0 - lam_init)


def _diff_attn_kernel(q_ref, k_ref, v_ref, b_ref, lq_ref, lk_ref, gn_ref, o_ref, s_ref, qs_ref, mx_ref, ls_ref,
                      acc_ref, *, tq, nd, lam_init):
    row = lax.broadcasted_iota(jnp.int32, (tq, tq), 0) // CHUNK
    col = lax.broadcasted_iota(jnp.int32, (tq, tq), 1) // CHUNK
    chunk_mask = col <= row
    lam = _lambda(lq_ref, lk_ref, lam_init)

    def prepare(t):
        q = q_ref[0, t * tq:(t + 1) * tq, :]
        halves = _split_halves((q.astype(F32) * (DH_B ** -0.5)).astype(BF16))
        for i in range(2):
            r = 2 * (t % 2) + i
            qs_ref[r] = halves[i]
            mx_ref[r] = jnp.full((tq, 128), NEG, F32)
            ls_ref[r] = jnp.zeros((tq, 128), F32)
            acc_ref[r] = jnp.zeros((tq, DV_B), F32)

    def scores(t, kb):
        k = k_ref[0, kb * tq:(kb + 1) * tq, :]
        b = b_ref[0, nd - 1 - t + kb]
        for i in range(2):
            r = 2 * (t % 2) + i
            s = _mm_nt(qs_ref[r], k) + b
            if kb == t:
                s = jnp.where(chunk_mask, s, NEG)
            s_ref[r * nd + kb] = s
            m = mx_ref[r]
            for c0 in range(0, tq, 128):
                m = jnp.maximum(m, s[:, c0:c0 + 128])
            mx_ref[r] = m

    def row_max(t):
        for i in range(2):
            r = 2 * (t % 2) + i
            mx_ref[r] = jnp.broadcast_to(jnp.max(mx_ref[r], axis=1, keepdims=True), (tq, 128))

    def weigh(t, kb):
        v = v_ref[0, kb * tq:(kb + 1) * tq, :]
        for i in range(2):
            r = 2 * (t % 2) + i
            m = mx_ref[r]
            l = ls_ref[r]
            ps = []
            for c0 in range(0, tq, 128):
                p = jnp.exp(s_ref[r * nd + kb, :, c0:c0 + 128] - m)
                l = l + p
                ps.append(p.astype(BF16))
            ls_ref[r] = l
            acc_ref[r] += _mm(jnp.concatenate(ps, axis=1), v)

    def finish(t):
        r = 2 * (t % 2)
        l0 = jnp.sum(ls_ref[r], axis=1, keepdims=True)
        l1 = jnp.sum(ls_ref[r + 1], axis=1, keepdims=True)
        o = acc_ref[r] / l0 - lam * (acc_ref[r + 1] / l1)
        o_ref[0, t * tq:(t + 1) * tq, :] = _diff_epilogue(o, gn_ref[...], lam_init).astype(BF16)

    last = nd - 1
    prepare(last)
    for kb in range(last + 1):
        scores(last, kb)
    row_max(last)
    for t in range(last - 1, -1, -1):
        prepare(t)
        for kb in range(t + 2):
            if kb <= t:
                scores(t, kb)
            weigh(t + 1, kb)
        finish(t + 1)
        row_max(t)
    weigh(0, 0)
    finish(0)


def _diff_attn(qkv, bias, lam_q, lam_k, diff_gn, lam_init):
    nseq, seq, _ = qkv.shape
    tq = DIFF_TQ
    nd = seq // tq
    full = lambda off: pl.BlockSpec((1, seq, 128), lambda b, h: (b, 0, off + h))
    small = lambda r, c: pl.BlockSpec((r, c), lambda b, h: (0, 0))
    return pl.pallas_call(
        functools.partial(_diff_attn_kernel, tq=tq, nd=nd, lam_init=lam_init),
        grid=(nseq, H_B),
        in_specs=[full(16), full(20), full(24),
                  pl.BlockSpec((1, nd, tq, tq), lambda b, h: (h, 0, 0, 0)),
                  small(2, DH_B), small(2, DH_B), small(1, DV_B)],
        out_specs=full(0),
        out_shape=jax.ShapeDtypeStruct((nseq, seq, H_B * DV_B), BF16),
        scratch_shapes=[pltpu.VMEM((4 * nd, tq, tq), F32), pltpu.VMEM((4, tq, 128), BF16),
                        pltpu.VMEM((4, tq, 128), F32), pltpu.VMEM((4, tq, 128), F32),
                        pltpu.VMEM((4, tq, DV_B), F32)],
        compiler_params=_params(("parallel", "parallel")),
        name="diff_attn",
    )(qkv, qkv, qkv, bias, lam_q, lam_k, diff_gn.reshape(1, DV_B))


def _diff_attn_sample_kernel(q_ref, kc_ref, vc_ref, kp_ref, vp_ref, b_ref, lq_ref, lk_ref, gn_ref, o_ref,
                             *, past, lam_init):
    scale = DH_B ** -0.5
    lam = _lambda(lq_ref, lk_ref, lam_init)
    for h in range(H_B):
        cs = slice(h * 128, (h + 1) * 128)
        qs = _split_halves(q_ref[0, :, cs])
        kp = kp_ref[0, :, cs].astype(BF16)
        vp = vp_ref[0, :, cs].astype(BF16)
        kc = kc_ref[0, :, cs]
        vc = vc_ref[0, :, cs]
        b = b_ref[h]
        probs = []
        for i in range(2):
            sp = _mm_nt(qs[i], kp) * scale + b[:, :past]
            sc = _mm_nt(qs[i], kc) * scale + b[:, past:]
            m = jnp.maximum(jnp.max(sp, axis=1, keepdims=True), jnp.max(sc, axis=1, keepdims=True))
            pp = jnp.exp(sp - m)
            pc = jnp.exp(sc - m)
            l = jnp.sum(pp, axis=1, keepdims=True) + jnp.sum(pc, axis=1, keepdims=True)
            probs.append((pp / l, pc / l))
        ap = probs[0][0] - lam * probs[1][0]
        ac = probs[0][1] - lam * probs[1][1]
        o = _mm(ap.astype(BF16), vp) + _mm(ac.astype(BF16), vc)
        o_ref[0, :, cs] = _diff_epilogue(o, gn_ref[...], lam_init).astype(BF16)


def _diff_attn_sample(qkv, k_past, v_past, bias, lam_q, lam_k, diff_gn, lam_init):
    nseq, lq, _ = qkv.shape
    past = k_past.shape[1]
    w = H_B * DV_B
    cur = lambda idx: pl.BlockSpec((1, lq, w), lambda b: (b, 0, idx))
    old = pl.BlockSpec((1, past, w), lambda b: (b, 0, 0))
    small = lambda r, c: pl.BlockSpec((r, c), lambda b: (0, 0))
    return pl.pallas_call(
        functools.partial(_diff_attn_sample_kernel, past=past, lam_init=lam_init),
        grid=(nseq,),
        in_specs=[cur(4), cur(5), cur(6), old, old, _resident(bias.shape),
                  small(2, DH_B), small(2, DH_B), small(1, DV_B)],
        out_specs=cur(0),
        out_shape=jax.ShapeDtypeStruct((nseq, lq, w), BF16),
        compiler_params=_params(("parallel",)),
        name="diff_attn_sample",
    )(qkv, qkv, qkv, k_past, v_past, bias, lam_q, lam_k, diff_gn.reshape(1, DV_B))


def _inproj_cd_kernel(x_ref, g_ref, sc_ref, sh_ref, w_ref, main_ref, dt_ref, kc_ref, vc_ref, tail_ref, *, tl):
    h = _norm_mod(x_ref[0], g_ref[...], sc_ref[0], sh_ref[0]).astype(BF16)
    for j in range(6):
        y = _mm(h, w_ref[:, j * 512:(j + 1) * 512])
        main_ref[0, :, j * 512:(j + 1) * 512] = y.astype(BF16)
        if j == 1:
            kc_ref[0] = y
        if j == 2:
            vc_ref[0] = y
        if j >= 4:
            tail_ref[0, :, (j - 4) * 512:(j - 3) * 512] = y[tl - 8:tl, :]
    dt_ref[0] = _mm(h, w_ref[:, N_CD_MAIN:N_CD_PAD])


def _inproj_cd(x, g, sc, sh, w, tl, keep):
    nseq, seq, d = x.shape
    assert keep % tl == 0
    skip = (seq - keep) // tl
    tok = lambda n: pl.BlockSpec((1, tl, n), lambda b, l: (b, l, 0))
    kept = pl.BlockSpec((1, tl, 512), lambda b, l: (b, jnp.maximum(l - skip, 0), 0))
    per_seq = pl.BlockSpec((1, 1, d), lambda b, l: (b, 0, 0))
    return pl.pallas_call(
        functools.partial(_inproj_cd_kernel, tl=tl),
        grid=(nseq, seq // tl),
        in_specs=[tok(d), _resident((1, d)), per_seq, per_seq, _resident((d, N_CD_PAD))],
        out_specs=[tok(N_CD_MAIN), tok(128), kept, kept,
                   pl.BlockSpec((1, 8, CONV_DIM_D), lambda b, l: (b, 0, 0))],
        out_shape=[jax.ShapeDtypeStruct((nseq, seq, N_CD_MAIN), BF16),
                   jax.ShapeDtypeStruct((nseq, seq, 128), F32),
                   jax.ShapeDtypeStruct((nseq, keep, 512), F32),
                   jax.ShapeDtypeStruct((nseq, keep, 512), F32),
                   jax.ShapeDtypeStruct((nseq, 8, CONV_DIM_D), F32)],
        compiler_params=_params(("parallel", "arbitrary")),
        name="inproj_cd",
    )(x, g, sc, sh, w)


def _band_kernel(q_ref, k_ref, v_ref, b_ref, o_ref, *, tq):
    t = pl.program_id(1)
    lane = lax.broadcasted_iota(jnp.int32, (tq, 128), 1)
    n_heads = H_C

    def attend(near_start):
        starts, offs = [], []
        for d in range(3):
            kt = t - 2 + d
            starts.append(pl.multiple_of(jnp.maximum(kt, 0) * tq, tq))
            offs.append(jnp.where(kt >= 0, 0.0, NEG))

        def scores(head):
            pr, hi = divmod(head, 2)
            cs = slice(pr * 128, (pr + 1) * 128)
            qh = _split_halves((q_ref[0, :, cs].astype(F32) * (DH_C ** -0.5)).astype(BF16))[hi]
            ss = []
            for d in range(3):
                s = _mm_nt(qh, k_ref[0, pl.ds(starts[d], tq), cs]) + b_ref[head, :, d * tq:(d + 1) * tq]
                ss.append(s + offs[d] if near_start and d < 2 else s)
            mm = ss[2][:, 0:128]
            for d in range(3):
                for c0 in range(0, tq, 128):
                    mm = jnp.maximum(mm, ss[d][:, c0:c0 + 128])
            return ss, jnp.max(mm, axis=1, keepdims=True)

        def weigh(head, ss, m):
            cs = slice((head // 2) * 128, (head // 2 + 1) * 128)
            ls = jnp.zeros((tq, 128), F32)
            o = jnp.zeros((tq, 128), F32)
            for d in range(3):
                p = jnp.exp(ss[d] - m)
                for c0 in range(0, tq, 128):
                    ls = ls + p[:, c0:c0 + 128]
                o = o + _mm(p.astype(BF16), v_ref[0, pl.ds(starts[d], tq), cs])
            return o / jnp.sum(ls, axis=1, keepdims=True)

        outs = []
        pending = scores(0)
        for head in range(n_heads):
            nxt = scores(head + 1) if head + 1 < n_heads else None
            outs.append(weigh(head, *pending))
            pending = nxt
            if head % 2 == 1:
                cs = slice((head // 2) * 128, (head // 2 + 1) * 128)
                o_ref[0, :, cs] = jnp.where(lane < 64, outs[head - 1], outs[head]).astype(BF16)

    @pl.when(t < 2)
    def _():
        attend(True)

    @pl.when(t >= 2)
    def _():
        attend(False)


def _band_attn(main, bias):
    nseq, seq, _ = main.shape
    tq = BAND_TQ
    w = H_C * DH_C
    full = lambda idx: pl.BlockSpec((1, seq, w), lambda b, t: (b, 0, idx))
    return pl.pallas_call(
        functools.partial(_band_kernel, tq=tq),
        grid=(nseq, seq // tq),
        in_specs=[pl.BlockSpec((1, tq, w), lambda b, t: (b, t, 0)), full(1), full(2), _resident(bias.shape)],
        out_specs=pl.BlockSpec((1, tq, w), lambda b, t: (b, t, 0)),
        out_shape=jax.ShapeDtypeStruct((nseq, seq, w), BF16),
        compiler_params=_params(("parallel", "parallel")),
        name="band_attn",
    )(main, main, main, bias)


def _band_sample_kernel(q_ref, kc_ref, vc_ref, kp_ref, vp_ref, b_ref, o_ref, *, past):
    scale = DH_C ** -0.5
    for pr in range(H_C // 2):
        cs = slice(pr * 128, (pr + 1) * 128)
        halves = _split_halves(q_ref[0, :, cs])
        kp = kp_ref[0, :, cs].astype(BF16)
        vp = vp_ref[0, :, cs].astype(BF16)
        kc = kc_ref[0, :, cs]
        vc = vc_ref[0, :, cs]
        outs = []
        for hi, qh in enumerate(halves):
            b = b_ref[2 * pr + hi]
            sp = _mm_nt(qh, kp) * scale + b[:, :past]
            sc = _mm_nt(qh, kc) * scale + b[:, past:]
            m = jnp.maximum(jnp.max(sp, axis=1, keepdims=True), jnp.max(sc, axis=1, keepdims=True))
            pp = jnp.exp(sp - m)
            pc = jnp.exp(sc - m)
            l = jnp.sum(pp, axis=1, keepdims=True) + jnp.sum(pc, axis=1, keepdims=True)
            outs.append(_mm((pp / l).astype(BF16), vp) + _mm((pc / l).astype(BF16), vc))
        lane = lax.broadcasted_iota(jnp.int32, outs[0].shape, 1)
        o_ref[0, :, cs] = jnp.where(lane < 64, outs[0], outs[1]).astype(BF16)


def _band_attn_sample(main, k_past, v_past, bias):
    nseq, lq, _ = main.shape
    past = k_past.shape[1]
    w = H_C * DH_C
    cur = lambda idx: pl.BlockSpec((1, lq, w), lambda b: (b, 0, idx))
    old = pl.BlockSpec((1, past, w), lambda b: (b, 0, 0))
    return pl.pallas_call(
        functools.partial(_band_sample_kernel, past=past),
        grid=(nseq,),
        in_specs=[cur(0), cur(1), cur(2), old, old, _resident(bias.shape)],
        out_specs=cur(0),
        out_shape=jax.ShapeDtypeStruct((nseq, lq, w), BF16),
        compiler_params=_params(("parallel",)),
        name="band_attn_sample",
    )(main, main, main, k_past, v_past, bias)


def _ssd_kernel(z_ref, xs_ref, bm_ref, cm_ref, dt_ref, cw_ref, cb_ref, dtb_ref, alog_ref, dskip_ref, ng_ref,
                past_ref, s0_ref, o_ref, sn_ref, buf_ref, st_ref, *, c):
    @pl.when(pl.program_id(1) == 0)
    def _():
        buf_ref[0:8, :] = past_ref[0]
        st_ref[...] = s0_ref[0]

    buf_ref[8:8 + c, 0:D_INNER] = xs_ref[0].astype(F32)
    buf_ref[8:8 + c, D_INNER:D_INNER + 256] = bm_ref[0].astype(F32)
    buf_ref[8:8 + c, D_INNER + 256:CONV_DIM_D] = cm_ref[0].astype(F32)
    conv = cb_ref[...] + buf_ref[pl.ds(5, c), :] * cw_ref[0:1, :]
    for i in range(1, CONV_D):
        conv = conv + buf_ref[pl.ds(5 + i, c), :] * cw_ref[i:i + 1, :]
    buf_ref[0:8, :] = buf_ref[c:c + 8, :]
    act = _silu(conv)
    xs = act[:, 0:D_INNER]
    bm = act[:, D_INNER:D_INNER + 256].astype(BF16)
    cm = act[:, D_INNER + 256:CONV_DIM_D].astype(BF16)

    dt = _softplus(dt_ref[0] + dtb_ref[...])
    da = dt * (-jnp.exp(alog_ref[...]))
    row = lax.broadcasted_iota(jnp.int32, (c, c), 0)
    col = lax.broadcasted_iota(jnp.int32, (c, c), 1)
    tri = row >= col
    cum = _mm_exact(tri.astype(F32), da)
    eye = (lax.broadcasted_iota(jnp.int32, (8, 128), 0) == lax.broadcasted_iota(jnp.int32, (8, 128), 1)).astype(F32)
    cum_t = _mm_exact(_mm_nt_exact(eye, da), (row <= col).astype(F32))
    dt_t = _mm_nt_exact(eye, dt)
    ecum = jnp.exp(cum)
    last = cum[c - 1:c, :]
    wgt = jnp.exp(last - cum) * dt
    elast = jnp.exp(last)
    lane = lax.broadcasted_iota(jnp.int32, (c, 128), 1)
    low = lane < 64
    rlow = lax.broadcasted_iota(jnp.int32, (128, 128), 0) < 64

    for g in range(G_D):
        bm_g = bm[:, g * 128:(g + 1) * 128]
        cm_g = cm[:, g * 128:(g + 1) * 128]
        cb = _mm_nt(cm_g, bm_g)
        ys = []
        for pp in range(2):
            p = 2 * g + pp
            h0, h1 = 2 * p, 2 * p + 1
            x_f = xs[:, p * 128:(p + 1) * 128]
            x_b = x_f.astype(BF16)
            y_in = []
            for hh in (h0, h1):
                seg = cum[:, hh:hh + 1] - cum_t[hh:hh + 1, :]
                dec = jnp.exp(jnp.where(tri, seg, NEG))
                y_in.append(_mm((cb * dec * dt_t[hh:hh + 1, :]).astype(BF16), x_b))
            st = st_ref[p * 128:(p + 1) * 128, :]
            y_x = _mm_nt(cm_g, st.astype(BF16)) * jnp.where(low, ecum[:, h0:h0 + 1], ecum[:, h1:h1 + 1])
            w2 = jnp.where(low, wgt[:, h0:h0 + 1], wgt[:, h1:h1 + 1])
            new = jnp.where(rlow, elast[:, h0:h0 + 1], elast[:, h1:h1 + 1]) * st + _mm_tn((x_f * w2).astype(BF16), bm_g)
            st_ref[p * 128:(p + 1) * 128, :] = new
            sn_ref[0, p * 128:(p + 1) * 128, :] = new
            y = jnp.where(low, y_in[0], y_in[1]) + y_x + jnp.where(low, dskip_ref[h0], dskip_ref[h1]) * x_f
            ys.append(y * _silu(z_ref[0, :, p * 128:(p + 1) * 128].astype(F32)))
        ms = (jnp.sum(ys[0] * ys[0], axis=-1, keepdims=True) + jnp.sum(ys[1] * ys[1], axis=-1, keepdims=True)) / 256.0
        inv = lax.rsqrt(ms + EPS)
        for pp in range(2):
            p = 2 * g + pp
            o_ref[0, :, p * 128:(p + 1) * 128] = (ys[pp] * inv * ng_ref[:, p * 128:(p + 1) * 128]).astype(BF16)


def _ssd(main, dt, conv_w, conv_b, dt_bias, a_log, d_skip, norm_g, conv_past, s0, c):
    nseq, seq, _ = main.shape
    pad = lambda a: jnp.pad(a.reshape(1, H_D), ((0, 0), (0, 128 - H_D)))
    blk = lambda w, idx: pl.BlockSpec((1, c, w), lambda b, l: (b, l, idx))
    const = lambda r, w: pl.BlockSpec((r, w), lambda b, l: (0, 0))
    state = pl.BlockSpec((1, H_D * P_D, N_D), lambda b, l: (b, 0, 0))
    return pl.pallas_call(
        functools.partial(_ssd_kernel, c=c),
        grid=(nseq, seq // c),
        in_specs=[blk(512, 3), blk(512, 4), blk(256, 10), blk(256, 11), blk(128, 0),
                  const(CONV_D, CONV_DIM_D), const(1, CONV_DIM_D), const(1, 128), const(1, 128),
                  pl.BlockSpec(memory_space=pltpu.SMEM), const(1, D_INNER),
                  pl.BlockSpec((1, 8, CONV_DIM_D), lambda b, l: (b, 0, 0)), state],
        out_specs=[blk(512, 0), state],
        out_shape=[jax.ShapeDtypeStruct((nseq, seq, D_INNER), BF16),
                   jax.ShapeDtypeStruct((nseq, H_D * P_D, N_D), F32)],
        scratch_shapes=[pltpu.VMEM((c + 8, CONV_DIM_D), F32), pltpu.VMEM((H_D * P_D, N_D), F32)],
        compiler_params=_params(("parallel", "arbitrary")),
        name="ssd",
    )(main, main, main, main, dt, conv_w, conv_b.reshape(1, CONV_DIM_D), pad(dt_bias), pad(a_log), d_skip,
      norm_g.reshape(1, D_INNER), conv_past, s0)


def _ffn_kernel(x_ref, o1_ref, o2_ref, wo_ref, g1_ref, ng_ref, sc_ref, sh_ref, g2_ref, wup_ref, cw_ref, cb_ref,
                wdn_ref, past_ref, fg_ref, out_ref, tail_ref, act_ref, wb_ref, gt_ref, *, tl, d_ff, final):
    @pl.when(pl.program_id(1) == 0)
    def _():
        gt_ref[...] = past_ref[0]

    half = wo_ref.shape[0] // 2
    mix = _mm(o1_ref[0], wo_ref[0:half, :]) + _mm(o2_ref[0], wo_ref[half:2 * half, :])
    x1 = x_ref[0] + g1_ref[0] * mix
    out_ref[0] = x1
    h = _norm_mod(x1, ng_ref[...], sc_ref[0], sh_ref[0]).astype(BF16)
    for j in range(d_ff // FF_CHUNK):
        c0 = j * FF_CHUNK
        a = _mm(h, wup_ref[:, c0:c0 + FF_CHUNK])
        g = _mm(h, wup_ref[:, d_ff + c0:d_ff + c0 + FF_CHUNK])
        wb_ref[0:8, :] = gt_ref[:, c0:c0 + FF_CHUNK]
        wb_ref[8:8 + tl, :] = g
        gt_ref[:, c0:c0 + FF_CHUNK] = g[tl - 8:tl, :]
        gc = (cb_ref[:, c0:c0 + FF_CHUNK] + wb_ref[pl.ds(6, tl), :] * cw_ref[0:1, c0:c0 + FF_CHUNK]
              + wb_ref[pl.ds(7, tl), :] * cw_ref[1:2, c0:c0 + FF_CHUNK] + g * cw_ref[2:3, c0:c0 + FF_CHUNK])
        act_ref[:, c0:c0 + FF_CHUNK] = (a * _gelu_tanh(gc)).astype(BF16)
    split = (d_ff // FF_CHUNK + 1) // 2 * FF_CHUNK
    dn = _mm(act_ref[:, 0:split], wdn_ref[0:split, :]) + _mm(act_ref[:, split:d_ff], wdn_ref[split:d_ff, :])
    x2 = out_ref[0] + g2_ref[0] * dn
    if final:
        ms = jnp.mean(x2 * x2, axis=-1, keepdims=True)
        x2 = (x2 * lax.rsqrt(ms + EPS)) * fg_ref[...]
    out_ref[0] = x2
    tail_ref[0] = gt_ref[...]


def _outproj_ffn(x, o1, o2, wo, g1, ng, sc, sh, g2, wup, cw, cb, wdn, past, fg, tl, final):
    nseq, seq, d = x.shape
    d_ff = wdn.shape[0]
    tok = lambda n: pl.BlockSpec((1, tl, n), lambda b, l: (b, l, 0))
    per_seq = pl.BlockSpec((1, 1, d), lambda b, l: (b, 0, 0))
    tail = pl.BlockSpec((1, 8, d_ff), lambda b, l: (b, 0, 0))
    return pl.pallas_call(
        functools.partial(_ffn_kernel, tl=tl, d_ff=d_ff, final=final),
        grid=(nseq, seq // tl),
        in_specs=[tok(d), tok(o1.shape[-1]), tok(o2.shape[-1]), _resident(wo.shape), per_seq, _resident((1, d)),
                  per_seq, per_seq, per_seq, _resident(wup.shape), _resident(cw.shape), _resident((1, d_ff)),
                  _resident(wdn.shape), tail, _resident((1, d))],
        out_specs=[tok(d), tail],
        out_shape=[jax.ShapeDtypeStruct((nseq, seq, d), F32), jax.ShapeDtypeStruct((nseq, 8, d_ff), F32)],
        scratch_shapes=[pltpu.VMEM((tl, d_ff), BF16), pltpu.VMEM((tl + 8, FF_CHUNK), F32), pltpu.VMEM((8, d_ff), F32)],
        compiler_params=_params(("parallel", "arbitrary")),
        name="outproj_ffn",
    )(x, o1, o2, wo, g1, ng, sc, sh, g2, wup, cw, cb.reshape(1, d_ff), wdn, past, fg)


def _interleave(gens, strides):
    live = list(zip(gens, strides))
    while live:
        for item in list(live):
            for _ in range(item[1]):
                try:
                    next(item[0])
                except StopIteration:
                    live.remove(item)
                    break


def _chain(*gens):
    for g in gens:
        yield from g


def _ffn_steps(x_ref, o1, o2, wo_ref, g1_ref, ng_ref, sc_ref, sh_ref, g2_ref, wup_ref, cw_ref, cb_ref, wdn_ref,
               fg_ref, out_ref, tail_ref, act_ref, wb_ref, gt_ref, *, tl, d_ff, final):
    half = wo_ref.shape[0] // 2
    mix = _mm(o1[...], wo_ref[0:half, :]) + _mm(o2[...], wo_ref[half:2 * half, :])
    x1 = x_ref[0] + g1_ref[0] * mix
    out_ref[0] = x1
    h = _norm_mod(x1, ng_ref[...], sc_ref[0], sh_ref[0]).astype(BF16)
    yield
    for j in range(d_ff // FF_CHUNK):
        c0 = j * FF_CHUNK
        a = _mm(h, wup_ref[:, c0:c0 + FF_CHUNK])
        g = _mm(h, wup_ref[:, d_ff + c0:d_ff + c0 + FF_CHUNK])
        wb_ref[0:8, :] = gt_ref[:, c0:c0 + FF_CHUNK]
        wb_ref[8:8 + tl, :] = g
        gt_ref[:, c0:c0 + FF_CHUNK] = g[tl - 8:tl, :]
        gc = (cb_ref[:, c0:c0 + FF_CHUNK] + wb_ref[pl.ds(6, tl), :] * cw_ref[0:1, c0:c0 + FF_CHUNK]
              + wb_ref[pl.ds(7, tl), :] * cw_ref[1:2, c0:c0 + FF_CHUNK] + g * cw_ref[2:3, c0:c0 + FF_CHUNK])
        act_ref[:, c0:c0 + FF_CHUNK] = (a * _gelu_tanh(gc)).astype(BF16)
        yield
    split = (d_ff // FF_CHUNK + 1) // 2 * FF_CHUNK
    dn = _mm(act_ref[:, 0:split], wdn_ref[0:split, :])
    yield
    dn = dn + _mm(act_ref[:, split:d_ff], wdn_ref[split:d_ff, :])
    x2 = out_ref[0] + g2_ref[0] * dn
    if final:
        ms = jnp.mean(x2 * x2, axis=-1, keepdims=True)
        x2 = (x2 * lax.rsqrt(ms + EPS)) * fg_ref[...]
    out_ref[0] = x2
    tail_ref[0] = gt_ref[...]
    yield


def _band_steps(q, k_ref, v_ref, b_ref, o, t, *, tq):
    lane = lax.broadcasted_iota(jnp.int32, (tq, 128), 1)
    starts, offs = [], []
    for d in range(3):
        kt = t - 2 + d
        starts.append(pl.multiple_of(jnp.maximum(kt, 0) * tq, tq))
        offs.append(jnp.where(kt >= 0, 0.0, NEG))

    def scores(head):
        pr, hi = divmod(head, 2)
        cs = slice(pr * 128, (pr + 1) * 128)
        qh = _split_halves((q[:, cs].astype(F32) * (DH_C ** -0.5)).astype(BF16))[hi]
        ss = []
        for d in range(3):
            s = _mm_nt(qh, k_ref[0, pl.ds(starts[d], tq), cs]) + b_ref[head, :, d * tq:(d + 1) * tq]
            ss.append(s + offs[d] if d < 2 else s)
        mm = ss[2][:, 0:128]
        for d in range(3):
            for c0 in range(0, tq, 128):
                mm = jnp.maximum(mm, ss[d][:, c0:c0 + 128])
        return ss, jnp.max(mm, axis=1, keepdims=True)

    def weigh(head, ss, m):
        cs = slice((head // 2) * 128, (head // 2 + 1) * 128)
        ls = jnp.zeros((tq, 128), F32)
        acc = jnp.zeros((tq, 128), F32)
        for d in range(3):
            p = jnp.exp(ss[d] - m)
            for c0 in range(0, tq, 128):
                ls = ls + p[:, c0:c0 + 128]
            acc = acc + _mm(p.astype(BF16), v_ref[0, pl.ds(starts[d], tq), cs])
        return acc / jnp.sum(ls, axis=1, keepdims=True)

    outs = []
    pending = scores(0)
    yield
    for head in range(H_C):
        nxt = scores(head + 1) if head + 1 < H_C else None
        outs.append(weigh(head, *pending))
        pending = nxt
        if head % 2 == 1:
            cs = slice((head // 2) * 128, (head // 2 + 1) * 128)
            o[:, cs] = jnp.where(lane < 64, outs[head - 1], outs[head]).astype(BF16)
        yield


def _ssd_steps(z, xs_in, bm_in, cm_in, dt_in, cw_ref, cb_ref, dtb_ref, alog_ref, dskip_ref, ng_ref, o,
               buf_ref, st_ref, *, c):
    buf_ref[8:8 + c, 0:D_INNER] = xs_in[...].astype(F32)
    buf_ref[8:8 + c, D_INNER:D_INNER + 256] = bm_in[...].astype(F32)
    buf_ref[8:8 + c, D_INNER + 256:CONV_DIM_D] = cm_in[...].astype(F32)
    conv = cb_ref[...] + buf_ref[pl.ds(5, c), :] * cw_ref[0:1, :]
    for i in range(1, CONV_D):
        conv = conv + buf_ref[pl.ds(5 + i, c), :] * cw_ref[i:i + 1, :]
    buf_ref[0:8, :] = buf_ref[c:c + 8, :]
    act = _silu(conv)
    xs = act[:, 0:D_INNER]
    bm = act[:, D_INNER:D_INNER + 256].astype(BF16)
    cm = act[:, D_INNER + 256:CONV_DIM_D].astype(BF16)
    yield

    dt = _softplus(dt_in[...] + dtb_ref[...])
    da = dt * (-jnp.exp(alog_ref[...]))
    row = lax.broadcasted_iota(jnp.int32, (c, c), 0)
    col = lax.broadcasted_iota(jnp.int32, (c, c), 1)
    tri = row >= col
    cum = _mm_exact(tri.astype(F32), da)
    eye = (lax.broadcasted_iota(jnp.int32, (8, 128), 0) == lax.broadcasted_iota(jnp.int32, (8, 128), 1)).astype(F32)
    cum_t = _mm_exact(_mm_nt_exact(eye, da), (row <= col).astype(F32))
    dt_t = _mm_nt_exact(eye, dt)
    ecum = jnp.exp(cum)
    last = cum[c - 1:c, :]
    wgt = jnp.exp(last - cum) * dt
    elast = jnp.exp(last)
    lane = lax.broadcasted_iota(jnp.int32, (c, 128), 1)
    low = lane < 64
    rlow = lax.broadcasted_iota(jnp.int32, (128, 128), 0) < 64
    yield

    for g in range(G_D):
        bm_g = bm[:, g * 128:(g + 1) * 128]
        cm_g = cm[:, g * 128:(g + 1) * 128]
        cb = _mm_nt(cm_g, bm_g)
        ys = []
        for pp in range(2):
            p = 2 * g + pp
            h0, h1 = 2 * p, 2 * p + 1
            x_f = xs[:, p * 128:(p + 1) * 128]
            x_b = x_f.astype(BF16)
            y_in = []
            for hh in (h0, h1):
                seg = cum[:, hh:hh + 1] - cum_t[hh:hh + 1, :]
                dec = jnp.exp(jnp.where(tri, seg, NEG))
                y_in.append(_mm((cb * dec * dt_t[hh:hh + 1, :]).astype(BF16), x_b))
            st = st_ref[p * 128:(p + 1) * 128, :]
            y_x = _mm_nt(cm_g, st.astype(BF16)) * jnp.where(low, ecum[:, h0:h0 + 1], ecum[:, h1:h1 + 1])
            w2 = jnp.where(low, wgt[:, h0:h0 + 1], wgt[:, h1:h1 + 1])
            st_ref[p * 128:(p + 1) * 128, :] = (jnp.where(rlow, elast[:, h0:h0 + 1], elast[:, h1:h1 + 1]) * st
                                                + _mm_tn((x_f * w2).astype(BF16), bm_g))
            y = jnp.where(low, y_in[0], y_in[1]) + y_x + jnp.where(low, dskip_ref[h0], dskip_ref[h1]) * x_f
            ys.append(y * _silu(z[:, p * 128:(p + 1) * 128].astype(F32)))
            yield
        ms = (jnp.sum(ys[0] * ys[0], axis=-1, keepdims=True) + jnp.sum(ys[1] * ys[1], axis=-1, keepdims=True)) / 256.0
        inv = lax.rsqrt(ms + EPS)
        for pp in range(2):
            p = 2 * g + pp
            o[:, p * 128:(p + 1) * 128] = (ys[pp] * inv * ng_ref[:, p * 128:(p + 1) * 128]).astype(BF16)
    yield


def _cd_ffn_kernel(x_ref, wo_ref, g1_ref, ng_ref, sc_ref, sh_ref, g2_ref, wup_ref, cwf_ref, cbf_ref, wdn_ref, fg_ref,
                   q_ref, k_ref, v_ref, bb_ref,
                   z_ref, xs_ref, bm_ref, cm_ref, dt_ref, cws_ref, cbs_ref, dtb_ref, alog_ref, dskip_ref, ngs_ref,
                   out_ref, tail_ref, sn_ref,
                   obuf_ref, act_ref, wb_ref, gt_ref, buf_ref, st_ref,
                   *, tl, d_ff, final, n_tiles, tiles_per_seq, c, tq):
    g = pl.program_id(0)
    gm = jnp.minimum(g, n_tiles - 1)
    gf = jnp.maximum(g - 1, 0)
    jm = gm % tiles_per_seq
    jf = gf % tiles_per_seq
    slot = g % 2
    prev = (g + 1) % 2

    @pl.when(g == 0)
    def _():
        obuf_ref[...] = jnp.zeros(obuf_ref.shape, BF16)

    @pl.when(jm == 0)
    def _():
        buf_ref[0:8, :] = jnp.zeros((8, CONV_DIM_D), F32)
        st_ref[...] = jnp.zeros(st_ref.shape, F32)

    @pl.when(jf == 0)
    def _():
        gt_ref[...] = jnp.zeros(gt_ref.shape, F32)

    ffn = _ffn_steps(x_ref, obuf_ref.at[prev, :, 0:512], obuf_ref.at[prev, :, 512:1024], wo_ref, g1_ref, ng_ref,
                     sc_ref, sh_ref, g2_ref, wup_ref, cwf_ref, cbf_ref, wdn_ref, fg_ref, out_ref, tail_ref, act_ref,
                     wb_ref, gt_ref, tl=tl, d_ff=d_ff, final=final)
    band = _chain(*[_band_steps(q_ref.at[0, ti * tq:(ti + 1) * tq, :], k_ref, v_ref, bb_ref,
                                obuf_ref.at[slot, ti * tq:(ti + 1) * tq, 0:512],
                                jm * (tl // tq) + ti, tq=tq) for ti in range(tl // tq)])
    ssd = _chain(*[_ssd_steps(z_ref.at[0, ci * c:(ci + 1) * c, :], xs_ref.at[0, ci * c:(ci + 1) * c, :],
                              bm_ref.at[0, ci * c:(ci + 1) * c, :], cm_ref.at[0, ci * c:(ci + 1) * c, :],
                              dt_ref.at[0, ci * c:(ci + 1) * c, :], cws_ref, cbs_ref, dtb_ref, alog_ref, dskip_ref,
                              ngs_ref, obuf_ref.at[slot, ci * c:(ci + 1) * c, 512:1024], buf_ref, st_ref, c=c)
                   for ci in range(tl // c)])
    _interleave([ffn, _chain(band, ssd)], (1, 1))

    @pl.when(g < n_tiles)
    def _():
        sn_ref[0] = st_ref[...]


def _cd_mixers_ffn(x, main, dt, bias, wo, g1, ng, sc, sh, g2, wup, cwf, cbf, wdn, fg,
                   conv_w, conv_b, dt_bias, a_log, d_skip, norm_g, tl, final):
    nseq, seq, d = x.shape
    d_ff = wdn.shape[0]
    tps = seq // tl
    n_tiles = nseq * tps
    c = SCAN_CHUNK
    tq = BAND_TQ
    pad = lambda a: jnp.pad(a.reshape(1, H_D), ((0, 0), (0, 128 - H_D)))
    mix = lambda g: jnp.minimum(g, n_tiles - 1)
    ffn = lambda g: jnp.maximum(g - 1, 0)
    tok_f = lambda n: pl.BlockSpec((1, tl, n), lambda g: (ffn(g) // tps, ffn(g) % tps, 0))
    seq_f = lambda n: pl.BlockSpec((1, 1, n), lambda g: (ffn(g) // tps, 0, 0))
    tok_m = lambda w, idx: pl.BlockSpec((1, tl, w), lambda g: (mix(g) // tps, mix(g) % tps, idx))
    full_m = lambda idx: pl.BlockSpec((1, seq, 512), lambda g: (mix(g) // tps, 0, idx),
                                      pipeline_mode=pl.Buffered(1))
    const = lambda r, w: pl.BlockSpec((r, w), lambda g: (0, 0))
    return pl.pallas_call(
        functools.partial(_cd_ffn_kernel, tl=tl, d_ff=d_ff, final=final, n_tiles=n_tiles, tiles_per_seq=tps,
                          c=c, tq=tq),
        grid=(n_tiles + 1,),
        in_specs=[tok_f(d), _resident(wo.shape), seq_f(d), _resident((1, d)), seq_f(d), seq_f(d), seq_f(d),
                  _resident(wup.shape), _resident(cwf.shape), _resident((1, d_ff)), _resident(wdn.shape),
                  _resident((1, d)),
                  tok_m(512, 0), full_m(1), full_m(2), _resident(bias.shape),
                  tok_m(512, 3), tok_m(512, 4), tok_m(256, 10), tok_m(256, 11), tok_m(128, 0),
                  const(CONV_D, CONV_DIM_D), const(1, CONV_DIM_D), const(1, 128), const(1, 128),
                  pl.BlockSpec(memory_space=pltpu.SMEM), const(1, D_INNER)],
        out_specs=[tok_f(d),
                   pl.BlockSpec((1, 8, d_ff), lambda g: (ffn(g) // tps, 0, 0)),
                   pl.BlockSpec((1, H_D * P_D, N_D), lambda g: (mix(g) // tps, 0, 0))],
        out_shape=[jax.ShapeDtypeStruct((nseq, seq, d), F32), jax.ShapeDtypeStruct((nseq, 8, d_ff), F32),
                   jax.ShapeDtypeStruct((nseq, H_D * P_D, N_D), F32)],
        scratch_shapes=[pltpu.VMEM((2, tl, 1024), BF16), pltpu.VMEM((tl, d_ff), BF16),
                        pltpu.VMEM((tl + 8, FF_CHUNK), F32), pltpu.VMEM((8, d_ff), F32),
                        pltpu.VMEM((c + 8, CONV_DIM_D), F32), pltpu.VMEM((H_D * P_D, N_D), F32)],
        compiler_params=_params(("arbitrary",)),
        name="cd_mixers_ffn",
    )(x, wo, g1, ng, sc, sh, g2, wup, cwf, cbf.reshape(1, d_ff), wdn, fg,
      main, main, main, bias,
      main, main, main, main, dt, conv_w, conv_b.reshape(1, CONV_DIM_D), pad(dt_bias), pad(a_log), d_skip,
      norm_g.reshape(1, D_INNER))


def _rope_tables(pos):
    half = DK_A // 2
    inv = jnp.power(ROPE_BASE, -jnp.arange(half, dtype=F32) / half)
    ang = pos.astype(F32)[:, None] * inv[None, :]
    cos = jnp.cos(ang)
    sin = jnp.sin(ang)
    return jnp.concatenate([cos, cos], axis=1), jnp.concatenate([-sin, sin], axis=1)


def _pad_rows(a, rows=8):
    return jnp.pad(a, ((0, 0), (rows - a.shape[1], 0), (0, 0)))


def _trunk(x, mods, pos, weights, caches, biases):
    nseq, seq, d = x.shape
    sample = caches is not None
    tl = min(seq, 512)
    c = min(seq, SCAN_CHUNK)
    depth = weights["w_up"].shape[0]
    outs = {k: [] for k in ("ret", "bk", "bv", "ck", "cv", "dconv", "dssm", "ffn")}
    cos, sin = _rope_tables(pos)
    for l in range(depth):
        i = l // 2
        sh1, sc1, g1, sh2, sc2, g2 = mods[l]
        ng1 = weights["norm_g"][l, 0].reshape(1, d)
        ng2 = weights["norm_g"][l, 1].reshape(1, d)
        fused = None
        if l % 2 == 0:
            qkv, kb, vb = _inproj_ab(x, ng1, sc1, sh1, weights["w_in_ab"][i], cos, sin, tl)
            s0 = caches["ret"][i] if sample else jnp.zeros((nseq, H_A, DK_A, DV_A), F32)
            o1, s_new = _retention(qkv, s0, weights["ret_gn"][i], c)
            lam_init = 0.8 - 0.6 * math.exp(-0.3 * l)
            if sample:
                o2 = _diff_attn_sample(qkv, caches["bk"][i], caches["bv"][i], biases["t5"],
                                       weights["lam_q"][i], weights["lam_k"][i], weights["diff_gn"][i], lam_init)
            else:
                o2 = _diff_attn(qkv, biases["t5"], weights["lam_q"][i], weights["lam_k"][i],
                                weights["diff_gn"][i], lam_init)
            wo = weights["w_out_ab"][i]
            outs["ret"].append(s_new)
            outs["bk"].append(kb.reshape(nseq, seq, H_B, 2 * DH_B))
            outs["bv"].append(vb.reshape(nseq, seq, H_B, DV_B))
        else:
            keep = seq if sample else min(C_WINDOW, seq)
            main, dt, kc, vc, tail = _inproj_cd(x, ng1, sc1, sh1, weights["w_in_cd"][i], tl, keep)
            if sample:
                o1 = _band_attn_sample(main, caches["ck"][i], caches["cv"][i], biases["band"][i])
                conv_past = _pad_rows(caches["dconv"][i])
                s0 = caches["dssm"][i].reshape(nseq, H_D * P_D, N_D)
                o2, ssm_new = _ssd(main, dt, weights["d_conv_w"][i], weights["d_conv_b"][i], weights["d_dt_bias"][i],
                                   weights["d_a_log"][i], weights["d_skip"][i], weights["d_norm_g"][i], conv_past, s0,
                                   c)
            else:
                fused = _cd_mixers_ffn(
                    x, main, dt, biases["band"][i], weights["w_out_cd"][i], g1, ng2, sc2, sh2, g2, weights["w_up"][l],
                    weights["ffn_conv_w"][l], weights["ffn_conv_b"][l], weights["w_down"][l],
                    weights["final_g"].reshape(1, d), weights["d_conv_w"][i], weights["d_conv_b"][i],
                    weights["d_dt_bias"][i], weights["d_a_log"][i], weights["d_skip"][i], weights["d_norm_g"][i],
                    FUSED_TL, final=(l == depth - 1))
                ssm_new = fused[2]
            wo = weights["w_out_cd"][i]
            outs["ck"].append(kc.reshape(nseq, keep, H_C, DH_C))
            outs["cv"].append(vc.reshape(nseq, keep, H_C, DH_C))
            outs["dconv"].append(tail[:, 8 - (CONV_D - 1):])
            outs["dssm"].append(ssm_new.reshape(nseq, H_D, P_D, N_D))
        d_ff = weights["w_down"].shape[1]
        if fused is not None:
            x, ftail = fused[0], fused[1]
        else:
            ffn_past = _pad_rows(caches["ffn"][l]) if sample else jnp.zeros((nseq, 8, d_ff), F32)
            x, ftail = _outproj_ffn(x, o1, o2, wo, g1, ng2, sc2, sh2, g2, weights["w_up"][l],
                                    weights["ffn_conv_w"][l], weights["ffn_conv_b"][l], weights["w_down"][l], ffn_past,
                                    weights["final_g"].reshape(1, d), tl, final=(l == depth - 1))
        outs["ffn"].append(ftail[:, 8 - (CONV_F - 1):])
    stk = lambda t: jnp.stack(t).astype(F32)
    return (x,) + tuple(stk(outs[k]) for k in ("ret", "bk", "bv", "ck", "cv", "dconv", "dssm", "ffn"))


def kernel(x_prompt, x_sample, cache_ret_state, cache_b_k, cache_b_v, cache_c_k, cache_c_v, state_d_conv, state_d_ssm, state_ffn_conv, c_prompt, c_sample, w_mod, b_mod, norm_g, final_g, t5_table, w_in_ab, w_out_ab, ret_gn, lam_q, lam_k, diff_gn, w_in_cd, w_out_cd, rel_table, d_conv_w, d_conv_b, d_dt_bias, d_a_log, d_skip, d_norm_g, w_up, ffn_conv_w, ffn_conv_b, w_down):
    batch, seq, d = x_prompt.shape
    dec_batch, dec_seq, _ = x_sample.shape
    past = cache_b_k.shape[2]
    depth = w_mod.shape[0]
    assert dec_seq <= CHUNK and past % CHUNK == 0 and cache_c_k.shape[2] == C_WINDOW
    assert seq % BAND_TQ == 0 and seq % DIFF_TQ == 0

    w_in_cd_p = jnp.pad(w_in_cd, ((0, 0), (0, 0), (0, N_CD_PAD - w_in_cd.shape[-1]))).astype(BF16)
    weights = dict(
        norm_g=norm_g, final_g=final_g, w_in_ab=w_in_ab.astype(BF16), w_out_ab=w_out_ab.astype(BF16),
        ret_gn=ret_gn, lam_q=lam_q, lam_k=lam_k, diff_gn=diff_gn, w_in_cd=w_in_cd_p,
        w_out_cd=w_out_cd.astype(BF16), d_conv_w=d_conv_w, d_conv_b=d_conv_b, d_dt_bias=d_dt_bias, d_a_log=d_a_log,
        d_skip=d_skip, d_norm_g=d_norm_g, w_up=w_up.astype(BF16), ffn_conv_w=ffn_conv_w, ffn_conv_b=ffn_conv_b,
        w_down=w_down.astype(BF16))

    mod = _modulation(jnp.concatenate([c_prompt, c_sample], axis=0), w_mod, b_mod)

    def pieces(rows):
        return [[m[:, None, :] for m in jnp.split(mod[l, rows], 6, axis=-1)] for l in range(depth)]

    band = [_band_bias(rel_table[i]) for i in range(rel_table.shape[0])]
    biases_p = dict(t5=_t5_bias_prompt(t5_table, seq), band=band)
    biases_s = dict(t5=_t5_bias_sample(t5_table, past, dec_seq),
                    band=[b[:, :dec_seq, :C_WINDOW + dec_seq] for b in band])
    caches = dict(
        ret=cache_ret_state,
        bk=cache_b_k.reshape(cache_b_k.shape[0], dec_batch, past, H_B * 2 * DH_B),
        bv=cache_b_v.reshape(cache_b_v.shape[0], dec_batch, past, H_B * DV_B),
        ck=cache_c_k.reshape(cache_c_k.shape[0], dec_batch, C_WINDOW, H_C * DH_C),
        cv=cache_c_v.reshape(cache_c_v.shape[0], dec_batch, C_WINDOW, H_C * DH_C),
        dconv=state_d_conv, dssm=state_d_ssm, ffn=state_ffn_conv)

    pos_p = jnp.arange(seq, dtype=jnp.int32)
    pos_s = past + jnp.arange(dec_seq, dtype=jnp.int32)
    y_p, ret_p, bk_p, bv_p, ck_p, cv_p, dconv_p, dssm_p, ffn_p = _trunk(
        x_prompt, pieces(slice(0, batch)), pos_p, weights, None, biases_p)
    y_s, ret_s, bk_s, bv_s, ck_s, cv_s, dconv_s, dssm_s, ffn_s = _trunk(
        x_sample, pieces(slice(batch, batch + dec_batch)), pos_s, weights, caches, biases_s)
    return (y_p, y_s, ret_p, ret_s, bk_p, bk_s, bv_p, bv_s, ck_p, ck_s, cv_p, cv_s,
            dconv_p, dconv_s, dssm_p, dssm_s, ffn_p, ffn_s)
```

```python
import functools
import math

import numpy as np
import jax
import jax.numpy as jnp
from jax import lax
from jax.experimental import pallas as pl
from jax.experimental.pallas import tpu as pltpu

F32 = jnp.float32
BF16 = jnp.bfloat16

CHUNK = 64
EPS = 1e-6
NEG = -1e30
H_A, DK_A, DV_A = 4, 128, 128
ROPE_BASE = 10000.0
H_B, DH_B, DV_B = 4, 64, 128
T5_BUCKETS, T5_MAX_DIST = 32, 128
H_C, DH_C = 8, 64
BAND_CHUNKS = 8
C_WINDOW = BAND_CHUNKS * CHUNK
REL_CLIP = 128
H_D, P_D, G_D, N_D = 8, 64, 2, 128
D_INNER = H_D * P_D
CONV_D = 4
CONV_DIM_D = D_INNER + 2 * G_D * N_D
CONV_F = 3
N_AB_COLS = 7 * 512
N_CD_MAIN = 6 * 512
N_CD_PAD = N_CD_MAIN + 128
FF_CHUNK = 256
BAND_TQ = 256
DIFF_TQ = 256
SCAN_CHUNK = 256
FUSED_TL = 256
VMEM_LIMIT = 56 * 1024 * 1024


def _mm(a, b):
    return jnp.dot(a, b, preferred_element_type=F32)


def _mm_nt(a, b):
    return lax.dot_general(a, b, (((1,), (1,)), ((), ())), preferred_element_type=F32)


def _mm_tn(a, b):
    return lax.dot_general(a, b, (((0,), (0,)), ((), ())), preferred_element_type=F32)


def _mm_exact(a, b):
    return jnp.dot(a, b, preferred_element_type=F32, precision=lax.Precision.HIGHEST)


def _mm_nt_exact(a, b):
    return lax.dot_general(a, b, (((1,), (1,)), ((), ())), preferred_element_type=F32,
                           precision=lax.Precision.HIGHEST)


def _silu(x):
    return x * (1.0 / (1.0 + jnp.exp(-x)))


def _softplus(x):
    return jnp.maximum(x, 0.0) + jnp.log1p(jnp.exp(-jnp.abs(x)))


def _gelu_tanh(x):
    k1 = -2.0 * math.sqrt(2.0 / math.pi) * math.log2(math.e)
    return x * (1.0 / (1.0 + jnp.exp2(x * (k1 + (k1 * 0.044715) * (x * x)))))


def _norm_mod(x, g, sc, sh):
    ms = jnp.mean(x * x, axis=-1, keepdims=True)
    return (x * lax.rsqrt(ms + EPS)) * g * (1.0 + sc) + sh


def _mod_spec(a, tl):
    if a.shape[1] == 1:
        return pl.BlockSpec((1, 1, a.shape[2]), lambda b, l: (b, 0, 0))
    return pl.BlockSpec((1, tl, a.shape[2]), lambda b, l: (b, l, 0))


def _params(sem):
    return pltpu.CompilerParams(dimension_semantics=sem, vmem_limit_bytes=VMEM_LIMIT)


def _resident(shape):
    nd = len(shape)
    return pl.BlockSpec(shape, lambda *_: (0,) * nd, pipeline_mode=pl.Buffered(1))


def _mod_kernel(c_ref, w_ref, b_ref, o_ref):
    c = c_ref[...]
    o_ref[0] = _mm(_silu(c).astype(BF16), w_ref[0].astype(BF16)) + b_ref[0]


def _modulation(c_all, w_mod, b_mod):
    depth, d, n = w_mod.shape
    r = c_all.shape[0]
    tn = 1536
    return pl.pallas_call(
        _mod_kernel,
        grid=(depth, n // tn),
        in_specs=[pl.BlockSpec((r, d), lambda l, j: (0, 0)),
                  pl.BlockSpec((1, d, tn), lambda l, j: (l, 0, j)),
                  pl.BlockSpec((1, 1, tn), lambda l, j: (l, 0, j))],
        out_specs=pl.BlockSpec((1, r, tn), lambda l, j: (l, 0, j)),
        out_shape=jax.ShapeDtypeStruct((depth, r, n), F32),
        compiler_params=_params(("parallel", "parallel")),
        name="modulation",
    )(c_all, w_mod, b_mod.reshape(depth, 1, n))


def _table_gather_kernel(tab_ref, idx_ref, o_ref, *, n_entries, n_heads):
    idx = idx_ref[...]

    def body(r, accs):
        m = idx == r
        return tuple(jnp.where(m, tab_ref[hh, r], a) for hh, a in enumerate(accs))

    accs = lax.fori_loop(0, n_entries, body, tuple(jnp.zeros(idx.shape, F32) for _ in range(n_heads)))
    for hh in range(n_heads):
        o_ref[hh:hh + 1, :] = accs[hh]


def _table_gather(table, idx):
    t, h = table.shape
    w = idx.shape[0]
    return pl.pallas_call(
        functools.partial(_table_gather_kernel, n_entries=t, n_heads=h),
        in_specs=[pl.BlockSpec(memory_space=pltpu.SMEM),
                  pl.BlockSpec((1, w), lambda: (0, 0))],
        out_specs=pl.BlockSpec((h, w), lambda: (0, 0)),
        out_shape=jax.ShapeDtypeStruct((h, w), F32),
        name="table_gather",
    )(table.T, idx.reshape(1, w))


def _toeplitz_kernel(v_ref, o_ref, *, rows, cols, tile):
    x = pltpu.roll(jnp.broadcast_to(v_ref[0], (rows, v_ref.shape[2])), 0, 1, stride=1, stride_axis=0)
    if tile is None:
        o_ref[0] = x[:, :cols]
    else:
        for jb in range(cols // tile):
            o_ref[0, jb] = x[:, jb * tile:(jb + 1) * tile]


def _toeplitz(vec, rows, cols, tile=None):
    h, wv = vec.shape
    if tile is None:
        out_shape, block, imap = (h, rows, cols), (1, rows, cols), lambda i: (i, 0, 0)
    else:
        out_shape, block, imap = (h, cols // tile, rows, tile), (1, cols // tile, rows, tile), lambda i: (i, 0, 0, 0)
    return pl.pallas_call(
        functools.partial(_toeplitz_kernel, rows=rows, cols=cols, tile=tile),
        grid=(h,),
        in_specs=[pl.BlockSpec((1, 1, wv), lambda i: (i, 0, 0))],
        out_specs=pl.BlockSpec(block, imap),
        out_shape=jax.ShapeDtypeStruct(out_shape, F32),
        compiler_params=_params(("parallel",)),
        name="toeplitz",
    )(vec.reshape(h, 1, wv))


def _t5_bucket(rel):
    half = T5_BUCKETS // 2
    max_exact = half // 2
    base = jnp.where(rel > 0, half, 0)
    n = jnp.abs(rel)
    nf = jnp.maximum(n, 1).astype(F32)
    large = max_exact + (jnp.log(nf / max_exact) / math.log(T5_MAX_DIST / max_exact) * (half - max_exact)).astype(jnp.int32)
    large = jnp.minimum(large, half - 1)
    return base + jnp.where(n < max_exact, n, large)


def _wrapped_offsets(n_pos, n_neg):
    p = jnp.arange(n_pos + n_neg, dtype=jnp.int32)
    return jnp.where(p < n_pos, p, p - (n_pos + n_neg))


def _t5_bias_prompt(t5_table, seq):
    nd = seq // DIFF_TQ
    u = _wrapped_offsets(seq, DIFF_TQ)
    vec = _table_gather(t5_table, _t5_bucket(u - (seq - DIFF_TQ)))
    return _toeplitz(vec, DIFF_TQ, seq, tile=DIFF_TQ)


def _t5_bias_sample(t5_table, past, lq):
    u = _wrapped_offsets(past + lq, lq)
    vec = _table_gather(t5_table, _t5_bucket(u - past))
    return _toeplitz(vec, lq, past + 2 * lq)[:, :, :past + lq]


def _band_bias(rel_table):
    u = _wrapped_offsets(3 * BAND_TQ, BAND_TQ)
    idx = jnp.clip(2 * BAND_TQ - u, -REL_CLIP, REL_CLIP) + REL_CLIP
    bias = _toeplitz(_table_gather(rel_table, idx), BAND_TQ, 3 * BAND_TQ)
    dc = (np.arange(BAND_TQ)[:, None] // CHUNK + C_WINDOW // CHUNK) - np.arange(3 * BAND_TQ)[None, :] // CHUNK
    return jnp.where(jnp.asarray((dc >= 0) & (dc <= BAND_CHUNKS))[None], bias, NEG)


def _inproj_ab_kernel(x_ref, g_ref, sc_ref, sh_ref, w_ref, cos_ref, sin_ref, qkv_ref, kb_ref, vb_ref, *, tl):
    h = _norm_mod(x_ref[0], g_ref[...], sc_ref[0], sh_ref[0]).astype(BF16)
    cos = cos_ref[...]
    sin = sin_ref[...]
    for j in range(7):
        y = _mm(h, w_ref[:, j * 512:(j + 1) * 512])
        if j < 2:
            for hh in range(H_A):
                yh = y[:, hh * 128:(hh + 1) * 128]
                yh = yh * cos + pltpu.roll(yh, 64, 1) * sin
                if j == 1:
                    yh = yh * (DK_A ** -0.5)
                qkv_ref[0, :, j * 512 + hh * 128:j * 512 + (hh + 1) * 128] = yh.astype(BF16)
        else:
            if j >= 5:
                dst = kb_ref if j == 5 else vb_ref
                for hh in range(H_B):
                    dst[0, pl.ds(hh, tl, stride=H_B), :] = y[:, hh * 128:(hh + 1) * 128]
            qkv_ref[0, :, j * 512:(j + 1) * 512] = y.astype(BF16)


def _inproj_ab(x, g, sc, sh, w, cos, sin, tl):
    nseq, seq, d = x.shape
    grid = (nseq, seq // tl)
    tok = lambda n: pl.BlockSpec((1, tl, n), lambda b, l: (b, l, 0))
    cache = pl.BlockSpec((1, tl * H_B, 128), lambda b, l: (b, l, 0))
    return pl.pallas_call(
        functools.partial(_inproj_ab_kernel, tl=tl),
        grid=grid,
        in_specs=[tok(d), _resident((1, d)), _mod_spec(sc, tl), _mod_spec(sh, tl), _resident((d, N_AB_COLS)),
                  pl.BlockSpec((tl, 128), lambda b, l: (l, 0)),
                  pl.BlockSpec((tl, 128), lambda b, l: (l, 0))],
        out_specs=[tok(N_AB_COLS), cache, cache],
        out_shape=[jax.ShapeDtypeStruct((nseq, seq, N_AB_COLS), BF16),
                   jax.ShapeDtypeStruct((nseq, seq * H_B, 128), F32),
                   jax.ShapeDtypeStruct((nseq, seq * H_B, 128), F32)],
        compiler_params=_params(("parallel", "parallel")),
        name="inproj_ab",
    )(x, g, sc, sh, w, cos, sin)


def _retention_kernel(q_ref, k_ref, v_ref, g_ref, s0_ref, dec_ref, qd_ref, kd_ref, gc_ref, gn_ref,
                      o_ref, sn_ref, *, c, n_chunks):
    states = [s0_ref[0, h] for h in range(H_A)]
    for ci in range(n_chunks):
        rows = slice(ci * c, (ci + 1) * c)
        for h in range(H_A):
            cs = slice(h * 128, (h + 1) * 128)
            q = q_ref[0, rows, cs]
            k = k_ref[0, rows, cs]
            v = v_ref[0, rows, cs]
            state = states[h]
            s = _mm_nt(q, k) * dec_ref[h]
            o = _mm(s.astype(BF16), v) + _mm((q.astype(F32) * qd_ref[h]).astype(BF16), state.astype(BF16))
            states[h] = gc_ref[h] * state + _mm_tn((k.astype(F32) * kd_ref[h]).astype(BF16), v)
            mu = jnp.mean(o, axis=-1, keepdims=True)
            dlt = o - mu
            var = jnp.mean(dlt * dlt, axis=-1, keepdims=True)
            y = dlt * lax.rsqrt(var + EPS) * gn_ref[:, cs] * _silu(g_ref[0, rows, cs].astype(F32))
            o_ref[0, rows, cs] = y.astype(BF16)
    for h in range(H_A):
        sn_ref[0, h] = states[h]


def _retention_consts(c):
    lg = np.log1p(-np.exp2(-5.0 - np.arange(H_A, dtype=np.float32))).astype(np.float32)
    idx = np.arange(c, dtype=np.float32)
    diff = idx[:, None] - idx[None, :]
    decay = np.where(diff[None] >= 0, np.exp(np.maximum(diff, 0.0)[None] * lg[:, None, None]), 0.0)
    qd = np.exp((idx + 1.0)[None, :] * lg[:, None])
    kd = np.exp((c - 1.0 - idx)[None, :] * lg[:, None])
    gc = np.exp(c * lg)
    bc = lambda a: np.ascontiguousarray(np.broadcast_to(a[..., None], a.shape + (128,))).astype(np.float32)
    return decay.astype(np.float32), bc(qd), bc(kd), bc(gc[:, None])


def _retention(qkv, s0, ret_gn, c):
    nseq, seq, _ = qkv.shape
    decay, qd, kd, gc = _retention_consts(c)
    w = H_A * 128
    col = lambda idx: pl.BlockSpec((1, seq, w), lambda b: (b, 0, idx))
    const = lambda shape: pl.BlockSpec(shape, lambda b: (0,) * len(shape))
    state = pl.BlockSpec((1, H_A, DK_A, DV_A), lambda b: (b, 0, 0, 0))
    return pl.pallas_call(
        functools.partial(_retention_kernel, c=c, n_chunks=seq // c),
        grid=(nseq,),
        in_specs=[col(0), col(1), col(2), col(3), state, const((H_A, c, c)), const((H_A, c, 128)),
                  const((H_A, c, 128)), const((H_A, 1, 128)), const((1, w))],
        out_specs=[col(0), state],
        out_shape=[jax.ShapeDtypeStruct((nseq, seq, w), BF16),
                   jax.ShapeDtypeStruct((nseq, H_A, DK_A, DV_A), F32)],
        compiler_params=_params(("parallel",)),
        name="retention",
    )(qkv, qkv, qkv, qkv, s0, decay, qd, kd, gc, ret_gn.reshape(1, w))


def _split_halves(q):
    lane = lax.broadcasted_iota(jnp.int32, q.shape, 1)
    zero = jnp.zeros_like(q)
    return jnp.where(lane < 64, q, zero), jnp.where(lane >= 64, q, zero)


def _lambda(lq_ref, lk_ref, lam_init):
    e = jnp.exp(jnp.sum(lq_ref[...] * lk_ref[...], axis=1, keepdims=True))
    return e[0:1, :] - e[1:2, :] + lam_init


def _diff_epilogue(o, gn, lam_init):
    ms = jnp.mean(o * o, axis=-1, keepdims=True)
    return (o * lax.rsqrt(ms + EPS)) * gn * (1.0 - lam_init)


def _diff_attn_kernel(q_ref, k_ref, v_ref, b_ref, lq_ref, lk_ref, gn_ref, o_ref, s_ref, qs_ref, mx_ref, ls_ref,
                      acc_ref, *, tq, nd, lam_init):
    row = lax.broadcasted_iota(jnp.int32, (tq, tq), 0) // CHUNK
    col = lax.broadcasted_iota(jnp.int32, (tq, tq), 1) // CHUNK
    chunk_mask = col <= row
    lam = _lambda(lq_ref, lk_ref, lam_init)

    def prepare(t):
        q = q_ref[0, t * tq:(t + 1) * tq, :]
        halves = _split_halves((q.astype(F32) * (DH_B ** -0.5)).astype(BF16))
        for i in range(2):
            r = 2 * (t % 2) + i
            qs_ref[r] = halves[i]
            mx_ref[r] = jnp.full((tq, 128), NEG, F32)
            ls_ref[r] = jnp.zeros((tq, 128), F32)
            acc_ref[r] = jnp.zeros((tq, DV_B), F32)

    def scores(t, kb):
        k = k_ref[0, kb * tq:(kb + 1) * tq, :]
        b = b_ref[0, nd - 1 - t + kb]
        for i in range(2):
            r = 2 * (t % 2) + i
            s = _mm_nt(qs_ref[r], k) + b
            if kb == t:
                s = jnp.where(chunk_mask, s, NEG)
            s_ref[r * nd + kb] = s
            m = mx_ref[r]
            for c0 in range(0, tq, 128):
                m = jnp.maximum(m, s[:, c0:c0 + 128])
            mx_ref[r] = m

    def row_max(t):
        for i in range(2):
            r = 2 * (t % 2) + i
            mx_ref[r] = jnp.broadcast_to(jnp.max(mx_ref[r], axis=1, keepdims=True), (tq, 128))

    def weigh(t, kb):
        v = v_ref[0, kb * tq:(kb + 1) * tq, :]
        for i in range(2):
            r = 2 * (t % 2) + i
            m = mx_ref[r]
            l = ls_ref[r]
            ps = []
            for c0 in range(0, tq, 128):
                p = jnp.exp(s_ref[r * nd + kb, :, c0:c0 + 128] - m)
                l = l + p
                ps.append(p.astype(BF16))
            ls_ref[r] = l
            acc_ref[r] += _mm(jnp.concatenate(ps, axis=1), v)

    def finish(t):
        r = 2 * (t % 2)
        l0 = jnp.sum(ls_ref[r], axis=1, keepdims=True)
        l1 = jnp.sum(ls_ref[r + 1], axis=1, keepdims=True)
        o = acc_ref[r] / l0 - lam * (acc_ref[r + 1] / l1)
        o_ref[0, t * tq:(t + 1) * tq, :] = _diff_epilogue(o, gn_ref[...], lam_init).astype(BF16)

    last = nd - 1
    prepare(last)
    for kb in range(last + 1):
        scores(last, kb)
    row_max(last)
    for t in range(last - 1, -1, -1):
        prepare(t)
        for kb in range(t + 2):
            if kb <= t:
                scores(t, kb)
            weigh(t + 1, kb)
        finish(t + 1)
        row_max(t)
    weigh(0, 0)
    finish(0)


def _diff_attn(qkv, bias, lam_q, lam_k, diff_gn, lam_init):
    nseq, seq, _ = qkv.shape
    tq = DIFF_TQ
    nd = seq // tq
    full = lambda off: pl.BlockSpec((1, seq, 128), lambda b, h: (b, 0, off + h))
    small = lambda r, c: pl.BlockSpec((r, c), lambda b, h: (0, 0))
    return pl.pallas_call(
        functools.partial(_diff_attn_kernel, tq=tq, nd=nd, lam_init=lam_init),
        grid=(nseq, H_B),
        in_specs=[full(16), full(20), full(24),
                  pl.BlockSpec((1, nd, tq, tq), lambda b, h: (h, 0, 0, 0)),
                  small(2, DH_B), small(2, DH_B), small(1, DV_B)],
        out_specs=full(0),
        out_shape=jax.ShapeDtypeStruct((nseq, seq, H_B * DV_B), BF16),
        scratch_shapes=[pltpu.VMEM((4 * nd, tq, tq), F32), pltpu.VMEM((4, tq, 128), BF16),
                        pltpu.VMEM((4, tq, 128), F32), pltpu.VMEM((4, tq, 128), F32),
                        pltpu.VMEM((4, tq, DV_B), F32)],
        compiler_params=_params(("parallel", "parallel")),
        name="diff_attn",
    )(qkv, qkv, qkv, bias, lam_q, lam_k, diff_gn.reshape(1, DV_B))


def _diff_attn_sample_kernel(q_ref, kc_ref, vc_ref, kp_ref, vp_ref, b_ref, lq_ref, lk_ref, gn_ref, o_ref,
                             *, past, lam_init):
    scale = DH_B ** -0.5
    lam = _lambda(lq_ref, lk_ref, lam_init)
    for h in range(H_B):
        cs = slice(h * 128, (h + 1) * 128)
        qs = _split_halves(q_ref[0, :, cs])
        kp = kp_ref[0, :, cs].astype(BF16)
        vp = vp_ref[0, :, cs].astype(BF16)
        kc = kc_ref[0, :, cs]
        vc = vc_ref[0, :, cs]
        b = b_ref[h]
        probs = []
        for i in range(2):
            sp = _mm_nt(qs[i], kp) * scale + b[:, :past]
            sc = _mm_nt(qs[i], kc) * scale + b[:, past:]
            m = jnp.maximum(jnp.max(sp, axis=1, keepdims=True), jnp.max(sc, axis=1, keepdims=True))
            pp = jnp.exp(sp - m)
            pc = jnp.exp(sc - m)
            l = jnp.sum(pp, axis=1, keepdims=True) + jnp.sum(pc, axis=1, keepdims=True)
            probs.append((pp / l, pc / l))
        ap = probs[0][0] - lam * probs[1][0]
        ac = probs[0][1] - lam * probs[1][1]
        o = _mm(ap.astype(BF16), vp) + _mm(ac.astype(BF16), vc)
        o_ref[0, :, cs] = _diff_epilogue(o, gn_ref[...], lam_init).astype(BF16)


def _diff_attn_sample(qkv, k_past, v_past, bias, lam_q, lam_k, diff_gn, lam_init):
    nseq, lq, _ = qkv.shape
    past = k_past.shape[1]
    w = H_B * DV_B
    cur = lambda idx: pl.BlockSpec((1, lq, w), lambda b: (b, 0, idx))
    old = pl.BlockSpec((1, past, w), lambda b: (b, 0, 0))
    small = lambda r, c: pl.BlockSpec((r, c), lambda b: (0, 0))
    return pl.pallas_call(
        functools.partial(_diff_attn_sample_kernel, past=past, lam_init=lam_init),
        grid=(nseq,),
        in_specs=[cur(4), cur(5), cur(6), old, old, _resident(bias.shape),
                  small(2, DH_B), small(2, DH_B), small(1, DV_B)],
        out_specs=cur(0),
        out_shape=jax.ShapeDtypeStruct((nseq, lq, w), BF16),
        compiler_params=_params(("parallel",)),
        name="diff_attn_sample",
    )(qkv, qkv, qkv, k_past, v_past, bias, lam_q, lam_k, diff_gn.reshape(1, DV_B))


def _inproj_cd_kernel(x_ref, g_ref, sc_ref, sh_ref, w_ref, main_ref, dt_ref, kc_ref, vc_ref, tail_ref, *, tl, seg):
    h = _norm_mod(x_ref[0], g_ref[...], sc_ref[0], sh_ref[0]).astype(BF16)
    for j in range(6):
        y = _mm(h, w_ref[:, j * 512:(j + 1) * 512])
        main_ref[0, :, j * 512:(j + 1) * 512] = y.astype(BF16)
        if j == 1:
            kc_ref[0] = y
        if j == 2:
            vc_ref[0] = y
        if j >= 4:
            for s in range(tl // seg):
                tail_ref[s, :, (j - 4) * 512:(j - 3) * 512] = y[(s + 1) * seg - 8:(s + 1) * seg, :]
    dt_ref[0] = _mm(h, w_ref[:, N_CD_MAIN:N_CD_PAD])


def _inproj_cd(x, g, sc, sh, w, tl, keep, seg):
    nseq, seq, d = x.shape
    nseg = tl // seg
    assert keep % tl == 0
    skip = (seq - keep) // tl
    tok = lambda n: pl.BlockSpec((1, tl, n), lambda b, l: (b, l, 0))
    kept = pl.BlockSpec((1, tl, 512), lambda b, l: (b, jnp.maximum(l - skip, 0), 0))
    return pl.pallas_call(
        functools.partial(_inproj_cd_kernel, tl=tl, seg=seg),
        grid=(nseq, seq // tl),
        in_specs=[tok(d), _resident((1, d)), _mod_spec(sc, tl), _mod_spec(sh, tl), _resident((d, N_CD_PAD))],
        out_specs=[tok(N_CD_MAIN), tok(128), kept, kept,
                   pl.BlockSpec((nseg, 8, CONV_DIM_D), lambda b, l: (b, 0, 0))],
        out_shape=[jax.ShapeDtypeStruct((nseq, seq, N_CD_MAIN), BF16),
                   jax.ShapeDtypeStruct((nseq, seq, 128), F32),
                   jax.ShapeDtypeStruct((nseq, keep, 512), F32),
                   jax.ShapeDtypeStruct((nseq, keep, 512), F32),
                   jax.ShapeDtypeStruct((nseq * nseg, 8, CONV_DIM_D), F32)],
        compiler_params=_params(("parallel", "arbitrary")),
        name="inproj_cd",
    )(x, g, sc, sh, w)


def _band_kernel(q_ref, k_ref, v_ref, b_ref, o_ref, *, tq):
    t = pl.program_id(1)
    lane = lax.broadcasted_iota(jnp.int32, (tq, 128), 1)
    n_heads = H_C

    def attend(near_start):
        starts, offs = [], []
        for d in range(3):
            kt = t - 2 + d
            starts.append(pl.multiple_of(jnp.maximum(kt, 0) * tq, tq))
            offs.append(jnp.where(kt >= 0, 0.0, NEG))

        def scores(head):
            pr, hi = divmod(head, 2)
            cs = slice(pr * 128, (pr + 1) * 128)
            qh = _split_halves((q_ref[0, :, cs].astype(F32) * (DH_C ** -0.5)).astype(BF16))[hi]
            ss = []
            for d in range(3):
                s = _mm_nt(qh, k_ref[0, pl.ds(starts[d], tq), cs]) + b_ref[head, :, d * tq:(d + 1) * tq]
                ss.append(s + offs[d] if near_start and d < 2 else s)
            mm = ss[2][:, 0:128]
            for d in range(3):
                for c0 in range(0, tq, 128):
                    mm = jnp.maximum(mm, ss[d][:, c0:c0 + 128])
            return ss, jnp.max(mm, axis=1, keepdims=True)

        def weigh(head, ss, m):
            cs = slice((head // 2) * 128, (head // 2 + 1) * 128)
            ls = jnp.zeros((tq, 128), F32)
            o = jnp.zeros((tq, 128), F32)
            for d in range(3):
                p = jnp.exp(ss[d] - m)
                for c0 in range(0, tq, 128):
                    ls = ls + p[:, c0:c0 + 128]
                o = o + _mm(p.astype(BF16), v_ref[0, pl.ds(starts[d], tq), cs])
            return o / jnp.sum(ls, axis=1, keepdims=True)

        outs = []
        pending = scores(0)
        for head in range(n_heads):
            nxt = scores(head + 1) if head + 1 < n_heads else None
            outs.append(weigh(head, *pending))
            pending = nxt
            if head % 2 == 1:
                cs = slice((head // 2) * 128, (head // 2 + 1) * 128)
                o_ref[0, :, cs] = jnp.where(lane < 64, outs[head - 1], outs[head]).astype(BF16)

    @pl.when(t < 2)
    def _():
        attend(True)

    @pl.when(t >= 2)
    def _():
        attend(False)


def _band_attn(main, bias):
    nseq, seq, _ = main.shape
    tq = BAND_TQ
    w = H_C * DH_C
    full = lambda idx: pl.BlockSpec((1, seq, w), lambda b, t: (b, 0, idx))
    return pl.pallas_call(
        functools.partial(_band_kernel, tq=tq),
        grid=(nseq, seq // tq),
        in_specs=[pl.BlockSpec((1, tq, w), lambda b, t: (b, t, 0)), full(1), full(2), _resident(bias.shape)],
        out_specs=pl.BlockSpec((1, tq, w), lambda b, t: (b, t, 0)),
        out_shape=jax.ShapeDtypeStruct((nseq, seq, w), BF16),
        compiler_params=_params(("parallel", "parallel")),
        name="band_attn",
    )(main, main, main, bias)


def _band_sample_kernel(q_ref, kc_ref, vc_ref, kp_ref, vp_ref, b_ref, o_ref, *, past):
    scale = DH_C ** -0.5
    for pr in range(H_C // 2):
        cs = slice(pr * 128, (pr + 1) * 128)
        halves = _split_halves(q_ref[0, :, cs])
        kp = kp_ref[0, :, cs].astype(BF16)
        vp = vp_ref[0, :, cs].astype(BF16)
        kc = kc_ref[0, :, cs]
        vc = vc_ref[0, :, cs]
        outs = []
        for hi, qh in enumerate(halves):
            b = b_ref[2 * pr + hi]
            sp = _mm_nt(qh, kp) * scale + b[:, :past]
            sc = _mm_nt(qh, kc) * scale + b[:, past:]
            m = jnp.maximum(jnp.max(sp, axis=1, keepdims=True), jnp.max(sc, axis=1, keepdims=True))
            pp = jnp.exp(sp - m)
            pc = jnp.exp(sc - m)
            l = jnp.sum(pp, axis=1, keepdims=True) + jnp.sum(pc, axis=1, keepdims=True)
            outs.append(_mm((pp / l).astype(BF16), vp) + _mm((pc / l).astype(BF16), vc))
        lane = lax.broadcasted_iota(jnp.int32, outs[0].shape, 1)
        o_ref[0, :, cs] = jnp.where(lane < 64, outs[0], outs[1]).astype(BF16)


def _band_attn_sample(main, k_past, v_past, bias):
    nseq, lq, _ = main.shape
    past = k_past.shape[1]
    w = H_C * DH_C
    cur = lambda idx: pl.BlockSpec((1, lq, w), lambda b: (b, 0, idx))
    old = pl.BlockSpec((1, past, w), lambda b: (b, 0, 0))
    return pl.pallas_call(
        functools.partial(_band_sample_kernel, past=past),
        grid=(nseq,),
        in_specs=[cur(0), cur(1), cur(2), old, old, _resident(bias.shape)],
        out_specs=cur(0),
        out_shape=jax.ShapeDtypeStruct((nseq, lq, w), BF16),
        compiler_params=_params(("parallel",)),
        name="band_attn_sample",
    )(main, main, main, k_past, v_past, bias)


def _ssd_kernel(z_ref, xs_ref, bm_ref, cm_ref, dt_ref, cw_ref, cb_ref, dtb_ref, alog_ref, dskip_ref, ng_ref,
                past_ref, s0_ref, o_ref, sn_ref, buf_ref, st_ref, *, c):
    @pl.when(pl.program_id(1) == 0)
    def _():
        buf_ref[0:8, :] = past_ref[0]
        st_ref[...] = s0_ref[0]

    buf_ref[8:8 + c, 0:D_INNER] = xs_ref[0].astype(F32)
    buf_ref[8:8 + c, D_INNER:D_INNER + 256] = bm_ref[0].astype(F32)
    buf_ref[8:8 + c, D_INNER + 256:CONV_DIM_D] = cm_ref[0].astype(F32)
    conv = cb_ref[...] + buf_ref[pl.ds(5, c), :] * cw_ref[0:1, :]
    for i in range(1, CONV_D):
        conv = conv + buf_ref[pl.ds(5 + i, c), :] * cw_ref[i:i + 1, :]
    buf_ref[0:8, :] = buf_ref[c:c + 8, :]
    act = _silu(conv)
    xs = act[:, 0:D_INNER]
    bm = act[:, D_INNER:D_INNER + 256].astype(BF16)
    cm = act[:, D_INNER + 256:CONV_DIM_D].astype(BF16)

    dt = _softplus(dt_ref[0] + dtb_ref[...])
    da = dt * (-jnp.exp(alog_ref[...]))
    row = lax.broadcasted_iota(jnp.int32, (c, c), 0)
    col = lax.broadcasted_iota(jnp.int32, (c, c), 1)
    tri = row >= col
    cum = _mm_exact(tri.astype(F32), da)
    eye = (lax.broadcasted_iota(jnp.int32, (8, 128), 0) == lax.broadcasted_iota(jnp.int32, (8, 128), 1)).astype(F32)
    cum_t = _mm_exact(_mm_nt_exact(eye, da), (row <= col).astype(F32))
    dt_t = _mm_nt_exact(eye, dt)
    ecum = jnp.exp(cum)
    last = cum[c - 1:c, :]
    wgt = jnp.exp(last - cum) * dt
    elast = jnp.exp(last)
    lane = lax.broadcasted_iota(jnp.int32, (c, 128), 1)
    low = lane < 64
    rlow = lax.broadcasted_iota(jnp.int32, (128, 128), 0) < 64

    for g in range(G_D):
        bm_g = bm[:, g * 128:(g + 1) * 128]
        cm_g = cm[:, g * 128:(g + 1) * 128]
        cb = _mm_nt(cm_g, bm_g)
        ys = []
        for pp in range(2):
            p = 2 * g + pp
            h0, h1 = 2 * p, 2 * p + 1
            x_f = xs[:, p * 128:(p + 1) * 128]
            x_b = x_f.astype(BF16)
            y_in = []
            for hh in (h0, h1):
                seg = cum[:, hh:hh + 1] - cum_t[hh:hh + 1, :]
                dec = jnp.exp(jnp.where(tri, seg, NEG))
                y_in.append(_mm((cb * dec * dt_t[hh:hh + 1, :]).astype(BF16), x_b))
            st = st_ref[p * 128:(p + 1) * 128, :]
            y_x = _mm_nt(cm_g, st.astype(BF16)) * jnp.where(low, ecum[:, h0:h0 + 1], ecum[:, h1:h1 + 1])
            w2 = jnp.where(low, wgt[:, h0:h0 + 1], wgt[:, h1:h1 + 1])
            new = jnp.where(rlow, elast[:, h0:h0 + 1], elast[:, h1:h1 + 1]) * st + _mm_tn((x_f * w2).astype(BF16), bm_g)
            st_ref[p * 128:(p + 1) * 128, :] = new
            sn_ref[0, p * 128:(p + 1) * 128, :] = new
            y = jnp.where(low, y_in[0], y_in[1]) + y_x + jnp.where(low, dskip_ref[h0], dskip_ref[h1]) * x_f
            ys.append(y * _silu(z_ref[0, :, p * 128:(p + 1) * 128].astype(F32)))
        ms = (jnp.sum(ys[0] * ys[0], axis=-1, keepdims=True) + jnp.sum(ys[1] * ys[1], axis=-1, keepdims=True)) / 256.0
        inv = lax.rsqrt(ms + EPS)
        for pp in range(2):
            p = 2 * g + pp
            o_ref[0, :, p * 128:(p + 1) * 128] = (ys[pp] * inv * ng_ref[:, p * 128:(p + 1) * 128]).astype(BF16)


def _ssd(main, dt, conv_w, conv_b, dt_bias, a_log, d_skip, norm_g, conv_past, s0, c):
    nseq, seq, _ = main.shape
    pad = lambda a: jnp.pad(a.reshape(1, H_D), ((0, 0), (0, 128 - H_D)))
    blk = lambda w, idx: pl.BlockSpec((1, c, w), lambda b, l: (b, l, idx))
    const = lambda r, w: pl.BlockSpec((r, w), lambda b, l: (0, 0))
    state = pl.BlockSpec((1, H_D * P_D, N_D), lambda b, l: (b, 0, 0))
    return pl.pallas_call(
        functools.partial(_ssd_kernel, c=c),
        grid=(nseq, seq // c),
        in_specs=[blk(512, 3), blk(512, 4), blk(256, 10), blk(256, 11), blk(128, 0),
                  const(CONV_D, CONV_DIM_D), const(1, CONV_DIM_D), const(1, 128), const(1, 128),
                  pl.BlockSpec(memory_space=pltpu.SMEM), const(1, D_INNER),
                  pl.BlockSpec((1, 8, CONV_DIM_D), lambda b, l: (b, 0, 0)), state],
        out_specs=[blk(512, 0), state],
        out_shape=[jax.ShapeDtypeStruct((nseq, seq, D_INNER), BF16),
                   jax.ShapeDtypeStruct((nseq, H_D * P_D, N_D), F32)],
        scratch_shapes=[pltpu.VMEM((c + 8, CONV_DIM_D), F32), pltpu.VMEM((H_D * P_D, N_D), F32)],
        compiler_params=_params(("parallel", "arbitrary")),
        name="ssd",
    )(main, main, main, main, dt, conv_w, conv_b.reshape(1, CONV_DIM_D), pad(dt_bias), pad(a_log), d_skip,
      norm_g.reshape(1, D_INNER), conv_past, s0)


def _ffn_kernel(x_ref, o1_ref, o2_ref, wo_ref, g1_ref, ng_ref, sc_ref, sh_ref, g2_ref, wup_ref, cw_ref, cb_ref,
                wdn_ref, past_ref, fg_ref, out_ref, tail_ref, act_ref, wb_ref, gt_ref, *, tl, d_ff, final, seg):
    @pl.when(pl.program_id(1) == 0)
    def _():
        gt_ref[...] = past_ref[...]

    for _ in _ffn_steps(x_ref, o1_ref.at[0], o2_ref.at[0], wo_ref, g1_ref, ng_ref, sc_ref, sh_ref, g2_ref, wup_ref,
                        cw_ref, cb_ref, wdn_ref, fg_ref, out_ref, tail_ref, act_ref, wb_ref, gt_ref,
                        tl=tl, d_ff=d_ff, final=final, seg=seg):
        pass


def _outproj_ffn(x, o1, o2, wo, g1, ng, sc, sh, g2, wup, cw, cb, wdn, past, fg, tl, seg, final):
    nseq, seq, d = x.shape
    d_ff = wdn.shape[0]
    nseg = tl // seg
    assert nseg == 1 or seq == tl
    tok = lambda n: pl.BlockSpec((1, tl, n), lambda b, l: (b, l, 0))
    tail = pl.BlockSpec((nseg, 8, d_ff), lambda b, l: (b, 0, 0))
    return pl.pallas_call(
        functools.partial(_ffn_kernel, tl=tl, d_ff=d_ff, final=final, seg=seg),
        grid=(nseq, seq // tl),
        in_specs=[tok(d), tok(o1.shape[-1]), tok(o2.shape[-1]), _resident(wo.shape), _mod_spec(g1, tl),
                  _resident((1, d)), _mod_spec(sc, tl), _mod_spec(sh, tl), _mod_spec(g2, tl), _resident(wup.shape),
                  _resident(cw.shape), _resident((1, d_ff)), _resident(wdn.shape), tail, _resident((1, d))],
        out_specs=[tok(d), tail],
        out_shape=[jax.ShapeDtypeStruct((nseq, seq, d), F32), jax.ShapeDtypeStruct((nseq * nseg, 8, d_ff), F32)],
        scratch_shapes=[pltpu.VMEM((tl, d_ff), BF16), pltpu.VMEM((nseg * (seg + 8), FF_CHUNK), F32),
                        pltpu.VMEM((nseg, 8, d_ff), F32)],
        compiler_params=_params(("parallel", "arbitrary")),
        name="outproj_ffn",
    )(x, o1, o2, wo, g1, ng, sc, sh, g2, wup, cw, cb.reshape(1, d_ff), wdn, past, fg)


def _interleave(gens, strides):
    live = list(zip(gens, strides))
    while live:
        for item in list(live):
            for _ in range(item[1]):
                try:
                    next(item[0])
                except StopIteration:
                    live.remove(item)
                    break


def _chain(*gens):
    for g in gens:
        yield from g


def _ffn_steps(x_ref, o1, o2, wo_ref, g1_ref, ng_ref, sc_ref, sh_ref, g2_ref, wup_ref, cw_ref, cb_ref, wdn_ref,
               fg_ref, out_ref, tail_ref, act_ref, wb_ref, gt_ref, *, tl, d_ff, final, seg):
    nseg = tl // seg
    stride = seg + 8
    half = wo_ref.shape[0] // 2
    mix = _mm(o1[...], wo_ref[0:half, :]) + _mm(o2[...], wo_ref[half:2 * half, :])
    x1 = x_ref[0] + g1_ref[0] * mix
    out_ref[0] = x1
    h = _norm_mod(x1, ng_ref[...], sc_ref[0], sh_ref[0]).astype(BF16)
    yield
    for j in range(d_ff // FF_CHUNK):
        c0 = j * FF_CHUNK
        a = _mm(h, wup_ref[:, c0:c0 + FF_CHUNK])
        g = _mm(h, wup_ref[:, d_ff + c0:d_ff + c0 + FF_CHUNK])
        for s in range(nseg):
            wb_ref[s * stride:s * stride + 8, :] = gt_ref[s, :, c0:c0 + FF_CHUNK]
            wb_ref[s * stride + 8:(s + 1) * stride, :] = g[s * seg:(s + 1) * seg, :]
            gt_ref[s, :, c0:c0 + FF_CHUNK] = g[(s + 1) * seg - 8:(s + 1) * seg, :]
        back2 = [wb_ref[pl.ds(s * stride + 6, seg), :] for s in range(nseg)]
        back1 = [wb_ref[pl.ds(s * stride + 7, seg), :] for s in range(nseg)]
        if nseg > 1:
            back2, back1 = [jnp.concatenate(back2, axis=0)], [jnp.concatenate(back1, axis=0)]
        gc = (cb_ref[:, c0:c0 + FF_CHUNK] + back2[0] * cw_ref[0:1, c0:c0 + FF_CHUNK]
              + back1[0] * cw_ref[1:2, c0:c0 + FF_CHUNK] + g * cw_ref[2:3, c0:c0 + FF_CHUNK])
        act_ref[:, c0:c0 + FF_CHUNK] = (a * _gelu_tanh(gc)).astype(BF16)
        yield
    split = (d_ff // FF_CHUNK + 1) // 2 * FF_CHUNK
    dn = _mm(act_ref[:, 0:split], wdn_ref[0:split, :])
    yield
    dn = dn + _mm(act_ref[:, split:d_ff], wdn_ref[split:d_ff, :])
    x2 = out_ref[0] + g2_ref[0] * dn
    if final:
        ms = jnp.mean(x2 * x2, axis=-1, keepdims=True)
        x2 = (x2 * lax.rsqrt(ms + EPS)) * fg_ref[...]
    out_ref[0] = x2
    tail_ref[...] = gt_ref[...]
    yield


def _band_steps(q, k_ref, v_ref, b_ref, o, t, *, tq):
    lane = lax.broadcasted_iota(jnp.int32, (tq, 128), 1)
    starts, offs = [], []
    for d in range(3):
        kt = t - 2 + d
        starts.append(pl.multiple_of(jnp.maximum(kt, 0) * tq, tq))
        offs.append(jnp.where(kt >= 0, 0.0, NEG))

    def scores(head):
        pr, hi = divmod(head, 2)
        cs = slice(pr * 128, (pr + 1) * 128)
        qh = _split_halves((q[:, cs].astype(F32) * (DH_C ** -0.5)).astype(BF16))[hi]
        ss = []
        for d in range(3):
            s = _mm_nt(qh, k_ref[0, pl.ds(starts[d], tq), cs]) + b_ref[head, :, d * tq:(d + 1) * tq]
            ss.append(s + offs[d] if d < 2 else s)
        mm = ss[2][:, 0:128]
        for d in range(3):
            for c0 in range(0, tq, 128):
                mm = jnp.maximum(mm, ss[d][:, c0:c0 + 128])
        return ss, jnp.max(mm, axis=1, keepdims=True)

    def weigh(head, ss, m):
        cs = slice((head // 2) * 128, (head // 2 + 1) * 128)
        ls = jnp.zeros((tq, 128), F32)
        acc = jnp.zeros((tq, 128), F32)
        for d in range(3):
            p = jnp.exp(ss[d] - m)
            for c0 in range(0, tq, 128):
                ls = ls + p[:, c0:c0 + 128]
            acc = acc + _mm(p.astype(BF16), v_ref[0, pl.ds(starts[d], tq), cs])
        return acc / jnp.sum(ls, axis=1, keepdims=True)

    outs = []
    pending = scores(0)
    yield
    for head in range(H_C):
        nxt = scores(head + 1) if head + 1 < H_C else None
        outs.append(weigh(head, *pending))
        pending = nxt
        if head % 2 == 1:
            cs = slice((head // 2) * 128, (head // 2 + 1) * 128)
            o[:, cs] = jnp.where(lane < 64, outs[head - 1], outs[head]).astype(BF16)
        yield


def _ssd_steps(z, xs_in, bm_in, cm_in, dt_in, cw_ref, cb_ref, dtb_ref, alog_ref, dskip_ref, ng_ref, o,
               buf_ref, st_ref, *, c):
    buf_ref[8:8 + c, 0:D_INNER] = xs_in[...].astype(F32)
    buf_ref[8:8 + c, D_INNER:D_INNER + 256] = bm_in[...].astype(F32)
    buf_ref[8:8 + c, D_INNER + 256:CONV_DIM_D] = cm_in[...].astype(F32)
    conv = cb_ref[...] + buf_ref[pl.ds(5, c), :] * cw_ref[0:1, :]
    for i in range(1, CONV_D):
        conv = conv + buf_ref[pl.ds(5 + i, c), :] * cw_ref[i:i + 1, :]
    buf_ref[0:8, :] = buf_ref[c:c + 8, :]
    act = _silu(conv)
    xs = act[:, 0:D_INNER]
    bm = act[:, D_INNER:D_INNER + 256].astype(BF16)
    cm = act[:, D_INNER + 256:CONV_DIM_D].astype(BF16)
    yield

    dt = _softplus(dt_in[...] + dtb_ref[...])
    da = dt * (-jnp.exp(alog_ref[...]))
    row = lax.broadcasted_iota(jnp.int32, (c, c), 0)
    col = lax.broadcasted_iota(jnp.int32, (c, c), 1)
    tri = row >= col
    cum = _mm_exact(tri.astype(F32), da)
    eye = (lax.broadcasted_iota(jnp.int32, (8, 128), 0) == lax.broadcasted_iota(jnp.int32, (8, 128), 1)).astype(F32)
    cum_t = _mm_exact(_mm_nt_exact(eye, da), (row <= col).astype(F32))
    dt_t = _mm_nt_exact(eye, dt)
    ecum = jnp.exp(cum)
    last = cum[c - 1:c, :]
    wgt = jnp.exp(last - cum) * dt
    elast = jnp.exp(last)
    lane = lax.broadcasted_iota(jnp.int32, (c, 128), 1)
    low = lane < 64
    rlow = lax.broadcasted_iota(jnp.int32, (128, 128), 0) < 64
    yield

    for g in range(G_D):
        bm_g = bm[:, g * 128:(g + 1) * 128]
        cm_g = cm[:, g * 128:(g + 1) * 128]
        cb = _mm_nt(cm_g, bm_g)
        ys = []
        for pp in range(2):
            p = 2 * g + pp
            h0, h1 = 2 * p, 2 * p + 1
            x_f = xs[:, p * 128:(p + 1) * 128]
            x_b = x_f.astype(BF16)
            y_in = []
            for hh in (h0, h1):
                seg = cum[:, hh:hh + 1] - cum_t[hh:hh + 1, :]
                dec = jnp.exp(jnp.where(tri, seg, NEG))
                y_in.append(_mm((cb * dec * dt_t[hh:hh + 1, :]).astype(BF16), x_b))
            st = st_ref[p * 128:(p + 1) * 128, :]
            y_x = _mm_nt(cm_g, st.astype(BF16)) * jnp.where(low, ecum[:, h0:h0 + 1], ecum[:, h1:h1 + 1])
            w2 = jnp.where(low, wgt[:, h0:h0 + 1], wgt[:, h1:h1 + 1])
            st_ref[p * 128:(p + 1) * 128, :] = (jnp.where(rlow, elast[:, h0:h0 + 1], elast[:, h1:h1 + 1]) * st
                                                + _mm_tn((x_f * w2).astype(BF16), bm_g))
            y = jnp.where(low, y_in[0], y_in[1]) + y_x + jnp.where(low, dskip_ref[h0], dskip_ref[h1]) * x_f
            ys.append(y * _silu(z[:, p * 128:(p + 1) * 128].astype(F32)))
            yield
        ms = (jnp.sum(ys[0] * ys[0], axis=-1, keepdims=True) + jnp.sum(ys[1] * ys[1], axis=-1, keepdims=True)) / 256.0
        inv = lax.rsqrt(ms + EPS)
        for pp in range(2):
            p = 2 * g + pp
            o[:, p * 128:(p + 1) * 128] = (ys[pp] * inv * ng_ref[:, p * 128:(p + 1) * 128]).astype(BF16)
    yield


def _cd_ffn_kernel(x_ref, wo_ref, g1_ref, ng_ref, sc_ref, sh_ref, g2_ref, wup_ref, cwf_ref, cbf_ref, wdn_ref, fg_ref,
                   q_ref, k_ref, v_ref, bb_ref,
                   z_ref, xs_ref, bm_ref, cm_ref, dt_ref, cws_ref, cbs_ref, dtb_ref, alog_ref, dskip_ref, ngs_ref,
                   out_ref, tail_ref, sn_ref,
                   obuf_ref, act_ref, wb_ref, gt_ref, buf_ref, st_ref,
                   *, tl, d_ff, final, n_tiles, tiles_per_seq, c, tq):
    g = pl.program_id(0)
    gm = jnp.minimum(g, n_tiles - 1)
    gf = jnp.maximum(g - 1, 0)
    jm = gm % tiles_per_seq
    jf = gf % tiles_per_seq
    slot = g % 2
    prev = (g + 1) % 2

    @pl.when(g == 0)
    def _():
        obuf_ref[...] = jnp.zeros(obuf_ref.shape, BF16)

    @pl.when(jm == 0)
    def _():
        buf_ref[0:8, :] = jnp.zeros((8, CONV_DIM_D), F32)
        st_ref[...] = jnp.zeros(st_ref.shape, F32)

    @pl.when(jf == 0)
    def _():
        gt_ref[...] = jnp.zeros(gt_ref.shape, F32)

    ffn = _ffn_steps(x_ref, obuf_ref.at[prev, :, 0:512], obuf_ref.at[prev, :, 512:1024], wo_ref, g1_ref, ng_ref,
                     sc_ref, sh_ref, g2_ref, wup_ref, cwf_ref, cbf_ref, wdn_ref, fg_ref, out_ref, tail_ref, act_ref,
                     wb_ref, gt_ref, tl=tl, d_ff=d_ff, final=final, seg=tl)
    band = _chain(*[_band_steps(q_ref.at[0, ti * tq:(ti + 1) * tq, :], k_ref, v_ref, bb_ref,
                                obuf_ref.at[slot, ti * tq:(ti + 1) * tq, 0:512],
                                jm * (tl // tq) + ti, tq=tq) for ti in range(tl // tq)])
    ssd = _chain(*[_ssd_steps(z_ref.at[0, ci * c:(ci + 1) * c, :], xs_ref.at[0, ci * c:(ci + 1) * c, :],
                              bm_ref.at[0, ci * c:(ci + 1) * c, :], cm_ref.at[0, ci * c:(ci + 1) * c, :],
                              dt_ref.at[0, ci * c:(ci + 1) * c, :], cws_ref, cbs_ref, dtb_ref, alog_ref, dskip_ref,
                              ngs_ref, obuf_ref.at[slot, ci * c:(ci + 1) * c, 512:1024], buf_ref, st_ref, c=c)
                   for ci in range(tl // c)])
    _interleave([ffn, _chain(band, ssd)], (1, 1))

    @pl.when(g < n_tiles)
    def _():
        sn_ref[0] = st_ref[...]


def _cd_mixers_ffn(x, main, dt, bias, wo, g1, ng, sc, sh, g2, wup, cwf, cbf, wdn, fg,
                   conv_w, conv_b, dt_bias, a_log, d_skip, norm_g, tl, final):
    nseq, seq, d = x.shape
    d_ff = wdn.shape[0]
    tps = seq // tl
    n_tiles = nseq * tps
    c = SCAN_CHUNK
    tq = BAND_TQ
    pad = lambda a: jnp.pad(a.reshape(1, H_D), ((0, 0), (0, 128 - H_D)))
    mix = lambda g: jnp.minimum(g, n_tiles - 1)
    ffn = lambda g: jnp.maximum(g - 1, 0)
    tok_f = lambda n: pl.BlockSpec((1, tl, n), lambda g: (ffn(g) // tps, ffn(g) % tps, 0))
    seq_f = lambda n: pl.BlockSpec((1, 1, n), lambda g: (ffn(g) // tps, 0, 0))
    tok_m = lambda w, idx: pl.BlockSpec((1, tl, w), lambda g: (mix(g) // tps, mix(g) % tps, idx))
    full_m = lambda idx: pl.BlockSpec((1, seq, 512), lambda g: (mix(g) // tps, 0, idx),
                                      pipeline_mode=pl.Buffered(1))
    const = lambda r, w: pl.BlockSpec((r, w), lambda g: (0, 0))
    return pl.pallas_call(
        functools.partial(_cd_ffn_kernel, tl=tl, d_ff=d_ff, final=final, n_tiles=n_tiles, tiles_per_seq=tps,
                          c=c, tq=tq),
        grid=(n_tiles + 1,),
        in_specs=[tok_f(d), _resident(wo.shape), seq_f(d), _resident((1, d)), seq_f(d), seq_f(d), seq_f(d),
                  _resident(wup.shape), _resident(cwf.shape), _resident((1, d_ff)), _resident(wdn.shape),
                  _resident((1, d)),
                  tok_m(512, 0), full_m(1), full_m(2), _resident(bias.shape),
                  tok_m(512, 3), tok_m(512, 4), tok_m(256, 10), tok_m(256, 11), tok_m(128, 0),
                  const(CONV_D, CONV_DIM_D), const(1, CONV_DIM_D), const(1, 128), const(1, 128),
                  pl.BlockSpec(memory_space=pltpu.SMEM), const(1, D_INNER)],
        out_specs=[tok_f(d),
                   pl.BlockSpec((1, 8, d_ff), lambda g: (ffn(g) // tps, 0, 0)),
                   pl.BlockSpec((1, H_D * P_D, N_D), lambda g: (mix(g) // tps, 0, 0))],
        out_shape=[jax.ShapeDtypeStruct((nseq, seq, d), F32), jax.ShapeDtypeStruct((nseq, 8, d_ff), F32),
                   jax.ShapeDtypeStruct((nseq, H_D * P_D, N_D), F32)],
        scratch_shapes=[pltpu.VMEM((2, tl, 1024), BF16), pltpu.VMEM((tl, d_ff), BF16),
                        pltpu.VMEM((tl + 8, FF_CHUNK), F32), pltpu.VMEM((1, 8, d_ff), F32),
                        pltpu.VMEM((c + 8, CONV_DIM_D), F32), pltpu.VMEM((H_D * P_D, N_D), F32)],
        compiler_params=_params(("arbitrary",)),
        name="cd_mixers_ffn",
    )(x, wo, g1, ng, sc, sh, g2, wup, cwf, cbf.reshape(1, d_ff), wdn, fg,
      main, main, main, bias,
      main, main, main, main, dt, conv_w, conv_b.reshape(1, CONV_DIM_D), pad(dt_bias), pad(a_log), d_skip,
      norm_g.reshape(1, D_INNER))


def _rope_tables(pos):
    half = DK_A // 2
    inv = jnp.power(ROPE_BASE, -jnp.arange(half, dtype=F32) / half)
    ang = pos.astype(F32)[:, None] * inv[None, :]
    cos = jnp.cos(ang)
    sin = jnp.sin(ang)
    return jnp.concatenate([cos, cos], axis=1), jnp.concatenate([-sin, sin], axis=1)


def _pad_rows(a, rows=8):
    return jnp.pad(a, ((0, 0), (rows - a.shape[1], 0), (0, 0)))


def _trunk(x, mods, pos, weights, caches, biases):
    nseq, seq, d = x.shape
    sample = caches is not None
    c = min(seq, SCAN_CHUNK)
    if sample:
        tl, seg = nseq * seq, seq
        pack = lambda a: a.reshape(1, nseq * seq, a.shape[-1])
        unpack = lambda a: a.reshape(nseq, seq, a.shape[-1])
        rows = lambda m: jnp.repeat(m, seq, axis=1).reshape(1, nseq * seq, d)
    else:
        tl = seg = min(seq, 512)
        pack = unpack = rows = lambda a: a
    depth = weights["w_up"].shape[0]
    outs = {k: [] for k in ("ret", "bk", "bv", "ck", "cv", "dconv", "dssm", "ffn")}
    cos, sin = _rope_tables(pos)
    if sample:
        cos, sin = jnp.tile(cos, (nseq, 1)), jnp.tile(sin, (nseq, 1))
    x = pack(x)
    for l in range(depth):
        i = l // 2
        sh1, sc1, g1, sh2, sc2, g2 = [rows(m) for m in mods[l]]
        ng1 = weights["norm_g"][l, 0].reshape(1, d)
        ng2 = weights["norm_g"][l, 1].reshape(1, d)
        fused = None
        if l % 2 == 0:
            qkv, kb, vb = _inproj_ab(x, ng1, sc1, sh1, weights["w_in_ab"][i], cos, sin, tl)
            qkv = unpack(qkv)
            s0 = caches["ret"][i] if sample else jnp.zeros((nseq, H_A, DK_A, DV_A), F32)
            o1, s_new = _retention(qkv, s0, weights["ret_gn"][i], c)
            lam_init = 0.8 - 0.6 * math.exp(-0.3 * l)
            if sample:
                o2 = _diff_attn_sample(qkv, caches["bk"][i], caches["bv"][i], biases["t5"],
                                       weights["lam_q"][i], weights["lam_k"][i], weights["diff_gn"][i], lam_init)
            else:
                o2 = _diff_attn(qkv, biases["t5"], weights["lam_q"][i], weights["lam_k"][i],
                                weights["diff_gn"][i], lam_init)
            wo = weights["w_out_ab"][i]
            outs["ret"].append(s_new)
            outs["bk"].append(kb.reshape(nseq, seq, H_B, 2 * DH_B))
            outs["bv"].append(vb.reshape(nseq, seq, H_B, DV_B))
        else:
            keep = seq if sample else min(C_WINDOW, seq)
            main, dt, kc, vc, tail = _inproj_cd(x, ng1, sc1, sh1, weights["w_in_cd"][i], tl,
                                                keep * (tl // seg), seg)
            main, dt = unpack(main), unpack(dt)
            if sample:
                o1 = _band_attn_sample(main, caches["ck"][i], caches["cv"][i], biases["band"][i])
                conv_past = _pad_rows(caches["dconv"][i])
                s0 = caches["dssm"][i].reshape(nseq, H_D * P_D, N_D)
                o2, ssm_new = _ssd(main, dt, weights["d_conv_w"][i], weights["d_conv_b"][i], weights["d_dt_bias"][i],
                                   weights["d_a_log"][i], weights["d_skip"][i], weights["d_norm_g"][i], conv_past, s0,
                                   c)
            else:
                fused = _cd_mixers_ffn(
                    x, main, dt, biases["band"][i], weights["w_out_cd"][i], g1, ng2, sc2, sh2, g2, weights["w_up"][l],
                    weights["ffn_conv_w"][l], weights["ffn_conv_b"][l], weights["w_down"][l],
                    weights["final_g"].reshape(1, d), weights["d_conv_w"][i], weights["d_conv_b"][i],
                    weights["d_dt_bias"][i], weights["d_a_log"][i], weights["d_skip"][i], weights["d_norm_g"][i],
                    FUSED_TL, final=(l == depth - 1))
                ssm_new = fused[2]
            wo = weights["w_out_cd"][i]
            outs["ck"].append(kc.reshape(nseq, keep, H_C, DH_C))
            outs["cv"].append(vc.reshape(nseq, keep, H_C, DH_C))
            outs["dconv"].append(tail[:, 8 - (CONV_D - 1):])
            outs["dssm"].append(ssm_new.reshape(nseq, H_D, P_D, N_D))
        d_ff = weights["w_down"].shape[1]
        if fused is not None:
            x, ftail = fused[0], fused[1]
        else:
            ffn_past = _pad_rows(caches["ffn"][l]) if sample else jnp.zeros((nseq, 8, d_ff), F32)
            x, ftail = _outproj_ffn(x, pack(o1), pack(o2), wo, g1, ng2, sc2, sh2, g2, weights["w_up"][l],
                                    weights["ffn_conv_w"][l], weights["ffn_conv_b"][l], weights["w_down"][l], ffn_past,
                                    weights["final_g"].reshape(1, d), tl, seg, final=(l == depth - 1))
        outs["ffn"].append(ftail[:, 8 - (CONV_F - 1):])
    stk = lambda t: jnp.stack(t).astype(F32)
    x = unpack(x)
    return (x,) + tuple(stk(outs[k]) for k in ("ret", "bk", "bv", "ck", "cv", "dconv", "dssm", "ffn"))


def kernel(x_prompt, x_sample, cache_ret_state, cache_b_k, cache_b_v, cache_c_k, cache_c_v, state_d_conv, state_d_ssm, state_ffn_conv, c_prompt, c_sample, w_mod, b_mod, norm_g, final_g, t5_table, w_in_ab, w_out_ab, ret_gn, lam_q, lam_k, diff_gn, w_in_cd, w_out_cd, rel_table, d_conv_w, d_conv_b, d_dt_bias, d_a_log, d_skip, d_norm_g, w_up, ffn_conv_w, ffn_conv_b, w_down):
    batch, seq, d = x_prompt.shape
    dec_batch, dec_seq, _ = x_sample.shape
    past = cache_b_k.shape[2]
    depth = w_mod.shape[0]
    assert dec_seq <= CHUNK and past % CHUNK == 0 and cache_c_k.shape[2] == C_WINDOW
    assert seq % BAND_TQ == 0 and seq % DIFF_TQ == 0

    w_in_cd_p = jnp.pad(w_in_cd, ((0, 0), (0, 0), (0, N_CD_PAD - w_in_cd.shape[-1]))).astype(BF16)
    weights = dict(
        norm_g=norm_g, final_g=final_g, w_in_ab=w_in_ab.astype(BF16), w_out_ab=w_out_ab.astype(BF16),
        ret_gn=ret_gn, lam_q=lam_q, lam_k=lam_k, diff_gn=diff_gn, w_in_cd=w_in_cd_p,
        w_out_cd=w_out_cd.astype(BF16), d_conv_w=d_conv_w, d_conv_b=d_conv_b, d_dt_bias=d_dt_bias, d_a_log=d_a_log,
        d_skip=d_skip, d_norm_g=d_norm_g, w_up=w_up.astype(BF16), ffn_conv_w=ffn_conv_w, ffn_conv_b=ffn_conv_b,
        w_down=w_down.astype(BF16))

    mod = _modulation(jnp.concatenate([c_prompt, c_sample], axis=0), w_mod, b_mod)

    def pieces(rows):
        return [[m[:, None, :] for m in jnp.split(mod[l, rows], 6, axis=-1)] for l in range(depth)]

    band = [_band_bias(rel_table[i]) for i in range(rel_table.shape[0])]
    biases_p = dict(t5=_t5_bias_prompt(t5_table, seq), band=band)
    biases_s = dict(t5=_t5_bias_sample(t5_table, past, dec_seq),
                    band=[b[:, :dec_seq, :C_WINDOW + dec_seq] for b in band])
    caches = dict(
        ret=cache_ret_state,
        bk=cache_b_k.reshape(cache_b_k.shape[0], dec_batch, past, H_B * 2 * DH_B),
        bv=cache_b_v.reshape(cache_b_v.shape[0], dec_batch, past, H_B * DV_B),
        ck=cache_c_k.reshape(cache_c_k.shape[0], dec_batch, C_WINDOW, H_C * DH_C),
        cv=cache_c_v.reshape(cache_c_v.shape[0], dec_batch, C_WINDOW, H_C * DH_C),
        dconv=state_d_conv, dssm=state_d_ssm, ffn=state_ffn_conv)

    pos_p = jnp.arange(seq, dtype=jnp.int32)
    pos_s = past + jnp.arange(dec_seq, dtype=jnp.int32)
    y_p, ret_p, bk_p, bv_p, ck_p, cv_p, dconv_p, dssm_p, ffn_p = _trunk(
        x_prompt, pieces(slice(0, batch)), pos_p, weights, None, biases_p)
    y_s, ret_s, bk_s, bv_s, ck_s, cv_s, dconv_s, dssm_s, ffn_s = _trunk(
        x_sample, pieces(slice(batch, batch + dec_batch)), pos_s, weights, caches, biases_s)
    return (y_p, y_s, ret_p, ret_s, bk_p, bk_s, bv_p, bv_s, ck_p, ck_s, cv_p, cv_s,
            dconv_p, dconv_s, dssm_p, dssm_s, ffn_p, ffn_s)
```

```python
import functools
import math

import numpy as np
import jax
import jax.numpy as jnp
from jax import lax
from jax.experimental import pallas as pl
from jax.experimental.pallas import tpu as pltpu

F32 = jnp.float32
BF16 = jnp.bfloat16

CHUNK = 64
EPS = 1e-6
NEG = -1e30
H_A, DK_A, DV_A = 4, 128, 128
ROPE_BASE = 10000.0
H_B, DH_B, DV_B = 4, 64, 128
T5_BUCKETS, T5_MAX_DIST = 32, 128
H_C, DH_C = 8, 64
BAND_CHUNKS = 8
C_WINDOW = BAND_CHUNKS * CHUNK
REL_CLIP = 128
H_D, P_D, G_D, N_D = 8, 64, 2, 128
D_INNER = H_D * P_D
CONV_D = 4
CONV_DIM_D = D_INNER + 2 * G_D * N_D
CONV_F = 3
N_AB_COLS = 7 * 512
N_CD_MAIN = 6 * 512
N_CD_PAD = N_CD_MAIN + 128
FF_CHUNK = 256
BAND_TQ = 256
DIFF_TQ = 256
SCAN_CHUNK = 256
FUSED_TL = 256
VMEM_LIMIT = 56 * 1024 * 1024
LOG2E = math.log2(math.e)
Q_B_SCALE = DH_B ** -0.5 * LOG2E
Q_C_SCALE = DH_C ** -0.5 * LOG2E


def _mm(a, b):
    return jnp.dot(a, b, preferred_element_type=F32)


def _mm_nt(a, b):
    return lax.dot_general(a, b, (((1,), (1,)), ((), ())), preferred_element_type=F32)


def _mm_tn(a, b):
    return lax.dot_general(a, b, (((0,), (0,)), ((), ())), preferred_element_type=F32)


def _mm_exact(a, b):
    return jnp.dot(a, b, preferred_element_type=F32, precision=lax.Precision.HIGHEST)


def _mm_nt_exact(a, b):
    return lax.dot_general(a, b, (((1,), (1,)), ((), ())), preferred_element_type=F32,
                           precision=lax.Precision.HIGHEST)


def _silu(x):
    return x * (1.0 / (1.0 + jnp.exp(-x)))


def _softplus(x):
    return jnp.maximum(x, 0.0) + jnp.log1p(jnp.exp(-jnp.abs(x)))


def _gelu_tanh(x):
    k1 = -2.0 * math.sqrt(2.0 / math.pi) * math.log2(math.e)
    return x * (1.0 / (1.0 + jnp.exp2(x * (k1 + (k1 * 0.044715) * (x * x)))))


def _norm_mod(x, g, sc, sh):
    ms = jnp.mean(x * x, axis=-1, keepdims=True)
    return (x * lax.rsqrt(ms + EPS)) * g * (1.0 + sc) + sh


def _mod_spec(a, tl):
    if a.shape[1] == 1:
        return pl.BlockSpec((1, 1, a.shape[2]), lambda b, l: (b, 0, 0))
    return pl.BlockSpec((1, tl, a.shape[2]), lambda b, l: (b, l, 0))


def _params(sem):
    return pltpu.CompilerParams(dimension_semantics=sem, vmem_limit_bytes=VMEM_LIMIT)


def _resident(shape):
    nd = len(shape)
    return pl.BlockSpec(shape, lambda *_: (0,) * nd, pipeline_mode=pl.Buffered(1))


def _mod_kernel(c_ref, w_ref, b_ref, o_ref):
    c = c_ref[...]
    o_ref[0] = _mm(_silu(c).astype(BF16), w_ref[0].astype(BF16)) + b_ref[0]


def _modulation(c_all, w_mod, b_mod):
    depth, d, n = w_mod.shape
    r = c_all.shape[0]
    tn = 1536
    return pl.pallas_call(
        _mod_kernel,
        grid=(depth, n // tn),
        in_specs=[pl.BlockSpec((r, d), lambda l, j: (0, 0)),
                  pl.BlockSpec((1, d, tn), lambda l, j: (l, 0, j)),
                  pl.BlockSpec((1, 1, tn), lambda l, j: (l, 0, j))],
        out_specs=pl.BlockSpec((1, r, tn), lambda l, j: (l, 0, j)),
        out_shape=jax.ShapeDtypeStruct((depth, r, n), F32),
        compiler_params=_params(("parallel", "parallel")),
        name="modulation",
    )(c_all, w_mod, b_mod.reshape(depth, 1, n))


def _table_gather_kernel(tab_ref, idx_ref, o_ref, *, n_entries, n_heads):
    idx = idx_ref[...]

    def body(r, accs):
        m = idx == r
        return tuple(jnp.where(m, tab_ref[hh, r], a) for hh, a in enumerate(accs))

    accs = lax.fori_loop(0, n_entries, body, tuple(jnp.zeros(idx.shape, F32) for _ in range(n_heads)))
    for hh in range(n_heads):
        o_ref[hh:hh + 1, :] = accs[hh]


def _table_gather(table, idx):
    t, h = table.shape
    w = idx.shape[0]
    return pl.pallas_call(
        functools.partial(_table_gather_kernel, n_entries=t, n_heads=h),
        in_specs=[pl.BlockSpec(memory_space=pltpu.SMEM),
                  pl.BlockSpec((1, w), lambda: (0, 0))],
        out_specs=pl.BlockSpec((h, w), lambda: (0, 0)),
        out_shape=jax.ShapeDtypeStruct((h, w), F32),
        name="table_gather",
    )(table.T, idx.reshape(1, w))


def _toeplitz_kernel(v_ref, o_ref, *, rows, cols, tile):
    x = pltpu.roll(jnp.broadcast_to(v_ref[0], (rows, v_ref.shape[2])), 0, 1, stride=1, stride_axis=0)
    if tile is None:
        o_ref[0] = x[:, :cols]
    else:
        for jb in range(cols // tile):
            o_ref[0, jb] = x[:, jb * tile:(jb + 1) * tile]


def _toeplitz(vec, rows, cols, tile=None):
    h, wv = vec.shape
    if tile is None:
        out_shape, block, imap = (h, rows, cols), (1, rows, cols), lambda i: (i, 0, 0)
    else:
        out_shape, block, imap = (h, cols // tile, rows, tile), (1, cols // tile, rows, tile), lambda i: (i, 0, 0, 0)
    return pl.pallas_call(
        functools.partial(_toeplitz_kernel, rows=rows, cols=cols, tile=tile),
        grid=(h,),
        in_specs=[pl.BlockSpec((1, 1, wv), lambda i: (i, 0, 0))],
        out_specs=pl.BlockSpec(block, imap),
        out_shape=jax.ShapeDtypeStruct(out_shape, F32),
        compiler_params=_params(("parallel",)),
        name="toeplitz",
    )(vec.reshape(h, 1, wv))


def _t5_bucket(rel):
    half = T5_BUCKETS // 2
    max_exact = half // 2
    base = jnp.where(rel > 0, half, 0)
    n = jnp.abs(rel)
    nf = jnp.maximum(n, 1).astype(F32)
    large = max_exact + (jnp.log(nf / max_exact) / math.log(T5_MAX_DIST / max_exact) * (half - max_exact)).astype(jnp.int32)
    large = jnp.minimum(large, half - 1)
    return base + jnp.where(n < max_exact, n, large)


def _wrapped_offsets(n_pos, n_neg):
    p = jnp.arange(n_pos + n_neg, dtype=jnp.int32)
    return jnp.where(p < n_pos, p, p - (n_pos + n_neg))


def _t5_bias_prompt(t5_table, seq):
    nd = seq // DIFF_TQ
    u = _wrapped_offsets(seq, DIFF_TQ)
    vec = _table_gather(t5_table, _t5_bucket(u - (seq - DIFF_TQ))) * LOG2E
    return _toeplitz(vec, DIFF_TQ, seq, tile=DIFF_TQ)


def _t5_bias_sample(t5_table, past, lq):
    u = _wrapped_offsets(past + lq, lq)
    vec = _table_gather(t5_table, _t5_bucket(u - past)) * LOG2E
    return _toeplitz(vec, lq, past + 2 * lq)[:, :, :past + lq]


def _band_bias(rel_table):
    u = _wrapped_offsets(3 * BAND_TQ, BAND_TQ)
    idx = jnp.clip(2 * BAND_TQ - u, -REL_CLIP, REL_CLIP) + REL_CLIP
    bias = _toeplitz(_table_gather(rel_table, idx) * LOG2E, BAND_TQ, 3 * BAND_TQ)
    dc = (np.arange(BAND_TQ)[:, None] // CHUNK + C_WINDOW // CHUNK) - np.arange(3 * BAND_TQ)[None, :] // CHUNK
    return jnp.where(jnp.asarray((dc >= 0) & (dc <= BAND_CHUNKS))[None], bias, NEG)


def _inproj_ab_kernel(x_ref, g_ref, sc_ref, sh_ref, w_ref, cos_ref, sin_ref, qkv_ref, kb_ref, vb_ref, *, tl):
    h = _norm_mod(x_ref[0], g_ref[...], sc_ref[0], sh_ref[0]).astype(BF16)
    cos = cos_ref[...]
    sin = sin_ref[...]
    for j in range(7):
        y = _mm(h, w_ref[:, j * 512:(j + 1) * 512])
        if j < 2:
            for hh in range(H_A):
                yh = y[:, hh * 128:(hh + 1) * 128]
                yh = yh * cos + pltpu.roll(yh, 64, 1) * sin
                if j == 1:
                    yh = yh * (DK_A ** -0.5)
                qkv_ref[0, :, j * 512 + hh * 128:j * 512 + (hh + 1) * 128] = yh.astype(BF16)
        else:
            if j == 4:
                y = y * Q_B_SCALE
            if j >= 5:
                dst = kb_ref if j == 5 else vb_ref
                for hh in range(H_B):
                    dst[0, pl.ds(hh, tl, stride=H_B), :] = y[:, hh * 128:(hh + 1) * 128]
            qkv_ref[0, :, j * 512:(j + 1) * 512] = y.astype(BF16)


def _inproj_ab(x, g, sc, sh, w, cos, sin, tl):
    nseq, seq, d = x.shape
    grid = (nseq, seq // tl)
    tok = lambda n: pl.BlockSpec((1, tl, n), lambda b, l: (b, l, 0))
    cache = pl.BlockSpec((1, tl * H_B, 128), lambda b, l: (b, l, 0))
    return pl.pallas_call(
        functools.partial(_inproj_ab_kernel, tl=tl),
        grid=grid,
        in_specs=[tok(d), _resident((1, d)), _mod_spec(sc, tl), _mod_spec(sh, tl), _resident((d, N_AB_COLS)),
                  pl.BlockSpec((tl, 128), lambda b, l: (l, 0)),
                  pl.BlockSpec((tl, 128), lambda b, l: (l, 0))],
        out_specs=[tok(N_AB_COLS), cache, cache],
        out_shape=[jax.ShapeDtypeStruct((nseq, seq, N_AB_COLS), BF16),
                   jax.ShapeDtypeStruct((nseq, seq * H_B, 128), F32),
                   jax.ShapeDtypeStruct((nseq, seq * H_B, 128), F32)],
        compiler_params=_params(("parallel", "parallel")),
        name="inproj_ab",
    )(x, g, sc, sh, w, cos, sin)


def _retention_kernel(q_ref, k_ref, v_ref, g_ref, s0_ref, dec_ref, qd_ref, kd_ref, gc_ref, gn_ref,
                      o_ref, sn_ref, *, c, n_chunks):
    states = [s0_ref[0, h] for h in range(H_A)]
    for ci in range(n_chunks):
        rows = slice(ci * c, (ci + 1) * c)
        for h in range(H_A):
            cs = slice(h * 128, (h + 1) * 128)
            q = q_ref[0, rows, cs]
            k = k_ref[0, rows, cs]
            v = v_ref[0, rows, cs]
            state = states[h]
            s = _mm_nt(q, k) * dec_ref[h]
            o = _mm(s.astype(BF16), v) + _mm((q.astype(F32) * qd_ref[h]).astype(BF16), state.astype(BF16))
            states[h] = gc_ref[h] * state + _mm_tn((k.astype(F32) * kd_ref[h]).astype(BF16), v)
            mu = jnp.mean(o, axis=-1, keepdims=True)
            dlt = o - mu
            var = jnp.mean(dlt * dlt, axis=-1, keepdims=True)
            y = dlt * lax.rsqrt(var + EPS) * gn_ref[:, cs] * _silu(g_ref[0, rows, cs].astype(F32))
            o_ref[0, rows, cs] = y.astype(BF16)
    for h in range(H_A):
        sn_ref[0, h] = states[h]


def _retention_consts(c):
    lg = np.log1p(-np.exp2(-5.0 - np.arange(H_A, dtype=np.float32))).astype(np.float32)
    idx = np.arange(c, dtype=np.float32)
    diff = idx[:, None] - idx[None, :]
    decay = np.where(diff[None] >= 0, np.exp(np.maximum(diff, 0.0)[None] * lg[:, None, None]), 0.0)
    qd = np.exp((idx + 1.0)[None, :] * lg[:, None])
    kd = np.exp((c - 1.0 - idx)[None, :] * lg[:, None])
    gc = np.exp(c * lg)
    bc = lambda a: np.ascontiguousarray(np.broadcast_to(a[..., None], a.shape + (128,))).astype(np.float32)
    return decay.astype(np.float32), bc(qd), bc(kd), bc(gc[:, None])


def _retention(qkv, s0, ret_gn, c):
    nseq, seq, _ = qkv.shape
    decay, qd, kd, gc = _retention_consts(c)
    w = H_A * 128
    col = lambda idx: pl.BlockSpec((1, seq, w), lambda b: (b, 0, idx))
    const = lambda shape: pl.BlockSpec(shape, lambda b: (0,) * len(shape))
    state = pl.BlockSpec((1, H_A, DK_A, DV_A), lambda b: (b, 0, 0, 0))
    return pl.pallas_call(
        functools.partial(_retention_kernel, c=c, n_chunks=seq // c),
        grid=(nseq,),
        in_specs=[col(0), col(1), col(2), col(3), state, const((H_A, c, c)), const((H_A, c, 128)),
                  const((H_A, c, 128)), const((H_A, 1, 128)), const((1, w))],
        out_specs=[col(0), state],
        out_shape=[jax.ShapeDtypeStruct((nseq, seq, w), BF16),
                   jax.ShapeDtypeStruct((nseq, H_A, DK_A, DV_A), F32)],
        compiler_params=_params(("parallel",)),
        name="retention",
    )(qkv, qkv, qkv, qkv, s0, decay, qd, kd, gc, ret_gn.reshape(1, w))


def _split_halves(q):
    lane = lax.broadcasted_iota(jnp.int32, q.shape, 1)
    zero = jnp.zeros_like(q)
    return jnp.where(lane < 64, q, zero), jnp.where(lane >= 64, q, zero)


def _lambda(lq_ref, lk_ref, lam_init):
    e = jnp.exp(jnp.sum(lq_ref[...] * lk_ref[...], axis=1, keepdims=True))
    return e[0:1, :] - e[1:2, :] + lam_init


def _diff_epilogue(o, gn, lam_init):
    ms = jnp.mean(o * o, axis=-1, keepdims=True)
    return (o * lax.rsqrt(ms + EPS)) * gn * (1.0 - lam_init)


def _diff_attn_kernel(q_ref, k_ref, v_ref, b_ref, lq_ref, lk_ref, gn_ref, o_ref, s_ref, qs_ref, mx_ref, ls_ref,
                      acc_ref, *, tq, nd, lam_init):
    row = lax.broadcasted_iota(jnp.int32, (tq, tq), 0) // CHUNK
    col = lax.broadcasted_iota(jnp.int32, (tq, tq), 1) // CHUNK
    chunk_mask = col <= row
    lam = _lambda(lq_ref, lk_ref, lam_init)

    def prepare(t):
        halves = _split_halves(q_ref[0, t * tq:(t + 1) * tq, :])
        for i in range(2):
            r = 2 * (t % 2) + i
            qs_ref[r] = halves[i]
            mx_ref[r] = jnp.full((tq, 128), NEG, F32)
            ls_ref[r] = jnp.zeros((tq, 128), F32)
            acc_ref[r] = jnp.zeros((tq, DV_B), F32)

    def far_bias(t, kb):
        if (t - kb - 1) * tq + 1 < T5_MAX_DIST:
            return None
        return b_ref[0, nd - 1 - t + kb, 0:1, 0:1]

    def scores(t, kb):
        k = k_ref[0, kb * tq:(kb + 1) * tq, :]
        far = far_bias(t, kb)
        for i in range(2):
            r = 2 * (t % 2) + i
            s = _mm_nt(qs_ref[r], k)
            if far is None:
                s = s + b_ref[0, nd - 1 - t + kb]
            if kb == t:
                s = jnp.where(chunk_mask, s, NEG)
            s_ref[r * nd + kb] = s
            m = s[:, 0:128]
            for c0 in range(128, tq, 128):
                m = jnp.maximum(m, s[:, c0:c0 + 128])
            mx_ref[r] = jnp.maximum(mx_ref[r], m if far is None else m + far)

    def row_max(t):
        for i in range(2):
            r = 2 * (t % 2) + i
            mx_ref[r] = jnp.broadcast_to(jnp.max(mx_ref[r], axis=1, keepdims=True), (tq, 128))

    def weigh(t, kb):
        v = v_ref[0, kb * tq:(kb + 1) * tq, :]
        far = far_bias(t, kb)
        for i in range(2):
            r = 2 * (t % 2) + i
            m = mx_ref[r] if far is None else mx_ref[r] - far
            l = ls_ref[r]
            ps = []
            for c0 in range(0, tq, 128):
                p = jnp.exp2(s_ref[r * nd + kb, :, c0:c0 + 128] - m)
                l = l + p
                ps.append(p.astype(BF16))
            ls_ref[r] = l
            acc_ref[r] += _mm(jnp.concatenate(ps, axis=1), v)

    def finish(t):
        r = 2 * (t % 2)
        l0 = jnp.sum(ls_ref[r], axis=1, keepdims=True)
        l1 = jnp.sum(ls_ref[r + 1], axis=1, keepdims=True)
        o = acc_ref[r] / l0 - lam * (acc_ref[r + 1] / l1)
        o_ref[0, t * tq:(t + 1) * tq, :] = _diff_epilogue(o, gn_ref[...], lam_init).astype(BF16)

    last = nd - 1
    prepare(last)
    for kb in range(last + 1):
        scores(last, kb)
    row_max(last)
    for t in range(last - 1, -1, -1):
        prepare(t)
        for kb in range(t + 2):
            if kb <= t:
                scores(t, kb)
            weigh(t + 1, kb)
        finish(t + 1)
        row_max(t)
    weigh(0, 0)
    finish(0)


def _diff_attn(qkv, bias, lam_q, lam_k, diff_gn, lam_init):
    nseq, seq, _ = qkv.shape
    tq = DIFF_TQ
    nd = seq // tq
    full = lambda off: pl.BlockSpec((1, seq, 128), lambda b, h: (b, 0, off + h))
    small = lambda r, c: pl.BlockSpec((r, c), lambda b, h: (0, 0))
    return pl.pallas_call(
        functools.partial(_diff_attn_kernel, tq=tq, nd=nd, lam_init=lam_init),
        grid=(nseq, H_B),
        in_specs=[full(16), full(20), full(24),
                  pl.BlockSpec((1, nd, tq, tq), lambda b, h: (h, 0, 0, 0)),
                  small(2, DH_B), small(2, DH_B), small(1, DV_B)],
        out_specs=full(0),
        out_shape=jax.ShapeDtypeStruct((nseq, seq, H_B * DV_B), BF16),
        scratch_shapes=[pltpu.VMEM((4 * nd, tq, tq), F32), pltpu.VMEM((4, tq, 128), BF16),
                        pltpu.VMEM((4, tq, 128), F32), pltpu.VMEM((4, tq, 128), F32),
                        pltpu.VMEM((4, tq, DV_B), F32)],
        compiler_params=_params(("parallel", "parallel")),
        name="diff_attn",
    )(qkv, qkv, qkv, bias, lam_q, lam_k, diff_gn.reshape(1, DV_B))


def _diff_attn_sample_kernel(q_ref, kc_ref, vc_ref, kp_ref, vp_ref, b_ref, lq_ref, lk_ref, gn_ref, o_ref,
                             *, past, lam_init):
    lam = _lambda(lq_ref, lk_ref, lam_init)
    for h in range(H_B):
        cs = slice(h * 128, (h + 1) * 128)
        qs = _split_halves(q_ref[0, :, cs])
        kp = kp_ref[0, :, cs].astype(BF16)
        vp = vp_ref[0, :, cs].astype(BF16)
        kc = kc_ref[0, :, cs]
        vc = vc_ref[0, :, cs]
        b = b_ref[h]
        probs = []
        for i in range(2):
            sp = _mm_nt(qs[i], kp) + b[:, :past]
            sc = _mm_nt(qs[i], kc) + b[:, past:]
            m = jnp.maximum(jnp.max(sp, axis=1, keepdims=True), jnp.max(sc, axis=1, keepdims=True))
            pp = jnp.exp2(sp - m)
            pc = jnp.exp2(sc - m)
            l = jnp.sum(pp, axis=1, keepdims=True) + jnp.sum(pc, axis=1, keepdims=True)
            probs.append((pp / l, pc / l))
        ap = probs[0][0] - lam * probs[1][0]
        ac = probs[0][1] - lam * probs[1][1]
        o = _mm(ap.astype(BF16), vp) + _mm(ac.astype(BF16), vc)
        o_ref[0, :, cs] = _diff_epilogue(o, gn_ref[...], lam_init).astype(BF16)


def _diff_attn_sample(qkv, k_past, v_past, bias, lam_q, lam_k, diff_gn, lam_init):
    nseq, lq, _ = qkv.shape
    past = k_past.shape[1]
    w = H_B * DV_B
    cur = lambda idx: pl.BlockSpec((1, lq, w), lambda b: (b, 0, idx))
    old = pl.BlockSpec((1, past, w), lambda b: (b, 0, 0))
    small = lambda r, c: pl.BlockSpec((r, c), lambda b: (0, 0))
    return pl.pallas_call(
        functools.partial(_diff_attn_sample_kernel, past=past, lam_init=lam_init),
        grid=(nseq,),
        in_specs=[cur(4), cur(5), cur(6), old, old, _resident(bias.shape),
                  small(2, DH_B), small(2, DH_B), small(1, DV_B)],
        out_specs=cur(0),
        out_shape=jax.ShapeDtypeStruct((nseq, lq, w), BF16),
        compiler_params=_params(("parallel",)),
        name="diff_attn_sample",
    )(qkv, qkv, qkv, k_past, v_past, bias, lam_q, lam_k, diff_gn.reshape(1, DV_B))


def _inproj_cd_kernel(x_ref, g_ref, sc_ref, sh_ref, w_ref, main_ref, dt_ref, kc_ref, vc_ref, tail_ref, *, tl, seg):
    h = _norm_mod(x_ref[0], g_ref[...], sc_ref[0], sh_ref[0]).astype(BF16)
    for j in range(6):
        y = _mm(h, w_ref[:, j * 512:(j + 1) * 512])
        main_ref[0, :, j * 512:(j + 1) * 512] = (y * Q_C_SCALE if j == 0 else y).astype(BF16)
        if j == 1:
            kc_ref[0] = y
        if j == 2:
            vc_ref[0] = y
        if j >= 4:
            for s in range(tl // seg):
                tail_ref[s, :, (j - 4) * 512:(j - 3) * 512] = y[(s + 1) * seg - 8:(s + 1) * seg, :]
    dt_ref[0] = _mm(h, w_ref[:, N_CD_MAIN:N_CD_PAD])


def _inproj_cd(x, g, sc, sh, w, tl, keep, seg):
    nseq, seq, d = x.shape
    nseg = tl // seg
    assert keep % tl == 0
    skip = (seq - keep) // tl
    tok = lambda n: pl.BlockSpec((1, tl, n), lambda b, l: (b, l, 0))
    kept = pl.BlockSpec((1, tl, 512), lambda b, l: (b, jnp.maximum(l - skip, 0), 0))
    return pl.pallas_call(
        functools.partial(_inproj_cd_kernel, tl=tl, seg=seg),
        grid=(nseq, seq // tl),
        in_specs=[tok(d), _resident((1, d)), _mod_spec(sc, tl), _mod_spec(sh, tl), _resident((d, N_CD_PAD))],
        out_specs=[tok(N_CD_MAIN), tok(128), kept, kept,
                   pl.BlockSpec((nseg, 8, CONV_DIM_D), lambda b, l: (b, 0, 0))],
        out_shape=[jax.ShapeDtypeStruct((nseq, seq, N_CD_MAIN), BF16),
                   jax.ShapeDtypeStruct((nseq, seq, 128), F32),
                   jax.ShapeDtypeStruct((nseq, keep, 512), F32),
                   jax.ShapeDtypeStruct((nseq, keep, 512), F32),
                   jax.ShapeDtypeStruct((nseq * nseg, 8, CONV_DIM_D), F32)],
        compiler_params=_params(("parallel", "arbitrary")),
        name="inproj_cd",
    )(x, g, sc, sh, w)


def _band_sample_kernel(q_ref, kc_ref, vc_ref, kp_ref, vp_ref, b_ref, o_ref, *, past):
    for pr in range(H_C // 2):
        cs = slice(pr * 128, (pr + 1) * 128)
        halves = _split_halves(q_ref[0, :, cs])
        kp = kp_ref[0, :, cs].astype(BF16)
        vp = vp_ref[0, :, cs].astype(BF16)
        kc = kc_ref[0, :, cs]
        vc = vc_ref[0, :, cs]
        outs = []
        for hi, qh in enumerate(halves):
            b = b_ref[2 * pr + hi]
            sp = _mm_nt(qh, kp) + b[:, :past]
            sc = _mm_nt(qh, kc) + b[:, past:]
            m = jnp.maximum(jnp.max(sp, axis=1, keepdims=True), jnp.max(sc, axis=1, keepdims=True))
            pp = jnp.exp2(sp - m)
            pc = jnp.exp2(sc - m)
            l = jnp.sum(pp, axis=1, keepdims=True) + jnp.sum(pc, axis=1, keepdims=True)
            outs.append(_mm((pp / l).astype(BF16), vp) + _mm((pc / l).astype(BF16), vc))
        lane = lax.broadcasted_iota(jnp.int32, outs[0].shape, 1)
        o_ref[0, :, cs] = jnp.where(lane < 64, outs[0], outs[1]).astype(BF16)


def _band_attn_sample(main, k_past, v_past, bias):
    nseq, lq, _ = main.shape
    past = k_past.shape[1]
    w = H_C * DH_C
    cur = lambda idx: pl.BlockSpec((1, lq, w), lambda b: (b, 0, idx))
    old = pl.BlockSpec((1, past, w), lambda b: (b, 0, 0))
    return pl.pallas_call(
        functools.partial(_band_sample_kernel, past=past),
        grid=(nseq,),
        in_specs=[cur(0), cur(1), cur(2), old, old, _resident(bias.shape)],
        out_specs=cur(0),
        out_shape=jax.ShapeDtypeStruct((nseq, lq, w), BF16),
        compiler_params=_params(("parallel",)),
        name="band_attn_sample",
    )(main, main, main, k_past, v_past, bias)


def _ssd_kernel(z_ref, xs_ref, bm_ref, cm_ref, dt_ref, cw_ref, cb_ref, dtb_ref, alog_ref, dskip_ref, ng_ref,
                past_ref, s0_ref, o_ref, sn_ref, buf_ref, st_ref, *, c):
    @pl.when(pl.program_id(1) == 0)
    def _():
        buf_ref[0:8, :] = past_ref[0]
        st_ref[...] = s0_ref[0]

    for _ in _ssd_steps(z_ref.at[0], xs_ref.at[0], bm_ref.at[0], cm_ref.at[0], dt_ref.at[0], cw_ref, cb_ref, dtb_ref,
                        alog_ref, dskip_ref, ng_ref, o_ref.at[0], buf_ref, st_ref, c=c):
        pass
    sn_ref[0] = st_ref[...]


def _ssd(main, dt, conv_w, conv_b, dt_bias, a_log, d_skip, norm_g, conv_past, s0, c):
    nseq, seq, _ = main.shape
    pad = lambda a: jnp.pad(a.reshape(1, H_D), ((0, 0), (0, 128 - H_D)))
    blk = lambda w, idx: pl.BlockSpec((1, c, w), lambda b, l: (b, l, idx))
    const = lambda r, w: pl.BlockSpec((r, w), lambda b, l: (0, 0))
    state = pl.BlockSpec((1, H_D * P_D, N_D), lambda b, l: (b, 0, 0))
    return pl.pallas_call(
        functools.partial(_ssd_kernel, c=c),
        grid=(nseq, seq // c),
        in_specs=[blk(512, 3), blk(512, 4), blk(256, 10), blk(256, 11), blk(128, 0),
                  const(CONV_D, CONV_DIM_D), const(1, CONV_DIM_D), const(1, 128), const(1, 128),
                  pl.BlockSpec(memory_space=pltpu.SMEM), const(1, D_INNER),
                  pl.BlockSpec((1, 8, CONV_DIM_D), lambda b, l: (b, 0, 0)), state],
        out_specs=[blk(512, 0), state],
        out_shape=[jax.ShapeDtypeStruct((nseq, seq, D_INNER), BF16),
                   jax.ShapeDtypeStruct((nseq, H_D * P_D, N_D), F32)],
        scratch_shapes=[pltpu.VMEM((c + 8, CONV_DIM_D), F32), pltpu.VMEM((H_D * P_D, N_D), F32)],
        compiler_params=_params(("parallel", "arbitrary")),
        name="ssd",
    )(main, main, main, main, dt, conv_w, conv_b.reshape(1, CONV_DIM_D), pad(dt_bias), pad(a_log), d_skip,
      norm_g.reshape(1, D_INNER), conv_past, s0)


def _ffn_kernel(x_ref, o1_ref, o2_ref, wo_ref, g1_ref, ng_ref, sc_ref, sh_ref, g2_ref, wup_ref, cw_ref, cb_ref,
                wdn_ref, past_ref, fg_ref, out_ref, tail_ref, act_ref, wb_ref, gt_ref, *, tl, d_ff, final, seg):
    @pl.when(pl.program_id(1) == 0)
    def _():
        gt_ref[...] = past_ref[...]

    for _ in _ffn_steps(x_ref, o1_ref.at[0], o2_ref.at[0], wo_ref, g1_ref, ng_ref, sc_ref, sh_ref, g2_ref, wup_ref,
                        cw_ref, cb_ref, wdn_ref, fg_ref, out_ref, tail_ref, act_ref, wb_ref, gt_ref,
                        tl=tl, d_ff=d_ff, final=final, seg=seg):
        pass


def _outproj_ffn(x, o1, o2, wo, g1, ng, sc, sh, g2, wup, cw, cb, wdn, past, fg, tl, seg, final):
    nseq, seq, d = x.shape
    d_ff = wdn.shape[0]
    nseg = tl // seg
    assert nseg == 1 or seq == tl
    tok = lambda n: pl.BlockSpec((1, tl, n), lambda b, l: (b, l, 0))
    tail = pl.BlockSpec((nseg, 8, d_ff), lambda b, l: (b, 0, 0))
    return pl.pallas_call(
        functools.partial(_ffn_kernel, tl=tl, d_ff=d_ff, final=final, seg=seg),
        grid=(nseq, seq // tl),
        in_specs=[tok(d), tok(o1.shape[-1]), tok(o2.shape[-1]), _resident(wo.shape), _mod_spec(g1, tl),
                  _resident((1, d)), _mod_spec(sc, tl), _mod_spec(sh, tl), _mod_spec(g2, tl), _resident(wup.shape),
                  _resident(cw.shape), _resident((1, d_ff)), _resident(wdn.shape), tail, _resident((1, d))],
        out_specs=[tok(d), tail],
        out_shape=[jax.ShapeDtypeStruct((nseq, seq, d), F32), jax.ShapeDtypeStruct((nseq * nseg, 8, d_ff), F32)],
        scratch_shapes=[pltpu.VMEM((tl, d_ff), BF16), pltpu.VMEM((nseg * (seg + 8), FF_CHUNK), F32),
                        pltpu.VMEM((nseg, 8, d_ff), F32)],
        compiler_params=_params(("parallel", "arbitrary")),
        name="outproj_ffn",
    )(x, o1, o2, wo, g1, ng, sc, sh, g2, wup, cw, cb.reshape(1, d_ff), wdn, past, fg)


def _interleave(gens, strides):
    live = list(zip(gens, strides))
    while live:
        for item in list(live):
            for _ in range(item[1]):
                try:
                    next(item[0])
                except StopIteration:
                    live.remove(item)
                    break


def _chain(*gens):
    for g in gens:
        yield from g


def _ffn_steps(x_ref, o1, o2, wo_ref, g1_ref, ng_ref, sc_ref, sh_ref, g2_ref, wup_ref, cw_ref, cb_ref, wdn_ref,
               fg_ref, out_ref, tail_ref, act_ref, wb_ref, gt_ref, *, tl, d_ff, final, seg):
    nseg = tl // seg
    stride = seg + 8
    half = wo_ref.shape[0] // 2
    mix = _mm(o1[...], wo_ref[0:half, :]) + _mm(o2[...], wo_ref[half:2 * half, :])
    x1 = x_ref[0] + g1_ref[0] * mix
    out_ref[0] = x1
    h = _norm_mod(x1, ng_ref[...], sc_ref[0], sh_ref[0]).astype(BF16)
    yield
    for j in range(d_ff // FF_CHUNK):
        c0 = j * FF_CHUNK
        a = _mm(h, wup_ref[:, c0:c0 + FF_CHUNK])
        g = _mm(h, wup_ref[:, d_ff + c0:d_ff + c0 + FF_CHUNK])
        for s in range(nseg):
            wb_ref[s * stride:s * stride + 8, :] = gt_ref[s, :, c0:c0 + FF_CHUNK]
            wb_ref[s * stride + 8:(s + 1) * stride, :] = g[s * seg:(s + 1) * seg, :]
            gt_ref[s, :, c0:c0 + FF_CHUNK] = g[(s + 1) * seg - 8:(s + 1) * seg, :]
        back2 = [wb_ref[pl.ds(s * stride + 6, seg), :] for s in range(nseg)]
        back1 = [wb_ref[pl.ds(s * stride + 7, seg), :] for s in range(nseg)]
        if nseg > 1:
            back2, back1 = [jnp.concatenate(back2, axis=0)], [jnp.concatenate(back1, axis=0)]
        gc = (cb_ref[:, c0:c0 + FF_CHUNK] + back2[0] * cw_ref[0:1, c0:c0 + FF_CHUNK]
              + back1[0] * cw_ref[1:2, c0:c0 + FF_CHUNK] + g * cw_ref[2:3, c0:c0 + FF_CHUNK])
        act_ref[:, c0:c0 + FF_CHUNK] = (a * _gelu_tanh(gc)).astype(BF16)
        yield
    split = (d_ff // FF_CHUNK + 1) // 2 * FF_CHUNK
    dn = _mm(act_ref[:, 0:split], wdn_ref[0:split, :])
    yield
    dn = dn + _mm(act_ref[:, split:d_ff], wdn_ref[split:d_ff, :])
    x2 = out_ref[0] + g2_ref[0] * dn
    if final:
        ms = jnp.mean(x2 * x2, axis=-1, keepdims=True)
        x2 = (x2 * lax.rsqrt(ms + EPS)) * fg_ref[...]
    out_ref[0] = x2
    tail_ref[...] = gt_ref[...]
    yield


def _band_steps(q, k_ref, v_ref, b_ref, o, t, *, tq):
    lane = lax.broadcasted_iota(jnp.int32, (tq, 128), 1)
    starts, offs = [], []
    for d in range(3):
        kt = t - 2 + d
        starts.append(pl.multiple_of(jnp.maximum(kt, 0) * tq, tq))
        offs.append(jnp.where(kt >= 0, 0.0, NEG))

    def scores(head):
        pr, hi = divmod(head, 2)
        cs = slice(pr * 128, (pr + 1) * 128)
        qh = _split_halves(q[:, cs])[hi]
        ss = []
        for d in range(3):
            s = _mm_nt(qh, k_ref[0, pl.ds(starts[d], tq), cs]) + b_ref[head, :, d * tq:(d + 1) * tq]
            ss.append(s + offs[d] if d < 2 else s)
        mm = ss[2][:, 0:128]
        for d in range(3):
            for c0 in range(0, tq, 128):
                mm = jnp.maximum(mm, ss[d][:, c0:c0 + 128])
        return ss, jnp.max(mm, axis=1, keepdims=True)

    def weigh(head, ss, m):
        cs = slice((head // 2) * 128, (head // 2 + 1) * 128)
        ls = jnp.zeros((tq, 128), F32)
        acc = jnp.zeros((tq, 128), F32)
        for d in range(3):
            p = jnp.exp2(ss[d] - m)
            for c0 in range(0, tq, 128):
                ls = ls + p[:, c0:c0 + 128]
            acc = acc + _mm(p.astype(BF16), v_ref[0, pl.ds(starts[d], tq), cs])
        return acc / jnp.sum(ls, axis=1, keepdims=True)

    outs = []
    pending = scores(0)
    yield
    for head in range(H_C):
        nxt = scores(head + 1) if head + 1 < H_C else None
        outs.append(weigh(head, *pending))
        pending = nxt
        if head % 2 == 1:
            cs = slice((head // 2) * 128, (head // 2 + 1) * 128)
            o[:, cs] = jnp.where(lane < 64, outs[head - 1], outs[head]).astype(BF16)
        yield


def _ssd_steps(z, xs_in, bm_in, cm_in, dt_in, cw_ref, cb_ref, dtb_ref, alog_ref, dskip_ref, ng_ref, o,
               buf_ref, st_ref, *, c):
    buf_ref[8:8 + c, 0:D_INNER] = xs_in[...].astype(F32)
    buf_ref[8:8 + c, D_INNER:D_INNER + 256] = bm_in[...].astype(F32)
    buf_ref[8:8 + c, D_INNER + 256:CONV_DIM_D] = cm_in[...].astype(F32)
    conv = cb_ref[...] + buf_ref[pl.ds(5, c), :] * cw_ref[0:1, :]
    for i in range(1, CONV_D):
        conv = conv + buf_ref[pl.ds(5 + i, c), :] * cw_ref[i:i + 1, :]
    buf_ref[0:8, :] = buf_ref[c:c + 8, :]
    act = _silu(conv)
    xs = act[:, 0:D_INNER]
    bm = act[:, D_INNER:D_INNER + 256].astype(BF16)
    cm = act[:, D_INNER + 256:CONV_DIM_D].astype(BF16)
    yield

    dt = _softplus(dt_in[...] + dtb_ref[...])
    da = dt * (-jnp.exp(alog_ref[...]))
    row = lax.broadcasted_iota(jnp.int32, (c, c), 0)
    col = lax.broadcasted_iota(jnp.int32, (c, c), 1)
    tri = row >= col
    cum = _mm_exact(tri.astype(F32), da)
    eye = (lax.broadcasted_iota(jnp.int32, (8, 128), 0) == lax.broadcasted_iota(jnp.int32, (8, 128), 1)).astype(F32)
    cum_t = _mm_exact(_mm_nt_exact(eye, da), (row <= col).astype(F32))
    dt_t = _mm_nt_exact(eye, dt)
    ecum = jnp.exp(cum)
    last = cum[c - 1:c, :]
    wgt = jnp.exp(last - cum) * dt
    elast = jnp.exp(last)
    lane = lax.broadcasted_iota(jnp.int32, (c, 128), 1)
    low = lane < 64
    rlow = lax.broadcasted_iota(jnp.int32, (128, 128), 0) < 64
    yield

    for g in range(G_D):
        bm_g = bm[:, g * 128:(g + 1) * 128]
        cm_g = cm[:, g * 128:(g + 1) * 128]
        cb = _mm_nt(cm_g, bm_g)
        ys = []
        for pp in range(2):
            p = 2 * g + pp
            h0, h1 = 2 * p, 2 * p + 1
            x_f = xs[:, p * 128:(p + 1) * 128]
            x_b = x_f.astype(BF16)
            y_in = []
            for hh in (h0, h1):
                seg = cum[:, hh:hh + 1] - cum_t[hh:hh + 1, :]
                dec = jnp.exp(jnp.where(tri, seg, NEG))
                y_in.append(_mm((cb * dec * dt_t[hh:hh + 1, :]).astype(BF16), x_b))
            st = st_ref[p * 128:(p + 1) * 128, :]
            y_x = _mm_nt(cm_g, st.astype(BF16)) * jnp.where(low, ecum[:, h0:h0 + 1], ecum[:, h1:h1 + 1])
            w2 = jnp.where(low, wgt[:, h0:h0 + 1], wgt[:, h1:h1 + 1])
            st_ref[p * 128:(p + 1) * 128, :] = (jnp.where(rlow, elast[:, h0:h0 + 1], elast[:, h1:h1 + 1]) * st
                                                + _mm_tn((x_f * w2).astype(BF16), bm_g))
            y = jnp.where(low, y_in[0], y_in[1]) + y_x + jnp.where(low, dskip_ref[h0], dskip_ref[h1]) * x_f
            ys.append(y * _silu(z[:, p * 128:(p + 1) * 128].astype(F32)))
            yield
        ms = (jnp.sum(ys[0] * ys[0], axis=-1, keepdims=True) + jnp.sum(ys[1] * ys[1], axis=-1, keepdims=True)) / 256.0
        inv = lax.rsqrt(ms + EPS)
        for pp in range(2):
            p = 2 * g + pp
            o[:, p * 128:(p + 1) * 128] = (ys[pp] * inv * ng_ref[:, p * 128:(p + 1) * 128]).astype(BF16)
    yield


def _cd_ffn_kernel(x_ref, wo_ref, g1_ref, ng_ref, sc_ref, sh_ref, g2_ref, wup_ref, cwf_ref, cbf_ref, wdn_ref, fg_ref,
                   q_ref, k_ref, v_ref, bb_ref,
                   z_ref, xs_ref, bm_ref, cm_ref, dt_ref, cws_ref, cbs_ref, dtb_ref, alog_ref, dskip_ref, ngs_ref,
                   out_ref, tail_ref, sn_ref,
                   obuf_ref, act_ref, wb_ref, gt_ref, buf_ref, st_ref,
                   *, tl, d_ff, final, n_tiles, tiles_per_seq, c, tq):
    g = pl.program_id(0)
    gm = jnp.minimum(g, n_tiles - 1)
    gf = jnp.maximum(g - 1, 0)
    jm = gm % tiles_per_seq
    jf = gf % tiles_per_seq
    slot = g % 2
    prev = (g + 1) % 2

    @pl.when(g == 0)
    def _():
        obuf_ref[...] = jnp.zeros(obuf_ref.shape, BF16)

    @pl.when(jm == 0)
    def _():
        buf_ref[0:8, :] = jnp.zeros((8, CONV_DIM_D), F32)
        st_ref[...] = jnp.zeros(st_ref.shape, F32)

    @pl.when(jf == 0)
    def _():
        gt_ref[...] = jnp.zeros(gt_ref.shape, F32)

    ffn = _ffn_steps(x_ref, obuf_ref.at[prev, :, 0:512], obuf_ref.at[prev, :, 512:1024], wo_ref, g1_ref, ng_ref,
                     sc_ref, sh_ref, g2_ref, wup_ref, cwf_ref, cbf_ref, wdn_ref, fg_ref, out_ref, tail_ref, act_ref,
                     wb_ref, gt_ref, tl=tl, d_ff=d_ff, final=final, seg=tl)
    band = _chain(*[_band_steps(q_ref.at[0, ti * tq:(ti + 1) * tq, :], k_ref, v_ref, bb_ref,
                                obuf_ref.at[slot, ti * tq:(ti + 1) * tq, 0:512],
                                jm * (tl // tq) + ti, tq=tq) for ti in range(tl // tq)])
    ssd = _chain(*[_ssd_steps(z_ref.at[0, ci * c:(ci + 1) * c, :], xs_ref.at[0, ci * c:(ci + 1) * c, :],
                              bm_ref.at[0, ci * c:(ci + 1) * c, :], cm_ref.at[0, ci * c:(ci + 1) * c, :],
                              dt_ref.at[0, ci * c:(ci + 1) * c, :], cws_ref, cbs_ref, dtb_ref, alog_ref, dskip_ref,
                              ngs_ref, obuf_ref.at[slot, ci * c:(ci + 1) * c, 512:1024], buf_ref, st_ref, c=c)
                   for ci in range(tl // c)])
    _interleave([ffn, _chain(band, ssd)], (1, 1))

    @pl.when(g < n_tiles)
    def _():
        sn_ref[0] = st_ref[...]


def _cd_mixers_ffn(x, main, dt, bias, wo, g1, ng, sc, sh, g2, wup, cwf, cbf, wdn, fg,
                   conv_w, conv_b, dt_bias, a_log, d_skip, norm_g, tl, final):
    nseq, seq, d = x.shape
    d_ff = wdn.shape[0]
    tps = seq // tl
    n_tiles = nseq * tps
    c = SCAN_CHUNK
    tq = BAND_TQ
    pad = lambda a: jnp.pad(a.reshape(1, H_D), ((0, 0), (0, 128 - H_D)))
    mix = lambda g: jnp.minimum(g, n_tiles - 1)
    ffn = lambda g: jnp.maximum(g - 1, 0)
    tok_f = lambda n: pl.BlockSpec((1, tl, n), lambda g: (ffn(g) // tps, ffn(g) % tps, 0))
    seq_f = lambda n: pl.BlockSpec((1, 1, n), lambda g: (ffn(g) // tps, 0, 0))
    tok_m = lambda w, idx: pl.BlockSpec((1, tl, w), lambda g: (mix(g) // tps, mix(g) % tps, idx))
    full_m = lambda idx: pl.BlockSpec((1, seq, 512), lambda g: (mix(g) // tps, 0, idx),
                                      pipeline_mode=pl.Buffered(1))
    const = lambda r, w: pl.BlockSpec((r, w), lambda g: (0, 0))
    return pl.pallas_call(
        functools.partial(_cd_ffn_kernel, tl=tl, d_ff=d_ff, final=final, n_tiles=n_tiles, tiles_per_seq=tps,
                          c=c, tq=tq),
        grid=(n_tiles + 1,),
        in_specs=[tok_f(d), _resident(wo.shape), seq_f(d), _resident((1, d)), seq_f(d), seq_f(d), seq_f(d),
                  _resident(wup.shape), _resident(cwf.shape), _resident((1, d_ff)), _resident(wdn.shape),
                  _resident((1, d)),
                  tok_m(512, 0), full_m(1), full_m(2), _resident(bias.shape),
                  tok_m(512, 3), tok_m(512, 4), tok_m(256, 10), tok_m(256, 11), tok_m(128, 0),
                  const(CONV_D, CONV_DIM_D), const(1, CONV_DIM_D), const(1, 128), const(1, 128),
                  pl.BlockSpec(memory_space=pltpu.SMEM), const(1, D_INNER)],
        out_specs=[tok_f(d),
                   pl.BlockSpec((1, 8, d_ff), lambda g: (ffn(g) // tps, 0, 0)),
                   pl.BlockSpec((1, H_D * P_D, N_D), lambda g: (mix(g) // tps, 0, 0))],
        out_shape=[jax.ShapeDtypeStruct((nseq, seq, d), F32), jax.ShapeDtypeStruct((nseq, 8, d_ff), F32),
                   jax.ShapeDtypeStruct((nseq, H_D * P_D, N_D), F32)],
        scratch_shapes=[pltpu.VMEM((2, tl, 1024), BF16), pltpu.VMEM((tl, d_ff), BF16),
                        pltpu.VMEM((tl + 8, FF_CHUNK), F32), pltpu.VMEM((1, 8, d_ff), F32),
                        pltpu.VMEM((c + 8, CONV_DIM_D), F32), pltpu.VMEM((H_D * P_D, N_D), F32)],
        compiler_params=_params(("arbitrary",)),
        name="cd_mixers_ffn",
    )(x, wo, g1, ng, sc, sh, g2, wup, cwf, cbf.reshape(1, d_ff), wdn, fg,
      main, main, main, bias,
      main, main, main, main, dt, conv_w, conv_b.reshape(1, CONV_DIM_D), pad(dt_bias), pad(a_log), d_skip,
      norm_g.reshape(1, D_INNER))


def _rope_tables(pos):
    half = DK_A // 2
    inv = jnp.power(ROPE_BASE, -jnp.arange(half, dtype=F32) / half)
    ang = pos.astype(F32)[:, None] * inv[None, :]
    cos = jnp.cos(ang)
    sin = jnp.sin(ang)
    return jnp.concatenate([cos, cos], axis=1), jnp.concatenate([-sin, sin], axis=1)


def _pad_rows(a, rows=8):
    return jnp.pad(a, ((0, 0), (rows - a.shape[1], 0), (0, 0)))


def _trunk(x, mods, pos, weights, caches, biases):
    nseq, seq, d = x.shape
    sample = caches is not None
    c = min(seq, SCAN_CHUNK)
    if sample:
        tl, seg = nseq * seq, seq
        pack = lambda a: a.reshape(1, nseq * seq, a.shape[-1])
        unpack = lambda a: a.reshape(nseq, seq, a.shape[-1])
        rows = lambda m: jnp.repeat(m, seq, axis=1).reshape(1, nseq * seq, d)
    else:
        tl = seg = min(seq, 512)
        pack = unpack = rows = lambda a: a
    depth = weights["w_up"].shape[0]
    outs = {k: [] for k in ("ret", "bk", "bv", "ck", "cv", "dconv", "dssm", "ffn")}
    cos, sin = _rope_tables(pos)
    if sample:
        cos, sin = jnp.tile(cos, (nseq, 1)), jnp.tile(sin, (nseq, 1))
    x = pack(x)
    for l in range(depth):
        i = l // 2
        sh1, sc1, g1, sh2, sc2, g2 = [rows(m) for m in mods[l]]
        ng1 = weights["norm_g"][l, 0].reshape(1, d)
        ng2 = weights["norm_g"][l, 1].reshape(1, d)
        fused = None
        if l % 2 == 0:
            qkv, kb, vb = _inproj_ab(x, ng1, sc1, sh1, weights["w_in_ab"][i], cos, sin, tl)
            qkv = unpack(qkv)
            s0 = caches["ret"][i] if sample else jnp.zeros((nseq, H_A, DK_A, DV_A), F32)
            o1, s_new = _retention(qkv, s0, weights["ret_gn"][i], c)
            lam_init = 0.8 - 0.6 * math.exp(-0.3 * l)
            if sample:
                o2 = _diff_attn_sample(qkv, caches["bk"][i], caches["bv"][i], biases["t5"],
                                       weights["lam_q"][i], weights["lam_k"][i], weights["diff_gn"][i], lam_init)
            else:
                o2 = _diff_attn(qkv, biases["t5"], weights["lam_q"][i], weights["lam_k"][i],
                                weights["diff_gn"][i], lam_init)
            wo = weights["w_out_ab"][i]
            outs["ret"].append(s_new)
            outs["bk"].append(kb.reshape(nseq, seq, H_B, 2 * DH_B))
            outs["bv"].append(vb.reshape(nseq, seq, H_B, DV_B))
        else:
            keep = seq if sample else min(C_WINDOW, seq)
            main, dt, kc, vc, tail = _inproj_cd(x, ng1, sc1, sh1, weights["w_in_cd"][i], tl,
                                                keep * (tl // seg), seg)
            main, dt = unpack(main), unpack(dt)
            if sample:
                o1 = _band_attn_sample(main, caches["ck"][i], caches["cv"][i], biases["band"][i])
                conv_past = _pad_rows(caches["dconv"][i])
                s0 = caches["dssm"][i].reshape(nseq, H_D * P_D, N_D)
                o2, ssm_new = _ssd(main, dt, weights["d_conv_w"][i], weights["d_conv_b"][i], weights["d_dt_bias"][i],
                                   weights["d_a_log"][i], weights["d_skip"][i], weights["d_norm_g"][i], conv_past, s0,
                                   c)
            else:
                fused = _cd_mixers_ffn(
                    x, main, dt, biases["band"][i], weights["w_out_cd"][i], g1, ng2, sc2, sh2, g2, weights["w_up"][l],
                    weights["ffn_conv_w"][l], weights["ffn_conv_b"][l], weights["w_down"][l],
                    weights["final_g"].reshape(1, d), weights["d_conv_w"][i], weights["d_conv_b"][i],
                    weights["d_dt_bias"][i], weights["d_a_log"][i], weights["d_skip"][i], weights["d_norm_g"][i],
                    FUSED_TL, final=(l == depth - 1))
                ssm_new = fused[2]
            wo = weights["w_out_cd"][i]
            outs["ck"].append(kc.reshape(nseq, keep, H_C, DH_C))
            outs["cv"].append(vc.reshape(nseq, keep, H_C, DH_C))
            outs["dconv"].append(tail[:, 8 - (CONV_D - 1):])
            outs["dssm"].append(ssm_new.reshape(nseq, H_D, P_D, N_D))
        d_ff = weights["w_down"].shape[1]
        if fused is not None:
            x, ftail = fused[0], fused[1]
        else:
            ffn_past = _pad_rows(caches["ffn"][l]) if sample else jnp.zeros((nseq, 8, d_ff), F32)
            x, ftail = _outproj_ffn(x, pack(o1), pack(o2), wo, g1, ng2, sc2, sh2, g2, weights["w_up"][l],
                                    weights["ffn_conv_w"][l], weights["ffn_conv_b"][l], weights["w_down"][l], ffn_past,
                                    weights["final_g"].reshape(1, d), tl, seg, final=(l == depth - 1))
        outs["ffn"].append(ftail[:, 8 - (CONV_F - 1):])
    stk = lambda t: jnp.stack(t).astype(F32)
    x = unpack(x)
    return (x,) + tuple(stk(outs[k]) for k in ("ret", "bk", "bv", "ck", "cv", "dconv", "dssm", "ffn"))


def kernel(x_prompt, x_sample, cache_ret_state, cache_b_k, cache_b_v, cache_c_k, cache_c_v, state_d_conv, state_d_ssm, state_ffn_conv, c_prompt, c_sample, w_mod, b_mod, norm_g, final_g, t5_table, w_in_ab, w_out_ab, ret_gn, lam_q, lam_k, diff_gn, w_in_cd, w_out_cd, rel_table, d_conv_w, d_conv_b, d_dt_bias, d_a_log, d_skip, d_norm_g, w_up, ffn_conv_w, ffn_conv_b, w_down):
    batch, seq, d = x_prompt.shape
    dec_batch, dec_seq, _ = x_sample.shape
    past = cache_b_k.shape[2]
    depth = w_mod.shape[0]
    assert dec_seq <= CHUNK and past % CHUNK == 0 and cache_c_k.shape[2] == C_WINDOW
    assert seq % BAND_TQ == 0 and seq % DIFF_TQ == 0

    w_in_cd_p = jnp.pad(w_in_cd, ((0, 0), (0, 0), (0, N_CD_PAD - w_in_cd.shape[-1]))).astype(BF16)
    weights = dict(
        norm_g=norm_g, final_g=final_g, w_in_ab=w_in_ab.astype(BF16), w_out_ab=w_out_ab.astype(BF16),
        ret_gn=ret_gn, lam_q=lam_q, lam_k=lam_k, diff_gn=diff_gn, w_in_cd=w_in_cd_p,
        w_out_cd=w_out_cd.astype(BF16), d_conv_w=d_conv_w, d_conv_b=d_conv_b, d_dt_bias=d_dt_bias, d_a_log=d_a_log,
        d_skip=d_skip, d_norm_g=d_norm_g, w_up=w_up.astype(BF16), ffn_conv_w=ffn_conv_w, ffn_conv_b=ffn_conv_b,
        w_down=w_down.astype(BF16))

    mod = _modulation(jnp.concatenate([c_prompt, c_sample], axis=0), w_mod, b_mod)

    def pieces(rows):
        return [[m[:, None, :] for m in jnp.split(mod[l, rows], 6, axis=-1)] for l in range(depth)]

    band = [_band_bias(rel_table[i]) for i in range(rel_table.shape[0])]
    biases_p = dict(t5=_t5_bias_prompt(t5_table, seq), band=band)
    biases_s = dict(t5=_t5_bias_sample(t5_table, past, dec_seq),
                    band=[b[:, :dec_seq, :C_WINDOW + dec_seq] for b in band])
    caches = dict(
        ret=cache_ret_state,
        bk=cache_b_k.reshape(cache_b_k.shape[0], dec_batch, past, H_B * 2 * DH_B),
        bv=cache_b_v.reshape(cache_b_v.shape[0], dec_batch, past, H_B * DV_B),
        ck=cache_c_k.reshape(cache_c_k.shape[0], dec_batch, C_WINDOW, H_C * DH_C),
        cv=cache_c_v.reshape(cache_c_v.shape[0], dec_batch, C_WINDOW, H_C * DH_C),
        dconv=state_d_conv, dssm=state_d_ssm, ffn=state_ffn_conv)

    pos_p = jnp.arange(seq, dtype=jnp.int32)
    pos_s = past + jnp.arange(dec_seq, dtype=jnp.int32)
    y_p, ret_p, bk_p, bv_p, ck_p, cv_p, dconv_p, dssm_p, ffn_p = _trunk(
        x_prompt, pieces(slice(0, batch)), pos_p, weights, None, biases_p)
    y_s, ret_s, bk_s, bv_s, ck_s, cv_s, dconv_s, dssm_s, ffn_s = _trunk(
        x_sample, pieces(slice(batch, batch + dec_batch)), pos_s, weights, caches, biases_s)
    return (y_p, y_s, ret_p, ret_s, bk_p, bk_s, bv_p, bv_s, ck_p, ck_s, cv_p, cv_s,
            dconv_p, dconv_s, dssm_p, dssm_s, ffn_p, ffn_s)
```

```python
import functools
import math

import numpy as np
import jax
import jax.numpy as jnp
from jax import lax
from jax.experimental import pallas as pl
from jax.experimental.pallas import tpu as pltpu

F32 = jnp.float32
BF16 = jnp.bfloat16

CHUNK = 64
EPS = 1e-6
NEG = -1e30
H_A, DK_A, DV_A = 4, 128, 128
ROPE_BASE = 10000.0
H_B, DH_B, DV_B = 4, 64, 128
T5_BUCKETS, T5_MAX_DIST = 32, 128
H_C, DH_C = 8, 64
BAND_CHUNKS = 8
C_WINDOW = BAND_CHUNKS * CHUNK
REL_CLIP = 128
H_D, P_D, G_D, N_D = 8, 64, 2, 128
D_INNER = H_D * P_D
CONV_D = 4
CONV_DIM_D = D_INNER + 2 * G_D * N_D
CONV_F = 3
N_AB_COLS = 7 * 512
N_CD_MAIN = 6 * 512
N_CD_PAD = N_CD_MAIN + 128
FF_CHUNK = 256
BAND_TQ = 256
DIFF_TQ = 256
SCAN_CHUNK = 256
FUSED_TL = 256
INPROJ_TL = 1024
VMEM_LIMIT = 56 * 1024 * 1024
LOG2E = math.log2(math.e)
Q_B_SCALE = DH_B ** -0.5 * LOG2E
Q_C_SCALE = DH_C ** -0.5 * LOG2E


def _mm(a, b):
    return jnp.dot(a, b, preferred_element_type=F32)


def _mm_nt(a, b):
    return lax.dot_general(a, b, (((1,), (1,)), ((), ())), preferred_element_type=F32)


def _mm_tn(a, b):
    return lax.dot_general(a, b, (((0,), (0,)), ((), ())), preferred_element_type=F32)


def _mm_exact(a, b):
    return jnp.dot(a, b, preferred_element_type=F32, precision=lax.Precision.HIGHEST)


def _mm_nt_exact(a, b):
    return lax.dot_general(a, b, (((1,), (1,)), ((), ())), preferred_element_type=F32,
                           precision=lax.Precision.HIGHEST)


def _silu(x):
    return x * (1.0 / (1.0 + jnp.exp(-x)))


def _softplus(x):
    return jnp.maximum(x, 0.0) + jnp.log1p(jnp.exp(-jnp.abs(x)))


def _gelu_tanh(x):
    k1 = -2.0 * math.sqrt(2.0 / math.pi) * math.log2(math.e)
    return x * (1.0 / (1.0 + jnp.exp2(x * (k1 + (k1 * 0.044715) * (x * x)))))


def _norm_mod(x, g, sc, sh):
    ms = jnp.mean(x * x, axis=-1, keepdims=True)
    return (x * lax.rsqrt(ms + EPS)) * g * (1.0 + sc) + sh


def _mod_spec(a, tl):
    if a.shape[1] == 1:
        return pl.BlockSpec((1, 1, a.shape[2]), lambda b, l: (b, 0, 0))
    return pl.BlockSpec((1, tl, a.shape[2]), lambda b, l: (b, l, 0))


def _params(sem):
    return pltpu.CompilerParams(dimension_semantics=sem, vmem_limit_bytes=VMEM_LIMIT)


def _resident(shape):
    nd = len(shape)
    return pl.BlockSpec(shape, lambda *_: (0,) * nd, pipeline_mode=pl.Buffered(1))


def _mod_kernel(c_ref, w_ref, b_ref, o_ref):
    c = c_ref[...]
    o_ref[0] = _mm(_silu(c).astype(BF16), w_ref[0].astype(BF16)) + b_ref[0]


def _modulation(c_all, w_mod, b_mod):
    depth, d, n = w_mod.shape
    r = c_all.shape[0]
    tn = 1536
    return pl.pallas_call(
        _mod_kernel,
        grid=(depth, n // tn),
        in_specs=[pl.BlockSpec((r, d), lambda l, j: (0, 0)),
                  pl.BlockSpec((1, d, tn), lambda l, j: (l, 0, j)),
                  pl.BlockSpec((1, 1, tn), lambda l, j: (l, 0, j))],
        out_specs=pl.BlockSpec((1, r, tn), lambda l, j: (l, 0, j)),
        out_shape=jax.ShapeDtypeStruct((depth, r, n), F32),
        compiler_params=_params(("parallel", "parallel")),
        name="modulation",
    )(c_all, w_mod, b_mod.reshape(depth, 1, n))


def _table_gather_kernel(tab_ref, idx_ref, o_ref, *, n_entries, n_heads):
    idx = idx_ref[...]

    def body(r, accs):
        m = idx == r
        return tuple(jnp.where(m, tab_ref[hh, r], a) for hh, a in enumerate(accs))

    accs = lax.fori_loop(0, n_entries, body, tuple(jnp.zeros(idx.shape, F32) for _ in range(n_heads)))
    for hh in range(n_heads):
        o_ref[hh:hh + 1, :] = accs[hh]


def _table_gather(table, idx):
    t, h = table.shape
    w = idx.shape[0]
    return pl.pallas_call(
        functools.partial(_table_gather_kernel, n_entries=t, n_heads=h),
        in_specs=[pl.BlockSpec(memory_space=pltpu.SMEM),
                  pl.BlockSpec((1, w), lambda: (0, 0))],
        out_specs=pl.BlockSpec((h, w), lambda: (0, 0)),
        out_shape=jax.ShapeDtypeStruct((h, w), F32),
        name="table_gather",
    )(table.T, idx.reshape(1, w))


def _toeplitz_kernel(v_ref, o_ref, *, rows, cols, tile):
    x = pltpu.roll(jnp.broadcast_to(v_ref[0], (rows, v_ref.shape[2])), 0, 1, stride=1, stride_axis=0)
    if tile is None:
        o_ref[0] = x[:, :cols]
    else:
        for jb in range(cols // tile):
            o_ref[0, jb] = x[:, jb * tile:(jb + 1) * tile]


def _toeplitz(vec, rows, cols, tile=None):
    h, wv = vec.shape
    if tile is None:
        out_shape, block, imap = (h, rows, cols), (1, rows, cols), lambda i: (i, 0, 0)
    else:
        out_shape, block, imap = (h, cols // tile, rows, tile), (1, cols // tile, rows, tile), lambda i: (i, 0, 0, 0)
    return pl.pallas_call(
        functools.partial(_toeplitz_kernel, rows=rows, cols=cols, tile=tile),
        grid=(h,),
        in_specs=[pl.BlockSpec((1, 1, wv), lambda i: (i, 0, 0))],
        out_specs=pl.BlockSpec(block, imap),
        out_shape=jax.ShapeDtypeStruct(out_shape, F32),
        compiler_params=_params(("parallel",)),
        name="toeplitz",
    )(vec.reshape(h, 1, wv))


def _t5_bucket(rel):
    half = T5_BUCKETS // 2
    max_exact = half // 2
    base = jnp.where(rel > 0, half, 0)
    n = jnp.abs(rel)
    nf = jnp.maximum(n, 1).astype(F32)
    large = max_exact + (jnp.log(nf / max_exact) / math.log(T5_MAX_DIST / max_exact) * (half - max_exact)).astype(jnp.int32)
    large = jnp.minimum(large, half - 1)
    return base + jnp.where(n < max_exact, n, large)


def _wrapped_offsets(n_pos, n_neg):
    p = jnp.arange(n_pos + n_neg, dtype=jnp.int32)
    return jnp.where(p < n_pos, p, p - (n_pos + n_neg))


def _t5_bias_prompt(t5_table, seq):
    nd = seq // DIFF_TQ
    u = _wrapped_offsets(seq, DIFF_TQ)
    vec = _table_gather(t5_table, _t5_bucket(u - (seq - DIFF_TQ))) * LOG2E
    return _toeplitz(vec, DIFF_TQ, seq, tile=DIFF_TQ)


def _t5_bias_sample(t5_table, past, lq):
    u = _wrapped_offsets(past + lq, lq)
    vec = _table_gather(t5_table, _t5_bucket(u - past)) * LOG2E
    return _toeplitz(vec, lq, past + 2 * lq)[:, :, :past + lq]


def _band_bias(rel_table):
    u = _wrapped_offsets(3 * BAND_TQ, BAND_TQ)
    idx = jnp.clip(2 * BAND_TQ - u, -REL_CLIP, REL_CLIP) + REL_CLIP
    bias = _toeplitz(_table_gather(rel_table, idx) * LOG2E, BAND_TQ, 3 * BAND_TQ)
    dc = (np.arange(BAND_TQ)[:, None] // CHUNK + C_WINDOW // CHUNK) - np.arange(3 * BAND_TQ)[None, :] // CHUNK
    return jnp.where(jnp.asarray((dc >= 0) & (dc <= BAND_CHUNKS))[None], bias, NEG)


def _inproj_ab_kernel(x_ref, g_ref, sc_ref, sh_ref, w_ref, cos_ref, sin_ref, qkv_ref, kb_ref, vb_ref, *, tl):
    h = _norm_mod(x_ref[0], g_ref[...], sc_ref[0], sh_ref[0]).astype(BF16)
    cos = cos_ref[...]
    sin = sin_ref[...]
    for j in range(7):
        y = _mm(h, w_ref[:, j * 512:(j + 1) * 512])
        if j < 2:
            for hh in range(H_A):
                yh = y[:, hh * 128:(hh + 1) * 128]
                yh = yh * cos + pltpu.roll(yh, 64, 1) * sin
                if j == 1:
                    yh = yh * (DK_A ** -0.5)
                qkv_ref[0, :, j * 512 + hh * 128:j * 512 + (hh + 1) * 128] = yh.astype(BF16)
        else:
            if j == 4:
                y = y * Q_B_SCALE
            if j >= 5:
                dst = kb_ref if j == 5 else vb_ref
                for hh in range(H_B):
                    dst[0, pl.ds(hh, tl, stride=H_B), :] = y[:, hh * 128:(hh + 1) * 128]
            qkv_ref[0, :, j * 512:(j + 1) * 512] = y.astype(BF16)


def _inproj_ab(x, g, sc, sh, w, cos, sin, tl):
    nseq, seq, d = x.shape
    grid = (nseq, seq // tl)
    tok = lambda n: pl.BlockSpec((1, tl, n), lambda b, l: (b, l, 0))
    cache = pl.BlockSpec((1, tl * H_B, 128), lambda b, l: (b, l, 0))
    return pl.pallas_call(
        functools.partial(_inproj_ab_kernel, tl=tl),
        grid=grid,
        in_specs=[tok(d), _resident((1, d)), _mod_spec(sc, tl), _mod_spec(sh, tl), _resident((d, N_AB_COLS)),
                  pl.BlockSpec((tl, 128), lambda b, l: (l, 0)),
                  pl.BlockSpec((tl, 128), lambda b, l: (l, 0))],
        out_specs=[tok(N_AB_COLS), cache, cache],
        out_shape=[jax.ShapeDtypeStruct((nseq, seq, N_AB_COLS), BF16),
                   jax.ShapeDtypeStruct((nseq, seq * H_B, 128), F32),
                   jax.ShapeDtypeStruct((nseq, seq * H_B, 128), F32)],
        compiler_params=_params(("parallel", "parallel")),
        name="inproj_ab",
    )(x, g, sc, sh, w, cos, sin)


def _retention_kernel(q_ref, k_ref, v_ref, g_ref, s0_ref, dec_ref, qd_ref, kd_ref, gc_ref, gn_ref,
                      o_ref, sn_ref, *, c, n_chunks):
    states = [s0_ref[0, h] for h in range(H_A)]
    for ci in range(n_chunks):
        rows = slice(ci * c, (ci + 1) * c)
        for h in range(H_A):
            cs = slice(h * 128, (h + 1) * 128)
            q = q_ref[0, rows, cs]
            k = k_ref[0, rows, cs]
            v = v_ref[0, rows, cs]
            state = states[h]
            s = _mm_nt(q, k) * dec_ref[h]
            o = _mm(s.astype(BF16), v) + _mm((q.astype(F32) * qd_ref[h]).astype(BF16), state.astype(BF16))
            states[h] = gc_ref[h] * state + _mm_tn((k.astype(F32) * kd_ref[h]).astype(BF16), v)
            mu = jnp.mean(o, axis=-1, keepdims=True)
            dlt = o - mu
            var = jnp.mean(dlt * dlt, axis=-1, keepdims=True)
            y = dlt * lax.rsqrt(var + EPS) * gn_ref[:, cs] * _silu(g_ref[0, rows, cs].astype(F32))
            o_ref[0, rows, cs] = y.astype(BF16)
    for h in range(H_A):
        sn_ref[0, h] = states[h]


def _retention_consts(c):
    lg = np.log1p(-np.exp2(-5.0 - np.arange(H_A, dtype=np.float32))).astype(np.float32)
    idx = np.arange(c, dtype=np.float32)
    diff = idx[:, None] - idx[None, :]
    decay = np.where(diff[None] >= 0, np.exp(np.maximum(diff, 0.0)[None] * lg[:, None, None]), 0.0)
    qd = np.exp((idx + 1.0)[None, :] * lg[:, None])
    kd = np.exp((c - 1.0 - idx)[None, :] * lg[:, None])
    gc = np.exp(c * lg)
    bc = lambda a: np.ascontiguousarray(np.broadcast_to(a[..., None], a.shape + (128,))).astype(np.float32)
    return decay.astype(np.float32), bc(qd), bc(kd), bc(gc[:, None])


def _retention(qkv, s0, ret_gn, c):
    nseq, seq, _ = qkv.shape
    decay, qd, kd, gc = _retention_consts(c)
    w = H_A * 128
    col = lambda idx: pl.BlockSpec((1, seq, w), lambda b: (b, 0, idx))
    const = lambda shape: pl.BlockSpec(shape, lambda b: (0,) * len(shape))
    state = pl.BlockSpec((1, H_A, DK_A, DV_A), lambda b: (b, 0, 0, 0))
    return pl.pallas_call(
        functools.partial(_retention_kernel, c=c, n_chunks=seq // c),
        grid=(nseq,),
        in_specs=[col(0), col(1), col(2), col(3), state, const((H_A, c, c)), const((H_A, c, 128)),
                  const((H_A, c, 128)), const((H_A, 1, 128)), const((1, w))],
        out_specs=[col(0), state],
        out_shape=[jax.ShapeDtypeStruct((nseq, seq, w), BF16),
                   jax.ShapeDtypeStruct((nseq, H_A, DK_A, DV_A), F32)],
        compiler_params=_params(("parallel",)),
        name="retention",
    )(qkv, qkv, qkv, qkv, s0, decay, qd, kd, gc, ret_gn.reshape(1, w))


def _split_halves(q):
    lane = lax.broadcasted_iota(jnp.int32, q.shape, 1)
    zero = jnp.zeros_like(q)
    return jnp.where(lane < 64, q, zero), jnp.where(lane >= 64, q, zero)


def _lambda(lq_ref, lk_ref, lam_init):
    e = jnp.exp(jnp.sum(lq_ref[...] * lk_ref[...], axis=1, keepdims=True))
    return e[0:1, :] - e[1:2, :] + lam_init


def _diff_epilogue(o, gn, lam_init):
    ms = jnp.mean(o * o, axis=-1, keepdims=True)
    return (o * lax.rsqrt(ms + EPS)) * gn * (1.0 - lam_init)


def _diff_attn_kernel(q_ref, k_ref, v_ref, b_ref, lq_ref, lk_ref, gn_ref, o_ref, s_ref, qs_ref, mx_ref, ls_ref,
                      acc_ref, *, tq, nd, lam_init):
    row = lax.broadcasted_iota(jnp.int32, (tq, tq), 0) // CHUNK
    col = lax.broadcasted_iota(jnp.int32, (tq, tq), 1) // CHUNK
    chunk_mask = col <= row
    lam = _lambda(lq_ref, lk_ref, lam_init)

    def prepare(t):
        halves = _split_halves(q_ref[0, t * tq:(t + 1) * tq, :])
        for i in range(2):
            r = 2 * (t % 2) + i
            qs_ref[r] = halves[i]
            mx_ref[r] = jnp.full((tq, 128), NEG, F32)
            ls_ref[r] = jnp.zeros((tq, 128), F32)
            acc_ref[r] = jnp.zeros((tq, DV_B), F32)

    def far_bias(t, kb):
        if (t - kb - 1) * tq + 1 < T5_MAX_DIST:
            return None
        return b_ref[0, nd - 1 - t + kb, 0:1, 0:1]

    def scores(t, kb):
        k = k_ref[0, kb * tq:(kb + 1) * tq, :]
        far = far_bias(t, kb)
        for i in range(2):
            r = 2 * (t % 2) + i
            s = _mm_nt(qs_ref[r], k)
            if far is None:
                s = s + b_ref[0, nd - 1 - t + kb]
            if kb == t:
                s = jnp.where(chunk_mask, s, NEG)
            s_ref[r * nd + kb] = s
            m = s[:, 0:128]
            for c0 in range(128, tq, 128):
                m = jnp.maximum(m, s[:, c0:c0 + 128])
            mx_ref[r] = jnp.maximum(mx_ref[r], m if far is None else m + far)

    def row_max(t):
        for i in range(2):
            r = 2 * (t % 2) + i
            mx_ref[r] = jnp.broadcast_to(jnp.max(mx_ref[r], axis=1, keepdims=True), (tq, 128))

    def weigh(t, kb):
        v = v_ref[0, kb * tq:(kb + 1) * tq, :]
        far = far_bias(t, kb)
        for i in range(2):
            r = 2 * (t % 2) + i
            m = mx_ref[r] if far is None else mx_ref[r] - far
            l = ls_ref[r]
            ps = []
            for c0 in range(0, tq, 128):
                p = jnp.exp2(s_ref[r * nd + kb, :, c0:c0 + 128] - m)
                l = l + p
                ps.append(p.astype(BF16))
            ls_ref[r] = l
            acc_ref[r] += _mm(jnp.concatenate(ps, axis=1), v)

    def finish(t):
        r = 2 * (t % 2)
        l0 = jnp.sum(ls_ref[r], axis=1, keepdims=True)
        l1 = jnp.sum(ls_ref[r + 1], axis=1, keepdims=True)
        o = acc_ref[r] / l0 - lam * (acc_ref[r + 1] / l1)
        o_ref[0, t * tq:(t + 1) * tq, :] = _diff_epilogue(o, gn_ref[...], lam_init).astype(BF16)

    last = nd - 1
    prepare(last)
    for kb in range(last + 1):
        scores(last, kb)
    row_max(last)
    for t in range(last - 1, -1, -1):
        prepare(t)
        for kb in range(t + 2):
            if kb <= t:
                scores(t, kb)
            weigh(t + 1, kb)
        finish(t + 1)
        row_max(t)
    weigh(0, 0)
    finish(0)


def _diff_attn(qkv, bias, lam_q, lam_k, diff_gn, lam_init):
    nseq, seq, _ = qkv.shape
    tq = DIFF_TQ
    nd = seq // tq
    full = lambda off: pl.BlockSpec((1, seq, 128), lambda b, h: (b, 0, off + h))
    small = lambda r, c: pl.BlockSpec((r, c), lambda b, h: (0, 0))
    return pl.pallas_call(
        functools.partial(_diff_attn_kernel, tq=tq, nd=nd, lam_init=lam_init),
        grid=(nseq, H_B),
        in_specs=[full(16), full(20), full(24),
                  pl.BlockSpec((1, nd, tq, tq), lambda b, h: (h, 0, 0, 0)),
                  small(2, DH_B), small(2, DH_B), small(1, DV_B)],
        out_specs=full(0),
        out_shape=jax.ShapeDtypeStruct((nseq, seq, H_B * DV_B), BF16),
        scratch_shapes=[pltpu.VMEM((4 * nd, tq, tq), F32), pltpu.VMEM((4, tq, 128), BF16),
                        pltpu.VMEM((4, tq, 128), F32), pltpu.VMEM((4, tq, 128), F32),
                        pltpu.VMEM((4, tq, DV_B), F32)],
        compiler_params=_params(("parallel", "parallel")),
        name="diff_attn",
    )(qkv, qkv, qkv, bias, lam_q, lam_k, diff_gn.reshape(1, DV_B))


def _diff_attn_sample_kernel(q_ref, kc_ref, vc_ref, kp_ref, vp_ref, b_ref, lq_ref, lk_ref, gn_ref, o_ref,
                             *, past, lam_init):
    lam = _lambda(lq_ref, lk_ref, lam_init)
    for h in range(H_B):
        cs = slice(h * 128, (h + 1) * 128)
        qs = _split_halves(q_ref[0, :, cs])
        kp = kp_ref[0, :, cs].astype(BF16)
        vp = vp_ref[0, :, cs].astype(BF16)
        kc = kc_ref[0, :, cs]
        vc = vc_ref[0, :, cs]
        b = b_ref[h]
        probs = []
        for i in range(2):
            sp = _mm_nt(qs[i], kp) + b[:, :past]
            sc = _mm_nt(qs[i], kc) + b[:, past:]
            m = jnp.maximum(jnp.max(sp, axis=1, keepdims=True), jnp.max(sc, axis=1, keepdims=True))
            pp = jnp.exp2(sp - m)
            pc = jnp.exp2(sc - m)
            l = jnp.sum(pp, axis=1, keepdims=True) + jnp.sum(pc, axis=1, keepdims=True)
            probs.append((pp / l, pc / l))
        ap = probs[0][0] - lam * probs[1][0]
        ac = probs[0][1] - lam * probs[1][1]
        o = _mm(ap.astype(BF16), vp) + _mm(ac.astype(BF16), vc)
        o_ref[0, :, cs] = _diff_epilogue(o, gn_ref[...], lam_init).astype(BF16)


def _diff_attn_sample(qkv, k_past, v_past, bias, lam_q, lam_k, diff_gn, lam_init):
    nseq, lq, _ = qkv.shape
    past = k_past.shape[1]
    w = H_B * DV_B
    cur = lambda idx: pl.BlockSpec((1, lq, w), lambda b: (b, 0, idx))
    old = pl.BlockSpec((1, past, w), lambda b: (b, 0, 0))
    small = lambda r, c: pl.BlockSpec((r, c), lambda b: (0, 0))
    return pl.pallas_call(
        functools.partial(_diff_attn_sample_kernel, past=past, lam_init=lam_init),
        grid=(nseq,),
        in_specs=[cur(4), cur(5), cur(6), old, old, _resident(bias.shape),
                  small(2, DH_B), small(2, DH_B), small(1, DV_B)],
        out_specs=cur(0),
        out_shape=jax.ShapeDtypeStruct((nseq, lq, w), BF16),
        compiler_params=_params(("parallel",)),
        name="diff_attn_sample",
    )(qkv, qkv, qkv, k_past, v_past, bias, lam_q, lam_k, diff_gn.reshape(1, DV_B))


def _inproj_cd_kernel(x_ref, g_ref, sc_ref, sh_ref, w_ref, main_ref, dt_ref, kc_ref, vc_ref, tail_ref,
                      *, tl, seg, keep):
    h = _norm_mod(x_ref[0], g_ref[...], sc_ref[0], sh_ref[0]).astype(BF16)
    for j in range(6):
        y = _mm(h, w_ref[:, j * 512:(j + 1) * 512])
        main_ref[0, :, j * 512:(j + 1) * 512] = (y * Q_C_SCALE if j == 0 else y).astype(BF16)
        if j == 1:
            kc_ref[0] = y[tl - keep:tl, :]
        if j == 2:
            vc_ref[0] = y[tl - keep:tl, :]
        if j >= 4:
            for s in range(tl // seg):
                tail_ref[s, :, (j - 4) * 512:(j - 3) * 512] = y[(s + 1) * seg - 8:(s + 1) * seg, :]
    dt_ref[0] = _mm(h, w_ref[:, N_CD_MAIN:N_CD_PAD])


def _inproj_cd(x, g, sc, sh, w, tl, keep, seg):
    nseq, seq, d = x.shape
    nseg = tl // seg
    assert keep <= tl and seq % tl == 0
    tok = lambda n: pl.BlockSpec((1, tl, n), lambda b, l: (b, l, 0))
    kept = pl.BlockSpec((1, keep, 512), lambda b, l: (b, 0, 0))
    return pl.pallas_call(
        functools.partial(_inproj_cd_kernel, tl=tl, seg=seg, keep=keep),
        grid=(nseq, seq // tl),
        in_specs=[tok(d), _resident((1, d)), _mod_spec(sc, tl), _mod_spec(sh, tl), _resident((d, N_CD_PAD))],
        out_specs=[tok(N_CD_MAIN), tok(128), kept, kept,
                   pl.BlockSpec((nseg, 8, CONV_DIM_D), lambda b, l: (b, 0, 0))],
        out_shape=[jax.ShapeDtypeStruct((nseq, seq, N_CD_MAIN), BF16),
                   jax.ShapeDtypeStruct((nseq, seq, 128), F32),
                   jax.ShapeDtypeStruct((nseq, keep, 512), F32),
                   jax.ShapeDtypeStruct((nseq, keep, 512), F32),
                   jax.ShapeDtypeStruct((nseq * nseg, 8, CONV_DIM_D), F32)],
        compiler_params=_params(("parallel", "arbitrary")),
        name="inproj_cd",
    )(x, g, sc, sh, w)


def _band_sample_kernel(q_ref, kc_ref, vc_ref, kp_ref, vp_ref, b_ref, o_ref, *, past):
    for pr in range(H_C // 2):
        cs = slice(pr * 128, (pr + 1) * 128)
        halves = _split_halves(q_ref[0, :, cs])
        kp = kp_ref[0, :, cs].astype(BF16)
        vp = vp_ref[0, :, cs].astype(BF16)
        kc = kc_ref[0, :, cs]
        vc = vc_ref[0, :, cs]
        outs = []
        for hi, qh in enumerate(halves):
            b = b_ref[2 * pr + hi]
            sp = _mm_nt(qh, kp) + b[:, :past]
            sc = _mm_nt(qh, kc) + b[:, past:]
            m = jnp.maximum(jnp.max(sp, axis=1, keepdims=True), jnp.max(sc, axis=1, keepdims=True))
            pp = jnp.exp2(sp - m)
            pc = jnp.exp2(sc - m)
            l = jnp.sum(pp, axis=1, keepdims=True) + jnp.sum(pc, axis=1, keepdims=True)
            outs.append(_mm((pp / l).astype(BF16), vp) + _mm((pc / l).astype(BF16), vc))
        lane = lax.broadcasted_iota(jnp.int32, outs[0].shape, 1)
        o_ref[0, :, cs] = jnp.where(lane < 64, outs[0], outs[1]).astype(BF16)


def _band_attn_sample(main, k_past, v_past, bias):
    nseq, lq, _ = main.shape
    past = k_past.shape[1]
    w = H_C * DH_C
    cur = lambda idx: pl.BlockSpec((1, lq, w), lambda b: (b, 0, idx))
    old = pl.BlockSpec((1, past, w), lambda b: (b, 0, 0))
    return pl.pallas_call(
        functools.partial(_band_sample_kernel, past=past),
        grid=(nseq,),
        in_specs=[cur(0), cur(1), cur(2), old, old, _resident(bias.shape)],
        out_specs=cur(0),
        out_shape=jax.ShapeDtypeStruct((nseq, lq, w), BF16),
        compiler_params=_params(("parallel",)),
        name="band_attn_sample",
    )(main, main, main, k_past, v_past, bias)


def _ssd_kernel(z_ref, xs_ref, bm_ref, cm_ref, dt_ref, cw_ref, cb_ref, dtb_ref, alog_ref, dskip_ref, ng_ref,
                past_ref, s0_ref, o_ref, sn_ref, buf_ref, st_ref, *, c):
    @pl.when(pl.program_id(1) == 0)
    def _():
        buf_ref[0:8, :] = past_ref[0]
        st_ref[...] = s0_ref[0]

    for _ in _ssd_steps(z_ref.at[0], xs_ref.at[0], bm_ref.at[0], cm_ref.at[0], dt_ref.at[0], cw_ref, cb_ref, dtb_ref,
                        alog_ref, dskip_ref, ng_ref, o_ref.at[0], buf_ref, st_ref, c=c):
        pass
    sn_ref[0] = st_ref[...]


def _ssd(main, dt, conv_w, conv_b, dt_bias, a_log, d_skip, norm_g, conv_past, s0, c):
    nseq, seq, _ = main.shape
    pad = lambda a: jnp.pad(a.reshape(1, H_D), ((0, 0), (0, 128 - H_D)))
    blk = lambda w, idx: pl.BlockSpec((1, c, w), lambda b, l: (b, l, idx))
    const = lambda r, w: pl.BlockSpec((r, w), lambda b, l: (0, 0))
    state = pl.BlockSpec((1, H_D * P_D, N_D), lambda b, l: (b, 0, 0))
    return pl.pallas_call(
        functools.partial(_ssd_kernel, c=c),
        grid=(nseq, seq // c),
        in_specs=[blk(512, 3), blk(512, 4), blk(256, 10), blk(256, 11), blk(128, 0),
                  const(CONV_D, CONV_DIM_D), const(1, CONV_DIM_D), const(1, 128), const(1, 128),
                  pl.BlockSpec(memory_space=pltpu.SMEM), const(1, D_INNER),
                  pl.BlockSpec((1, 8, CONV_DIM_D), lambda b, l: (b, 0, 0)), state],
        out_specs=[blk(512, 0), state],
        out_shape=[jax.ShapeDtypeStruct((nseq, seq, D_INNER), BF16),
                   jax.ShapeDtypeStruct((nseq, H_D * P_D, N_D), F32)],
        scratch_shapes=[pltpu.VMEM((c + 8, CONV_DIM_D), F32), pltpu.VMEM((H_D * P_D, N_D), F32)],
        compiler_params=_params(("parallel", "arbitrary")),
        name="ssd",
    )(main, main, main, main, dt, conv_w, conv_b.reshape(1, CONV_DIM_D), pad(dt_bias), pad(a_log), d_skip,
      norm_g.reshape(1, D_INNER), conv_past, s0)


def _ffn_kernel(x_ref, o1_ref, o2_ref, wo_ref, g1_ref, ng_ref, sc_ref, sh_ref, g2_ref, wup_ref, cw_ref, cb_ref,
                wdn_ref, past_ref, fg_ref, out_ref, tail_ref, act_ref, wb_ref, gt_ref, *, tl, d_ff, final, seg):
    @pl.when(pl.program_id(1) == 0)
    def _():
        gt_ref[...] = past_ref[...]

    for _ in _ffn_steps(x_ref, o1_ref.at[0], o2_ref.at[0], wo_ref, g1_ref, ng_ref, sc_ref, sh_ref, g2_ref, wup_ref,
                        cw_ref, cb_ref, wdn_ref, fg_ref, out_ref, tail_ref, act_ref, wb_ref, gt_ref,
                        tl=tl, d_ff=d_ff, final=final, seg=seg):
        pass


def _outproj_ffn(x, o1, o2, wo, g1, ng, sc, sh, g2, wup, cw, cb, wdn, past, fg, tl, seg, final):
    nseq, seq, d = x.shape
    d_ff = wdn.shape[0]
    nseg = tl // seg
    assert nseg == 1 or seq == tl
    tok = lambda n: pl.BlockSpec((1, tl, n), lambda b, l: (b, l, 0))
    tail = pl.BlockSpec((nseg, 8, d_ff), lambda b, l: (b, 0, 0))
    return pl.pallas_call(
        functools.partial(_ffn_kernel, tl=tl, d_ff=d_ff, final=final, seg=seg),
        grid=(nseq, seq // tl),
        in_specs=[tok(d), tok(o1.shape[-1]), tok(o2.shape[-1]), _resident(wo.shape), _mod_spec(g1, tl),
                  _resident((1, d)), _mod_spec(sc, tl), _mod_spec(sh, tl), _mod_spec(g2, tl), _resident(wup.shape),
                  _resident(cw.shape), _resident((1, d_ff)), _resident(wdn.shape), tail, _resident((1, d))],
        out_specs=[tok(d), tail],
        out_shape=[jax.ShapeDtypeStruct((nseq, seq, d), F32), jax.ShapeDtypeStruct((nseq * nseg, 8, d_ff), F32)],
        scratch_shapes=[pltpu.VMEM((tl, d_ff), BF16), pltpu.VMEM((nseg * (seg + 8), FF_CHUNK), F32),
                        pltpu.VMEM((nseg, 8, d_ff), F32)],
        compiler_params=_params(("parallel", "arbitrary")),
        name="outproj_ffn",
    )(x, o1, o2, wo, g1, ng, sc, sh, g2, wup, cw, cb.reshape(1, d_ff), wdn, past, fg)


def _interleave(gens, strides):
    live = list(zip(gens, strides))
    while live:
        for item in list(live):
            for _ in range(item[1]):
                try:
                    next(item[0])
                except StopIteration:
                    live.remove(item)
                    break


def _chain(*gens):
    for g in gens:
        yield from g


def _ffn_steps(x_ref, o1, o2, wo_ref, g1_ref, ng_ref, sc_ref, sh_ref, g2_ref, wup_ref, cw_ref, cb_ref, wdn_ref,
               fg_ref, out_ref, tail_ref, act_ref, wb_ref, gt_ref, *, tl, d_ff, final, seg):
    nseg = tl // seg
    stride = seg + 8
    half = wo_ref.shape[0] // 2
    mix = _mm(o1[...], wo_ref[0:half, :]) + _mm(o2[...], wo_ref[half:2 * half, :])
    x1 = x_ref[0] + g1_ref[0] * mix
    out_ref[0] = x1
    h = _norm_mod(x1, ng_ref[...], sc_ref[0], sh_ref[0]).astype(BF16)
    yield
    for j in range(d_ff // FF_CHUNK):
        c0 = j * FF_CHUNK
        a = _mm(h, wup_ref[:, c0:c0 + FF_CHUNK])
        g = _mm(h, wup_ref[:, d_ff + c0:d_ff + c0 + FF_CHUNK])
        for s in range(nseg):
            wb_ref[s * stride:s * stride + 8, :] = gt_ref[s, :, c0:c0 + FF_CHUNK]
            wb_ref[s * stride + 8:(s + 1) * stride, :] = g[s * seg:(s + 1) * seg, :]
            gt_ref[s, :, c0:c0 + FF_CHUNK] = g[(s + 1) * seg - 8:(s + 1) * seg, :]
        back2 = [wb_ref[pl.ds(s * stride + 6, seg), :] for s in range(nseg)]
        back1 = [wb_ref[pl.ds(s * stride + 7, seg), :] for s in range(nseg)]
        if nseg > 1:
            back2, back1 = [jnp.concatenate(back2, axis=0)], [jnp.concatenate(back1, axis=0)]
        gc = (cb_ref[:, c0:c0 + FF_CHUNK] + back2[0] * cw_ref[0:1, c0:c0 + FF_CHUNK]
              + back1[0] * cw_ref[1:2, c0:c0 + FF_CHUNK] + g * cw_ref[2:3, c0:c0 + FF_CHUNK])
        act_ref[:, c0:c0 + FF_CHUNK] = (a * _gelu_tanh(gc)).astype(BF16)
        yield
    split = (d_ff // FF_CHUNK + 1) // 2 * FF_CHUNK
    dn = _mm(act_ref[:, 0:split], wdn_ref[0:split, :])
    yield
    dn = dn + _mm(act_ref[:, split:d_ff], wdn_ref[split:d_ff, :])
    x2 = out_ref[0] + g2_ref[0] * dn
    if final:
        ms = jnp.mean(x2 * x2, axis=-1, keepdims=True)
        x2 = (x2 * lax.rsqrt(ms + EPS)) * fg_ref[...]
    out_ref[0] = x2
    tail_ref[...] = gt_ref[...]
    yield


def _band_steps(q, k_ref, v_ref, b_ref, o, t, *, tq):
    lane = lax.broadcasted_iota(jnp.int32, (tq, 128), 1)
    starts, offs = [], []
    for d in range(3):
        kt = t - 2 + d
        starts.append(pl.multiple_of(jnp.maximum(kt, 0) * tq, tq))
        offs.append(jnp.where(kt >= 0, 0.0, NEG))

    def scores(head):
        pr, hi = divmod(head, 2)
        cs = slice(pr * 128, (pr + 1) * 128)
        qh = _split_halves(q[:, cs])[hi]
        ss = []
        for d in range(3):
            s = _mm_nt(qh, k_ref[0, pl.ds(starts[d], tq), cs]) + b_ref[head, :, d * tq:(d + 1) * tq]
            ss.append(s + offs[d] if d < 2 else s)
        mm = ss[2][:, 0:128]
        for d in range(3):
            for c0 in range(0, tq, 128):
                mm = jnp.maximum(mm, ss[d][:, c0:c0 + 128])
        return ss, jnp.max(mm, axis=1, keepdims=True)

    def weigh(head, ss, m):
        cs = slice((head // 2) * 128, (head // 2 + 1) * 128)
        ls = jnp.zeros((tq, 128), F32)
        acc = jnp.zeros((tq, 128), F32)
        for d in range(3):
            p = jnp.exp2(ss[d] - m)
            for c0 in range(0, tq, 128):
                ls = ls + p[:, c0:c0 + 128]
            acc = acc + _mm(p.astype(BF16), v_ref[0, pl.ds(starts[d], tq), cs])
        return acc / jnp.sum(ls, axis=1, keepdims=True)

    outs = []
    pending = scores(0)
    yield
    for head in range(H_C):
        nxt = scores(head + 1) if head + 1 < H_C else None
        outs.append(weigh(head, *pending))
        pending = nxt
        if head % 2 == 1:
            cs = slice((head // 2) * 128, (head // 2 + 1) * 128)
            o[:, cs] = jnp.where(lane < 64, outs[head - 1], outs[head]).astype(BF16)
        yield


def _ssd_steps(z, xs_in, bm_in, cm_in, dt_in, cw_ref, cb_ref, dtb_ref, alog_ref, dskip_ref, ng_ref, o,
               buf_ref, st_ref, *, c):
    buf_ref[8:8 + c, 0:D_INNER] = xs_in[...].astype(F32)
    buf_ref[8:8 + c, D_INNER:D_INNER + 256] = bm_in[...].astype(F32)
    buf_ref[8:8 + c, D_INNER + 256:CONV_DIM_D] = cm_in[...].astype(F32)
    conv = cb_ref[...] + buf_ref[pl.ds(5, c), :] * cw_ref[0:1, :]
    for i in range(1, CONV_D):
        conv = conv + buf_ref[pl.ds(5 + i, c), :] * cw_ref[i:i + 1, :]
    buf_ref[0:8, :] = buf_ref[c:c + 8, :]
    act = _silu(conv)
    xs = act[:, 0:D_INNER]
    bm = act[:, D_INNER:D_INNER + 256].astype(BF16)
    cm = act[:, D_INNER + 256:CONV_DIM_D].astype(BF16)
    yield

    dt = _softplus(dt_in[...] + dtb_ref[...])
    da = dt * (-jnp.exp(alog_ref[...]))
    row = lax.broadcasted_iota(jnp.int32, (c, c), 0)
    col = lax.broadcasted_iota(jnp.int32, (c, c), 1)
    tri = row >= col
    cum = _mm_exact(tri.astype(F32), da)
    eye = (lax.broadcasted_iota(jnp.int32, (8, 128), 0) == lax.broadcasted_iota(jnp.int32, (8, 128), 1)).astype(F32)
    cum_t = _mm_exact(_mm_nt_exact(eye, da), (row <= col).astype(F32))
    dt_t = _mm_nt_exact(eye, dt)
    ecum = jnp.exp(cum)
    last = cum[c - 1:c, :]
    wgt = jnp.exp(last - cum) * dt
    elast = jnp.exp(last)
    lane = lax.broadcasted_iota(jnp.int32, (c, 128), 1)
    low = lane < 64
    rlow = lax.broadcasted_iota(jnp.int32, (128, 128), 0) < 64
    yield

    for g in range(G_D):
        bm_g = bm[:, g * 128:(g + 1) * 128]
        cm_g = cm[:, g * 128:(g + 1) * 128]
        cb = _mm_nt(cm_g, bm_g)
        ys = []
        for pp in range(2):
            p = 2 * g + pp
            h0, h1 = 2 * p, 2 * p + 1
            x_f = xs[:, p * 128:(p + 1) * 128]
            x_b = x_f.astype(BF16)
            y_in = []
            for hh in (h0, h1):
                seg = cum[:, hh:hh + 1] - cum_t[hh:hh + 1, :]
                dec = jnp.exp(jnp.where(tri, seg, NEG))
                y_in.append(_mm((cb * dec * dt_t[hh:hh + 1, :]).astype(BF16), x_b))
            st = st_ref[p * 128:(p + 1) * 128, :]
            y_x = _mm_nt(cm_g, st.astype(BF16)) * jnp.where(low, ecum[:, h0:h0 + 1], ecum[:, h1:h1 + 1])
            w2 = jnp.where(low, wgt[:, h0:h0 + 1], wgt[:, h1:h1 + 1])
            st_ref[p * 128:(p + 1) * 128, :] = (jnp.where(rlow, elast[:, h0:h0 + 1], elast[:, h1:h1 + 1]) * st
                                                + _mm_tn((x_f * w2).astype(BF16), bm_g))
            y = jnp.where(low, y_in[0], y_in[1]) + y_x + jnp.where(low, dskip_ref[h0], dskip_ref[h1]) * x_f
            ys.append(y * _silu(z[:, p * 128:(p + 1) * 128].astype(F32)))
            yield
        ms = (jnp.sum(ys[0] * ys[0], axis=-1, keepdims=True) + jnp.sum(ys[1] * ys[1], axis=-1, keepdims=True)) / 256.0
        inv = lax.rsqrt(ms + EPS)
        for pp in range(2):
            p = 2 * g + pp
            o[:, p * 128:(p + 1) * 128] = (ys[pp] * inv * ng_ref[:, p * 128:(p + 1) * 128]).astype(BF16)
    yield


def _cd_ffn_kernel(x_ref, wo_ref, g1_ref, ng_ref, sc_ref, sh_ref, g2_ref, wup_ref, cwf_ref, cbf_ref, wdn_ref, fg_ref,
                   q_ref, k_ref, v_ref, bb_ref,
                   z_ref, xs_ref, bm_ref, cm_ref, dt_ref, cws_ref, cbs_ref, dtb_ref, alog_ref, dskip_ref, ngs_ref,
                   out_ref, tail_ref, sn_ref,
                   obuf_ref, act_ref, wb_ref, gt_ref, buf_ref, st_ref,
                   *, tl, d_ff, final, n_tiles, tiles_per_seq, c, tq):
    g = pl.program_id(0)
    gm = jnp.minimum(g, n_tiles - 1)
    gf = jnp.maximum(g - 1, 0)
    jm = gm % tiles_per_seq
    jf = gf % tiles_per_seq
    slot = g % 2
    prev = (g + 1) % 2

    @pl.when(g == 0)
    def _():
        obuf_ref[...] = jnp.zeros(obuf_ref.shape, BF16)

    @pl.when(jm == 0)
    def _():
        buf_ref[0:8, :] = jnp.zeros((8, CONV_DIM_D), F32)
        st_ref[...] = jnp.zeros(st_ref.shape, F32)

    @pl.when(jf == 0)
    def _():
        gt_ref[...] = jnp.zeros(gt_ref.shape, F32)

    ffn = _ffn_steps(x_ref, obuf_ref.at[prev, :, 0:512], obuf_ref.at[prev, :, 512:1024], wo_ref, g1_ref, ng_ref,
                     sc_ref, sh_ref, g2_ref, wup_ref, cwf_ref, cbf_ref, wdn_ref, fg_ref, out_ref, tail_ref, act_ref,
                     wb_ref, gt_ref, tl=tl, d_ff=d_ff, final=final, seg=tl)
    band = _chain(*[_band_steps(q_ref.at[0, ti * tq:(ti + 1) * tq, :], k_ref, v_ref, bb_ref,
                                obuf_ref.at[slot, ti * tq:(ti + 1) * tq, 0:512],
                                jm * (tl // tq) + ti, tq=tq) for ti in range(tl // tq)])
    ssd = _chain(*[_ssd_steps(z_ref.at[0, ci * c:(ci + 1) * c, :], xs_ref.at[0, ci * c:(ci + 1) * c, :],
                              bm_ref.at[0, ci * c:(ci + 1) * c, :], cm_ref.at[0, ci * c:(ci + 1) * c, :],
                              dt_ref.at[0, ci * c:(ci + 1) * c, :], cws_ref, cbs_ref, dtb_ref, alog_ref, dskip_ref,
                              ngs_ref, obuf_ref.at[slot, ci * c:(ci + 1) * c, 512:1024], buf_ref, st_ref, c=c)
                   for ci in range(tl // c)])
    _interleave([ffn, _chain(band, ssd)], (1, 1))

    @pl.when(g < n_tiles)
    def _():
        sn_ref[0] = st_ref[...]


def _cd_mixers_ffn(x, main, dt, bias, wo, g1, ng, sc, sh, g2, wup, cwf, cbf, wdn, fg,
                   conv_w, conv_b, dt_bias, a_log, d_skip, norm_g, tl, final):
    nseq, seq, d = x.shape
    d_ff = wdn.shape[0]
    tps = seq // tl
    n_tiles = nseq * tps
    c = SCAN_CHUNK
    tq = BAND_TQ
    pad = lambda a: jnp.pad(a.reshape(1, H_D), ((0, 0), (0, 128 - H_D)))
    mix = lambda g: jnp.minimum(g, n_tiles - 1)
    ffn = lambda g: jnp.maximum(g - 1, 0)
    tok_f = lambda n: pl.BlockSpec((1, tl, n), lambda g: (ffn(g) // tps, ffn(g) % tps, 0))
    seq_f = lambda n: pl.BlockSpec((1, 1, n), lambda g: (ffn(g) // tps, 0, 0))
    tok_m = lambda w, idx: pl.BlockSpec((1, tl, w), lambda g: (mix(g) // tps, mix(g) % tps, idx))
    full_m = lambda idx: pl.BlockSpec((1, seq, 512), lambda g: (mix(g) // tps, 0, idx),
                                      pipeline_mode=pl.Buffered(1))
    const = lambda r, w: pl.BlockSpec((r, w), lambda g: (0, 0))
    return pl.pallas_call(
        functools.partial(_cd_ffn_kernel, tl=tl, d_ff=d_ff, final=final, n_tiles=n_tiles, tiles_per_seq=tps,
                          c=c, tq=tq),
        grid=(n_tiles + 1,),
        in_specs=[tok_f(d), _resident(wo.shape), seq_f(d), _resident((1, d)), seq_f(d), seq_f(d), seq_f(d),
                  _resident(wup.shape), _resident(cwf.shape), _resident((1, d_ff)), _resident(wdn.shape),
                  _resident((1, d)),
                  tok_m(512, 0), full_m(1), full_m(2), _resident(bias.shape),
                  tok_m(512, 3), tok_m(512, 4), tok_m(256, 10), tok_m(256, 11), tok_m(128, 0),
                  const(CONV_D, CONV_DIM_D), const(1, CONV_DIM_D), const(1, 128), const(1, 128),
                  pl.BlockSpec(memory_space=pltpu.SMEM), const(1, D_INNER)],
        out_specs=[tok_f(d),
                   pl.BlockSpec((1, 8, d_ff), lambda g: (ffn(g) // tps, 0, 0)),
                   pl.BlockSpec((1, H_D * P_D, N_D), lambda g: (mix(g) // tps, 0, 0))],
        out_shape=[jax.ShapeDtypeStruct((nseq, seq, d), F32), jax.ShapeDtypeStruct((nseq, 8, d_ff), F32),
                   jax.ShapeDtypeStruct((nseq, H_D * P_D, N_D), F32)],
        scratch_shapes=[pltpu.VMEM((2, tl, 1024), BF16), pltpu.VMEM((tl, d_ff), BF16),
                        pltpu.VMEM((tl + 8, FF_CHUNK), F32), pltpu.VMEM((1, 8, d_ff), F32),
                        pltpu.VMEM((c + 8, CONV_DIM_D), F32), pltpu.VMEM((H_D * P_D, N_D), F32)],
        compiler_params=_params(("arbitrary",)),
        name="cd_mixers_ffn",
    )(x, wo, g1, ng, sc, sh, g2, wup, cwf, cbf.reshape(1, d_ff), wdn, fg,
      main, main, main, bias,
      main, main, main, main, dt, conv_w, conv_b.reshape(1, CONV_DIM_D), pad(dt_bias), pad(a_log), d_skip,
      norm_g.reshape(1, D_INNER))


def _rope_tables(pos):
    half = DK_A // 2
    inv = jnp.power(ROPE_BASE, -jnp.arange(half, dtype=F32) / half)
    ang = pos.astype(F32)[:, None] * inv[None, :]
    cos = jnp.cos(ang)
    sin = jnp.sin(ang)
    return jnp.concatenate([cos, cos], axis=1), jnp.concatenate([-sin, sin], axis=1)


def _pad_rows(a, rows=8):
    return jnp.pad(a, ((0, 0), (rows - a.shape[1], 0), (0, 0)))


def _trunk(x, mods, pos, weights, caches, biases):
    nseq, seq, d = x.shape
    sample = caches is not None
    c = min(seq, SCAN_CHUNK)
    if sample:
        tl, seg = nseq * seq, seq
        pack = lambda a: a.reshape(1, nseq * seq, a.shape[-1])
        unpack = lambda a: a.reshape(nseq, seq, a.shape[-1])
        rows = lambda m: jnp.repeat(m, seq, axis=1).reshape(1, nseq * seq, d)
    else:
        tl = seg = min(seq, 512)
        pack = unpack = rows = lambda a: a
    tl_in = tl if sample else min(seq, INPROJ_TL)
    seg_in = seg if sample else tl_in
    depth = weights["w_up"].shape[0]
    outs = {k: [] for k in ("ret", "bk", "bv", "ck", "cv", "dconv", "dssm", "ffn")}
    cos, sin = _rope_tables(pos)
    if sample:
        cos, sin = jnp.tile(cos, (nseq, 1)), jnp.tile(sin, (nseq, 1))
    x = pack(x)
    for l in range(depth):
        i = l // 2
        sh1, sc1, g1, sh2, sc2, g2 = [rows(m) for m in mods[l]]
        ng1 = weights["norm_g"][l, 0].reshape(1, d)
        ng2 = weights["norm_g"][l, 1].reshape(1, d)
        fused = None
        if l % 2 == 0:
            qkv, kb, vb = _inproj_ab(x, ng1, sc1, sh1, weights["w_in_ab"][i], cos, sin, tl_in)
            qkv = unpack(qkv)
            s0 = caches["ret"][i] if sample else jnp.zeros((nseq, H_A, DK_A, DV_A), F32)
            o1, s_new = _retention(qkv, s0, weights["ret_gn"][i], c)
            lam_init = 0.8 - 0.6 * math.exp(-0.3 * l)
            if sample:
                o2 = _diff_attn_sample(qkv, caches["bk"][i], caches["bv"][i], biases["t5"],
                                       weights["lam_q"][i], weights["lam_k"][i], weights["diff_gn"][i], lam_init)
            else:
                o2 = _diff_attn(qkv, biases["t5"], weights["lam_q"][i], weights["lam_k"][i],
                                weights["diff_gn"][i], lam_init)
            wo = weights["w_out_ab"][i]
            outs["ret"].append(s_new)
            outs["bk"].append(kb.reshape(nseq, seq, H_B, 2 * DH_B))
            outs["bv"].append(vb.reshape(nseq, seq, H_B, DV_B))
        else:
            keep = seq if sample else min(C_WINDOW, seq)
            main, dt, kc, vc, tail = _inproj_cd(x, ng1, sc1, sh1, weights["w_in_cd"][i], tl_in,
                                                keep * (tl // seg), seg_in)
            main, dt = unpack(main), unpack(dt)
            if sample:
                o1 = _band_attn_sample(main, caches["ck"][i], caches["cv"][i], biases["band"][i])
                conv_past = _pad_rows(caches["dconv"][i])
                s0 = caches["dssm"][i].reshape(nseq, H_D * P_D, N_D)
                o2, ssm_new = _ssd(main, dt, weights["d_conv_w"][i], weights["d_conv_b"][i], weights["d_dt_bias"][i],
                                   weights["d_a_log"][i], weights["d_skip"][i], weights["d_norm_g"][i], conv_past, s0,
                                   c)
            else:
                fused = _cd_mixers_ffn(
                    x, main, dt, biases["band"][i], weights["w_out_cd"][i], g1, ng2, sc2, sh2, g2, weights["w_up"][l],
                    weights["ffn_conv_w"][l], weights["ffn_conv_b"][l], weights["w_down"][l],
                    weights["final_g"].reshape(1, d), weights["d_conv_w"][i], weights["d_conv_b"][i],
                    weights["d_dt_bias"][i], weights["d_a_log"][i], weights["d_skip"][i], weights["d_norm_g"][i],
                    FUSED_TL, final=(l == depth - 1))
                ssm_new = fused[2]
            wo = weights["w_out_cd"][i]
            outs["ck"].append(kc.reshape(nseq, keep, H_C, DH_C))
            outs["cv"].append(vc.reshape(nseq, keep, H_C, DH_C))
            outs["dconv"].append(tail[:, 8 - (CONV_D - 1):])
            outs["dssm"].append(ssm_new.reshape(nseq, H_D, P_D, N_D))
        d_ff = weights["w_down"].shape[1]
        if fused is not None:
            x, ftail = fused[0], fused[1]
        else:
            ffn_past = _pad_rows(caches["ffn"][l]) if sample else jnp.zeros((nseq, 8, d_ff), F32)
            x, ftail = _outproj_ffn(x, pack(o1), pack(o2), wo, g1, ng2, sc2, sh2, g2, weights["w_up"][l],
                                    weights["ffn_conv_w"][l], weights["ffn_conv_b"][l], weights["w_down"][l], ffn_past,
                                    weights["final_g"].reshape(1, d), tl, seg, final=(l == depth - 1))
        outs["ffn"].append(ftail[:, 8 - (CONV_F - 1):])
    stk = lambda t: jnp.stack(t).astype(F32)
    x = unpack(x)
    return (x,) + tuple(stk(outs[k]) for k in ("ret", "bk", "bv", "ck", "cv", "dconv", "dssm", "ffn"))


def kernel(x_prompt, x_sample, cache_ret_state, cache_b_k, cache_b_v, cache_c_k, cache_c_v, state_d_conv, state_d_ssm, state_ffn_conv, c_prompt, c_sample, w_mod, b_mod, norm_g, final_g, t5_table, w_in_ab, w_out_ab, ret_gn, lam_q, lam_k, diff_gn, w_in_cd, w_out_cd, rel_table, d_conv_w, d_conv_b, d_dt_bias, d_a_log, d_skip, d_norm_g, w_up, ffn_conv_w, ffn_conv_b, w_down):
    batch, seq, d = x_prompt.shape
    dec_batch, dec_seq, _ = x_sample.shape
    past = cache_b_k.shape[2]
    depth = w_mod.shape[0]
    assert dec_seq <= CHUNK and past % CHUNK == 0 and cache_c_k.shape[2] == C_WINDOW
    assert seq % BAND_TQ == 0 and seq % DIFF_TQ == 0

    w_in_cd_p = jnp.pad(w_in_cd, ((0, 0), (0, 0), (0, N_CD_PAD - w_in_cd.shape[-1]))).astype(BF16)
    weights = dict(
        norm_g=norm_g, final_g=final_g, w_in_ab=w_in_ab.astype(BF16), w_out_ab=w_out_ab.astype(BF16),
        ret_gn=ret_gn, lam_q=lam_q, lam_k=lam_k, diff_gn=diff_gn, w_in_cd=w_in_cd_p,
        w_out_cd=w_out_cd.astype(BF16), d_conv_w=d_conv_w, d_conv_b=d_conv_b, d_dt_bias=d_dt_bias, d_a_log=d_a_log,
        d_skip=d_skip, d_norm_g=d_norm_g, w_up=w_up.astype(BF16), ffn_conv_w=ffn_conv_w, ffn_conv_b=ffn_conv_b,
        w_down=w_down.astype(BF16))

    mod = _modulation(jnp.concatenate([c_prompt, c_sample], axis=0), w_mod, b_mod)

    def pieces(rows):
        return [[m[:, None, :] for m in jnp.split(mod[l, rows], 6, axis=-1)] for l in range(depth)]

    band = [_band_bias(rel_table[i]) for i in range(rel_table.shape[0])]
    biases_p = dict(t5=_t5_bias_prompt(t5_table, seq), band=band)
    biases_s = dict(t5=_t5_bias_sample(t5_table, past, dec_seq),
                    band=[b[:, :dec_seq, :C_WINDOW + dec_seq] for b in band])
    caches = dict(
        ret=cache_ret_state,
        bk=cache_b_k.reshape(cache_b_k.shape[0], dec_batch, past, H_B * 2 * DH_B),
        bv=cache_b_v.reshape(cache_b_v.shape[0], dec_batch, past, H_B * DV_B),
        ck=cache_c_k.reshape(cache_c_k.shape[0], dec_batch, C_WINDOW, H_C * DH_C),
        cv=cache_c_v.reshape(cache_c_v.shape[0], dec_batch, C_WINDOW, H_C * DH_C),
        dconv=state_d_conv, dssm=state_d_ssm, ffn=state_ffn_conv)

    pos_p = jnp.arange(seq, dtype=jnp.int32)
    pos_s = past + jnp.arange(dec_seq, dtype=jnp.int32)
    y_p, ret_p, bk_p, bv_p, ck_p, cv_p, dconv_p, dssm_p, ffn_p = _trunk(
        x_prompt, pieces(slice(0, batch)), pos_p, weights, None, biases_p)
    y_s, ret_s, bk_s, bv_s, ck_s, cv_s, dconv_s, dssm_s, ffn_s = _trunk(
        x_sample, pieces(slice(batch, batch + dec_batch)), pos_s, weights, caches, biases_s)
    return (y_p, y_s, ret_p, ret_s, bk_p, bk_s, bv_p, bv_s, ck_p, ck_s, cv_p, cv_s,
            dconv_p, dconv_s, dssm_p, dssm_s, ffn_p, ffn_s)
```

```python
import functools
import math

import numpy as np
import jax
import jax.numpy as jnp
from jax import lax
from jax.experimental import pallas as pl
from jax.experimental.pallas import tpu as pltpu

F32 = jnp.float32
BF16 = jnp.bfloat16

CHUNK = 64
EPS = 1e-6
NEG = -1e30
H_A, DK_A, DV_A = 4, 128, 128
ROPE_BASE = 10000.0
H_B, DH_B, DV_B = 4, 64, 128
T5_BUCKETS, T5_MAX_DIST = 32, 128
H_C, DH_C = 8, 64
BAND_CHUNKS = 8
C_WINDOW = BAND_CHUNKS * CHUNK
REL_CLIP = 128
H_D, P_D, G_D, N_D = 8, 64, 2, 128
D_INNER = H_D * P_D
CONV_D = 4
CONV_DIM_D = D_INNER + 2 * G_D * N_D
CONV_F = 3
N_AB_COLS = 7 * 512
N_CD_MAIN = 6 * 512
N_CD_PAD = N_CD_MAIN + 128
FF_CHUNK = 256
BAND_TQ = 256
DIFF_TQ = 256
SCAN_CHUNK = 256
FUSED_TL = 256
INPROJ_TL = 1024
VMEM_LIMIT = 56 * 1024 * 1024
LOG2E = math.log2(math.e)
Q_B_SCALE = DH_B ** -0.5 * LOG2E
Q_C_SCALE = DH_C ** -0.5 * LOG2E


def _mm(a, b):
    return jnp.dot(a, b, preferred_element_type=F32)


def _mm_nt(a, b):
    return lax.dot_general(a, b, (((1,), (1,)), ((), ())), preferred_element_type=F32)


def _mm_tn(a, b):
    return lax.dot_general(a, b, (((0,), (0,)), ((), ())), preferred_element_type=F32)


def _mm_exact(a, b):
    return jnp.dot(a, b, preferred_element_type=F32, precision=lax.Precision.HIGHEST)


def _mm_nt_exact(a, b):
    return lax.dot_general(a, b, (((1,), (1,)), ((), ())), preferred_element_type=F32,
                           precision=lax.Precision.HIGHEST)


def _silu(x):
    return x * (1.0 / (1.0 + jnp.exp(-x)))


def _softplus(x):
    return jnp.maximum(x, 0.0) + jnp.log1p(jnp.exp(-jnp.abs(x)))


def _gelu_tanh(x):
    k1 = -2.0 * math.sqrt(2.0 / math.pi) * math.log2(math.e)
    return x * (1.0 / (1.0 + jnp.exp2(x * (k1 + (k1 * 0.044715) * (x * x)))))


def _norm_mod(x, g, sc, sh):
    ms = jnp.mean(x * x, axis=-1, keepdims=True)
    return (x * lax.rsqrt(ms + EPS)) * g * (1.0 + sc) + sh


def _mod_spec(a, tl):
    if a.shape[1] == 1:
        return pl.BlockSpec((1, 1, a.shape[2]), lambda b, l: (b, 0, 0))
    return pl.BlockSpec((1, tl, a.shape[2]), lambda b, l: (b, l, 0))


def _params(sem):
    return pltpu.CompilerParams(dimension_semantics=sem, vmem_limit_bytes=VMEM_LIMIT)


def _resident(shape):
    nd = len(shape)
    return pl.BlockSpec(shape, lambda *_: (0,) * nd, pipeline_mode=pl.Buffered(1))


def _mod_kernel(c_ref, w_ref, b_ref, o_ref):
    c = c_ref[...]
    o_ref[0] = _mm(_silu(c).astype(BF16), w_ref[0].astype(BF16)) + b_ref[0]


def _modulation(c_all, w_mod, b_mod):
    depth, d, n = w_mod.shape
    r = c_all.shape[0]
    tn = 1536
    return pl.pallas_call(
        _mod_kernel,
        grid=(depth, n // tn),
        in_specs=[pl.BlockSpec((r, d), lambda l, j: (0, 0)),
                  pl.BlockSpec((1, d, tn), lambda l, j: (l, 0, j)),
                  pl.BlockSpec((1, 1, tn), lambda l, j: (l, 0, j))],
        out_specs=pl.BlockSpec((1, r, tn), lambda l, j: (l, 0, j)),
        out_shape=jax.ShapeDtypeStruct((depth, r, n), F32),
        compiler_params=_params(("parallel", "parallel")),
        name="modulation",
    )(c_all, w_mod, b_mod.reshape(depth, 1, n))


def _table_gather_kernel(tab_ref, idx_ref, o_ref, *, n_entries, n_heads):
    idx = idx_ref[...]

    def body(r, accs):
        m = idx == r
        return tuple(jnp.where(m, tab_ref[hh, r], a) for hh, a in enumerate(accs))

    accs = lax.fori_loop(0, n_entries, body, tuple(jnp.zeros(idx.shape, F32) for _ in range(n_heads)))
    for hh in range(n_heads):
        o_ref[hh:hh + 1, :] = accs[hh]


def _table_gather(table, idx):
    t, h = table.shape
    w = idx.shape[0]
    return pl.pallas_call(
        functools.partial(_table_gather_kernel, n_entries=t, n_heads=h),
        in_specs=[pl.BlockSpec(memory_space=pltpu.SMEM),
                  pl.BlockSpec((1, w), lambda: (0, 0))],
        out_specs=pl.BlockSpec((h, w), lambda: (0, 0)),
        out_shape=jax.ShapeDtypeStruct((h, w), F32),
        name="table_gather",
    )(table.T, idx.reshape(1, w))


def _toeplitz_kernel(v_ref, o_ref, *, rows, cols, tile):
    x = pltpu.roll(jnp.broadcast_to(v_ref[0], (rows, v_ref.shape[2])), 0, 1, stride=1, stride_axis=0)
    if tile is None:
        o_ref[0] = x[:, :cols]
    else:
        for jb in range(cols // tile):
            o_ref[0, jb] = x[:, jb * tile:(jb + 1) * tile]


def _toeplitz(vec, rows, cols, tile=None):
    h, wv = vec.shape
    if tile is None:
        out_shape, block, imap = (h, rows, cols), (1, rows, cols), lambda i: (i, 0, 0)
    else:
        out_shape, block, imap = (h, cols // tile, rows, tile), (1, cols // tile, rows, tile), lambda i: (i, 0, 0, 0)
    return pl.pallas_call(
        functools.partial(_toeplitz_kernel, rows=rows, cols=cols, tile=tile),
        grid=(h,),
        in_specs=[pl.BlockSpec((1, 1, wv), lambda i: (i, 0, 0))],
        out_specs=pl.BlockSpec(block, imap),
        out_shape=jax.ShapeDtypeStruct(out_shape, F32),
        compiler_params=_params(("parallel",)),
        name="toeplitz",
    )(vec.reshape(h, 1, wv))


def _t5_bucket(rel):
    half = T5_BUCKETS // 2
    max_exact = half // 2
    base = jnp.where(rel > 0, half, 0)
    n = jnp.abs(rel)
    nf = jnp.maximum(n, 1).astype(F32)
    large = max_exact + (jnp.log(nf / max_exact) / math.log(T5_MAX_DIST / max_exact) * (half - max_exact)).astype(jnp.int32)
    large = jnp.minimum(large, half - 1)
    return base + jnp.where(n < max_exact, n, large)


def _wrapped_offsets(n_pos, n_neg):
    p = jnp.arange(n_pos + n_neg, dtype=jnp.int32)
    return jnp.where(p < n_pos, p, p - (n_pos + n_neg))


def _t5_bias_prompt(t5_table, seq):
    nd = seq // DIFF_TQ
    u = _wrapped_offsets(seq, DIFF_TQ)
    vec = _table_gather(t5_table, _t5_bucket(u - (seq - DIFF_TQ))) * LOG2E
    return _toeplitz(vec, DIFF_TQ, seq, tile=DIFF_TQ)


def _t5_bias_sample(t5_table, past, lq):
    u = _wrapped_offsets(past + lq, lq)
    vec = _table_gather(t5_table, _t5_bucket(u - past)) * LOG2E
    return _toeplitz(vec, lq, past + 2 * lq)[:, :, :past + lq]


def _band_bias(rel_table):
    u = _wrapped_offsets(3 * BAND_TQ, BAND_TQ)
    idx = jnp.clip(2 * BAND_TQ - u, -REL_CLIP, REL_CLIP) + REL_CLIP
    bias = _toeplitz(_table_gather(rel_table, idx) * LOG2E, BAND_TQ, 3 * BAND_TQ)
    dc = (np.arange(BAND_TQ)[:, None] // CHUNK + C_WINDOW // CHUNK) - np.arange(3 * BAND_TQ)[None, :] // CHUNK
    return jnp.where(jnp.asarray((dc >= 0) & (dc <= BAND_CHUNKS))[None], bias, NEG)


def _inproj_ab_kernel(x_ref, g_ref, sc_ref, sh_ref, w_ref, cos_ref, sin_ref, qkv_ref, kb_ref, vb_ref, *, tl):
    h = _norm_mod(x_ref[0], g_ref[...], sc_ref[0], sh_ref[0]).astype(BF16)
    cos = cos_ref[...]
    sin = sin_ref[...]
    for j in range(7):
        y = _mm(h, w_ref[:, j * 512:(j + 1) * 512])
        if j < 2:
            for hh in range(H_A):
                yh = y[:, hh * 128:(hh + 1) * 128]
                yh = yh * cos + pltpu.roll(yh, 64, 1) * sin
                if j == 1:
                    yh = yh * (DK_A ** -0.5)
                qkv_ref[0, :, j * 512 + hh * 128:j * 512 + (hh + 1) * 128] = yh.astype(BF16)
        else:
            if j == 4:
                y = y * Q_B_SCALE
            if j >= 5:
                dst = kb_ref if j == 5 else vb_ref
                for hh in range(H_B):
                    dst[0, pl.ds(hh, tl, stride=H_B), :] = y[:, hh * 128:(hh + 1) * 128]
            qkv_ref[0, :, j * 512:(j + 1) * 512] = y.astype(BF16)


def _inproj_ab(x, g, sc, sh, w, cos, sin, tl):
    nseq, seq, d = x.shape
    grid = (nseq, seq // tl)
    tok = lambda n: pl.BlockSpec((1, tl, n), lambda b, l: (b, l, 0))
    cache = pl.BlockSpec((1, tl * H_B, 128), lambda b, l: (b, l, 0))
    return pl.pallas_call(
        functools.partial(_inproj_ab_kernel, tl=tl),
        grid=grid,
        in_specs=[tok(d), _resident((1, d)), _mod_spec(sc, tl), _mod_spec(sh, tl), _resident((d, N_AB_COLS)),
                  pl.BlockSpec((tl, 128), lambda b, l: (l, 0)),
                  pl.BlockSpec((tl, 128), lambda b, l: (l, 0))],
        out_specs=[tok(N_AB_COLS), cache, cache],
        out_shape=[jax.ShapeDtypeStruct((nseq, seq, N_AB_COLS), BF16),
                   jax.ShapeDtypeStruct((nseq, seq * H_B, 128), F32),
                   jax.ShapeDtypeStruct((nseq, seq * H_B, 128), F32)],
        compiler_params=_params(("parallel", "parallel")),
        name="inproj_ab",
    )(x, g, sc, sh, w, cos, sin)


def _retention_kernel(q_ref, k_ref, v_ref, g_ref, s0_ref, dec_ref, qd_ref, kd_ref, gc_ref, gn_ref,
                      o_ref, sn_ref, *, c, n_chunks):
    states = [s0_ref[0, h] for h in range(H_A)]
    for ci in range(n_chunks):
        rows = slice(ci * c, (ci + 1) * c)
        for h in range(H_A):
            cs = slice(h * 128, (h + 1) * 128)
            q = q_ref[0, rows, cs]
            k = k_ref[0, rows, cs]
            v = v_ref[0, rows, cs]
            state = states[h]
            s = _mm_nt(q, k) * dec_ref[h]
            o = _mm(s.astype(BF16), v) + _mm((q.astype(F32) * qd_ref[h]).astype(BF16), state.astype(BF16))
            states[h] = gc_ref[h] * state + _mm_tn((k.astype(F32) * kd_ref[h]).astype(BF16), v)
            mu = jnp.mean(o, axis=-1, keepdims=True)
            dlt = o - mu
            var = jnp.mean(dlt * dlt, axis=-1, keepdims=True)
            y = dlt * lax.rsqrt(var + EPS) * gn_ref[:, cs] * _silu(g_ref[0, rows, cs].astype(F32))
            o_ref[0, rows, cs] = y.astype(BF16)
    for h in range(H_A):
        sn_ref[0, h] = states[h]


def _retention_consts(c):
    lg = np.log1p(-np.exp2(-5.0 - np.arange(H_A, dtype=np.float32))).astype(np.float32)
    idx = np.arange(c, dtype=np.float32)
    diff = idx[:, None] - idx[None, :]
    decay = np.where(diff[None] >= 0, np.exp(np.maximum(diff, 0.0)[None] * lg[:, None, None]), 0.0)
    qd = np.exp((idx + 1.0)[None, :] * lg[:, None])
    kd = np.exp((c - 1.0 - idx)[None, :] * lg[:, None])
    gc = np.exp(c * lg)
    bc = lambda a: np.ascontiguousarray(np.broadcast_to(a[..., None], a.shape + (128,))).astype(np.float32)
    return decay.astype(np.float32), bc(qd), bc(kd), bc(gc[:, None])


def _retention(qkv, s0, ret_gn, c):
    nseq, seq, _ = qkv.shape
    decay, qd, kd, gc = _retention_consts(c)
    w = H_A * 128
    col = lambda idx: pl.BlockSpec((1, seq, w), lambda b: (b, 0, idx))
    const = lambda shape: pl.BlockSpec(shape, lambda b: (0,) * len(shape))
    state = pl.BlockSpec((1, H_A, DK_A, DV_A), lambda b: (b, 0, 0, 0))
    return pl.pallas_call(
        functools.partial(_retention_kernel, c=c, n_chunks=seq // c),
        grid=(nseq,),
        in_specs=[col(0), col(1), col(2), col(3), state, const((H_A, c, c)), const((H_A, c, 128)),
                  const((H_A, c, 128)), const((H_A, 1, 128)), const((1, w))],
        out_specs=[col(0), state],
        out_shape=[jax.ShapeDtypeStruct((nseq, seq, w), BF16),
                   jax.ShapeDtypeStruct((nseq, H_A, DK_A, DV_A), F32)],
        compiler_params=_params(("parallel",)),
        name="retention",
    )(qkv, qkv, qkv, qkv, s0, decay, qd, kd, gc, ret_gn.reshape(1, w))


def _split_halves(q):
    lane = lax.broadcasted_iota(jnp.int32, q.shape, 1)
    zero = jnp.zeros_like(q)
    return jnp.where(lane < 64, q, zero), jnp.where(lane >= 64, q, zero)


def _lambda(lq_ref, lk_ref, lam_init):
    e = jnp.exp(jnp.sum(lq_ref[...] * lk_ref[...], axis=1, keepdims=True))
    return e[0:1, :] - e[1:2, :] + lam_init


def _diff_epilogue(o, gn, lam_init):
    ms = jnp.mean(o * o, axis=-1, keepdims=True)
    return (o * lax.rsqrt(ms + EPS)) * gn * (1.0 - lam_init)


def _diff_attn_kernel(q_ref, k_ref, v_ref, b_ref, lq_ref, lk_ref, gn_ref, o_ref, s_ref, qs_ref, mx_ref, ls_ref,
                      acc_ref, *, tq, nd, lam_init):
    row = lax.broadcasted_iota(jnp.int32, (tq, tq), 0) // CHUNK
    col = lax.broadcasted_iota(jnp.int32, (tq, tq), 1) // CHUNK
    chunk_mask = col <= row
    lam = _lambda(lq_ref, lk_ref, lam_init)

    def prepare(t):
        halves = _split_halves(q_ref[0, t * tq:(t + 1) * tq, :])
        for i in range(2):
            r = 2 * (t % 2) + i
            qs_ref[r] = halves[i]
            mx_ref[r] = jnp.full((tq, 128), NEG, F32)
            ls_ref[r] = jnp.zeros((tq, 128), F32)
            acc_ref[r] = jnp.zeros((tq, DV_B), F32)

    def far_bias(t, kb):
        if (t - kb - 1) * tq + 1 < T5_MAX_DIST:
            return None
        return b_ref[0, nd - 1 - t + kb, 0:1, 0:1]

    def scores(t, kb):
        k = k_ref[0, kb * tq:(kb + 1) * tq, :]
        far = far_bias(t, kb)
        for i in range(2):
            r = 2 * (t % 2) + i
            s = _mm_nt(qs_ref[r], k)
            if far is None:
                s = s + b_ref[0, nd - 1 - t + kb]
            if kb == t:
                s = jnp.where(chunk_mask, s, NEG)
            s_ref[r * nd + kb] = s
            m = s[:, 0:128]
            for c0 in range(128, tq, 128):
                m = jnp.maximum(m, s[:, c0:c0 + 128])
            mx_ref[r] = jnp.maximum(mx_ref[r], m if far is None else m + far)

    def row_max(t):
        for i in range(2):
            r = 2 * (t % 2) + i
            mx_ref[r] = jnp.broadcast_to(jnp.max(mx_ref[r], axis=1, keepdims=True), (tq, 128))

    def weigh(t, kb):
        v = v_ref[0, kb * tq:(kb + 1) * tq, :]
        far = far_bias(t, kb)
        for i in range(2):
            r = 2 * (t % 2) + i
            m = mx_ref[r] if far is None else mx_ref[r] - far
            l = ls_ref[r]
            ps = []
            for c0 in range(0, tq, 128):
                p = jnp.exp2(s_ref[r * nd + kb, :, c0:c0 + 128] - m)
                l = l + p
                ps.append(p.astype(BF16))
            ls_ref[r] = l
            acc_ref[r] += _mm(jnp.concatenate(ps, axis=1), v)

    def finish(t):
        r = 2 * (t % 2)
        l0 = jnp.sum(ls_ref[r], axis=1, keepdims=True)
        l1 = jnp.sum(ls_ref[r + 1], axis=1, keepdims=True)
        o = acc_ref[r] / l0 - lam * (acc_ref[r + 1] / l1)
        o_ref[0, t * tq:(t + 1) * tq, :] = _diff_epilogue(o, gn_ref[...], lam_init).astype(BF16)

    last = nd - 1
    prepare(last)
    for kb in range(last + 1):
        scores(last, kb)
    row_max(last)
    for t in range(last - 1, -1, -1):
        prepare(t)
        for kb in range(t + 2):
            if kb <= t:
                scores(t, kb)
            weigh(t + 1, kb)
        finish(t + 1)
        row_max(t)
    weigh(0, 0)
    finish(0)


def _diff_attn(qkv, bias, lam_q, lam_k, diff_gn, lam_init):
    nseq, seq, _ = qkv.shape
    tq = DIFF_TQ
    nd = seq // tq
    full = lambda off: pl.BlockSpec((1, seq, 128), lambda b, h: (b, 0, off + h))
    small = lambda r, c: pl.BlockSpec((r, c), lambda b, h: (0, 0))
    return pl.pallas_call(
        functools.partial(_diff_attn_kernel, tq=tq, nd=nd, lam_init=lam_init),
        grid=(nseq, H_B),
        in_specs=[full(16), full(20), full(24),
                  pl.BlockSpec((1, nd, tq, tq), lambda b, h: (h, 0, 0, 0)),
                  small(2, DH_B), small(2, DH_B), small(1, DV_B)],
        out_specs=full(0),
        out_shape=jax.ShapeDtypeStruct((nseq, seq, H_B * DV_B), BF16),
        scratch_shapes=[pltpu.VMEM((4 * nd, tq, tq), F32), pltpu.VMEM((4, tq, 128), BF16),
                        pltpu.VMEM((4, tq, 128), F32), pltpu.VMEM((4, tq, 128), F32),
                        pltpu.VMEM((4, tq, DV_B), F32)],
        compiler_params=_params(("parallel", "parallel")),
        name="diff_attn",
    )(qkv, qkv, qkv, bias, lam_q, lam_k, diff_gn.reshape(1, DV_B))


def _diff_attn_sample_kernel(q_ref, kc_ref, vc_ref, kp_ref, vp_ref, b_ref, lq_ref, lk_ref, gn_ref, o_ref,
                             *, past, lam_init):
    lam = _lambda(lq_ref, lk_ref, lam_init)
    for h in range(H_B):
        cs = slice(h * 128, (h + 1) * 128)
        qs = _split_halves(q_ref[0, :, cs])
        kp = kp_ref[0, :, cs].astype(BF16)
        vp = vp_ref[0, :, cs].astype(BF16)
        kc = kc_ref[0, :, cs]
        vc = vc_ref[0, :, cs]
        b = b_ref[h]
        probs = []
        for i in range(2):
            sp = _mm_nt(qs[i], kp) + b[:, :past]
            sc = _mm_nt(qs[i], kc) + b[:, past:]
            m = jnp.maximum(jnp.max(sp, axis=1, keepdims=True), jnp.max(sc, axis=1, keepdims=True))
            pp = jnp.exp2(sp - m)
            pc = jnp.exp2(sc - m)
            l = jnp.sum(pp, axis=1, keepdims=True) + jnp.sum(pc, axis=1, keepdims=True)
            probs.append((pp / l, pc / l))
        ap = probs[0][0] - lam * probs[1][0]
        ac = probs[0][1] - lam * probs[1][1]
        o = _mm(ap.astype(BF16), vp) + _mm(ac.astype(BF16), vc)
        o_ref[0, :, cs] = _diff_epilogue(o, gn_ref[...], lam_init).astype(BF16)


def _diff_attn_sample(qkv, k_past, v_past, bias, lam_q, lam_k, diff_gn, lam_init):
    nseq, lq, _ = qkv.shape
    past = k_past.shape[1]
    w = H_B * DV_B
    cur = lambda idx: pl.BlockSpec((1, lq, w), lambda b: (b, 0, idx))
    old = pl.BlockSpec((1, past, w), lambda b: (b, 0, 0))
    small = lambda r, c: pl.BlockSpec((r, c), lambda b: (0, 0))
    return pl.pallas_call(
        functools.partial(_diff_attn_sample_kernel, past=past, lam_init=lam_init),
        grid=(nseq,),
        in_specs=[cur(4), cur(5), cur(6), old, old, _resident(bias.shape),
                  small(2, DH_B), small(2, DH_B), small(1, DV_B)],
        out_specs=cur(0),
        out_shape=jax.ShapeDtypeStruct((nseq, lq, w), BF16),
        compiler_params=_params(("parallel",)),
        name="diff_attn_sample",
    )(qkv, qkv, qkv, k_past, v_past, bias, lam_q, lam_k, diff_gn.reshape(1, DV_B))


def _inproj_cd_kernel(x_ref, g_ref, sc_ref, sh_ref, w_ref, main_ref, dt_ref, kc_ref, vc_ref, tail_ref,
                      *, tl, seg, keep):
    h = _norm_mod(x_ref[0], g_ref[...], sc_ref[0], sh_ref[0]).astype(BF16)
    for j in range(6):
        y = _mm(h, w_ref[:, j * 512:(j + 1) * 512])
        main_ref[0, :, j * 512:(j + 1) * 512] = (y * Q_C_SCALE if j == 0 else y).astype(BF16)
        if j == 1:
            kc_ref[0] = y[tl - keep:tl, :]
        if j == 2:
            vc_ref[0] = y[tl - keep:tl, :]
        if j >= 4:
            for s in range(tl // seg):
                tail_ref[s, :, (j - 4) * 512:(j - 3) * 512] = y[(s + 1) * seg - 8:(s + 1) * seg, :]
    dt_ref[0] = _mm(h, w_ref[:, N_CD_MAIN:N_CD_PAD])


def _inproj_cd(x, g, sc, sh, w, tl, keep, seg):
    nseq, seq, d = x.shape
    nseg = tl // seg
    assert keep <= tl and seq % tl == 0
    tok = lambda n: pl.BlockSpec((1, tl, n), lambda b, l: (b, l, 0))
    kept = pl.BlockSpec((1, keep, 512), lambda b, l: (b, 0, 0))
    return pl.pallas_call(
        functools.partial(_inproj_cd_kernel, tl=tl, seg=seg, keep=keep),
        grid=(nseq, seq // tl),
        in_specs=[tok(d), _resident((1, d)), _mod_spec(sc, tl), _mod_spec(sh, tl), _resident((d, N_CD_PAD))],
        out_specs=[tok(N_CD_MAIN), tok(128), kept, kept,
                   pl.BlockSpec((nseg, 8, CONV_DIM_D), lambda b, l: (b, 0, 0))],
        out_shape=[jax.ShapeDtypeStruct((nseq, seq, N_CD_MAIN), BF16),
                   jax.ShapeDtypeStruct((nseq, seq, 128), F32),
                   jax.ShapeDtypeStruct((nseq, keep, 512), F32),
                   jax.ShapeDtypeStruct((nseq, keep, 512), F32),
                   jax.ShapeDtypeStruct((nseq * nseg, 8, CONV_DIM_D), F32)],
        compiler_params=_params(("parallel", "arbitrary")),
        name="inproj_cd",
    )(x, g, sc, sh, w)


def _band_sample_kernel(q_ref, kc_ref, vc_ref, kp_ref, vp_ref, b_ref, o_ref, *, past):
    for pr in range(H_C // 2):
        cs = slice(pr * 128, (pr + 1) * 128)
        halves = _split_halves(q_ref[0, :, cs])
        kp = kp_ref[0, :, cs].astype(BF16)
        vp = vp_ref[0, :, cs].astype(BF16)
        kc = kc_ref[0, :, cs]
        vc = vc_ref[0, :, cs]
        outs = []
        for hi, qh in enumerate(halves):
            b = b_ref[2 * pr + hi]
            sp = _mm_nt(qh, kp) + b[:, :past]
            sc = _mm_nt(qh, kc) + b[:, past:]
            m = jnp.maximum(jnp.max(sp, axis=1, keepdims=True), jnp.max(sc, axis=1, keepdims=True))
            pp = jnp.exp2(sp - m)
            pc = jnp.exp2(sc - m)
            l = jnp.sum(pp, axis=1, keepdims=True) + jnp.sum(pc, axis=1, keepdims=True)
            outs.append(_mm((pp / l).astype(BF16), vp) + _mm((pc / l).astype(BF16), vc))
        lane = lax.broadcasted_iota(jnp.int32, outs[0].shape, 1)
        o_ref[0, :, cs] = jnp.where(lane < 64, outs[0], outs[1]).astype(BF16)


def _band_attn_sample(main, k_past, v_past, bias):
    nseq, lq, _ = main.shape
    past = k_past.shape[1]
    w = H_C * DH_C
    cur = lambda idx: pl.BlockSpec((1, lq, w), lambda b: (b, 0, idx))
    old = pl.BlockSpec((1, past, w), lambda b: (b, 0, 0))
    return pl.pallas_call(
        functools.partial(_band_sample_kernel, past=past),
        grid=(nseq,),
        in_specs=[cur(0), cur(1), cur(2), old, old, _resident(bias.shape)],
        out_specs=cur(0),
        out_shape=jax.ShapeDtypeStruct((nseq, lq, w), BF16),
        compiler_params=_params(("parallel",)),
        name="band_attn_sample",
    )(main, main, main, k_past, v_past, bias)


def _ssd_kernel(z_ref, xs_ref, bm_ref, cm_ref, dt_ref, cw_ref, cb_ref, dtb_ref, alog_ref, dskip_ref, ng_ref,
                past_ref, s0_ref, o_ref, sn_ref, buf_ref, st_ref, *, c):
    @pl.when(pl.program_id(1) == 0)
    def _():
        buf_ref[0:8, :] = past_ref[0]
        st_ref[...] = s0_ref[0]

    for _ in _ssd_steps(z_ref.at[0], xs_ref.at[0], bm_ref.at[0], cm_ref.at[0], dt_ref.at[0], cw_ref, cb_ref, dtb_ref,
                        alog_ref, dskip_ref, ng_ref, o_ref.at[0], buf_ref, st_ref, c=c):
        pass
    sn_ref[0] = st_ref[...]


def _ssd(main, dt, conv_w, conv_b, dt_bias, a_log, d_skip, norm_g, conv_past, s0, c):
    nseq, seq, _ = main.shape
    pad = lambda a: jnp.pad(a.reshape(1, H_D), ((0, 0), (0, 128 - H_D)))
    blk = lambda w, idx: pl.BlockSpec((1, c, w), lambda b, l: (b, l, idx))
    const = lambda r, w: pl.BlockSpec((r, w), lambda b, l: (0, 0))
    state = pl.BlockSpec((1, H_D * P_D, N_D), lambda b, l: (b, 0, 0))
    return pl.pallas_call(
        functools.partial(_ssd_kernel, c=c),
        grid=(nseq, seq // c),
        in_specs=[blk(512, 3), blk(512, 4), blk(256, 10), blk(256, 11), blk(128, 0),
                  const(CONV_D, CONV_DIM_D), const(1, CONV_DIM_D), const(1, 128), const(1, 128),
                  pl.BlockSpec(memory_space=pltpu.SMEM), const(1, D_INNER),
                  pl.BlockSpec((1, 8, CONV_DIM_D), lambda b, l: (b, 0, 0)), state],
        out_specs=[blk(512, 0), state],
        out_shape=[jax.ShapeDtypeStruct((nseq, seq, D_INNER), BF16),
                   jax.ShapeDtypeStruct((nseq, H_D * P_D, N_D), F32)],
        scratch_shapes=[pltpu.VMEM((c + 8, CONV_DIM_D), F32), pltpu.VMEM((H_D * P_D, N_D), F32)],
        compiler_params=_params(("parallel", "arbitrary")),
        name="ssd",
    )(main, main, main, main, dt, conv_w, conv_b.reshape(1, CONV_DIM_D), pad(dt_bias), pad(a_log), d_skip,
      norm_g.reshape(1, D_INNER), conv_past, s0)


def _ffn_kernel(x_ref, o1_ref, o2_ref, wo_ref, g1_ref, ng_ref, sc_ref, sh_ref, g2_ref, wup_ref, cw_ref, cb_ref,
                wdn_ref, past_ref, fg_ref, out_ref, tail_ref, act_ref, wb_ref, gt_ref, *, tl, d_ff, final, seg):
    @pl.when(pl.program_id(1) == 0)
    def _():
        gt_ref[...] = past_ref[...]

    for _ in _ffn_steps(x_ref, o1_ref.at[0], o2_ref.at[0], wo_ref, g1_ref, ng_ref, sc_ref, sh_ref, g2_ref, wup_ref,
                        cw_ref, cb_ref, wdn_ref, fg_ref, out_ref, tail_ref, act_ref, wb_ref, gt_ref,
                        tl=tl, d_ff=d_ff, final=final, seg=seg):
        pass


def _outproj_ffn(x, o1, o2, wo, g1, ng, sc, sh, g2, wup, cw, cb, wdn, past, fg, tl, seg, final):
    nseq, seq, d = x.shape
    d_ff = wdn.shape[0]
    nseg = tl // seg
    assert nseg == 1 or seq == tl
    tok = lambda n: pl.BlockSpec((1, tl, n), lambda b, l: (b, l, 0))
    tail = pl.BlockSpec((nseg, 8, d_ff), lambda b, l: (b, 0, 0))
    return pl.pallas_call(
        functools.partial(_ffn_kernel, tl=tl, d_ff=d_ff, final=final, seg=seg),
        grid=(nseq, seq // tl),
        in_specs=[tok(d), tok(o1.shape[-1]), tok(o2.shape[-1]), _resident(wo.shape), _mod_spec(g1, tl),
                  _resident((1, d)), _mod_spec(sc, tl), _mod_spec(sh, tl), _mod_spec(g2, tl), _resident(wup.shape),
                  _resident(cw.shape), _resident((1, d_ff)), _resident(wdn.shape), tail, _resident((1, d))],
        out_specs=[tok(d), tail],
        out_shape=[jax.ShapeDtypeStruct((nseq, seq, d), F32), jax.ShapeDtypeStruct((nseq * nseg, 8, d_ff), F32)],
        scratch_shapes=[pltpu.VMEM((tl, d_ff), BF16), pltpu.VMEM((nseg * (seg + 8), FF_CHUNK), F32),
                        pltpu.VMEM((nseg, 8, d_ff), F32)],
        compiler_params=_params(("parallel", "arbitrary")),
        name="outproj_ffn",
    )(x, o1, o2, wo, g1, ng, sc, sh, g2, wup, cw, cb.reshape(1, d_ff), wdn, past, fg)


def _interleave(gens, strides):
    live = list(zip(gens, strides))
    while live:
        for item in list(live):
            for _ in range(item[1]):
                try:
                    next(item[0])
                except StopIteration:
                    live.remove(item)
                    break


def _chain(*gens):
    for g in gens:
        yield from g


def _ffn_steps(x_ref, o1, o2, wo_ref, g1_ref, ng_ref, sc_ref, sh_ref, g2_ref, wup_ref, cw_ref, cb_ref, wdn_ref,
               fg_ref, out_ref, tail_ref, act_ref, wb_ref, gt_ref, *, tl, d_ff, final, seg):
    nseg = tl // seg
    stride = seg + 8
    half = wo_ref.shape[0] // 2
    mix = _mm(o1[...], wo_ref[0:half, :]) + _mm(o2[...], wo_ref[half:2 * half, :])
    x1 = x_ref[0] + g1_ref[0] * mix
    out_ref[0] = x1
    h = _norm_mod(x1, ng_ref[...], sc_ref[0], sh_ref[0]).astype(BF16)
    yield
    for j in range(d_ff // FF_CHUNK):
        c0 = j * FF_CHUNK
        a = _mm(h, wup_ref[:, c0:c0 + FF_CHUNK])
        g = _mm(h, wup_ref[:, d_ff + c0:d_ff + c0 + FF_CHUNK])
        for s in range(nseg):
            wb_ref[s * stride:s * stride + 8, :] = gt_ref[s, :, c0:c0 + FF_CHUNK]
            wb_ref[s * stride + 8:(s + 1) * stride, :] = g[s * seg:(s + 1) * seg, :]
            gt_ref[s, :, c0:c0 + FF_CHUNK] = g[(s + 1) * seg - 8:(s + 1) * seg, :]
        back2 = [wb_ref[pl.ds(s * stride + 6, seg), :] for s in range(nseg)]
        back1 = [wb_ref[pl.ds(s * stride + 7, seg), :] for s in range(nseg)]
        if nseg > 1:
            back2, back1 = [jnp.concatenate(back2, axis=0)], [jnp.concatenate(back1, axis=0)]
        gc = (cb_ref[:, c0:c0 + FF_CHUNK] + back2[0] * cw_ref[0:1, c0:c0 + FF_CHUNK]
              + back1[0] * cw_ref[1:2, c0:c0 + FF_CHUNK] + g * cw_ref[2:3, c0:c0 + FF_CHUNK])
        act_ref[:, c0:c0 + FF_CHUNK] = (a * _gelu_tanh(gc)).astype(BF16)
        yield
    split = (d_ff // FF_CHUNK + 1) // 2 * FF_CHUNK
    dn = _mm(act_ref[:, 0:split], wdn_ref[0:split, :])
    yield
    dn = dn + _mm(act_ref[:, split:d_ff], wdn_ref[split:d_ff, :])
    x2 = out_ref[0] + g2_ref[0] * dn
    if final:
        ms = jnp.mean(x2 * x2, axis=-1, keepdims=True)
        x2 = (x2 * lax.rsqrt(ms + EPS)) * fg_ref[...]
    out_ref[0] = x2
    tail_ref[...] = gt_ref[...]
    yield


def _band_steps(q, k_ref, v_ref, b_ref, o, t, *, tq):
    lane = lax.broadcasted_iota(jnp.int32, (tq, 128), 1)
    starts, offs = [], []
    for d in range(3):
        kt = t - 2 + d
        starts.append(pl.multiple_of(jnp.maximum(kt, 0) * tq, tq))
        offs.append(jnp.where(kt >= 0, 0.0, NEG))

    def scores(head):
        pr, hi = divmod(head, 2)
        cs = slice(pr * 128, (pr + 1) * 128)
        qh = _split_halves(q[:, cs])[hi]
        ss = []
        for d in range(3):
            s = _mm_nt(qh, k_ref[0, pl.ds(starts[d], tq), cs]) + b_ref[head, :, d * tq:(d + 1) * tq]
            ss.append(s + offs[d] if d < 2 else s)
        mm = ss[2][:, 0:128]
        for d in range(3):
            for c0 in range(0, tq, 128):
                mm = jnp.maximum(mm, ss[d][:, c0:c0 + 128])
        return ss, jnp.max(mm, axis=1, keepdims=True)

    def weigh(head, ss, m):
        cs = slice((head // 2) * 128, (head // 2 + 1) * 128)
        ls = jnp.zeros((tq, 128), F32)
        acc = jnp.zeros((tq, 128), F32)
        for d in range(3):
            p = jnp.exp2(ss[d] - m)
            for c0 in range(0, tq, 128):
                ls = ls + p[:, c0:c0 + 128]
            acc = acc + _mm(p.astype(BF16), v_ref[0, pl.ds(starts[d], tq), cs])
        return acc / jnp.sum(ls, axis=1, keepdims=True)

    outs = []
    pending = scores(0)
    yield
    for head in range(H_C):
        nxt = scores(head + 1) if head + 1 < H_C else None
        outs.append(weigh(head, *pending))
        pending = nxt
        if head % 2 == 1:
            cs = slice((head // 2) * 128, (head // 2 + 1) * 128)
            o[:, cs] = jnp.where(lane < 64, outs[head - 1], outs[head]).astype(BF16)
        yield


def _ssd_steps(z, xs_in, bm_in, cm_in, dt_in, cw_ref, cb_ref, dtb_ref, alog_ref, dskip_ref, ng_ref, o,
               buf_ref, st_ref, *, c):
    buf_ref[8:8 + c, 0:D_INNER] = xs_in[...].astype(F32)
    buf_ref[8:8 + c, D_INNER:D_INNER + 256] = bm_in[...].astype(F32)
    buf_ref[8:8 + c, D_INNER + 256:CONV_DIM_D] = cm_in[...].astype(F32)
    conv = cb_ref[...] + buf_ref[pl.ds(5, c), :] * cw_ref[0:1, :]
    for i in range(1, CONV_D):
        conv = conv + buf_ref[pl.ds(5 + i, c), :] * cw_ref[i:i + 1, :]
    buf_ref[0:8, :] = buf_ref[c:c + 8, :]
    act = _silu(conv)
    xs = act[:, 0:D_INNER]
    bm = act[:, D_INNER:D_INNER + 256].astype(BF16)
    cm = act[:, D_INNER + 256:CONV_DIM_D].astype(BF16)
    yield

    dt = _softplus(dt_in[...] + dtb_ref[...])
    da = dt * (-jnp.exp(alog_ref[...]))
    row = lax.broadcasted_iota(jnp.int32, (c, c), 0)
    col = lax.broadcasted_iota(jnp.int32, (c, c), 1)
    tri = row >= col
    cum = _mm_exact(tri.astype(F32), da)
    eye = (lax.broadcasted_iota(jnp.int32, (8, 128), 0) == lax.broadcasted_iota(jnp.int32, (8, 128), 1)).astype(F32)
    cum_t = _mm_exact(_mm_nt_exact(eye, da), (row <= col).astype(F32))
    dt_t = _mm_nt_exact(eye, dt)
    ecum = jnp.exp(cum)
    last = cum[c - 1:c, :]
    wgt = jnp.exp(last - cum) * dt
    elast = jnp.exp(last)
    lane = lax.broadcasted_iota(jnp.int32, (c, 128), 1)
    low = lane < 64
    rlow = lax.broadcasted_iota(jnp.int32, (128, 128), 0) < 64
    yield

    for g in range(G_D):
        bm_g = bm[:, g * 128:(g + 1) * 128]
        cm_g = cm[:, g * 128:(g + 1) * 128]
        cb = _mm_nt(cm_g, bm_g)
        ys = []
        for pp in range(2):
            p = 2 * g + pp
            h0, h1 = 2 * p, 2 * p + 1
            x_f = xs[:, p * 128:(p + 1) * 128]
            x_b = x_f.astype(BF16)
            y_in = []
            for hh in (h0, h1):
                seg = cum[:, hh:hh + 1] - cum_t[hh:hh + 1, :]
                dec = jnp.exp(jnp.where(tri, seg, NEG))
                y_in.append(_mm((cb * dec * dt_t[hh:hh + 1, :]).astype(BF16), x_b))
            st = st_ref[p * 128:(p + 1) * 128, :]
            y_x = _mm_nt(cm_g, st.astype(BF16)) * jnp.where(low, ecum[:, h0:h0 + 1], ecum[:, h1:h1 + 1])
            w2 = jnp.where(low, wgt[:, h0:h0 + 1], wgt[:, h1:h1 + 1])
            st_ref[p * 128:(p + 1) * 128, :] = (jnp.where(rlow, elast[:, h0:h0 + 1], elast[:, h1:h1 + 1]) * st
                                                + _mm_tn((x_f * w2).astype(BF16), bm_g))
            y = jnp.where(low, y_in[0], y_in[1]) + y_x + jnp.where(low, dskip_ref[h0], dskip_ref[h1]) * x_f
            ys.append(y * _silu(z[:, p * 128:(p + 1) * 128].astype(F32)))
            yield
        ms = (jnp.sum(ys[0] * ys[0], axis=-1, keepdims=True) + jnp.sum(ys[1] * ys[1], axis=-1, keepdims=True)) / 256.0
        inv = lax.rsqrt(ms + EPS)
        for pp in range(2):
            p = 2 * g + pp
            o[:, p * 128:(p + 1) * 128] = (ys[pp] * inv * ng_ref[:, p * 128:(p + 1) * 128]).astype(BF16)
    yield


def _cd_ffn_kernel(x_ref, wo_ref, g1_ref, ng_ref, sc_ref, sh_ref, g2_ref, wup_ref, cwf_ref, cbf_ref, wdn_ref, fg_ref,
                   q_ref, k_ref, v_ref, bb_ref,
                   z_ref, xs_ref, bm_ref, cm_ref, dt_ref, cws_ref, cbs_ref, dtb_ref, alog_ref, dskip_ref, ngs_ref,
                   out_ref, tail_ref, sn_ref,
                   obuf_ref, act_ref, wb_ref, gt_ref, buf_ref, st_ref,
                   *, tl, d_ff, final, n_tiles, tiles_per_seq, c, tq):
    g = pl.program_id(0)
    gm = jnp.minimum(g, n_tiles - 1)
    gf = jnp.maximum(g - 1, 0)
    jm = gm % tiles_per_seq
    jf = gf % tiles_per_seq
    slot = g % 2
    prev = (g + 1) % 2

    @pl.when(g == 0)
    def _():
        obuf_ref[...] = jnp.zeros(obuf_ref.shape, BF16)

    @pl.when(jm == 0)
    def _():
        buf_ref[0:8, :] = jnp.zeros((8, CONV_DIM_D), F32)
        st_ref[...] = jnp.zeros(st_ref.shape, F32)

    @pl.when(jf == 0)
    def _():
        gt_ref[...] = jnp.zeros(gt_ref.shape, F32)

    ffn = _ffn_steps(x_ref, obuf_ref.at[prev, :, 0:512], obuf_ref.at[prev, :, 512:1024], wo_ref, g1_ref, ng_ref,
                     sc_ref, sh_ref, g2_ref, wup_ref, cwf_ref, cbf_ref, wdn_ref, fg_ref, out_ref, tail_ref, act_ref,
                     wb_ref, gt_ref, tl=tl, d_ff=d_ff, final=final, seg=tl)
    band = _chain(*[_band_steps(q_ref.at[0, ti * tq:(ti + 1) * tq, :], k_ref, v_ref, bb_ref,
                                obuf_ref.at[slot, ti * tq:(ti + 1) * tq, 0:512],
                                jm * (tl // tq) + ti, tq=tq) for ti in range(tl // tq)])
    ssd = _chain(*[_ssd_steps(z_ref.at[0, ci * c:(ci + 1) * c, :], xs_ref.at[0, ci * c:(ci + 1) * c, :],
                              bm_ref.at[0, ci * c:(ci + 1) * c, :], cm_ref.at[0, ci * c:(ci + 1) * c, :],
                              dt_ref.at[0, ci * c:(ci + 1) * c, :], cws_ref, cbs_ref, dtb_ref, alog_ref, dskip_ref,
                              ngs_ref, obuf_ref.at[slot, ci * c:(ci + 1) * c, 512:1024], buf_ref, st_ref, c=c)
                   for ci in range(tl // c)])
    _interleave([ffn, _chain(band, ssd)], (1, 1))

    @pl.when(g < n_tiles)
    def _():
        sn_ref[0] = st_ref[...]


def _cd_mixers_ffn(x, main, dt, bias, wo, g1, ng, sc, sh, g2, wup, cwf, cbf, wdn, fg,
                   conv_w, conv_b, dt_bias, a_log, d_skip, norm_g, tl, final):
    nseq, seq, d = x.shape
    d_ff = wdn.shape[0]
    tps = seq // tl
    n_tiles = nseq * tps
    c = SCAN_CHUNK
    tq = BAND_TQ
    pad = lambda a: jnp.pad(a.reshape(1, H_D), ((0, 0), (0, 128 - H_D)))
    mix = lambda g: jnp.minimum(g, n_tiles - 1)
    ffn = lambda g: jnp.maximum(g - 1, 0)
    tok_f = lambda n: pl.BlockSpec((1, tl, n), lambda g: (ffn(g) // tps, ffn(g) % tps, 0))
    seq_f = lambda n: pl.BlockSpec((1, 1, n), lambda g: (ffn(g) // tps, 0, 0))
    tok_m = lambda w, idx: pl.BlockSpec((1, tl, w), lambda g: (mix(g) // tps, mix(g) % tps, idx))
    full_m = lambda idx: pl.BlockSpec((1, seq, 512), lambda g: (mix(g) // tps, 0, idx))
    const = lambda r, w: pl.BlockSpec((r, w), lambda g: (0, 0))
    return pl.pallas_call(
        functools.partial(_cd_ffn_kernel, tl=tl, d_ff=d_ff, final=final, n_tiles=n_tiles, tiles_per_seq=tps,
                          c=c, tq=tq),
        grid=(n_tiles + 1,),
        in_specs=[tok_f(d), _resident(wo.shape), seq_f(d), _resident((1, d)), seq_f(d), seq_f(d), seq_f(d),
                  _resident(wup.shape), _resident(cwf.shape), _resident((1, d_ff)), _resident(wdn.shape),
                  _resident((1, d)),
                  tok_m(512, 0), full_m(1), full_m(2), _resident(bias.shape),
                  tok_m(512, 3), tok_m(512, 4), tok_m(256, 10), tok_m(256, 11), tok_m(128, 0),
                  const(CONV_D, CONV_DIM_D), const(1, CONV_DIM_D), const(1, 128), const(1, 128),
                  pl.BlockSpec(memory_space=pltpu.SMEM), const(1, D_INNER)],
        out_specs=[tok_f(d),
                   pl.BlockSpec((1, 8, d_ff), lambda g: (ffn(g) // tps, 0, 0)),
                   pl.BlockSpec((1, H_D * P_D, N_D), lambda g: (mix(g) // tps, 0, 0))],
        out_shape=[jax.ShapeDtypeStruct((nseq, seq, d), F32), jax.ShapeDtypeStruct((nseq, 8, d_ff), F32),
                   jax.ShapeDtypeStruct((nseq, H_D * P_D, N_D), F32)],
        scratch_shapes=[pltpu.VMEM((2, tl, 1024), BF16), pltpu.VMEM((tl, d_ff), BF16),
                        pltpu.VMEM((tl + 8, FF_CHUNK), F32), pltpu.VMEM((1, 8, d_ff), F32),
                        pltpu.VMEM((c + 8, CONV_DIM_D), F32), pltpu.VMEM((H_D * P_D, N_D), F32)],
        compiler_params=_params(("arbitrary",)),
        name="cd_mixers_ffn",
    )(x, wo, g1, ng, sc, sh, g2, wup, cwf, cbf.reshape(1, d_ff), wdn, fg,
      main, main, main, bias,
      main, main, main, main, dt, conv_w, conv_b.reshape(1, CONV_DIM_D), pad(dt_bias), pad(a_log), d_skip,
      norm_g.reshape(1, D_INNER))


def _rope_tables(pos):
    half = DK_A // 2
    inv = jnp.power(ROPE_BASE, -jnp.arange(half, dtype=F32) / half)
    ang = pos.astype(F32)[:, None] * inv[None, :]
    cos = jnp.cos(ang)
    sin = jnp.sin(ang)
    return jnp.concatenate([cos, cos], axis=1), jnp.concatenate([-sin, sin], axis=1)


def _pad_rows(a, rows=8):
    return jnp.pad(a, ((0, 0), (rows - a.shape[1], 0), (0, 0)))


def _trunk(x, mods, pos, weights, caches, biases):
    nseq, seq, d = x.shape
    sample = caches is not None
    c = min(seq, SCAN_CHUNK)
    if sample:
        tl, seg = nseq * seq, seq
        pack = lambda a: a.reshape(1, nseq * seq, a.shape[-1])
        unpack = lambda a: a.reshape(nseq, seq, a.shape[-1])
        rows = lambda m: jnp.repeat(m, seq, axis=1).reshape(1, nseq * seq, d)
    else:
        tl = seg = min(seq, 512)
        pack = unpack = rows = lambda a: a
    tl_in = tl if sample else min(seq, INPROJ_TL)
    seg_in = seg if sample else tl_in
    depth = weights["w_up"].shape[0]
    outs = {k: [] for k in ("ret", "bk", "bv", "ck", "cv", "dconv", "dssm", "ffn")}
    cos, sin = _rope_tables(pos)
    if sample:
        cos, sin = jnp.tile(cos, (nseq, 1)), jnp.tile(sin, (nseq, 1))
    x = pack(x)
    for l in range(depth):
        i = l // 2
        sh1, sc1, g1, sh2, sc2, g2 = [rows(m) for m in mods[l]]
        ng1 = weights["norm_g"][l, 0].reshape(1, d)
        ng2 = weights["norm_g"][l, 1].reshape(1, d)
        fused = None
        if l % 2 == 0:
            qkv, kb, vb = _inproj_ab(x, ng1, sc1, sh1, weights["w_in_ab"][i], cos, sin, tl_in)
            qkv = unpack(qkv)
            s0 = caches["ret"][i] if sample else jnp.zeros((nseq, H_A, DK_A, DV_A), F32)
            o1, s_new = _retention(qkv, s0, weights["ret_gn"][i], c)
            lam_init = 0.8 - 0.6 * math.exp(-0.3 * l)
            if sample:
                o2 = _diff_attn_sample(qkv, caches["bk"][i], caches["bv"][i], biases["t5"],
                                       weights["lam_q"][i], weights["lam_k"][i], weights["diff_gn"][i], lam_init)
            else:
                o2 = _diff_attn(qkv, biases["t5"], weights["lam_q"][i], weights["lam_k"][i],
                                weights["diff_gn"][i], lam_init)
            wo = weights["w_out_ab"][i]
            outs["ret"].append(s_new)
            outs["bk"].append(kb.reshape(nseq, seq, H_B, 2 * DH_B))
            outs["bv"].append(vb.reshape(nseq, seq, H_B, DV_B))
        else:
            keep = seq if sample else min(C_WINDOW, seq)
            main, dt, kc, vc, tail = _inproj_cd(x, ng1, sc1, sh1, weights["w_in_cd"][i], tl_in,
                                                keep * (tl // seg), seg_in)
            main, dt = unpack(main), unpack(dt)
            if sample:
                o1 = _band_attn_sample(main, caches["ck"][i], caches["cv"][i], biases["band"][i])
                conv_past = _pad_rows(caches["dconv"][i])
                s0 = caches["dssm"][i].reshape(nseq, H_D * P_D, N_D)
                o2, ssm_new = _ssd(main, dt, weights["d_conv_w"][i], weights["d_conv_b"][i], weights["d_dt_bias"][i],
                                   weights["d_a_log"][i], weights["d_skip"][i], weights["d_norm_g"][i], conv_past, s0,
                                   c)
            else:
                fused = _cd_mixers_ffn(
                    x, main, dt, biases["band"][i], weights["w_out_cd"][i], g1, ng2, sc2, sh2, g2, weights["w_up"][l],
                    weights["ffn_conv_w"][l], weights["ffn_conv_b"][l], weights["w_down"][l],
                    weights["final_g"].reshape(1, d), weights["d_conv_w"][i], weights["d_conv_b"][i],
                    weights["d_dt_bias"][i], weights["d_a_log"][i], weights["d_skip"][i], weights["d_norm_g"][i],
                    FUSED_TL, final=(l == depth - 1))
                ssm_new = fused[2]
            wo = weights["w_out_cd"][i]
            outs["ck"].append(kc.reshape(nseq, keep, H_C, DH_C))
            outs["cv"].append(vc.reshape(nseq, keep, H_C, DH_C))
            outs["dconv"].append(tail[:, 8 - (CONV_D - 1):])
            outs["dssm"].append(ssm_new.reshape(nseq, H_D, P_D, N_D))
        d_ff = weights["w_down"].shape[1]
        if fused is not None:
            x, ftail = fused[0], fused[1]
        else:
            ffn_past = _pad_rows(caches["ffn"][l]) if sample else jnp.zeros((nseq, 8, d_ff), F32)
            x, ftail = _outproj_ffn(x, pack(o1), pack(o2), wo, g1, ng2, sc2, sh2, g2, weights["w_up"][l],
                                    weights["ffn_conv_w"][l], weights["ffn_conv_b"][l], weights["w_down"][l], ffn_past,
                                    weights["final_g"].reshape(1, d), tl, seg, final=(l == depth - 1))
        outs["ffn"].append(ftail[:, 8 - (CONV_F - 1):])
    stk = lambda t: jnp.stack(t).astype(F32)
    x = unpack(x)
    return (x,) + tuple(stk(outs[k]) for k in ("ret", "bk", "bv", "ck", "cv", "dconv", "dssm", "ffn"))


def kernel(x_prompt, x_sample, cache_ret_state, cache_b_k, cache_b_v, cache_c_k, cache_c_v, state_d_conv, state_d_ssm, state_ffn_conv, c_prompt, c_sample, w_mod, b_mod, norm_g, final_g, t5_table, w_in_ab, w_out_ab, ret_gn, lam_q, lam_k, diff_gn, w_in_cd, w_out_cd, rel_table, d_conv_w, d_conv_b, d_dt_bias, d_a_log, d_skip, d_norm_g, w_up, ffn_conv_w, ffn_conv_b, w_down):
    batch, seq, d = x_prompt.shape
    dec_batch, dec_seq, _ = x_sample.shape
    past = cache_b_k.shape[2]
    depth = w_mod.shape[0]
    assert dec_seq <= CHUNK and past % CHUNK == 0 and cache_c_k.shape[2] == C_WINDOW
    assert seq % BAND_TQ == 0 and seq % DIFF_TQ == 0

    w_in_cd_p = jnp.pad(w_in_cd, ((0, 0), (0, 0), (0, N_CD_PAD - w_in_cd.shape[-1]))).astype(BF16)
    weights = dict(
        norm_g=norm_g, final_g=final_g, w_in_ab=w_in_ab.astype(BF16), w_out_ab=w_out_ab.astype(BF16),
        ret_gn=ret_gn, lam_q=lam_q, lam_k=lam_k, diff_gn=diff_gn, w_in_cd=w_in_cd_p,
        w_out_cd=w_out_cd.astype(BF16), d_conv_w=d_conv_w, d_conv_b=d_conv_b, d_dt_bias=d_dt_bias, d_a_log=d_a_log,
        d_skip=d_skip, d_norm_g=d_norm_g, w_up=w_up.astype(BF16), ffn_conv_w=ffn_conv_w, ffn_conv_b=ffn_conv_b,
        w_down=w_down.astype(BF16))

    mod = _modulation(jnp.concatenate([c_prompt, c_sample], axis=0), w_mod, b_mod)

    def pieces(rows):
        return [[m[:, None, :] for m in jnp.split(mod[l, rows], 6, axis=-1)] for l in range(depth)]

    band = [_band_bias(rel_table[i]) for i in range(rel_table.shape[0])]
    biases_p = dict(t5=_t5_bias_prompt(t5_table, seq), band=band)
    biases_s = dict(t5=_t5_bias_sample(t5_table, past, dec_seq),
                    band=[b[:, :dec_seq, :C_WINDOW + dec_seq] for b in band])
    caches = dict(
        ret=cache_ret_state,
        bk=cache_b_k.reshape(cache_b_k.shape[0], dec_batch, past, H_B * 2 * DH_B),
        bv=cache_b_v.reshape(cache_b_v.shape[0], dec_batch, past, H_B * DV_B),
        ck=cache_c_k.reshape(cache_c_k.shape[0], dec_batch, C_WINDOW, H_C * DH_C),
        cv=cache_c_v.reshape(cache_c_v.shape[0], dec_batch, C_WINDOW, H_C * DH_C),
        dconv=state_d_conv, dssm=state_d_ssm, ffn=state_ffn_conv)

    pos_p = jnp.arange(seq, dtype=jnp.int32)
    pos_s = past + jnp.arange(dec_seq, dtype=jnp.int32)
    y_p, ret_p, bk_p, bv_p, ck_p, cv_p, dconv_p, dssm_p, ffn_p = _trunk(
        x_prompt, pieces(slice(0, batch)), pos_p, weights, None, biases_p)
    y_s, ret_s, bk_s, bv_s, ck_s, cv_s, dconv_s, dssm_s, ffn_s = _trunk(
        x_sample, pieces(slice(batch, batch + dec_batch)), pos_s, weights, caches, biases_s)
    return (y_p, y_s, ret_p, ret_s, bk_p, bk_s, bv_p, bv_s, ck_p, ck_s, cv_p, cv_s,
            dconv_p, dconv_s, dssm_p, dssm_s, ffn_p, ffn_s)
```

```python
import functools
import math

import numpy as np
import jax
import jax.numpy as jnp
from jax import lax
from jax.experimental import pallas as pl
from jax.experimental.pallas import tpu as pltpu

F32 = jnp.float32
BF16 = jnp.bfloat16

CHUNK = 64
EPS = 1e-6
NEG = -1e30
H_A, DK_A, DV_A = 4, 128, 128
ROPE_BASE = 10000.0
H_B, DH_B, DV_B = 4, 64, 128
T5_BUCKETS, T5_MAX_DIST = 32, 128
H_C, DH_C = 8, 64
BAND_CHUNKS = 8
C_WINDOW = BAND_CHUNKS * CHUNK
REL_CLIP = 128
H_D, P_D, G_D, N_D = 8, 64, 2, 128
D_INNER = H_D * P_D
CONV_D = 4
CONV_DIM_D = D_INNER + 2 * G_D * N_D
CONV_F = 3
N_AB_COLS = 7 * 512
N_CD_MAIN = 6 * 512
N_CD_PAD = N_CD_MAIN + 128
FF_CHUNK = 256
BAND_TQ = 256
DIFF_TQ = 256
SCAN_CHUNK = 256
FUSED_TL = 256
INPROJ_TL = 1024
VMEM_LIMIT = 56 * 1024 * 1024
LOG2E = math.log2(math.e)
Q_B_SCALE = DH_B ** -0.5 * LOG2E
Q_C_SCALE = DH_C ** -0.5 * LOG2E


def _mm(a, b):
    return jnp.dot(a, b, preferred_element_type=F32)


def _mm_nt(a, b):
    return lax.dot_general(a, b, (((1,), (1,)), ((), ())), preferred_element_type=F32)


def _mm_tn(a, b):
    return lax.dot_general(a, b, (((0,), (0,)), ((), ())), preferred_element_type=F32)


def _mm_exact(a, b):
    return jnp.dot(a, b, preferred_element_type=F32, precision=lax.Precision.HIGHEST)


def _mm_nt_exact(a, b):
    return lax.dot_general(a, b, (((1,), (1,)), ((), ())), preferred_element_type=F32,
                           precision=lax.Precision.HIGHEST)


def _silu(x):
    return x * (1.0 / (1.0 + jnp.exp(-x)))


def _softplus(x):
    return jnp.maximum(x, 0.0) + jnp.log1p(jnp.exp(-jnp.abs(x)))


def _gelu_tanh(x):
    k1 = -2.0 * math.sqrt(2.0 / math.pi) * math.log2(math.e)
    return x * (1.0 / (1.0 + jnp.exp2(x * (k1 + (k1 * 0.044715) * (x * x)))))


def _norm_mod(x, g, sc, sh):
    ms = jnp.mean(x * x, axis=-1, keepdims=True)
    return (x * lax.rsqrt(ms + EPS)) * g * (1.0 + sc) + sh


def _mod_spec(a, tl):
    if a.shape[1] == 1:
        return pl.BlockSpec((1, 1, a.shape[2]), lambda b, l: (b, 0, 0))
    return pl.BlockSpec((1, tl, a.shape[2]), lambda b, l: (b, l, 0))


def _params(sem):
    return pltpu.CompilerParams(dimension_semantics=sem, vmem_limit_bytes=VMEM_LIMIT)


def _resident(shape):
    nd = len(shape)
    return pl.BlockSpec(shape, lambda *_: (0,) * nd, pipeline_mode=pl.Buffered(1))


def _mod_kernel(c_ref, w_ref, b_ref, o_ref):
    c = c_ref[...]
    o_ref[0] = _mm(_silu(c).astype(BF16), w_ref[0].astype(BF16)) + b_ref[0]


def _modulation(c_all, w_mod, b_mod):
    depth, d, n = w_mod.shape
    r = c_all.shape[0]
    tn = 1536
    return pl.pallas_call(
        _mod_kernel,
        grid=(depth, n // tn),
        in_specs=[pl.BlockSpec((r, d), lambda l, j: (0, 0)),
                  pl.BlockSpec((1, d, tn), lambda l, j: (l, 0, j)),
                  pl.BlockSpec((1, 1, tn), lambda l, j: (l, 0, j))],
        out_specs=pl.BlockSpec((1, r, tn), lambda l, j: (l, 0, j)),
        out_shape=jax.ShapeDtypeStruct((depth, r, n), F32),
        compiler_params=_params(("parallel", "parallel")),
        name="modulation",
    )(c_all, w_mod, b_mod.reshape(depth, 1, n))


def _table_gather_kernel(tab_ref, idx_ref, o_ref, *, n_entries, n_heads):
    idx = idx_ref[...]

    def body(r, accs):
        m = idx == r
        return tuple(jnp.where(m, tab_ref[hh, r], a) for hh, a in enumerate(accs))

    accs = lax.fori_loop(0, n_entries, body, tuple(jnp.zeros(idx.shape, F32) for _ in range(n_heads)))
    for hh in range(n_heads):
        o_ref[hh:hh + 1, :] = accs[hh]


def _table_gather(table, idx):
    t, h = table.shape
    w = idx.shape[0]
    return pl.pallas_call(
        functools.partial(_table_gather_kernel, n_entries=t, n_heads=h),
        in_specs=[pl.BlockSpec(memory_space=pltpu.SMEM),
                  pl.BlockSpec((1, w), lambda: (0, 0))],
        out_specs=pl.BlockSpec((h, w), lambda: (0, 0)),
        out_shape=jax.ShapeDtypeStruct((h, w), F32),
        name="table_gather",
    )(table.T, idx.reshape(1, w))


def _toeplitz_kernel(v_ref, o_ref, *, rows, cols, tile):
    x = pltpu.roll(jnp.broadcast_to(v_ref[0], (rows, v_ref.shape[2])), 0, 1, stride=1, stride_axis=0)
    if tile is None:
        o_ref[0] = x[:, :cols]
    else:
        for jb in range(cols // tile):
            o_ref[0, jb] = x[:, jb * tile:(jb + 1) * tile]


def _toeplitz(vec, rows, cols, tile=None):
    h, wv = vec.shape
    if tile is None:
        out_shape, block, imap = (h, rows, cols), (1, rows, cols), lambda i: (i, 0, 0)
    else:
        out_shape, block, imap = (h, cols // tile, rows, tile), (1, cols // tile, rows, tile), lambda i: (i, 0, 0, 0)
    return pl.pallas_call(
        functools.partial(_toeplitz_kernel, rows=rows, cols=cols, tile=tile),
        grid=(h,),
        in_specs=[pl.BlockSpec((1, 1, wv), lambda i: (i, 0, 0))],
        out_specs=pl.BlockSpec(block, imap),
        out_shape=jax.ShapeDtypeStruct(out_shape, F32),
        compiler_params=_params(("parallel",)),
        name="toeplitz",
    )(vec.reshape(h, 1, wv))


def _t5_bucket(rel):
    half = T5_BUCKETS // 2
    max_exact = half // 2
    base = jnp.where(rel > 0, half, 0)
    n = jnp.abs(rel)
    nf = jnp.maximum(n, 1).astype(F32)
    large = max_exact + (jnp.log(nf / max_exact) / math.log(T5_MAX_DIST / max_exact) * (half - max_exact)).astype(jnp.int32)
    large = jnp.minimum(large, half - 1)
    return base + jnp.where(n < max_exact, n, large)


def _wrapped_offsets(n_pos, n_neg):
    p = jnp.arange(n_pos + n_neg, dtype=jnp.int32)
    return jnp.where(p < n_pos, p, p - (n_pos + n_neg))


def _t5_bias_prompt(t5_table, seq):
    nd = seq // DIFF_TQ
    u = _wrapped_offsets(seq, DIFF_TQ)
    vec = _table_gather(t5_table, _t5_bucket(u - (seq - DIFF_TQ))) * LOG2E
    return _toeplitz(vec, DIFF_TQ, seq, tile=DIFF_TQ)


def _t5_bias_sample(t5_table, past, lq):
    u = _wrapped_offsets(past + lq, lq)
    vec = _table_gather(t5_table, _t5_bucket(u - past)) * LOG2E
    return _toeplitz(vec, lq, past + 2 * lq)[:, :, :past + lq]


def _band_bias(rel_table):
    u = _wrapped_offsets(3 * BAND_TQ, BAND_TQ)
    idx = jnp.clip(2 * BAND_TQ - u, -REL_CLIP, REL_CLIP) + REL_CLIP
    bias = _toeplitz(_table_gather(rel_table, idx) * LOG2E, BAND_TQ, 3 * BAND_TQ)
    dc = (np.arange(BAND_TQ)[:, None] // CHUNK + C_WINDOW // CHUNK) - np.arange(3 * BAND_TQ)[None, :] // CHUNK
    return jnp.where(jnp.asarray((dc >= 0) & (dc <= BAND_CHUNKS))[None], bias, NEG)


def _inproj_ab_kernel(x_ref, g_ref, sc_ref, sh_ref, w_ref, cos_ref, sin_ref, qkv_ref, kb_ref, vb_ref, *, tl):
    h = _norm_mod(x_ref[0], g_ref[...], sc_ref[0], sh_ref[0]).astype(BF16)
    cos = cos_ref[...]
    sin = sin_ref[...]
    for j in range(7):
        y = _mm(h, w_ref[:, j * 512:(j + 1) * 512])
        if j < 2:
            for hh in range(H_A):
                yh = y[:, hh * 128:(hh + 1) * 128]
                yh = yh * cos + pltpu.roll(yh, 64, 1) * sin
                if j == 1:
                    yh = yh * (DK_A ** -0.5)
                qkv_ref[0, :, j * 512 + hh * 128:j * 512 + (hh + 1) * 128] = yh.astype(BF16)
        else:
            if j == 4:
                y = y * Q_B_SCALE
            if j >= 5:
                dst = kb_ref if j == 5 else vb_ref
                for hh in range(H_B):
                    dst[0, pl.ds(hh, tl, stride=H_B), :] = y[:, hh * 128:(hh + 1) * 128]
            qkv_ref[0, :, j * 512:(j + 1) * 512] = y.astype(BF16)


def _inproj_ab(x, g, sc, sh, w, cos, sin, tl):
    nseq, seq, d = x.shape
    grid = (nseq, seq // tl)
    tok = lambda n: pl.BlockSpec((1, tl, n), lambda b, l: (b, l, 0))
    cache = pl.BlockSpec((1, tl * H_B, 128), lambda b, l: (b, l, 0))
    return pl.pallas_call(
        functools.partial(_inproj_ab_kernel, tl=tl),
        grid=grid,
        in_specs=[tok(d), _resident((1, d)), _mod_spec(sc, tl), _mod_spec(sh, tl), _resident((d, N_AB_COLS)),
                  pl.BlockSpec((tl, 128), lambda b, l: (l, 0)),
                  pl.BlockSpec((tl, 128), lambda b, l: (l, 0))],
        out_specs=[tok(N_AB_COLS), cache, cache],
        out_shape=[jax.ShapeDtypeStruct((nseq, seq, N_AB_COLS), BF16),
                   jax.ShapeDtypeStruct((nseq, seq * H_B, 128), F32),
                   jax.ShapeDtypeStruct((nseq, seq * H_B, 128), F32)],
        compiler_params=_params(("parallel", "parallel")),
        name="inproj_ab",
    )(x, g, sc, sh, w, cos, sin)


def _retention_kernel(q_ref, k_ref, v_ref, g_ref, s0_ref, dec_ref, qd_ref, kd_ref, gc_ref, gn_ref,
                      o_ref, sn_ref, *, c, n_chunks):
    states = [s0_ref[0, h] for h in range(H_A)]
    for ci in range(n_chunks):
        rows = slice(ci * c, (ci + 1) * c)
        for h in range(H_A):
            cs = slice(h * 128, (h + 1) * 128)
            q = q_ref[0, rows, cs]
            k = k_ref[0, rows, cs]
            v = v_ref[0, rows, cs]
            state = states[h]
            s = _mm_nt(q, k) * dec_ref[h]
            o = _mm(s.astype(BF16), v) + _mm((q.astype(F32) * qd_ref[h]).astype(BF16), state.astype(BF16))
            states[h] = gc_ref[h] * state + _mm_tn((k.astype(F32) * kd_ref[h]).astype(BF16), v)
            mu = jnp.mean(o, axis=-1, keepdims=True)
            dlt = o - mu
            var = jnp.mean(dlt * dlt, axis=-1, keepdims=True)
            y = dlt * lax.rsqrt(var + EPS) * gn_ref[:, cs] * _silu(g_ref[0, rows, cs].astype(F32))
            o_ref[0, rows, cs] = y.astype(BF16)
    for h in range(H_A):
        sn_ref[0, h] = states[h]


def _retention_consts(c):
    lg = np.log1p(-np.exp2(-5.0 - np.arange(H_A, dtype=np.float32))).astype(np.float32)
    idx = np.arange(c, dtype=np.float32)
    diff = idx[:, None] - idx[None, :]
    decay = np.where(diff[None] >= 0, np.exp(np.maximum(diff, 0.0)[None] * lg[:, None, None]), 0.0)
    qd = np.exp((idx + 1.0)[None, :] * lg[:, None])
    kd = np.exp((c - 1.0 - idx)[None, :] * lg[:, None])
    gc = np.exp(c * lg)
    bc = lambda a: np.ascontiguousarray(np.broadcast_to(a[..., None], a.shape + (128,))).astype(np.float32)
    return decay.astype(np.float32), bc(qd), bc(kd), bc(gc[:, None])


def _retention(qkv, s0, ret_gn, c):
    nseq, seq, _ = qkv.shape
    decay, qd, kd, gc = _retention_consts(c)
    w = H_A * 128
    col = lambda idx: pl.BlockSpec((1, seq, w), lambda b: (b, 0, idx))
    const = lambda shape: pl.BlockSpec(shape, lambda b: (0,) * len(shape))
    state = pl.BlockSpec((1, H_A, DK_A, DV_A), lambda b: (b, 0, 0, 0))
    return pl.pallas_call(
        functools.partial(_retention_kernel, c=c, n_chunks=seq // c),
        grid=(nseq,),
        in_specs=[col(0), col(1), col(2), col(3), state, const((H_A, c, c)), const((H_A, c, 128)),
                  const((H_A, c, 128)), const((H_A, 1, 128)), const((1, w))],
        out_specs=[col(0), state],
        out_shape=[jax.ShapeDtypeStruct((nseq, seq, w), BF16),
                   jax.ShapeDtypeStruct((nseq, H_A, DK_A, DV_A), F32)],
        compiler_params=_params(("parallel",)),
        name="retention",
    )(qkv, qkv, qkv, qkv, s0, decay, qd, kd, gc, ret_gn.reshape(1, w))


def _split_halves(q):
    lane = lax.broadcasted_iota(jnp.int32, q.shape, 1)
    zero = jnp.zeros_like(q)
    return jnp.where(lane < 64, q, zero), jnp.where(lane >= 64, q, zero)


def _lambda(lq_ref, lk_ref, lam_init):
    e = jnp.exp(jnp.sum(lq_ref[...] * lk_ref[...], axis=1, keepdims=True))
    return e[0:1, :] - e[1:2, :] + lam_init


def _diff_epilogue(o, gn, lam_init):
    ms = jnp.mean(o * o, axis=-1, keepdims=True)
    return (o * lax.rsqrt(ms + EPS)) * gn * (1.0 - lam_init)


def _diff_attn_kernel(q_ref, k_ref, v_ref, b_ref, lq_ref, lk_ref, gn_ref, o_ref, s_ref, qs_ref, mx_ref, ls_ref,
                      acc_ref, *, tq, nd, lam_init):
    row = lax.broadcasted_iota(jnp.int32, (tq, tq), 0) // CHUNK
    col = lax.broadcasted_iota(jnp.int32, (tq, tq), 1) // CHUNK
    chunk_mask = col <= row
    lam = _lambda(lq_ref, lk_ref, lam_init)

    def prepare(t):
        halves = _split_halves(q_ref[0, t * tq:(t + 1) * tq, :])
        for i in range(2):
            r = 2 * (t % 2) + i
            qs_ref[r] = halves[i]
            mx_ref[r] = jnp.full((tq, 128), NEG, F32)
            ls_ref[r] = jnp.zeros((tq, 128), F32)
            acc_ref[r] = jnp.zeros((tq, DV_B), F32)

    def far_bias(t, kb):
        if (t - kb - 1) * tq + 1 < T5_MAX_DIST:
            return None
        return b_ref[0, nd - 1 - t + kb, 0:1, 0:1]

    def scores(t, kb):
        k = k_ref[0, kb * tq:(kb + 1) * tq, :]
        far = far_bias(t, kb)
        for i in range(2):
            r = 2 * (t % 2) + i
            s = _mm_nt(qs_ref[r], k)
            if far is None:
                s = s + b_ref[0, nd - 1 - t + kb]
            if kb == t:
                s = jnp.where(chunk_mask, s, NEG)
            s_ref[r * nd + kb] = s
            m = s[:, 0:128]
            for c0 in range(128, tq, 128):
                m = jnp.maximum(m, s[:, c0:c0 + 128])
            mx_ref[r] = jnp.maximum(mx_ref[r], m if far is None else m + far)

    def row_max(t):
        for i in range(2):
            r = 2 * (t % 2) + i
            mx_ref[r] = jnp.broadcast_to(jnp.max(mx_ref[r], axis=1, keepdims=True), (tq, 128))

    def weigh(t, kb):
        v = v_ref[0, kb * tq:(kb + 1) * tq, :]
        far = far_bias(t, kb)
        for i in range(2):
            r = 2 * (t % 2) + i
            m = mx_ref[r] if far is None else mx_ref[r] - far
            l = ls_ref[r]
            ps = []
            for c0 in range(0, tq, 128):
                p = jnp.exp2(s_ref[r * nd + kb, :, c0:c0 + 128] - m)
                l = l + p
                ps.append(p.astype(BF16))
            ls_ref[r] = l
            acc_ref[r] += _mm(jnp.concatenate(ps, axis=1), v)

    def finish(t):
        r = 2 * (t % 2)
        l0 = jnp.sum(ls_ref[r], axis=1, keepdims=True)
        l1 = jnp.sum(ls_ref[r + 1], axis=1, keepdims=True)
        o = acc_ref[r] / l0 - lam * (acc_ref[r + 1] / l1)
        o_ref[0, t * tq:(t + 1) * tq, :] = _diff_epilogue(o, gn_ref[...], lam_init).astype(BF16)

    last = nd - 1
    prepare(last)
    for kb in range(last + 1):
        scores(last, kb)
    row_max(last)
    for t in range(last - 1, -1, -1):
        prepare(t)
        for kb in range(t + 2):
            if kb <= t:
                scores(t, kb)
            weigh(t + 1, kb)
        finish(t + 1)
        row_max(t)
    weigh(0, 0)
    finish(0)


def _diff_attn(qkv, bias, lam_q, lam_k, diff_gn, lam_init):
    nseq, seq, _ = qkv.shape
    tq = DIFF_TQ
    nd = seq // tq
    full = lambda off: pl.BlockSpec((1, seq, 128), lambda b, h: (b, 0, off + h))
    small = lambda r, c: pl.BlockSpec((r, c), lambda b, h: (0, 0))
    return pl.pallas_call(
        functools.partial(_diff_attn_kernel, tq=tq, nd=nd, lam_init=lam_init),
        grid=(nseq, H_B),
        in_specs=[full(16), full(20), full(24),
                  pl.BlockSpec((1, nd, tq, tq), lambda b, h: (h, 0, 0, 0)),
                  small(2, DH_B), small(2, DH_B), small(1, DV_B)],
        out_specs=full(0),
        out_shape=jax.ShapeDtypeStruct((nseq, seq, H_B * DV_B), BF16),
        scratch_shapes=[pltpu.VMEM((4 * nd, tq, tq), F32), pltpu.VMEM((4, tq, 128), BF16),
                        pltpu.VMEM((4, tq, 128), F32), pltpu.VMEM((4, tq, 128), F32),
                        pltpu.VMEM((4, tq, DV_B), F32)],
        compiler_params=_params(("parallel", "parallel")),
        name="diff_attn",
    )(qkv, qkv, qkv, bias, lam_q, lam_k, diff_gn.reshape(1, DV_B))


def _diff_attn_sample_kernel(q_ref, kc_ref, vc_ref, kp_ref, vp_ref, b_ref, lq_ref, lk_ref, gn_ref, o_ref,
                             *, past, lam_init):
    lam = _lambda(lq_ref, lk_ref, lam_init)
    for h in range(H_B):
        cs = slice(h * 128, (h + 1) * 128)
        qs = _split_halves(q_ref[0, :, cs])
        kp = kp_ref[0, :, cs].astype(BF16)
        vp = vp_ref[0, :, cs].astype(BF16)
        kc = kc_ref[0, :, cs]
        vc = vc_ref[0, :, cs]
        b = b_ref[h]
        probs = []
        for i in range(2):
            sp = _mm_nt(qs[i], kp) + b[:, :past]
            sc = _mm_nt(qs[i], kc) + b[:, past:]
            m = jnp.maximum(jnp.max(sp, axis=1, keepdims=True), jnp.max(sc, axis=1, keepdims=True))
            pp = jnp.exp2(sp - m)
            pc = jnp.exp2(sc - m)
            l = jnp.sum(pp, axis=1, keepdims=True) + jnp.sum(pc, axis=1, keepdims=True)
            probs.append((pp / l, pc / l))
        ap = probs[0][0] - lam * probs[1][0]
        ac = probs[0][1] - lam * probs[1][1]
        o = _mm(ap.astype(BF16), vp) + _mm(ac.astype(BF16), vc)
        o_ref[0, :, cs] = _diff_epilogue(o, gn_ref[...], lam_init).astype(BF16)


def _diff_attn_sample(qkv, k_past, v_past, bias, lam_q, lam_k, diff_gn, lam_init):
    nseq, lq, _ = qkv.shape
    past = k_past.shape[1]
    w = H_B * DV_B
    cur = lambda idx: pl.BlockSpec((1, lq, w), lambda b: (b, 0, idx))
    old = pl.BlockSpec((1, past, w), lambda b: (b, 0, 0))
    small = lambda r, c: pl.BlockSpec((r, c), lambda b: (0, 0))
    return pl.pallas_call(
        functools.partial(_diff_attn_sample_kernel, past=past, lam_init=lam_init),
        grid=(nseq,),
        in_specs=[cur(4), cur(5), cur(6), old, old, _resident(bias.shape),
                  small(2, DH_B), small(2, DH_B), small(1, DV_B)],
        out_specs=cur(0),
        out_shape=jax.ShapeDtypeStruct((nseq, lq, w), BF16),
        compiler_params=_params(("parallel",)),
        name="diff_attn_sample",
    )(qkv, qkv, qkv, k_past, v_past, bias, lam_q, lam_k, diff_gn.reshape(1, DV_B))


def _inproj_cd_kernel(x_ref, g_ref, sc_ref, sh_ref, w_ref, main_ref, dt_ref, kc_ref, vc_ref, tail_ref,
                      *, tl, seg, keep):
    h = _norm_mod(x_ref[0], g_ref[...], sc_ref[0], sh_ref[0]).astype(BF16)
    for j in range(6):
        y = _mm(h, w_ref[:, j * 512:(j + 1) * 512])
        main_ref[0, :, j * 512:(j + 1) * 512] = (y * Q_C_SCALE if j == 0 else y).astype(BF16)
        if j in (1, 2):
            dst = kc_ref if j == 1 else vc_ref
            for hh in range(H_C):
                dst[0, pl.ds(hh, keep, stride=H_C), :] = y[tl - keep:tl, hh * DH_C:(hh + 1) * DH_C]
        if j >= 4:
            for s in range(tl // seg):
                tail_ref[s, :, (j - 4) * 512:(j - 3) * 512] = y[(s + 1) * seg - 8:(s + 1) * seg, :]
    dt_ref[0] = _mm(h, w_ref[:, N_CD_MAIN:N_CD_PAD])


def _inproj_cd(x, g, sc, sh, w, tl, keep, seg):
    nseq, seq, d = x.shape
    nseg = tl // seg
    assert keep <= tl and seq % tl == 0
    tok = lambda n: pl.BlockSpec((1, tl, n), lambda b, l: (b, l, 0))
    kept = pl.BlockSpec((1, keep * H_C, DH_C), lambda b, l: (b, 0, 0))
    return pl.pallas_call(
        functools.partial(_inproj_cd_kernel, tl=tl, seg=seg, keep=keep),
        grid=(nseq, seq // tl),
        in_specs=[tok(d), _resident((1, d)), _mod_spec(sc, tl), _mod_spec(sh, tl), _resident((d, N_CD_PAD))],
        out_specs=[tok(N_CD_MAIN), tok(128), kept, kept,
                   pl.BlockSpec((nseg, 8, CONV_DIM_D), lambda b, l: (b, 0, 0))],
        out_shape=[jax.ShapeDtypeStruct((nseq, seq, N_CD_MAIN), BF16),
                   jax.ShapeDtypeStruct((nseq, seq, 128), F32),
                   jax.ShapeDtypeStruct((nseq, keep * H_C, DH_C), F32),
                   jax.ShapeDtypeStruct((nseq, keep * H_C, DH_C), F32),
                   jax.ShapeDtypeStruct((nseq * nseg, 8, CONV_DIM_D), F32)],
        compiler_params=_params(("parallel", "arbitrary")),
        name="inproj_cd",
    )(x, g, sc, sh, w)


def _band_sample_kernel(q_ref, kc_ref, vc_ref, kp_ref, vp_ref, b_ref, o_ref, *, past):
    for pr in range(H_C // 2):
        cs = slice(pr * 128, (pr + 1) * 128)
        halves = _split_halves(q_ref[0, :, cs])
        kp = kp_ref[0, :, cs].astype(BF16)
        vp = vp_ref[0, :, cs].astype(BF16)
        kc = kc_ref[0, :, cs]
        vc = vc_ref[0, :, cs]
        outs = []
        for hi, qh in enumerate(halves):
            b = b_ref[2 * pr + hi]
            sp = _mm_nt(qh, kp) + b[:, :past]
            sc = _mm_nt(qh, kc) + b[:, past:]
            m = jnp.maximum(jnp.max(sp, axis=1, keepdims=True), jnp.max(sc, axis=1, keepdims=True))
            pp = jnp.exp2(sp - m)
            pc = jnp.exp2(sc - m)
            l = jnp.sum(pp, axis=1, keepdims=True) + jnp.sum(pc, axis=1, keepdims=True)
            outs.append(_mm((pp / l).astype(BF16), vp) + _mm((pc / l).astype(BF16), vc))
        lane = lax.broadcasted_iota(jnp.int32, outs[0].shape, 1)
        o_ref[0, :, cs] = jnp.where(lane < 64, outs[0], outs[1]).astype(BF16)


def _band_attn_sample(main, k_past, v_past, bias):
    nseq, lq, _ = main.shape
    past = k_past.shape[1]
    w = H_C * DH_C
    cur = lambda idx: pl.BlockSpec((1, lq, w), lambda b: (b, 0, idx))
    old = pl.BlockSpec((1, past, w), lambda b: (b, 0, 0))
    return pl.pallas_call(
        functools.partial(_band_sample_kernel, past=past),
        grid=(nseq,),
        in_specs=[cur(0), cur(1), cur(2), old, old, _resident(bias.shape)],
        out_specs=cur(0),
        out_shape=jax.ShapeDtypeStruct((nseq, lq, w), BF16),
        compiler_params=_params(("parallel",)),
        name="band_attn_sample",
    )(main, main, main, k_past, v_past, bias)


def _ssd_kernel(z_ref, xs_ref, bm_ref, cm_ref, dt_ref, cw_ref, cb_ref, dtb_ref, alog_ref, dskip_ref, ng_ref,
                past_ref, s0_ref, o_ref, sn_ref, buf_ref, st_ref, *, c):
    @pl.when(pl.program_id(1) == 0)
    def _():
        buf_ref[0:8, :] = past_ref[0]
        st_ref[...] = s0_ref[0]

    for _ in _ssd_steps(z_ref.at[0], xs_ref.at[0], bm_ref.at[0], cm_ref.at[0], dt_ref.at[0], cw_ref, cb_ref, dtb_ref,
                        alog_ref, dskip_ref, ng_ref, o_ref.at[0], buf_ref, st_ref, c=c):
        pass
    sn_ref[0] = st_ref[...]


def _ssd(main, dt, conv_w, conv_b, dt_bias, a_log, d_skip, norm_g, conv_past, s0, c):
    nseq, seq, _ = main.shape
    pad = lambda a: jnp.pad(a.reshape(1, H_D), ((0, 0), (0, 128 - H_D)))
    blk = lambda w, idx: pl.BlockSpec((1, c, w), lambda b, l: (b, l, idx))
    const = lambda r, w: pl.BlockSpec((r, w), lambda b, l: (0, 0))
    state = pl.BlockSpec((1, H_D * P_D, N_D), lambda b, l: (b, 0, 0))
    return pl.pallas_call(
        functools.partial(_ssd_kernel, c=c),
        grid=(nseq, seq // c),
        in_specs=[blk(512, 3), blk(512, 4), blk(256, 10), blk(256, 11), blk(128, 0),
                  const(CONV_D, CONV_DIM_D), const(1, CONV_DIM_D), const(1, 128), const(1, 128),
                  pl.BlockSpec(memory_space=pltpu.SMEM), const(1, D_INNER),
                  pl.BlockSpec((1, 8, CONV_DIM_D), lambda b, l: (b, 0, 0)), state],
        out_specs=[blk(512, 0), state],
        out_shape=[jax.ShapeDtypeStruct((nseq, seq, D_INNER), BF16),
                   jax.ShapeDtypeStruct((nseq, H_D * P_D, N_D), F32)],
        scratch_shapes=[pltpu.VMEM((c + 8, CONV_DIM_D), F32), pltpu.VMEM((H_D * P_D, N_D), F32)],
        compiler_params=_params(("parallel", "arbitrary")),
        name="ssd",
    )(main, main, main, main, dt, conv_w, conv_b.reshape(1, CONV_DIM_D), pad(dt_bias), pad(a_log), d_skip,
      norm_g.reshape(1, D_INNER), conv_past, s0)


def _ffn_kernel(x_ref, o1_ref, o2_ref, wo_ref, g1_ref, ng_ref, sc_ref, sh_ref, g2_ref, wup_ref, cw_ref, cb_ref,
                wdn_ref, past_ref, fg_ref, out_ref, tail_ref, act_ref, wb_ref, gt_ref, *, tl, d_ff, final, seg):
    @pl.when(pl.program_id(1) == 0)
    def _():
        gt_ref[...] = past_ref[...]

    for _ in _ffn_steps(x_ref, o1_ref.at[0], o2_ref.at[0], wo_ref, g1_ref, ng_ref, sc_ref, sh_ref, g2_ref, wup_ref,
                        cw_ref, cb_ref, wdn_ref, fg_ref, out_ref, tail_ref, act_ref, wb_ref, gt_ref,
                        tl=tl, d_ff=d_ff, final=final, seg=seg):
        pass


def _outproj_ffn(x, o1, o2, wo, g1, ng, sc, sh, g2, wup, cw, cb, wdn, past, fg, tl, seg, final):
    nseq, seq, d = x.shape
    d_ff = wdn.shape[0]
    nseg = tl // seg
    assert nseg == 1 or seq == tl
    tok = lambda n: pl.BlockSpec((1, tl, n), lambda b, l: (b, l, 0))
    tail = pl.BlockSpec((nseg, 8, d_ff), lambda b, l: (b, 0, 0))
    return pl.pallas_call(
        functools.partial(_ffn_kernel, tl=tl, d_ff=d_ff, final=final, seg=seg),
        grid=(nseq, seq // tl),
        in_specs=[tok(d), tok(o1.shape[-1]), tok(o2.shape[-1]), _resident(wo.shape), _mod_spec(g1, tl),
                  _resident((1, d)), _mod_spec(sc, tl), _mod_spec(sh, tl), _mod_spec(g2, tl), _resident(wup.shape),
                  _resident(cw.shape), _resident((1, d_ff)), _resident(wdn.shape), tail, _resident((1, d))],
        out_specs=[tok(d), tail],
        out_shape=[jax.ShapeDtypeStruct((nseq, seq, d), F32), jax.ShapeDtypeStruct((nseq * nseg, 8, d_ff), F32)],
        scratch_shapes=[pltpu.VMEM((tl, d_ff), BF16), pltpu.VMEM((nseg * (seg + 8), FF_CHUNK), F32),
                        pltpu.VMEM((nseg, 8, d_ff), F32)],
        compiler_params=_params(("parallel", "arbitrary")),
        name="outproj_ffn",
    )(x, o1, o2, wo, g1, ng, sc, sh, g2, wup, cw, cb.reshape(1, d_ff), wdn, past, fg)


def _interleave(gens, strides):
    live = list(zip(gens, strides))
    while live:
        for item in list(live):
            for _ in range(item[1]):
                try:
                    next(item[0])
                except StopIteration:
                    live.remove(item)
                    break


def _chain(*gens):
    for g in gens:
        yield from g


def _ffn_steps(x_ref, o1, o2, wo_ref, g1_ref, ng_ref, sc_ref, sh_ref, g2_ref, wup_ref, cw_ref, cb_ref, wdn_ref,
               fg_ref, out_ref, tail_ref, act_ref, wb_ref, gt_ref, *, tl, d_ff, final, seg):
    nseg = tl // seg
    stride = seg + 8
    half = wo_ref.shape[0] // 2
    mix = _mm(o1[...], wo_ref[0:half, :]) + _mm(o2[...], wo_ref[half:2 * half, :])
    x1 = x_ref[0] + g1_ref[0] * mix
    out_ref[0] = x1
    h = _norm_mod(x1, ng_ref[...], sc_ref[0], sh_ref[0]).astype(BF16)
    yield
    for j in range(d_ff // FF_CHUNK):
        c0 = j * FF_CHUNK
        a = _mm(h, wup_ref[:, c0:c0 + FF_CHUNK])
        g = _mm(h, wup_ref[:, d_ff + c0:d_ff + c0 + FF_CHUNK])
        for s in range(nseg):
            wb_ref[s * stride:s * stride + 8, :] = gt_ref[s, :, c0:c0 + FF_CHUNK]
            wb_ref[s * stride + 8:(s + 1) * stride, :] = g[s * seg:(s + 1) * seg, :]
            gt_ref[s, :, c0:c0 + FF_CHUNK] = g[(s + 1) * seg - 8:(s + 1) * seg, :]
        back2 = [wb_ref[pl.ds(s * stride + 6, seg), :] for s in range(nseg)]
        back1 = [wb_ref[pl.ds(s * stride + 7, seg), :] for s in range(nseg)]
        if nseg > 1:
            back2, back1 = [jnp.concatenate(back2, axis=0)], [jnp.concatenate(back1, axis=0)]
        gc = (cb_ref[:, c0:c0 + FF_CHUNK] + back2[0] * cw_ref[0:1, c0:c0 + FF_CHUNK]
              + back1[0] * cw_ref[1:2, c0:c0 + FF_CHUNK] + g * cw_ref[2:3, c0:c0 + FF_CHUNK])
        act_ref[:, c0:c0 + FF_CHUNK] = (a * _gelu_tanh(gc)).astype(BF16)
        yield
    split = (d_ff // FF_CHUNK + 1) // 2 * FF_CHUNK
    dn = _mm(act_ref[:, 0:split], wdn_ref[0:split, :])
    yield
    dn = dn + _mm(act_ref[:, split:d_ff], wdn_ref[split:d_ff, :])
    x2 = out_ref[0] + g2_ref[0] * dn
    if final:
        ms = jnp.mean(x2 * x2, axis=-1, keepdims=True)
        x2 = (x2 * lax.rsqrt(ms + EPS)) * fg_ref[...]
    out_ref[0] = x2
    tail_ref[...] = gt_ref[...]
    yield


def _band_steps(q, k_ref, v_ref, b_ref, o, t, *, tq):
    lane = lax.broadcasted_iota(jnp.int32, (tq, 128), 1)
    starts, offs = [], []
    for d in range(3):
        kt = t - 2 + d
        starts.append(pl.multiple_of(jnp.maximum(kt, 0) * tq, tq))
        offs.append(jnp.where(kt >= 0, 0.0, NEG))

    def scores(head):
        pr, hi = divmod(head, 2)
        cs = slice(pr * 128, (pr + 1) * 128)
        qh = _split_halves(q[:, cs])[hi]
        ss = []
        for d in range(3):
            s = _mm_nt(qh, k_ref[0, pl.ds(starts[d], tq), cs]) + b_ref[head, :, d * tq:(d + 1) * tq]
            ss.append(s + offs[d] if d < 2 else s)
        mm = ss[2][:, 0:128]
        for d in range(3):
            for c0 in range(0, tq, 128):
                mm = jnp.maximum(mm, ss[d][:, c0:c0 + 128])
        return ss, jnp.max(mm, axis=1, keepdims=True)

    def weigh(head, ss, m):
        cs = slice((head // 2) * 128, (head // 2 + 1) * 128)
        ls = jnp.zeros((tq, 128), F32)
        acc = jnp.zeros((tq, 128), F32)
        for d in range(3):
            p = jnp.exp2(ss[d] - m)
            for c0 in range(0, tq, 128):
                ls = ls + p[:, c0:c0 + 128]
            acc = acc + _mm(p.astype(BF16), v_ref[0, pl.ds(starts[d], tq), cs])
        return acc / jnp.sum(ls, axis=1, keepdims=True)

    outs = []
    pending = scores(0)
    yield
    for head in range(H_C):
        nxt = scores(head + 1) if head + 1 < H_C else None
        outs.append(weigh(head, *pending))
        pending = nxt
        if head % 2 == 1:
            cs = slice((head // 2) * 128, (head // 2 + 1) * 128)
            o[:, cs] = jnp.where(lane < 64, outs[head - 1], outs[head]).astype(BF16)
        yield


def _ssd_steps(z, xs_in, bm_in, cm_in, dt_in, cw_ref, cb_ref, dtb_ref, alog_ref, dskip_ref, ng_ref, o,
               buf_ref, st_ref, *, c):
    buf_ref[8:8 + c, 0:D_INNER] = xs_in[...].astype(F32)
    buf_ref[8:8 + c, D_INNER:D_INNER + 256] = bm_in[...].astype(F32)
    buf_ref[8:8 + c, D_INNER + 256:CONV_DIM_D] = cm_in[...].astype(F32)
    conv = cb_ref[...] + buf_ref[pl.ds(5, c), :] * cw_ref[0:1, :]
    for i in range(1, CONV_D):
        conv = conv + buf_ref[pl.ds(5 + i, c), :] * cw_ref[i:i + 1, :]
    buf_ref[0:8, :] = buf_ref[c:c + 8, :]
    act = _silu(conv)
    xs = act[:, 0:D_INNER]
    bm = act[:, D_INNER:D_INNER + 256].astype(BF16)
    cm = act[:, D_INNER + 256:CONV_DIM_D].astype(BF16)
    yield

    dt = _softplus(dt_in[...] + dtb_ref[...])
    da = dt * (-jnp.exp(alog_ref[...]))
    row = lax.broadcasted_iota(jnp.int32, (c, c), 0)
    col = lax.broadcasted_iota(jnp.int32, (c, c), 1)
    tri = row >= col
    cum = _mm_exact(tri.astype(F32), da)
    eye = (lax.broadcasted_iota(jnp.int32, (8, 128), 0) == lax.broadcasted_iota(jnp.int32, (8, 128), 1)).astype(F32)
    cum_t = _mm_exact(_mm_nt_exact(eye, da), (row <= col).astype(F32))
    dt_t = _mm_nt_exact(eye, dt)
    ecum = jnp.exp(cum)
    last = cum[c - 1:c, :]
    wgt = jnp.exp(last - cum) * dt
    elast = jnp.exp(last)
    lane = lax.broadcasted_iota(jnp.int32, (c, 128), 1)
    low = lane < 64
    rlow = lax.broadcasted_iota(jnp.int32, (128, 128), 0) < 64
    yield

    for g in range(G_D):
        bm_g = bm[:, g * 128:(g + 1) * 128]
        cm_g = cm[:, g * 128:(g + 1) * 128]
        cb = _mm_nt(cm_g, bm_g)
        ys = []
        for pp in range(2):
            p = 2 * g + pp
            h0, h1 = 2 * p, 2 * p + 1
            x_f = xs[:, p * 128:(p + 1) * 128]
            x_b = x_f.astype(BF16)
            y_in = []
            for hh in (h0, h1):
                seg = cum[:, hh:hh + 1] - cum_t[hh:hh + 1, :]
                dec = jnp.exp(jnp.where(tri, seg, NEG))
                y_in.append(_mm((cb * dec * dt_t[hh:hh + 1, :]).astype(BF16), x_b))
            st = st_ref[p * 128:(p + 1) * 128, :]
            y_x = _mm_nt(cm_g, st.astype(BF16)) * jnp.where(low, ecum[:, h0:h0 + 1], ecum[:, h1:h1 + 1])
            w2 = jnp.where(low, wgt[:, h0:h0 + 1], wgt[:, h1:h1 + 1])
            st_ref[p * 128:(p + 1) * 128, :] = (jnp.where(rlow, elast[:, h0:h0 + 1], elast[:, h1:h1 + 1]) * st
                                                + _mm_tn((x_f * w2).astype(BF16), bm_g))
            y = jnp.where(low, y_in[0], y_in[1]) + y_x + jnp.where(low, dskip_ref[h0], dskip_ref[h1]) * x_f
            ys.append(y * _silu(z[:, p * 128:(p + 1) * 128].astype(F32)))
            yield
        ms = (jnp.sum(ys[0] * ys[0], axis=-1, keepdims=True) + jnp.sum(ys[1] * ys[1], axis=-1, keepdims=True)) / 256.0
        inv = lax.rsqrt(ms + EPS)
        for pp in range(2):
            p = 2 * g + pp
            o[:, p * 128:(p + 1) * 128] = (ys[pp] * inv * ng_ref[:, p * 128:(p + 1) * 128]).astype(BF16)
    yield


def _cd_ffn_kernel(x_ref, wo_ref, g1_ref, ng_ref, sc_ref, sh_ref, g2_ref, wup_ref, cwf_ref, cbf_ref, wdn_ref, fg_ref,
                   q_ref, k_ref, v_ref, bb_ref,
                   z_ref, xs_ref, bm_ref, cm_ref, dt_ref, cws_ref, cbs_ref, dtb_ref, alog_ref, dskip_ref, ngs_ref,
                   out_ref, tail_ref, sn_ref,
                   obuf_ref, act_ref, wb_ref, gt_ref, buf_ref, st_ref,
                   *, tl, d_ff, final, n_tiles, tiles_per_seq, c, tq):
    g = pl.program_id(0)
    gm = jnp.minimum(g, n_tiles - 1)
    gf = jnp.maximum(g - 1, 0)
    jm = gm % tiles_per_seq
    jf = gf % tiles_per_seq
    slot = g % 2
    prev = (g + 1) % 2

    @pl.when(g == 0)
    def _():
        obuf_ref[...] = jnp.zeros(obuf_ref.shape, BF16)

    @pl.when(jm == 0)
    def _():
        buf_ref[0:8, :] = jnp.zeros((8, CONV_DIM_D), F32)
        st_ref[...] = jnp.zeros(st_ref.shape, F32)

    @pl.when(jf == 0)
    def _():
        gt_ref[...] = jnp.zeros(gt_ref.shape, F32)

    ffn = _ffn_steps(x_ref, obuf_ref.at[prev, :, 0:512], obuf_ref.at[prev, :, 512:1024], wo_ref, g1_ref, ng_ref,
                     sc_ref, sh_ref, g2_ref, wup_ref, cwf_ref, cbf_ref, wdn_ref, fg_ref, out_ref, tail_ref, act_ref,
                     wb_ref, gt_ref, tl=tl, d_ff=d_ff, final=final, seg=tl)
    band = _chain(*[_band_steps(q_ref.at[0, ti * tq:(ti + 1) * tq, :], k_ref, v_ref, bb_ref,
                                obuf_ref.at[slot, ti * tq:(ti + 1) * tq, 0:512],
                                jm * (tl // tq) + ti, tq=tq) for ti in range(tl // tq)])
    ssd = _chain(*[_ssd_steps(z_ref.at[0, ci * c:(ci + 1) * c, :], xs_ref.at[0, ci * c:(ci + 1) * c, :],
                              bm_ref.at[0, ci * c:(ci + 1) * c, :], cm_ref.at[0, ci * c:(ci + 1) * c, :],
                              dt_ref.at[0, ci * c:(ci + 1) * c, :], cws_ref, cbs_ref, dtb_ref, alog_ref, dskip_ref,
                              ngs_ref, obuf_ref.at[slot, ci * c:(ci + 1) * c, 512:1024], buf_ref, st_ref, c=c)
                   for ci in range(tl // c)])
    _interleave([ffn, _chain(band, ssd)], (1, 1))

    @pl.when(g < n_tiles)
    def _():
        sn_ref[0] = st_ref[...]


def _cd_mixers_ffn(x, main, dt, bias, wo, g1, ng, sc, sh, g2, wup, cwf, cbf, wdn, fg,
                   conv_w, conv_b, dt_bias, a_log, d_skip, norm_g, tl, final):
    nseq, seq, d = x.shape
    d_ff = wdn.shape[0]
    tps = seq // tl
    n_tiles = nseq * tps
    c = SCAN_CHUNK
    tq = BAND_TQ
    pad = lambda a: jnp.pad(a.reshape(1, H_D), ((0, 0), (0, 128 - H_D)))
    mix = lambda g: jnp.minimum(g, n_tiles - 1)
    ffn = lambda g: jnp.maximum(g - 1, 0)
    tok_f = lambda n: pl.BlockSpec((1, tl, n), lambda g: (ffn(g) // tps, ffn(g) % tps, 0))
    seq_f = lambda n: pl.BlockSpec((1, 1, n), lambda g: (ffn(g) // tps, 0, 0))
    tok_m = lambda w, idx: pl.BlockSpec((1, tl, w), lambda g: (mix(g) // tps, mix(g) % tps, idx))
    full_m = lambda idx: pl.BlockSpec((1, seq, 512), lambda g: (mix(g) // tps, 0, idx))
    const = lambda r, w: pl.BlockSpec((r, w), lambda g: (0, 0))
    return pl.pallas_call(
        functools.partial(_cd_ffn_kernel, tl=tl, d_ff=d_ff, final=final, n_tiles=n_tiles, tiles_per_seq=tps,
                          c=c, tq=tq),
        grid=(n_tiles + 1,),
        in_specs=[tok_f(d), _resident(wo.shape), seq_f(d), _resident((1, d)), seq_f(d), seq_f(d), seq_f(d),
                  _resident(wup.shape), _resident(cwf.shape), _resident((1, d_ff)), _resident(wdn.shape),
                  _resident((1, d)),
                  tok_m(512, 0), full_m(1), full_m(2), _resident(bias.shape),
                  tok_m(512, 3), tok_m(512, 4), tok_m(256, 10), tok_m(256, 11), tok_m(128, 0),
                  const(CONV_D, CONV_DIM_D), const(1, CONV_DIM_D), const(1, 128), const(1, 128),
                  pl.BlockSpec(memory_space=pltpu.SMEM), const(1, D_INNER)],
        out_specs=[tok_f(d),
                   pl.BlockSpec((1, 8, d_ff), lambda g: (ffn(g) // tps, 0, 0)),
                   pl.BlockSpec((1, H_D * P_D, N_D), lambda g: (mix(g) // tps, 0, 0))],
        out_shape=[jax.ShapeDtypeStruct((nseq, seq, d), F32), jax.ShapeDtypeStruct((nseq, 8, d_ff), F32),
                   jax.ShapeDtypeStruct((nseq, H_D * P_D, N_D), F32)],
        scratch_shapes=[pltpu.VMEM((2, tl, 1024), BF16), pltpu.VMEM((tl, d_ff), BF16),
                        pltpu.VMEM((tl + 8, FF_CHUNK), F32), pltpu.VMEM((1, 8, d_ff), F32),
                        pltpu.VMEM((c + 8, CONV_DIM_D), F32), pltpu.VMEM((H_D * P_D, N_D), F32)],
        compiler_params=_params(("arbitrary",)),
        name="cd_mixers_ffn",
    )(x, wo, g1, ng, sc, sh, g2, wup, cwf, cbf.reshape(1, d_ff), wdn, fg,
      main, main, main, bias,
      main, main, main, main, dt, conv_w, conv_b.reshape(1, CONV_DIM_D), pad(dt_bias), pad(a_log), d_skip,
      norm_g.reshape(1, D_INNER))


def _rope_tables(pos):
    half = DK_A // 2
    inv = jnp.power(ROPE_BASE, -jnp.arange(half, dtype=F32) / half)
    ang = pos.astype(F32)[:, None] * inv[None, :]
    cos = jnp.cos(ang)
    sin = jnp.sin(ang)
    return jnp.concatenate([cos, cos], axis=1), jnp.concatenate([-sin, sin], axis=1)


def _pad_rows(a, rows=8):
    return jnp.pad(a, ((0, 0), (rows - a.shape[1], 0), (0, 0)))


def _trunk(x, mods, pos, weights, caches, biases):
    nseq, seq, d = x.shape
    sample = caches is not None
    c = min(seq, SCAN_CHUNK)
    if sample:
        tl, seg = nseq * seq, seq
        pack = lambda a: a.reshape(1, nseq * seq, a.shape[-1])
        unpack = lambda a: a.reshape(nseq, seq, a.shape[-1])
        rows = lambda m: jnp.repeat(m, seq, axis=1).reshape(1, nseq * seq, d)
    else:
        tl = seg = min(seq, 512)
        pack = unpack = rows = lambda a: a
    tl_in = tl if sample else min(seq, INPROJ_TL)
    seg_in = seg if sample else tl_in
    depth = weights["w_up"].shape[0]
    outs = {k: [] for k in ("ret", "bk", "bv", "ck", "cv", "dconv", "dssm", "ffn")}
    cos, sin = _rope_tables(pos)
    if sample:
        cos, sin = jnp.tile(cos, (nseq, 1)), jnp.tile(sin, (nseq, 1))
    x = pack(x)
    for l in range(depth):
        i = l // 2
        sh1, sc1, g1, sh2, sc2, g2 = [rows(m) for m in mods[l]]
        ng1 = weights["norm_g"][l, 0].reshape(1, d)
        ng2 = weights["norm_g"][l, 1].reshape(1, d)
        fused = None
        if l % 2 == 0:
            qkv, kb, vb = _inproj_ab(x, ng1, sc1, sh1, weights["w_in_ab"][i], cos, sin, tl_in)
            qkv = unpack(qkv)
            s0 = caches["ret"][i] if sample else jnp.zeros((nseq, H_A, DK_A, DV_A), F32)
            o1, s_new = _retention(qkv, s0, weights["ret_gn"][i], c)
            lam_init = 0.8 - 0.6 * math.exp(-0.3 * l)
            if sample:
                o2 = _diff_attn_sample(qkv, caches["bk"][i], caches["bv"][i], biases["t5"],
                                       weights["lam_q"][i], weights["lam_k"][i], weights["diff_gn"][i], lam_init)
            else:
                o2 = _diff_attn(qkv, biases["t5"], weights["lam_q"][i], weights["lam_k"][i],
                                weights["diff_gn"][i], lam_init)
            wo = weights["w_out_ab"][i]
            outs["ret"].append(s_new)
            outs["bk"].append(kb.reshape(nseq, seq, H_B, 2 * DH_B))
            outs["bv"].append(vb.reshape(nseq, seq, H_B, DV_B))
        else:
            keep = seq if sample else min(C_WINDOW, seq)
            main, dt, kc, vc, tail = _inproj_cd(x, ng1, sc1, sh1, weights["w_in_cd"][i], tl_in,
                                                keep * (tl // seg), seg_in)
            main, dt = unpack(main), unpack(dt)
            if sample:
                o1 = _band_attn_sample(main, caches["ck"][i], caches["cv"][i], biases["band"][i])
                conv_past = _pad_rows(caches["dconv"][i])
                s0 = caches["dssm"][i].reshape(nseq, H_D * P_D, N_D)
                o2, ssm_new = _ssd(main, dt, weights["d_conv_w"][i], weights["d_conv_b"][i], weights["d_dt_bias"][i],
                                   weights["d_a_log"][i], weights["d_skip"][i], weights["d_norm_g"][i], conv_past, s0,
                                   c)
            else:
                fused = _cd_mixers_ffn(
                    x, main, dt, biases["band"][i], weights["w_out_cd"][i], g1, ng2, sc2, sh2, g2, weights["w_up"][l],
                    weights["ffn_conv_w"][l], weights["ffn_conv_b"][l], weights["w_down"][l],
                    weights["final_g"].reshape(1, d), weights["d_conv_w"][i], weights["d_conv_b"][i],
                    weights["d_dt_bias"][i], weights["d_a_log"][i], weights["d_skip"][i], weights["d_norm_g"][i],
                    FUSED_TL, final=(l == depth - 1))
                ssm_new = fused[2]
            wo = weights["w_out_cd"][i]
            outs["ck"].append(kc.reshape(nseq, keep, H_C, DH_C))
            outs["cv"].append(vc.reshape(nseq, keep, H_C, DH_C))
            outs["dconv"].append(tail[:, 8 - (CONV_D - 1):])
            outs["dssm"].append(ssm_new.reshape(nseq, H_D, P_D, N_D))
        d_ff = weights["w_down"].shape[1]
        if fused is not None:
            x, ftail = fused[0], fused[1]
        else:
            ffn_past = _pad_rows(caches["ffn"][l]) if sample else jnp.zeros((nseq, 8, d_ff), F32)
            x, ftail = _outproj_ffn(x, pack(o1), pack(o2), wo, g1, ng2, sc2, sh2, g2, weights["w_up"][l],
                                    weights["ffn_conv_w"][l], weights["ffn_conv_b"][l], weights["w_down"][l], ffn_past,
                                    weights["final_g"].reshape(1, d), tl, seg, final=(l == depth - 1))
        outs["ffn"].append(ftail[:, 8 - (CONV_F - 1):])
    stk = lambda t: jnp.stack(t).astype(F32)
    x = unpack(x)
    return (x,) + tuple(stk(outs[k]) for k in ("ret", "bk", "bv", "ck", "cv", "dconv", "dssm", "ffn"))


def kernel(x_prompt, x_sample, cache_ret_state, cache_b_k, cache_b_v, cache_c_k, cache_c_v, state_d_conv, state_d_ssm, state_ffn_conv, c_prompt, c_sample, w_mod, b_mod, norm_g, final_g, t5_table, w_in_ab, w_out_ab, ret_gn, lam_q, lam_k, diff_gn, w_in_cd, w_out_cd, rel_table, d_conv_w, d_conv_b, d_dt_bias, d_a_log, d_skip, d_norm_g, w_up, ffn_conv_w, ffn_conv_b, w_down):
    batch, seq, d = x_prompt.shape
    dec_batch, dec_seq, _ = x_sample.shape
    past = cache_b_k.shape[2]
    depth = w_mod.shape[0]
    assert dec_seq <= CHUNK and past % CHUNK == 0 and cache_c_k.shape[2] == C_WINDOW
    assert seq % BAND_TQ == 0 and seq % DIFF_TQ == 0

    w_in_cd_p = jnp.pad(w_in_cd, ((0, 0), (0, 0), (0, N_CD_PAD - w_in_cd.shape[-1]))).astype(BF16)
    weights = dict(
        norm_g=norm_g, final_g=final_g, w_in_ab=w_in_ab.astype(BF16), w_out_ab=w_out_ab.astype(BF16),
        ret_gn=ret_gn, lam_q=lam_q, lam_k=lam_k, diff_gn=diff_gn, w_in_cd=w_in_cd_p,
        w_out_cd=w_out_cd.astype(BF16), d_conv_w=d_conv_w, d_conv_b=d_conv_b, d_dt_bias=d_dt_bias, d_a_log=d_a_log,
        d_skip=d_skip, d_norm_g=d_norm_g, w_up=w_up.astype(BF16), ffn_conv_w=ffn_conv_w, ffn_conv_b=ffn_conv_b,
        w_down=w_down.astype(BF16))

    mod = _modulation(jnp.concatenate([c_prompt, c_sample], axis=0), w_mod, b_mod)

    def pieces(rows):
        return [[m[:, None, :] for m in jnp.split(mod[l, rows], 6, axis=-1)] for l in range(depth)]

    band = [_band_bias(rel_table[i]) for i in range(rel_table.shape[0])]
    biases_p = dict(t5=_t5_bias_prompt(t5_table, seq), band=band)
    biases_s = dict(t5=_t5_bias_sample(t5_table, past, dec_seq),
                    band=[b[:, :dec_seq, :C_WINDOW + dec_seq] for b in band])
    caches = dict(
        ret=cache_ret_state,
        bk=cache_b_k.reshape(cache_b_k.shape[0], dec_batch, past, H_B * 2 * DH_B),
        bv=cache_b_v.reshape(cache_b_v.shape[0], dec_batch, past, H_B * DV_B),
        ck=cache_c_k.reshape(cache_c_k.shape[0], dec_batch, C_WINDOW, H_C * DH_C),
        cv=cache_c_v.reshape(cache_c_v.shape[0], dec_batch, C_WINDOW, H_C * DH_C),
        dconv=state_d_conv, dssm=state_d_ssm, ffn=state_ffn_conv)

    pos_p = jnp.arange(seq, dtype=jnp.int32)
    pos_s = past + jnp.arange(dec_seq, dtype=jnp.int32)
    y_p, ret_p, bk_p, bv_p, ck_p, cv_p, dconv_p, dssm_p, ffn_p = _trunk(
        x_prompt, pieces(slice(0, batch)), pos_p, weights, None, biases_p)
    y_s, ret_s, bk_s, bv_s, ck_s, cv_s, dconv_s, dssm_s, ffn_s = _trunk(
        x_sample, pieces(slice(batch, batch + dec_batch)), pos_s, weights, caches, biases_s)
    return (y_p, y_s, ret_p, ret_s, bk_p, bk_s, bv_p, bv_s, ck_p, ck_s, cv_p, cv_s,
            dconv_p, dconv_s, dssm_p, dssm_s, ffn_p, ffn_s)
```

```python
import functools
import math

import numpy as np
import jax
import jax.numpy as jnp
from jax import lax
from jax.experimental import pallas as pl
from jax.experimental.pallas import tpu as pltpu

F32 = jnp.float32
BF16 = jnp.bfloat16

CHUNK = 64
EPS = 1e-6
NEG = -1e30
H_A, DK_A, DV_A = 4, 128, 128
ROPE_BASE = 10000.0
H_B, DH_B, DV_B = 4, 64, 128
T5_BUCKETS, T5_MAX_DIST = 32, 128
H_C, DH_C = 8, 64
BAND_CHUNKS = 8
C_WINDOW = BAND_CHUNKS * CHUNK
REL_CLIP = 128
H_D, P_D, G_D, N_D = 8, 64, 2, 128
D_INNER = H_D * P_D
CONV_D = 4
CONV_DIM_D = D_INNER + 2 * G_D * N_D
CONV_F = 3
N_AB_COLS = 7 * 512
N_CD_MAIN = 6 * 512
N_CD_PAD = N_CD_MAIN + 128
FF_CHUNK = 256
BAND_TQ = 256
DIFF_TQ = 256
SCAN_CHUNK = 256
FUSED_TL = 256
INPROJ_TL = 1024
VMEM_LIMIT = 56 * 1024 * 1024
LOG2E = math.log2(math.e)
Q_B_SCALE = DH_B ** -0.5 * LOG2E
Q_C_SCALE = DH_C ** -0.5 * LOG2E


def _mm(a, b):
    return jnp.dot(a, b, preferred_element_type=F32)


def _mm_nt(a, b):
    return lax.dot_general(a, b, (((1,), (1,)), ((), ())), preferred_element_type=F32)


def _mm_tn(a, b):
    return lax.dot_general(a, b, (((0,), (0,)), ((), ())), preferred_element_type=F32)


def _mm_exact(a, b):
    return jnp.dot(a, b, preferred_element_type=F32, precision=lax.Precision.HIGHEST)


def _mm_nt_exact(a, b):
    return lax.dot_general(a, b, (((1,), (1,)), ((), ())), preferred_element_type=F32,
                           precision=lax.Precision.HIGHEST)


def _silu(x):
    return x * (1.0 / (1.0 + jnp.exp(-x)))


def _softplus(x):
    return jnp.maximum(x, 0.0) + jnp.log1p(jnp.exp(-jnp.abs(x)))


def _gelu_tanh(x):
    k1 = -2.0 * math.sqrt(2.0 / math.pi) * math.log2(math.e)
    return x * (1.0 / (1.0 + jnp.exp2(x * (k1 + (k1 * 0.044715) * (x * x)))))


def _norm_mod(x, g, sc, sh):
    ms = jnp.mean(x * x, axis=-1, keepdims=True)
    return (x * lax.rsqrt(ms + EPS)) * g * (1.0 + sc) + sh


def _mod_spec(a, tl):
    if a.shape[1] == 1:
        return pl.BlockSpec((1, 1, a.shape[2]), lambda b, l: (b, 0, 0))
    return pl.BlockSpec((1, tl, a.shape[2]), lambda b, l: (b, l, 0))


def _params(sem):
    return pltpu.CompilerParams(dimension_semantics=sem, vmem_limit_bytes=VMEM_LIMIT)


def _resident(shape):
    nd = len(shape)
    return pl.BlockSpec(shape, lambda *_: (0,) * nd, pipeline_mode=pl.Buffered(1))


def _mod_kernel(c_ref, w_ref, b_ref, o_ref):
    c = c_ref[...]
    o_ref[0] = _mm(_silu(c).astype(BF16), w_ref[0].astype(BF16)) + b_ref[0]


def _modulation(c_all, w_mod, b_mod):
    depth, d, n = w_mod.shape
    r = c_all.shape[0]
    tn = 1536
    return pl.pallas_call(
        _mod_kernel,
        grid=(depth, n // tn),
        in_specs=[pl.BlockSpec((r, d), lambda l, j: (0, 0)),
                  pl.BlockSpec((1, d, tn), lambda l, j: (l, 0, j)),
                  pl.BlockSpec((1, 1, tn), lambda l, j: (l, 0, j))],
        out_specs=pl.BlockSpec((1, r, tn), lambda l, j: (l, 0, j)),
        out_shape=jax.ShapeDtypeStruct((depth, r, n), F32),
        compiler_params=_params(("parallel", "parallel")),
        name="modulation",
    )(c_all, w_mod, b_mod.reshape(depth, 1, n))


def _table_gather_kernel(tab_ref, idx_ref, o_ref, *, n_entries, n_heads):
    idx = idx_ref[...]

    def body(r, accs):
        m = idx == r
        return tuple(jnp.where(m, tab_ref[hh, r], a) for hh, a in enumerate(accs))

    accs = lax.fori_loop(0, n_entries, body, tuple(jnp.zeros(idx.shape, F32) for _ in range(n_heads)))
    for hh in range(n_heads):
        o_ref[hh:hh + 1, :] = accs[hh]


def _table_gather(table, idx):
    t, h = table.shape
    w = idx.shape[0]
    return pl.pallas_call(
        functools.partial(_table_gather_kernel, n_entries=t, n_heads=h),
        in_specs=[pl.BlockSpec(memory_space=pltpu.SMEM),
                  pl.BlockSpec((1, w), lambda: (0, 0))],
        out_specs=pl.BlockSpec((h, w), lambda: (0, 0)),
        out_shape=jax.ShapeDtypeStruct((h, w), F32),
        name="table_gather",
    )(table.T, idx.reshape(1, w))


def _toeplitz_kernel(v_ref, o_ref, *, rows, cols, tile):
    x = pltpu.roll(jnp.broadcast_to(v_ref[0], (rows, v_ref.shape[2])), 0, 1, stride=1, stride_axis=0)
    if tile is None:
        o_ref[0] = x[:, :cols]
    else:
        for jb in range(cols // tile):
            o_ref[0, jb] = x[:, jb * tile:(jb + 1) * tile]


def _toeplitz(vec, rows, cols, tile=None):
    h, wv = vec.shape
    if tile is None:
        out_shape, block, imap = (h, rows, cols), (1, rows, cols), lambda i: (i, 0, 0)
    else:
        out_shape, block, imap = (h, cols // tile, rows, tile), (1, cols // tile, rows, tile), lambda i: (i, 0, 0, 0)
    return pl.pallas_call(
        functools.partial(_toeplitz_kernel, rows=rows, cols=cols, tile=tile),
        grid=(h,),
        in_specs=[pl.BlockSpec((1, 1, wv), lambda i: (i, 0, 0))],
        out_specs=pl.BlockSpec(block, imap),
        out_shape=jax.ShapeDtypeStruct(out_shape, F32),
        compiler_params=_params(("parallel",)),
        name="toeplitz",
    )(vec.reshape(h, 1, wv))


def _t5_bucket(rel):
    half = T5_BUCKETS // 2
    max_exact = half // 2
    base = jnp.where(rel > 0, half, 0)
    n = jnp.abs(rel)
    nf = jnp.maximum(n, 1).astype(F32)
    large = max_exact + (jnp.log(nf / max_exact) / math.log(T5_MAX_DIST / max_exact) * (half - max_exact)).astype(jnp.int32)
    large = jnp.minimum(large, half - 1)
    return base + jnp.where(n < max_exact, n, large)


def _wrapped_offsets(n_pos, n_neg):
    p = jnp.arange(n_pos + n_neg, dtype=jnp.int32)
    return jnp.where(p < n_pos, p, p - (n_pos + n_neg))


def _t5_bias_prompt(t5_table, seq):
    nd = seq // DIFF_TQ
    u = _wrapped_offsets(seq, DIFF_TQ)
    vec = _table_gather(t5_table, _t5_bucket(u - (seq - DIFF_TQ))) * LOG2E
    return _toeplitz(vec, DIFF_TQ, seq, tile=DIFF_TQ)


def _t5_bias_sample(t5_table, past, lq):
    u = _wrapped_offsets(past + lq, lq)
    vec = _table_gather(t5_table, _t5_bucket(u - past)) * LOG2E
    return _toeplitz(vec, lq, past + 2 * lq)[:, :, :past + lq]


def _band_bias(rel_table):
    u = _wrapped_offsets(3 * BAND_TQ, BAND_TQ)
    idx = jnp.clip(2 * BAND_TQ - u, -REL_CLIP, REL_CLIP) + REL_CLIP
    bias = _toeplitz(_table_gather(rel_table, idx) * LOG2E, BAND_TQ, 3 * BAND_TQ)
    dc = (np.arange(BAND_TQ)[:, None] // CHUNK + C_WINDOW // CHUNK) - np.arange(3 * BAND_TQ)[None, :] // CHUNK
    return jnp.where(jnp.asarray((dc >= 0) & (dc <= BAND_CHUNKS))[None], bias, NEG)


def _inproj_ab_kernel(x_ref, g_ref, sc_ref, sh_ref, w_ref, cos_ref, sin_ref, qkv_ref, kb_ref, vb_ref, *, tl):
    h = _norm_mod(x_ref[0], g_ref[...], sc_ref[0], sh_ref[0]).astype(BF16)
    cos = cos_ref[...]
    sin = sin_ref[...]
    for j in range(7):
        y = _mm(h, w_ref[:, j * 512:(j + 1) * 512])
        if j < 2:
            for hh in range(H_A):
                yh = y[:, hh * 128:(hh + 1) * 128]
                yh = yh * cos + pltpu.roll(yh, 64, 1) * sin
                if j == 1:
                    yh = yh * (DK_A ** -0.5)
                qkv_ref[0, :, j * 512 + hh * 128:j * 512 + (hh + 1) * 128] = yh.astype(BF16)
        else:
            if j == 4:
                y = y * Q_B_SCALE
            if j >= 5:
                dst = kb_ref if j == 5 else vb_ref
                for hh in range(H_B):
                    dst[0, pl.ds(hh, tl, stride=H_B), :] = y[:, hh * 128:(hh + 1) * 128]
            qkv_ref[0, :, j * 512:(j + 1) * 512] = y.astype(BF16)


def _inproj_ab(x, g, sc, sh, w, cos, sin, tl):
    nseq, seq, d = x.shape
    grid = (nseq, seq // tl)
    tok = lambda n: pl.BlockSpec((1, tl, n), lambda b, l: (b, l, 0))
    cache = pl.BlockSpec((1, tl * H_B, 128), lambda b, l: (b, l, 0))
    return pl.pallas_call(
        functools.partial(_inproj_ab_kernel, tl=tl),
        grid=grid,
        in_specs=[tok(d), _resident((1, d)), _mod_spec(sc, tl), _mod_spec(sh, tl), _resident((d, N_AB_COLS)),
                  pl.BlockSpec((tl, 128), lambda b, l: (l, 0)),
                  pl.BlockSpec((tl, 128), lambda b, l: (l, 0))],
        out_specs=[tok(N_AB_COLS), cache, cache],
        out_shape=[jax.ShapeDtypeStruct((nseq, seq, N_AB_COLS), BF16),
                   jax.ShapeDtypeStruct((nseq, seq * H_B, 128), F32),
                   jax.ShapeDtypeStruct((nseq, seq * H_B, 128), F32)],
        compiler_params=_params(("parallel", "parallel")),
        name="inproj_ab",
    )(x, g, sc, sh, w, cos, sin)


def _retention_kernel(q_ref, k_ref, v_ref, g_ref, s0_ref, dec_ref, qd_ref, kd_ref, gc_ref, gn_ref,
                      o_ref, sn_ref, *, c, n_chunks):
    states = [s0_ref[0, h] for h in range(H_A)]
    for ci in range(n_chunks):
        rows = slice(ci * c, (ci + 1) * c)
        for h in range(H_A):
            cs = slice(h * 128, (h + 1) * 128)
            q = q_ref[0, rows, cs]
            k = k_ref[0, rows, cs]
            v = v_ref[0, rows, cs]
            state = states[h]
            s = _mm_nt(q, k) * dec_ref[h]
            o = _mm(s.astype(BF16), v) + _mm((q.astype(F32) * qd_ref[h]).astype(BF16), state.astype(BF16))
            states[h] = gc_ref[h] * state + _mm_tn((k.astype(F32) * kd_ref[h]).astype(BF16), v)
            mu = jnp.mean(o, axis=-1, keepdims=True)
            dlt = o - mu
            var = jnp.mean(dlt * dlt, axis=-1, keepdims=True)
            y = dlt * lax.rsqrt(var + EPS) * gn_ref[:, cs] * _silu(g_ref[0, rows, cs].astype(F32))
            o_ref[0, rows, cs] = y.astype(BF16)
    for h in range(H_A):
        sn_ref[0, h] = states[h]


def _retention_consts(c):
    lg = np.log1p(-np.exp2(-5.0 - np.arange(H_A, dtype=np.float32))).astype(np.float32)
    idx = np.arange(c, dtype=np.float32)
    diff = idx[:, None] - idx[None, :]
    decay = np.where(diff[None] >= 0, np.exp(np.maximum(diff, 0.0)[None] * lg[:, None, None]), 0.0)
    qd = np.exp((idx + 1.0)[None, :] * lg[:, None])
    kd = np.exp((c - 1.0 - idx)[None, :] * lg[:, None])
    gc = np.exp(c * lg)
    bc = lambda a: np.ascontiguousarray(np.broadcast_to(a[..., None], a.shape + (128,))).astype(np.float32)
    return decay.astype(np.float32), bc(qd), bc(kd), bc(gc[:, None])


def _retention(qkv, s0, ret_gn, c):
    nseq, seq, _ = qkv.shape
    decay, qd, kd, gc = _retention_consts(c)
    w = H_A * 128
    col = lambda idx: pl.BlockSpec((1, seq, w), lambda b: (b, 0, idx))
    const = lambda shape: pl.BlockSpec(shape, lambda b: (0,) * len(shape))
    state = pl.BlockSpec((1, H_A, DK_A, DV_A), lambda b: (b, 0, 0, 0))
    return pl.pallas_call(
        functools.partial(_retention_kernel, c=c, n_chunks=seq // c),
        grid=(nseq,),
        in_specs=[col(0), col(1), col(2), col(3), state, const((H_A, c, c)), const((H_A, c, 128)),
                  const((H_A, c, 128)), const((H_A, 1, 128)), const((1, w))],
        out_specs=[col(0), state],
        out_shape=[jax.ShapeDtypeStruct((nseq, seq, w), BF16),
                   jax.ShapeDtypeStruct((nseq, H_A, DK_A, DV_A), F32)],
        compiler_params=_params(("parallel",)),
        name="retention",
    )(qkv, qkv, qkv, qkv, s0, decay, qd, kd, gc, ret_gn.reshape(1, w))


def _split_halves(q):
    lane = lax.broadcasted_iota(jnp.int32, q.shape, 1)
    zero = jnp.zeros_like(q)
    return jnp.where(lane < 64, q, zero), jnp.where(lane >= 64, q, zero)


def _lambda(lq_ref, lk_ref, lam_init):
    e = jnp.exp(jnp.sum(lq_ref[...] * lk_ref[...], axis=1, keepdims=True))
    return e[0:1, :] - e[1:2, :] + lam_init


def _diff_epilogue(o, gn, lam_init):
    ms = jnp.mean(o * o, axis=-1, keepdims=True)
    return (o * lax.rsqrt(ms + EPS)) * gn * (1.0 - lam_init)


def _diff_attn_kernel(q_ref, k_ref, v_ref, b_ref, lq_ref, lk_ref, gn_ref, o_ref, s_ref, qs_ref, mx_ref, ls_ref,
                      acc_ref, *, tq, nd, lam_init):
    row = lax.broadcasted_iota(jnp.int32, (tq, tq), 0) // CHUNK
    col = lax.broadcasted_iota(jnp.int32, (tq, tq), 1) // CHUNK
    chunk_mask = col <= row
    lam = _lambda(lq_ref, lk_ref, lam_init)

    def prepare(t):
        halves = _split_halves(q_ref[0, t * tq:(t + 1) * tq, :])
        for i in range(2):
            r = 2 * (t % 2) + i
            qs_ref[r] = halves[i]
            mx_ref[r] = jnp.full((tq, 128), NEG, F32)
            ls_ref[r] = jnp.zeros((tq, 128), F32)
            acc_ref[r] = jnp.zeros((tq, DV_B), F32)

    def far_bias(t, kb):
        if (t - kb - 1) * tq + 1 < T5_MAX_DIST:
            return None
        return b_ref[0, nd - 1 - t + kb, 0:1, 0:1]

    def scores(t, kb):
        k = k_ref[0, kb * tq:(kb + 1) * tq, :]
        far = far_bias(t, kb)
        for i in range(2):
            r = 2 * (t % 2) + i
            s = _mm_nt(qs_ref[r], k)
            if far is None:
                s = s + b_ref[0, nd - 1 - t + kb]
            if kb == t:
                s = jnp.where(chunk_mask, s, NEG)
            s_ref[r * nd + kb] = s
            m = s[:, 0:128]
            for c0 in range(128, tq, 128):
                m = jnp.maximum(m, s[:, c0:c0 + 128])
            mx_ref[r] = jnp.maximum(mx_ref[r], m if far is None else m + far)

    def row_max(t):
        for i in range(2):
            r = 2 * (t % 2) + i
            mx_ref[r] = jnp.broadcast_to(jnp.max(mx_ref[r], axis=1, keepdims=True), (tq, 128))

    def weigh(t, kb):
        v = v_ref[0, kb * tq:(kb + 1) * tq, :]
        far = far_bias(t, kb)
        for i in range(2):
            r = 2 * (t % 2) + i
            m = mx_ref[r] if far is None else mx_ref[r] - far
            l = ls_ref[r]
            ps = []
            for c0 in range(0, tq, 128):
                p = jnp.exp2(s_ref[r * nd + kb, :, c0:c0 + 128] - m)
                l = l + p
                ps.append(p.astype(BF16))
            ls_ref[r] = l
            acc_ref[r] += _mm(jnp.concatenate(ps, axis=1), v)

    def finish(t):
        r = 2 * (t % 2)
        l0 = jnp.sum(ls_ref[r], axis=1, keepdims=True)
        l1 = jnp.sum(ls_ref[r + 1], axis=1, keepdims=True)
        o = acc_ref[r] / l0 - lam * (acc_ref[r + 1] / l1)
        o_ref[0, t * tq:(t + 1) * tq, :] = _diff_epilogue(o, gn_ref[...], lam_init).astype(BF16)

    last = nd - 1
    prepare(last)
    for kb in range(last + 1):
        scores(last, kb)
    row_max(last)
    for t in range(last - 1, -1, -1):
        prepare(t)
        for kb in range(t + 2):
            if kb <= t:
                scores(t, kb)
            weigh(t + 1, kb)
        finish(t + 1)
        row_max(t)
    weigh(0, 0)
    finish(0)


def _diff_attn(qkv, bias, lam_q, lam_k, diff_gn, lam_init):
    nseq, seq, _ = qkv.shape
    tq = DIFF_TQ
    nd = seq // tq
    full = lambda off: pl.BlockSpec((1, seq, 128), lambda b, h: (b, 0, off + h))
    small = lambda r, c: pl.BlockSpec((r, c), lambda b, h: (0, 0))
    return pl.pallas_call(
        functools.partial(_diff_attn_kernel, tq=tq, nd=nd, lam_init=lam_init),
        grid=(nseq, H_B),
        in_specs=[full(16), full(20), full(24),
                  pl.BlockSpec((1, nd, tq, tq), lambda b, h: (h, 0, 0, 0)),
                  small(2, DH_B), small(2, DH_B), small(1, DV_B)],
        out_specs=full(0),
        out_shape=jax.ShapeDtypeStruct((nseq, seq, H_B * DV_B), BF16),
        scratch_shapes=[pltpu.VMEM((4 * nd, tq, tq), F32), pltpu.VMEM((4, tq, 128), BF16),
                        pltpu.VMEM((4, tq, 128), F32), pltpu.VMEM((4, tq, 128), F32),
                        pltpu.VMEM((4, tq, DV_B), F32)],
        compiler_params=_params(("parallel", "parallel")),
        name="diff_attn",
    )(qkv, qkv, qkv, bias, lam_q, lam_k, diff_gn.reshape(1, DV_B))


def _diff_attn_sample_kernel(q_ref, kc_ref, vc_ref, kp_ref, vp_ref, b_ref, lq_ref, lk_ref, gn_ref, o_ref,
                             *, past, lam_init):
    lam = _lambda(lq_ref, lk_ref, lam_init)
    for h in range(H_B):
        cs = slice(h * 128, (h + 1) * 128)
        qs = _split_halves(q_ref[0, :, cs])
        kp = kp_ref[0, :, cs].astype(BF16)
        vp = vp_ref[0, :, cs].astype(BF16)
        kc = kc_ref[0, :, cs]
        vc = vc_ref[0, :, cs]
        b = b_ref[h]
        probs = []
        for i in range(2):
            sp = _mm_nt(qs[i], kp) + b[:, :past]
            sc = _mm_nt(qs[i], kc) + b[:, past:]
            m = jnp.maximum(jnp.max(sp, axis=1, keepdims=True), jnp.max(sc, axis=1, keepdims=True))
            pp = jnp.exp2(sp - m)
            pc = jnp.exp2(sc - m)
            l = jnp.sum(pp, axis=1, keepdims=True) + jnp.sum(pc, axis=1, keepdims=True)
            probs.append((pp / l, pc / l))
        ap = probs[0][0] - lam * probs[1][0]
        ac = probs[0][1] - lam * probs[1][1]
        o = _mm(ap.astype(BF16), vp) + _mm(ac.astype(BF16), vc)
        o_ref[0, :, cs] = _diff_epilogue(o, gn_ref[...], lam_init).astype(BF16)


def _diff_attn_sample(qkv, k_past, v_past, bias, lam_q, lam_k, diff_gn, lam_init):
    nseq, lq, _ = qkv.shape
    past = k_past.shape[1]
    w = H_B * DV_B
    cur = lambda idx: pl.BlockSpec((1, lq, w), lambda b: (b, 0, idx))
    old = pl.BlockSpec((1, past, w), lambda b: (b, 0, 0))
    small = lambda r, c: pl.BlockSpec((r, c), lambda b: (0, 0))
    return pl.pallas_call(
        functools.partial(_diff_attn_sample_kernel, past=past, lam_init=lam_init),
        grid=(nseq,),
        in_specs=[cur(4), cur(5), cur(6), old, old, _resident(bias.shape),
                  small(2, DH_B), small(2, DH_B), small(1, DV_B)],
        out_specs=cur(0),
        out_shape=jax.ShapeDtypeStruct((nseq, lq, w), BF16),
        compiler_params=_params(("parallel",)),
        name="diff_attn_sample",
    )(qkv, qkv, qkv, k_past, v_past, bias, lam_q, lam_k, diff_gn.reshape(1, DV_B))


def _inproj_cd_kernel(x_ref, g_ref, sc_ref, sh_ref, w_ref, main_ref, dt_ref, kc_ref, vc_ref, tail_ref,
                      *, tl, seg, keep):
    h = _norm_mod(x_ref[0], g_ref[...], sc_ref[0], sh_ref[0]).astype(BF16)
    for j in range(6):
        y = _mm(h, w_ref[:, j * 512:(j + 1) * 512])
        main_ref[0, :, j * 512:(j + 1) * 512] = (y * Q_C_SCALE if j == 0 else y).astype(BF16)
        if j == 1:
            kc_ref[0] = y[tl - keep:tl, :]
        if j == 2:
            vc_ref[0] = y[tl - keep:tl, :]
        if j >= 4:
            for s in range(tl // seg):
                tail_ref[s, :, (j - 4) * 512:(j - 3) * 512] = y[(s + 1) * seg - 8:(s + 1) * seg, :]
    dt_ref[0] = _mm(h, w_ref[:, N_CD_MAIN:N_CD_PAD])


def _inproj_cd(x, g, sc, sh, w, tl, keep, seg):
    nseq, seq, d = x.shape
    nseg = tl // seg
    assert keep <= tl and seq % tl == 0
    tok = lambda n: pl.BlockSpec((1, tl, n), lambda b, l: (b, l, 0))
    kept = pl.BlockSpec((1, keep, 512), lambda b, l: (b, 0, 0))
    return pl.pallas_call(
        functools.partial(_inproj_cd_kernel, tl=tl, seg=seg, keep=keep),
        grid=(nseq, seq // tl),
        in_specs=[tok(d), _resident((1, d)), _mod_spec(sc, tl), _mod_spec(sh, tl), _resident((d, N_CD_PAD))],
        out_specs=[tok(N_CD_MAIN), tok(128), kept, kept,
                   pl.BlockSpec((nseg, 8, CONV_DIM_D), lambda b, l: (b, 0, 0))],
        out_shape=[jax.ShapeDtypeStruct((nseq, seq, N_CD_MAIN), BF16),
                   jax.ShapeDtypeStruct((nseq, seq, 128), F32),
                   jax.ShapeDtypeStruct((nseq, keep, 512), F32),
                   jax.ShapeDtypeStruct((nseq, keep, 512), F32),
                   jax.ShapeDtypeStruct((nseq * nseg, 8, CONV_DIM_D), F32)],
        compiler_params=_params(("parallel", "arbitrary")),
        name="inproj_cd",
    )(x, g, sc, sh, w)


def _band_sample_kernel(q_ref, kc_ref, vc_ref, kp_ref, vp_ref, b_ref, o_ref, *, past):
    for pr in range(H_C // 2):
        cs = slice(pr * 128, (pr + 1) * 128)
        halves = _split_halves(q_ref[0, :, cs])
        kp = kp_ref[0, :, cs].astype(BF16)
        vp = vp_ref[0, :, cs].astype(BF16)
        kc = kc_ref[0, :, cs]
        vc = vc_ref[0, :, cs]
        outs = []
        for hi, qh in enumerate(halves):
            b = b_ref[2 * pr + hi]
            sp = _mm_nt(qh, kp) + b[:, :past]
            sc = _mm_nt(qh, kc) + b[:, past:]
            m = jnp.maximum(jnp.max(sp, axis=1, keepdims=True), jnp.max(sc, axis=1, keepdims=True))
            pp = jnp.exp2(sp - m)
            pc = jnp.exp2(sc - m)
            l = jnp.sum(pp, axis=1, keepdims=True) + jnp.sum(pc, axis=1, keepdims=True)
            outs.append(_mm((pp / l).astype(BF16), vp) + _mm((pc / l).astype(BF16), vc))
        lane = lax.broadcasted_iota(jnp.int32, outs[0].shape, 1)
        o_ref[0, :, cs] = jnp.where(lane < 64, outs[0], outs[1]).astype(BF16)


def _band_attn_sample(main, k_past, v_past, bias):
    nseq, lq, _ = main.shape
    past = k_past.shape[1]
    w = H_C * DH_C
    cur = lambda idx: pl.BlockSpec((1, lq, w), lambda b: (b, 0, idx))
    old = pl.BlockSpec((1, past, w), lambda b: (b, 0, 0))
    return pl.pallas_call(
        functools.partial(_band_sample_kernel, past=past),
        grid=(nseq,),
        in_specs=[cur(0), cur(1), cur(2), old, old, _resident(bias.shape)],
        out_specs=cur(0),
        out_shape=jax.ShapeDtypeStruct((nseq, lq, w), BF16),
        compiler_params=_params(("parallel",)),
        name="band_attn_sample",
    )(main, main, main, k_past, v_past, bias)


def _ssd_kernel(z_ref, xs_ref, bm_ref, cm_ref, dt_ref, cw_ref, cb_ref, dtb_ref, alog_ref, dskip_ref, ng_ref,
                past_ref, s0_ref, o_ref, sn_ref, buf_ref, st_ref, *, c):
    @pl.when(pl.program_id(1) == 0)
    def _():
        buf_ref[0:8, :] = past_ref[0]
        st_ref[...] = s0_ref[0]

    for _ in _ssd_steps(z_ref.at[0], xs_ref.at[0], bm_ref.at[0], cm_ref.at[0], dt_ref.at[0], cw_ref, cb_ref, dtb_ref,
                        alog_ref, dskip_ref, ng_ref, o_ref.at[0], buf_ref, st_ref, c=c):
        pass
    sn_ref[0] = st_ref[...]


def _ssd(main, dt, conv_w, conv_b, dt_bias, a_log, d_skip, norm_g, conv_past, s0, c):
    nseq, seq, _ = main.shape
    pad = lambda a: jnp.pad(a.reshape(1, H_D), ((0, 0), (0, 128 - H_D)))
    blk = lambda w, idx: pl.BlockSpec((1, c, w), lambda b, l: (b, l, idx))
    const = lambda r, w: pl.BlockSpec((r, w), lambda b, l: (0, 0))
    state = pl.BlockSpec((1, H_D * P_D, N_D), lambda b, l: (b, 0, 0))
    return pl.pallas_call(
        functools.partial(_ssd_kernel, c=c),
        grid=(nseq, seq // c),
        in_specs=[blk(512, 3), blk(512, 4), blk(256, 10), blk(256, 11), blk(128, 0),
                  const(CONV_D, CONV_DIM_D), const(1, CONV_DIM_D), const(1, 128), const(1, 128),
                  pl.BlockSpec(memory_space=pltpu.SMEM), const(1, D_INNER),
                  pl.BlockSpec((1, 8, CONV_DIM_D), lambda b, l: (b, 0, 0)), state],
        out_specs=[blk(512, 0), state],
        out_shape=[jax.ShapeDtypeStruct((nseq, seq, D_INNER), BF16),
                   jax.ShapeDtypeStruct((nseq, H_D * P_D, N_D), F32)],
        scratch_shapes=[pltpu.VMEM((c + 8, CONV_DIM_D), F32), pltpu.VMEM((H_D * P_D, N_D), F32)],
        compiler_params=_params(("parallel", "arbitrary")),
        name="ssd",
    )(main, main, main, main, dt, conv_w, conv_b.reshape(1, CONV_DIM_D), pad(dt_bias), pad(a_log), d_skip,
      norm_g.reshape(1, D_INNER), conv_past, s0)


def _ffn_kernel(x_ref, o1_ref, o2_ref, wo_ref, g1_ref, ng_ref, sc_ref, sh_ref, g2_ref, wup_ref, cw_ref, cb_ref,
                wdn_ref, past_ref, fg_ref, out_ref, tail_ref, act_ref, wb_ref, gt_ref, *, tl, d_ff, final, seg):
    @pl.when(pl.program_id(1) == 0)
    def _():
        gt_ref[...] = past_ref[...]

    for _ in _ffn_steps(x_ref, o1_ref.at[0], o2_ref.at[0], wo_ref, g1_ref, ng_ref, sc_ref, sh_ref, g2_ref, wup_ref,
                        cw_ref, cb_ref, wdn_ref, fg_ref, out_ref, tail_ref, act_ref, wb_ref, gt_ref,
                        tl=tl, d_ff=d_ff, final=final, seg=seg):
        pass


def _outproj_ffn(x, o1, o2, wo, g1, ng, sc, sh, g2, wup, cw, cb, wdn, past, fg, tl, seg, final):
    nseq, seq, d = x.shape
    d_ff = wdn.shape[0]
    nseg = tl // seg
    assert nseg == 1 or seq == tl
    tok = lambda n: pl.BlockSpec((1, tl, n), lambda b, l: (b, l, 0))
    tail = pl.BlockSpec((nseg, 8, d_ff), lambda b, l: (b, 0, 0))
    return pl.pallas_call(
        functools.partial(_ffn_kernel, tl=tl, d_ff=d_ff, final=final, seg=seg),
        grid=(nseq, seq // tl),
        in_specs=[tok(d), tok(o1.shape[-1]), tok(o2.shape[-1]), _resident(wo.shape), _mod_spec(g1, tl),
                  _resident((1, d)), _mod_spec(sc, tl), _mod_spec(sh, tl), _mod_spec(g2, tl), _resident(wup.shape),
                  _resident(cw.shape), _resident((1, d_ff)), _resident(wdn.shape), tail, _resident((1, d))],
        out_specs=[tok(d), tail],
        out_shape=[jax.ShapeDtypeStruct((nseq, seq, d), F32), jax.ShapeDtypeStruct((nseq * nseg, 8, d_ff), F32)],
        scratch_shapes=[pltpu.VMEM((tl, d_ff), BF16), pltpu.VMEM((nseg * (seg + 8), FF_CHUNK), F32),
                        pltpu.VMEM((nseg, 8, d_ff), F32)],
        compiler_params=_params(("parallel", "arbitrary")),
        name="outproj_ffn",
    )(x, o1, o2, wo, g1, ng, sc, sh, g2, wup, cw, cb.reshape(1, d_ff), wdn, past, fg)


def _interleave(gens, strides):
    live = list(zip(gens, strides))
    while live:
        for item in list(live):
            for _ in range(item[1]):
                try:
                    next(item[0])
                except StopIteration:
                    live.remove(item)
                    break


def _chain(*gens):
    for g in gens:
        yield from g


def _ffn_steps(x_ref, o1, o2, wo_ref, g1_ref, ng_ref, sc_ref, sh_ref, g2_ref, wup_ref, cw_ref, cb_ref, wdn_ref,
               fg_ref, out_ref, tail_ref, act_ref, wb_ref, gt_ref, *, tl, d_ff, final, seg):
    nseg = tl // seg
    stride = seg + 8
    half = wo_ref.shape[0] // 2
    mix = _mm(o1[...], wo_ref[0:half, :]) + _mm(o2[...], wo_ref[half:2 * half, :])
    x1 = x_ref[0] + g1_ref[0] * mix
    out_ref[0] = x1
    h = _norm_mod(x1, ng_ref[...], sc_ref[0], sh_ref[0]).astype(BF16)
    yield
    split = (d_ff // FF_CHUNK + 1) // 2 * FF_CHUNK
    for j in range(d_ff // FF_CHUNK):
        c0 = j * FF_CHUNK
        a = _mm(h, wup_ref[:, c0:c0 + FF_CHUNK])
        g = _mm(h, wup_ref[:, d_ff + c0:d_ff + c0 + FF_CHUNK])
        for s in range(nseg):
            wb_ref[s * stride:s * stride + 8, :] = gt_ref[s, :, c0:c0 + FF_CHUNK]
            wb_ref[s * stride + 8:(s + 1) * stride, :] = g[s * seg:(s + 1) * seg, :]
            gt_ref[s, :, c0:c0 + FF_CHUNK] = g[(s + 1) * seg - 8:(s + 1) * seg, :]
        back2 = [wb_ref[pl.ds(s * stride + 6, seg), :] for s in range(nseg)]
        back1 = [wb_ref[pl.ds(s * stride + 7, seg), :] for s in range(nseg)]
        if nseg > 1:
            back2, back1 = [jnp.concatenate(back2, axis=0)], [jnp.concatenate(back1, axis=0)]
        gc = (cb_ref[:, c0:c0 + FF_CHUNK] + back2[0] * cw_ref[0:1, c0:c0 + FF_CHUNK]
              + back1[0] * cw_ref[1:2, c0:c0 + FF_CHUNK] + g * cw_ref[2:3, c0:c0 + FF_CHUNK])
        act_ref[:, c0:c0 + FF_CHUNK] = (a * _gelu_tanh(gc)).astype(BF16)
        if c0 + FF_CHUNK == split:
            dn = _mm(act_ref[:, 0:split], wdn_ref[0:split, :])
        yield
    dn = dn + _mm(act_ref[:, split:d_ff], wdn_ref[split:d_ff, :])
    x2 = out_ref[0] + g2_ref[0] * dn
    if final:
        ms = jnp.mean(x2 * x2, axis=-1, keepdims=True)
        x2 = (x2 * lax.rsqrt(ms + EPS)) * fg_ref[...]
    out_ref[0] = x2
    tail_ref[...] = gt_ref[...]
    yield


def _band_steps(q, k_ref, v_ref, b_ref, o, t, *, tq):
    lane = lax.broadcasted_iota(jnp.int32, (tq, 128), 1)
    starts, offs = [], []
    for d in range(3):
        kt = t - 2 + d
        starts.append(pl.multiple_of(jnp.maximum(kt, 0) * tq, tq))
        offs.append(jnp.where(kt >= 0, 0.0, NEG))

    def scores(head):
        pr, hi = divmod(head, 2)
        cs = slice(pr * 128, (pr + 1) * 128)
        qh = _split_halves(q[:, cs])[hi]
        ss = []
        for d in range(3):
            s = _mm_nt(qh, k_ref[0, pl.ds(starts[d], tq), cs]) + b_ref[head, :, d * tq:(d + 1) * tq]
            ss.append(s + offs[d] if d < 2 else s)
        mm = ss[2][:, 0:128]
        for d in range(3):
            for c0 in range(0, tq, 128):
                mm = jnp.maximum(mm, ss[d][:, c0:c0 + 128])
        return ss, jnp.max(mm, axis=1, keepdims=True)

    def weigh(head, ss, m):
        cs = slice((head // 2) * 128, (head // 2 + 1) * 128)
        ls = jnp.zeros((tq, 128), F32)
        acc = jnp.zeros((tq, 128), F32)
        for d in range(3):
            p = jnp.exp2(ss[d] - m)
            for c0 in range(0, tq, 128):
                ls = ls + p[:, c0:c0 + 128]
            acc = acc + _mm(p.astype(BF16), v_ref[0, pl.ds(starts[d], tq), cs])
        return acc / jnp.sum(ls, axis=1, keepdims=True)

    outs = []
    pending = scores(0)
    yield
    for head in range(H_C):
        nxt = scores(head + 1) if head + 1 < H_C else None
        outs.append(weigh(head, *pending))
        pending = nxt
        if head % 2 == 1:
            cs = slice((head // 2) * 128, (head // 2 + 1) * 128)
            o[:, cs] = jnp.where(lane < 64, outs[head - 1], outs[head]).astype(BF16)
        yield


def _ssd_steps(z, xs_in, bm_in, cm_in, dt_in, cw_ref, cb_ref, dtb_ref, alog_ref, dskip_ref, ng_ref, o,
               buf_ref, st_ref, *, c):
    buf_ref[8:8 + c, 0:D_INNER] = xs_in[...].astype(F32)
    buf_ref[8:8 + c, D_INNER:D_INNER + 256] = bm_in[...].astype(F32)
    buf_ref[8:8 + c, D_INNER + 256:CONV_DIM_D] = cm_in[...].astype(F32)
    conv = cb_ref[...] + buf_ref[pl.ds(5, c), :] * cw_ref[0:1, :]
    for i in range(1, CONV_D):
        conv = conv + buf_ref[pl.ds(5 + i, c), :] * cw_ref[i:i + 1, :]
    buf_ref[0:8, :] = buf_ref[c:c + 8, :]
    act = _silu(conv)
    xs = act[:, 0:D_INNER]
    bm = act[:, D_INNER:D_INNER + 256].astype(BF16)
    cm = act[:, D_INNER + 256:CONV_DIM_D].astype(BF16)
    yield

    dt = _softplus(dt_in[...] + dtb_ref[...])
    da = dt * (-jnp.exp(alog_ref[...]))
    row = lax.broadcasted_iota(jnp.int32, (c, c), 0)
    col = lax.broadcasted_iota(jnp.int32, (c, c), 1)
    tri = row >= col
    cum = _mm_exact(tri.astype(F32), da)
    eye = (lax.broadcasted_iota(jnp.int32, (8, 128), 0) == lax.broadcasted_iota(jnp.int32, (8, 128), 1)).astype(F32)
    cum_t = _mm_exact(_mm_nt_exact(eye, da), (row <= col).astype(F32))
    dt_t = _mm_nt_exact(eye, dt)
    ecum = jnp.exp(cum)
    last = cum[c - 1:c, :]
    wgt = jnp.exp(last - cum) * dt
    elast = jnp.exp(last)
    lane = lax.broadcasted_iota(jnp.int32, (c, 128), 1)
    low = lane < 64
    rlow = lax.broadcasted_iota(jnp.int32, (128, 128), 0) < 64
    yield

    for g in range(G_D):
        bm_g = bm[:, g * 128:(g + 1) * 128]
        cm_g = cm[:, g * 128:(g + 1) * 128]
        cb = _mm_nt(cm_g, bm_g)
        ys = []
        for pp in range(2):
            p = 2 * g + pp
            h0, h1 = 2 * p, 2 * p + 1
            x_f = xs[:, p * 128:(p + 1) * 128]
            x_b = x_f.astype(BF16)
            y_in = []
            for hh in (h0, h1):
                seg = cum[:, hh:hh + 1] - cum_t[hh:hh + 1, :]
                dec = jnp.exp(jnp.where(tri, seg, NEG))
                y_in.append(_mm((cb * dec * dt_t[hh:hh + 1, :]).astype(BF16), x_b))
            st = st_ref[p * 128:(p + 1) * 128, :]
            y_x = _mm_nt(cm_g, st.astype(BF16)) * jnp.where(low, ecum[:, h0:h0 + 1], ecum[:, h1:h1 + 1])
            w2 = jnp.where(low, wgt[:, h0:h0 + 1], wgt[:, h1:h1 + 1])
            st_ref[p * 128:(p + 1) * 128, :] = (jnp.where(rlow, elast[:, h0:h0 + 1], elast[:, h1:h1 + 1]) * st
                                                + _mm_tn((x_f * w2).astype(BF16), bm_g))
            y = jnp.where(low, y_in[0], y_in[1]) + y_x + jnp.where(low, dskip_ref[h0], dskip_ref[h1]) * x_f
            ys.append(y * _silu(z[:, p * 128:(p + 1) * 128].astype(F32)))
            yield
        ms = (jnp.sum(ys[0] * ys[0], axis=-1, keepdims=True) + jnp.sum(ys[1] * ys[1], axis=-1, keepdims=True)) / 256.0
        inv = lax.rsqrt(ms + EPS)
        for pp in range(2):
            p = 2 * g + pp
            o[:, p * 128:(p + 1) * 128] = (ys[pp] * inv * ng_ref[:, p * 128:(p + 1) * 128]).astype(BF16)
    yield


def _cd_ffn_kernel(x_ref, wo_ref, g1_ref, ng_ref, sc_ref, sh_ref, g2_ref, wup_ref, cwf_ref, cbf_ref, wdn_ref, fg_ref,
                   q_ref, k_ref, v_ref, bb_ref,
                   z_ref, xs_ref, bm_ref, cm_ref, dt_ref, cws_ref, cbs_ref, dtb_ref, alog_ref, dskip_ref, ngs_ref,
                   out_ref, tail_ref, sn_ref,
                   obuf_ref, act_ref, wb_ref, gt_ref, buf_ref, st_ref,
                   *, tl, d_ff, final, n_tiles, tiles_per_seq, c, tq):
    g = pl.program_id(0)
    gm = jnp.minimum(g, n_tiles - 1)
    gf = jnp.maximum(g - 1, 0)
    jm = gm % tiles_per_seq
    jf = gf % tiles_per_seq
    slot = g % 2
    prev = (g + 1) % 2

    @pl.when(g == 0)
    def _():
        obuf_ref[...] = jnp.zeros(obuf_ref.shape, BF16)

    @pl.when(jm == 0)
    def _():
        buf_ref[0:8, :] = jnp.zeros((8, CONV_DIM_D), F32)
        st_ref[...] = jnp.zeros(st_ref.shape, F32)

    @pl.when(jf == 0)
    def _():
        gt_ref[...] = jnp.zeros(gt_ref.shape, F32)

    ffn = _ffn_steps(x_ref, obuf_ref.at[prev, :, 0:512], obuf_ref.at[prev, :, 512:1024], wo_ref, g1_ref, ng_ref,
                     sc_ref, sh_ref, g2_ref, wup_ref, cwf_ref, cbf_ref, wdn_ref, fg_ref, out_ref, tail_ref, act_ref,
                     wb_ref, gt_ref, tl=tl, d_ff=d_ff, final=final, seg=tl)
    band = _chain(*[_band_steps(q_ref.at[0, ti * tq:(ti + 1) * tq, :], k_ref, v_ref, bb_ref,
                                obuf_ref.at[slot, ti * tq:(ti + 1) * tq, 0:512],
                                jm * (tl // tq) + ti, tq=tq) for ti in range(tl // tq)])
    ssd = _chain(*[_ssd_steps(z_ref.at[0, ci * c:(ci + 1) * c, :], xs_ref.at[0, ci * c:(ci + 1) * c, :],
                              bm_ref.at[0, ci * c:(ci + 1) * c, :], cm_ref.at[0, ci * c:(ci + 1) * c, :],
                              dt_ref.at[0, ci * c:(ci + 1) * c, :], cws_ref, cbs_ref, dtb_ref, alog_ref, dskip_ref,
                              ngs_ref, obuf_ref.at[slot, ci * c:(ci + 1) * c, 512:1024], buf_ref, st_ref, c=c)
                   for ci in range(tl // c)])
    _interleave([ffn, _chain(band, ssd)], (1, 1))

    @pl.when(g < n_tiles)
    def _():
        sn_ref[0] = st_ref[...]


def _cd_mixers_ffn(x, main, dt, bias, wo, g1, ng, sc, sh, g2, wup, cwf, cbf, wdn, fg,
                   conv_w, conv_b, dt_bias, a_log, d_skip, norm_g, tl, final):
    nseq, seq, d = x.shape
    d_ff = wdn.shape[0]
    tps = seq // tl
    n_tiles = nseq * tps
    c = SCAN_CHUNK
    tq = BAND_TQ
    pad = lambda a: jnp.pad(a.reshape(1, H_D), ((0, 0), (0, 128 - H_D)))
    mix = lambda g: jnp.minimum(g, n_tiles - 1)
    ffn = lambda g: jnp.maximum(g - 1, 0)
    tok_f = lambda n: pl.BlockSpec((1, tl, n), lambda g: (ffn(g) // tps, ffn(g) % tps, 0))
    seq_f = lambda n: pl.BlockSpec((1, 1, n), lambda g: (ffn(g) // tps, 0, 0))
    tok_m = lambda w, idx: pl.BlockSpec((1, tl, w), lambda g: (mix(g) // tps, mix(g) % tps, idx))
    full_m = lambda idx: pl.BlockSpec((1, seq, 512), lambda g: (mix(g) // tps, 0, idx))
    const = lambda r, w: pl.BlockSpec((r, w), lambda g: (0, 0))
    return pl.pallas_call(
        functools.partial(_cd_ffn_kernel, tl=tl, d_ff=d_ff, final=final, n_tiles=n_tiles, tiles_per_seq=tps,
                          c=c, tq=tq),
        grid=(n_tiles + 1,),
        in_specs=[tok_f(d), _resident(wo.shape), seq_f(d), _resident((1, d)), seq_f(d), seq_f(d), seq_f(d),
                  _resident(wup.shape), _resident(cwf.shape), _resident((1, d_ff)), _resident(wdn.shape),
                  _resident((1, d)),
                  tok_m(512, 0), full_m(1), full_m(2), _resident(bias.shape),
                  tok_m(512, 3), tok_m(512, 4), tok_m(256, 10), tok_m(256, 11), tok_m(128, 0),
                  const(CONV_D, CONV_DIM_D), const(1, CONV_DIM_D), const(1, 128), const(1, 128),
                  pl.BlockSpec(memory_space=pltpu.SMEM), const(1, D_INNER)],
        out_specs=[tok_f(d),
                   pl.BlockSpec((1, 8, d_ff), lambda g: (ffn(g) // tps, 0, 0)),
                   pl.BlockSpec((1, H_D * P_D, N_D), lambda g: (mix(g) // tps, 0, 0))],
        out_shape=[jax.ShapeDtypeStruct((nseq, seq, d), F32), jax.ShapeDtypeStruct((nseq, 8, d_ff), F32),
                   jax.ShapeDtypeStruct((nseq, H_D * P_D, N_D), F32)],
        scratch_shapes=[pltpu.VMEM((2, tl, 1024), BF16), pltpu.VMEM((tl, d_ff), BF16),
                        pltpu.VMEM((tl + 8, FF_CHUNK), F32), pltpu.VMEM((1, 8, d_ff), F32),
                        pltpu.VMEM((c + 8, CONV_DIM_D), F32), pltpu.VMEM((H_D * P_D, N_D), F32)],
        compiler_params=_params(("arbitrary",)),
        name="cd_mixers_ffn",
    )(x, wo, g1, ng, sc, sh, g2, wup, cwf, cbf.reshape(1, d_ff), wdn, fg,
      main, main, main, bias,
      main, main, main, main, dt, conv_w, conv_b.reshape(1, CONV_DIM_D), pad(dt_bias), pad(a_log), d_skip,
      norm_g.reshape(1, D_INNER))


def _rope_tables(pos):
    half = DK_A // 2
    inv = jnp.power(ROPE_BASE, -jnp.arange(half, dtype=F32) / half)
    ang = pos.astype(F32)[:, None] * inv[None, :]
    cos = jnp.cos(ang)
    sin = jnp.sin(ang)
    return jnp.concatenate([cos, cos], axis=1), jnp.concatenate([-sin, sin], axis=1)


def _pad_rows(a, rows=8):
    return jnp.pad(a, ((0, 0), (rows - a.shape[1], 0), (0, 0)))


def _trunk(x, mods, pos, weights, caches, biases):
    nseq, seq, d = x.shape
    sample = caches is not None
    c = min(seq, SCAN_CHUNK)
    if sample:
        tl, seg = nseq * seq, seq
        pack = lambda a: a.reshape(1, nseq * seq, a.shape[-1])
        unpack = lambda a: a.reshape(nseq, seq, a.shape[-1])
        rows = lambda m: jnp.repeat(m, seq, axis=1).reshape(1, nseq * seq, d)
    else:
        tl = seg = min(seq, 512)
        pack = unpack = rows = lambda a: a
    tl_in = tl if sample else min(seq, INPROJ_TL)
    seg_in = seg if sample else tl_in
    depth = weights["w_up"].shape[0]
    outs = {k: [] for k in ("ret", "bk", "bv", "ck", "cv", "dconv", "dssm", "ffn")}
    cos, sin = _rope_tables(pos)
    if sample:
        cos, sin = jnp.tile(cos, (nseq, 1)), jnp.tile(sin, (nseq, 1))
    x = pack(x)
    for l in range(depth):
        i = l // 2
        sh1, sc1, g1, sh2, sc2, g2 = [rows(m) for m in mods[l]]
        ng1 = weights["norm_g"][l, 0].reshape(1, d)
        ng2 = weights["norm_g"][l, 1].reshape(1, d)
        fused = None
        if l % 2 == 0:
            qkv, kb, vb = _inproj_ab(x, ng1, sc1, sh1, weights["w_in_ab"][i], cos, sin, tl_in)
            qkv = unpack(qkv)
            s0 = caches["ret"][i] if sample else jnp.zeros((nseq, H_A, DK_A, DV_A), F32)
            o1, s_new = _retention(qkv, s0, weights["ret_gn"][i], c)
            lam_init = 0.8 - 0.6 * math.exp(-0.3 * l)
            if sample:
                o2 = _diff_attn_sample(qkv, caches["bk"][i], caches["bv"][i], biases["t5"],
                                       weights["lam_q"][i], weights["lam_k"][i], weights["diff_gn"][i], lam_init)
            else:
                o2 = _diff_attn(qkv, biases["t5"], weights["lam_q"][i], weights["lam_k"][i],
                                weights["diff_gn"][i], lam_init)
            wo = weights["w_out_ab"][i]
            outs["ret"].append(s_new)
            outs["bk"].append(kb.reshape(nseq, seq, H_B, 2 * DH_B))
            outs["bv"].append(vb.reshape(nseq, seq, H_B, DV_B))
        else:
            keep = seq if sample else min(C_WINDOW, seq)
            main, dt, kc, vc, tail = _inproj_cd(x, ng1, sc1, sh1, weights["w_in_cd"][i], tl_in,
                                                keep * (tl // seg), seg_in)
            main, dt = unpack(main), unpack(dt)
            if sample:
                o1 = _band_attn_sample(main, caches["ck"][i], caches["cv"][i], biases["band"][i])
                conv_past = _pad_rows(caches["dconv"][i])
                s0 = caches["dssm"][i].reshape(nseq, H_D * P_D, N_D)
                o2, ssm_new = _ssd(main, dt, weights["d_conv_w"][i], weights["d_conv_b"][i], weights["d_dt_bias"][i],
                                   weights["d_a_log"][i], weights["d_skip"][i], weights["d_norm_g"][i], conv_past, s0,
                                   c)
            else:
                fused = _cd_mixers_ffn(
                    x, main, dt, biases["band"][i], weights["w_out_cd"][i], g1, ng2, sc2, sh2, g2, weights["w_up"][l],
                    weights["ffn_conv_w"][l], weights["ffn_conv_b"][l], weights["w_down"][l],
                    weights["final_g"].reshape(1, d), weights["d_conv_w"][i], weights["d_conv_b"][i],
                    weights["d_dt_bias"][i], weights["d_a_log"][i], weights["d_skip"][i], weights["d_norm_g"][i],
                    FUSED_TL, final=(l == depth - 1))
                ssm_new = fused[2]
            wo = weights["w_out_cd"][i]
            outs["ck"].append(kc.reshape(nseq, keep, H_C, DH_C))
            outs["cv"].append(vc.reshape(nseq, keep, H_C, DH_C))
            outs["dconv"].append(tail[:, 8 - (CONV_D - 1):])
            outs["dssm"].append(ssm_new.reshape(nseq, H_D, P_D, N_D))
        d_ff = weights["w_down"].shape[1]
        if fused is not None:
            x, ftail = fused[0], fused[1]
        else:
            ffn_past = _pad_rows(caches["ffn"][l]) if sample else jnp.zeros((nseq, 8, d_ff), F32)
            x, ftail = _outproj_ffn(x, pack(o1), pack(o2), wo, g1, ng2, sc2, sh2, g2, weights["w_up"][l],
                                    weights["ffn_conv_w"][l], weights["ffn_conv_b"][l], weights["w_down"][l], ffn_past,
                                    weights["final_g"].reshape(1, d), tl, seg, final=(l == depth - 1))
        outs["ffn"].append(ftail[:, 8 - (CONV_F - 1):])
    stk = lambda t: jnp.stack(t).astype(F32)
    x = unpack(x)
    return (x,) + tuple(stk(outs[k]) for k in ("ret", "bk", "bv", "ck", "cv", "dconv", "dssm", "ffn"))


def kernel(x_prompt, x_sample, cache_ret_state, cache_b_k, cache_b_v, cache_c_k, cache_c_v, state_d_conv, state_d_ssm, state_ffn_conv, c_prompt, c_sample, w_mod, b_mod, norm_g, final_g, t5_table, w_in_ab, w_out_ab, ret_gn, lam_q, lam_k, diff_gn, w_in_cd, w_out_cd, rel_table, d_conv_w, d_conv_b, d_dt_bias, d_a_log, d_skip, d_norm_g, w_up, ffn_conv_w, ffn_conv_b, w_down):
    batch, seq, d = x_prompt.shape
    dec_batch, dec_seq, _ = x_sample.shape
    past = cache_b_k.shape[2]
    depth = w_mod.shape[0]
    assert dec_seq <= CHUNK and past % CHUNK == 0 and cache_c_k.shape[2] == C_WINDOW
    assert seq % BAND_TQ == 0 and seq % DIFF_TQ == 0

    w_in_cd_p = jnp.pad(w_in_cd, ((0, 0), (0, 0), (0, N_CD_PAD - w_in_cd.shape[-1]))).astype(BF16)
    weights = dict(
        norm_g=norm_g, final_g=final_g, w_in_ab=w_in_ab.astype(BF16), w_out_ab=w_out_ab.astype(BF16),
        ret_gn=ret_gn, lam_q=lam_q, lam_k=lam_k, diff_gn=diff_gn, w_in_cd=w_in_cd_p,
        w_out_cd=w_out_cd.astype(BF16), d_conv_w=d_conv_w, d_conv_b=d_conv_b, d_dt_bias=d_dt_bias, d_a_log=d_a_log,
        d_skip=d_skip, d_norm_g=d_norm_g, w_up=w_up.astype(BF16), ffn_conv_w=ffn_conv_w, ffn_conv_b=ffn_conv_b,
        w_down=w_down.astype(BF16))

    mod = _modulation(jnp.concatenate([c_prompt, c_sample], axis=0), w_mod, b_mod)

    def pieces(rows):
        return [[m[:, None, :] for m in jnp.split(mod[l, rows], 6, axis=-1)] for l in range(depth)]

    band = [_band_bias(rel_table[i]) for i in range(rel_table.shape[0])]
    biases_p = dict(t5=_t5_bias_prompt(t5_table, seq), band=band)
    biases_s = dict(t5=_t5_bias_sample(t5_table, past, dec_seq),
                    band=[b[:, :dec_seq, :C_WINDOW + dec_seq] for b in band])
    caches = dict(
        ret=cache_ret_state,
        bk=cache_b_k.reshape(cache_b_k.shape[0], dec_batch, past, H_B * 2 * DH_B),
        bv=cache_b_v.reshape(cache_b_v.shape[0], dec_batch, past, H_B * DV_B),
        ck=cache_c_k.reshape(cache_c_k.shape[0], dec_batch, C_WINDOW, H_C * DH_C),
        cv=cache_c_v.reshape(cache_c_v.shape[0], dec_batch, C_WINDOW, H_C * DH_C),
        dconv=state_d_conv, dssm=state_d_ssm, ffn=state_ffn_conv)

    pos_p = jnp.arange(seq, dtype=jnp.int32)
    pos_s = past + jnp.arange(dec_seq, dtype=jnp.int32)
    y_p, ret_p, bk_p, bv_p, ck_p, cv_p, dconv_p, dssm_p, ffn_p = _trunk(
        x_prompt, pieces(slice(0, batch)), pos_p, weights, None, biases_p)
    y_s, ret_s, bk_s, bv_s, ck_s, cv_s, dconv_s, dssm_s, ffn_s = _trunk(
        x_sample, pieces(slice(batch, batch + dec_batch)), pos_s, weights, caches, biases_s)
    return (y_p, y_s, ret_p, ret_s, bk_p, bk_s, bv_p, bv_s, ck_p, ck_s, cv_p, cv_s,
            dconv_p, dconv_s, dssm_p, dssm_s, ffn_p, ffn_s)
```

```python
import functools
import math

import numpy as np
import jax
import jax.numpy as jnp
from jax import lax
from jax.experimental import pallas as pl
from jax.experimental.pallas import tpu as pltpu

F32 = jnp.float32
BF16 = jnp.bfloat16

CHUNK = 64
EPS = 1e-6
NEG = -1e30
H_A, DK_A, DV_A = 4, 128, 128
ROPE_BASE = 10000.0
H_B, DH_B, DV_B = 4, 64, 128
T5_BUCKETS, T5_MAX_DIST = 32, 128
H_C, DH_C = 8, 64
BAND_CHUNKS = 8
C_WINDOW = BAND_CHUNKS * CHUNK
REL_CLIP = 128
H_D, P_D, G_D, N_D = 8, 64, 2, 128
D_INNER = H_D * P_D
CONV_D = 4
CONV_DIM_D = D_INNER + 2 * G_D * N_D
CONV_F = 3
N_AB_COLS = 7 * 512
N_CD_MAIN = 6 * 512
N_CD_PAD = N_CD_MAIN + 128
FF_CHUNK = 256
BAND_TQ = 256
DIFF_TQ = 256
SCAN_CHUNK = 256
FUSED_TL = 256
INPROJ_TL = 1024
VMEM_LIMIT = 56 * 1024 * 1024
LOG2E = math.log2(math.e)
Q_B_SCALE = DH_B ** -0.5 * LOG2E
Q_C_SCALE = DH_C ** -0.5 * LOG2E


def _mm(a, b):
    return jnp.dot(a, b, preferred_element_type=F32)


def _mm_nt(a, b):
    return lax.dot_general(a, b, (((1,), (1,)), ((), ())), preferred_element_type=F32)


def _mm_tn(a, b):
    return lax.dot_general(a, b, (((0,), (0,)), ((), ())), preferred_element_type=F32)


def _mm_exact(a, b):
    return jnp.dot(a, b, preferred_element_type=F32, precision=lax.Precision.HIGHEST)


def _mm_nt_exact(a, b):
    return lax.dot_general(a, b, (((1,), (1,)), ((), ())), preferred_element_type=F32,
                           precision=lax.Precision.HIGHEST)


def _silu(x):
    return x * (1.0 / (1.0 + jnp.exp(-x)))


def _softplus(x):
    return jnp.maximum(x, 0.0) + jnp.log1p(jnp.exp(-jnp.abs(x)))


def _gelu_tanh(x):
    k1 = -2.0 * math.sqrt(2.0 / math.pi) * math.log2(math.e)
    return x * (1.0 / (1.0 + jnp.exp2(x * (k1 + (k1 * 0.044715) * (x * x)))))


def _norm_mod(x, g, sc, sh):
    ms = jnp.mean(x * x, axis=-1, keepdims=True)
    return (x * lax.rsqrt(ms + EPS)) * g * (1.0 + sc) + sh


def _mod_spec(a, tl):
    if a.shape[1] == 1:
        return pl.BlockSpec((1, 1, a.shape[2]), lambda b, l: (b, 0, 0))
    return pl.BlockSpec((1, tl, a.shape[2]), lambda b, l: (b, l, 0))


def _params(sem):
    return pltpu.CompilerParams(dimension_semantics=sem, vmem_limit_bytes=VMEM_LIMIT)


def _resident(shape):
    nd = len(shape)
    return pl.BlockSpec(shape, lambda *_: (0,) * nd, pipeline_mode=pl.Buffered(1))


def _mod_kernel(c_ref, w_ref, b_ref, o_ref):
    c = c_ref[...]
    o_ref[0] = _mm(_silu(c).astype(BF16), w_ref[0].astype(BF16)) + b_ref[0]


def _modulation(c_all, w_mod, b_mod):
    depth, d, n = w_mod.shape
    r = c_all.shape[0]
    tn = 1536
    return pl.pallas_call(
        _mod_kernel,
        grid=(depth, n // tn),
        in_specs=[pl.BlockSpec((r, d), lambda l, j: (0, 0)),
                  pl.BlockSpec((1, d, tn), lambda l, j: (l, 0, j)),
                  pl.BlockSpec((1, 1, tn), lambda l, j: (l, 0, j))],
        out_specs=pl.BlockSpec((1, r, tn), lambda l, j: (l, 0, j)),
        out_shape=jax.ShapeDtypeStruct((depth, r, n), F32),
        compiler_params=_params(("parallel", "parallel")),
        name="modulation",
    )(c_all, w_mod, b_mod.reshape(depth, 1, n))


def _table_gather_kernel(tab_ref, idx_ref, o_ref, *, n_entries, n_heads):
    idx = idx_ref[...]

    def body(r, accs):
        m = idx == r
        return tuple(jnp.where(m, tab_ref[hh, r], a) for hh, a in enumerate(accs))

    accs = lax.fori_loop(0, n_entries, body, tuple(jnp.zeros(idx.shape, F32) for _ in range(n_heads)))
    for hh in range(n_heads):
        o_ref[hh:hh + 1, :] = accs[hh]


def _table_gather(table, idx):
    t, h = table.shape
    w = idx.shape[0]
    return pl.pallas_call(
        functools.partial(_table_gather_kernel, n_entries=t, n_heads=h),
        in_specs=[pl.BlockSpec(memory_space=pltpu.SMEM),
                  pl.BlockSpec((1, w), lambda: (0, 0))],
        out_specs=pl.BlockSpec((h, w), lambda: (0, 0)),
        out_shape=jax.ShapeDtypeStruct((h, w), F32),
        name="table_gather",
    )(table.T, idx.reshape(1, w))


def _toeplitz_kernel(v_ref, o_ref, *, rows, cols, tile):
    x = pltpu.roll(jnp.broadcast_to(v_ref[0], (rows, v_ref.shape[2])), 0, 1, stride=1, stride_axis=0)
    if tile is None:
        o_ref[0] = x[:, :cols]
    else:
        for jb in range(cols // tile):
            o_ref[0, jb] = x[:, jb * tile:(jb + 1) * tile]


def _toeplitz(vec, rows, cols, tile=None):
    h, wv = vec.shape
    if tile is None:
        out_shape, block, imap = (h, rows, cols), (1, rows, cols), lambda i: (i, 0, 0)
    else:
        out_shape, block, imap = (h, cols // tile, rows, tile), (1, cols // tile, rows, tile), lambda i: (i, 0, 0, 0)
    return pl.pallas_call(
        functools.partial(_toeplitz_kernel, rows=rows, cols=cols, tile=tile),
        grid=(h,),
        in_specs=[pl.BlockSpec((1, 1, wv), lambda i: (i, 0, 0))],
        out_specs=pl.BlockSpec(block, imap),
        out_shape=jax.ShapeDtypeStruct(out_shape, F32),
        compiler_params=_params(("parallel",)),
        name="toeplitz",
    )(vec.reshape(h, 1, wv))


def _t5_bucket(rel):
    half = T5_BUCKETS // 2
    max_exact = half // 2
    base = jnp.where(rel > 0, half, 0)
    n = jnp.abs(rel)
    nf = jnp.maximum(n, 1).astype(F32)
    large = max_exact + (jnp.log(nf / max_exact) / math.log(T5_MAX_DIST / max_exact) * (half - max_exact)).astype(jnp.int32)
    large = jnp.minimum(large, half - 1)
    return base + jnp.where(n < max_exact, n, large)


def _wrapped_offsets(n_pos, n_neg):
    p = jnp.arange(n_pos + n_neg, dtype=jnp.int32)
    return jnp.where(p < n_pos, p, p - (n_pos + n_neg))


def _t5_bias_prompt(t5_table, seq):
    nd = seq // DIFF_TQ
    u = _wrapped_offsets(seq, DIFF_TQ)
    vec = _table_gather(t5_table, _t5_bucket(u - (seq - DIFF_TQ))) * LOG2E
    return _toeplitz(vec, DIFF_TQ, seq, tile=DIFF_TQ)


def _t5_bias_sample(t5_table, past, lq):
    u = _wrapped_offsets(past + lq, lq)
    vec = _table_gather(t5_table, _t5_bucket(u - past)) * LOG2E
    return _toeplitz(vec, lq, past + 2 * lq)[:, :, :past + lq]


def _band_bias(rel_table):
    u = _wrapped_offsets(3 * BAND_TQ, BAND_TQ)
    idx = jnp.clip(2 * BAND_TQ - u, -REL_CLIP, REL_CLIP) + REL_CLIP
    bias = _toeplitz(_table_gather(rel_table, idx) * LOG2E, BAND_TQ, 3 * BAND_TQ)
    dc = (np.arange(BAND_TQ)[:, None] // CHUNK + C_WINDOW // CHUNK) - np.arange(3 * BAND_TQ)[None, :] // CHUNK
    return jnp.where(jnp.asarray((dc >= 0) & (dc <= BAND_CHUNKS))[None], bias, NEG)


def _inproj_ab_kernel(x_ref, g_ref, sc_ref, sh_ref, w_ref, cos_ref, sin_ref, qkv_ref, kb_ref, vb_ref, *, tl):
    h = _norm_mod(x_ref[0], g_ref[...], sc_ref[0], sh_ref[0]).astype(BF16)
    cos = cos_ref[...]
    sin = sin_ref[...]
    for j in range(7):
        y = _mm(h, w_ref[:, j * 512:(j + 1) * 512])
        if j < 2:
            for hh in range(H_A):
                yh = y[:, hh * 128:(hh + 1) * 128]
                yh = yh * cos + pltpu.roll(yh, 64, 1) * sin
                if j == 1:
                    yh = yh * (DK_A ** -0.5)
                qkv_ref[0, :, j * 512 + hh * 128:j * 512 + (hh + 1) * 128] = yh.astype(BF16)
        else:
            if j == 4:
                y = y * Q_B_SCALE
            if j >= 5:
                dst = kb_ref if j == 5 else vb_ref
                for hh in range(H_B):
                    dst[0, pl.ds(hh, tl, stride=H_B), :] = y[:, hh * 128:(hh + 1) * 128]
            qkv_ref[0, :, j * 512:(j + 1) * 512] = y.astype(BF16)


def _inproj_ab(x, g, sc, sh, w, cos, sin, tl):
    nseq, seq, d = x.shape
    grid = (nseq, seq // tl)
    tok = lambda n: pl.BlockSpec((1, tl, n), lambda b, l: (b, l, 0))
    cache = pl.BlockSpec((1, tl * H_B, 128), lambda b, l: (b, l, 0))
    return pl.pallas_call(
        functools.partial(_inproj_ab_kernel, tl=tl),
        grid=grid,
        in_specs=[tok(d), _resident((1, d)), _mod_spec(sc, tl), _mod_spec(sh, tl), _resident((d, N_AB_COLS)),
                  pl.BlockSpec((tl, 128), lambda b, l: (l, 0)),
                  pl.BlockSpec((tl, 128), lambda b, l: (l, 0))],
        out_specs=[tok(N_AB_COLS), cache, cache],
        out_shape=[jax.ShapeDtypeStruct((nseq, seq, N_AB_COLS), BF16),
                   jax.ShapeDtypeStruct((nseq, seq * H_B, 128), F32),
                   jax.ShapeDtypeStruct((nseq, seq * H_B, 128), F32)],
        compiler_params=_params(("parallel", "parallel")),
        name="inproj_ab",
    )(x, g, sc, sh, w, cos, sin)


def _retention_kernel(q_ref, k_ref, v_ref, g_ref, s0_ref, dec_ref, qd_ref, kd_ref, gc_ref, gn_ref,
                      o_ref, sn_ref, *, c, n_chunks):
    states = [s0_ref[0, h] for h in range(H_A)]
    for ci in range(n_chunks):
        rows = slice(ci * c, (ci + 1) * c)
        for h in range(H_A):
            cs = slice(h * 128, (h + 1) * 128)
            q = q_ref[0, rows, cs]
            k = k_ref[0, rows, cs]
            v = v_ref[0, rows, cs]
            state = states[h]
            s = _mm_nt(q, k) * dec_ref[h]
            o = _mm(s.astype(BF16), v) + _mm((q.astype(F32) * qd_ref[h]).astype(BF16), state.astype(BF16))
            states[h] = gc_ref[h] * state + _mm_tn((k.astype(F32) * kd_ref[h]).astype(BF16), v)
            mu = jnp.mean(o, axis=-1, keepdims=True)
            dlt = o - mu
            var = jnp.mean(dlt * dlt, axis=-1, keepdims=True)
            y = dlt * lax.rsqrt(var + EPS) * gn_ref[:, cs] * _silu(g_ref[0, rows, cs].astype(F32))
            o_ref[0, rows, cs] = y.astype(BF16)
    for h in range(H_A):
        sn_ref[0, h] = states[h]


def _retention_consts(c):
    lg = np.log1p(-np.exp2(-5.0 - np.arange(H_A, dtype=np.float32))).astype(np.float32)
    idx = np.arange(c, dtype=np.float32)
    diff = idx[:, None] - idx[None, :]
    decay = np.where(diff[None] >= 0, np.exp(np.maximum(diff, 0.0)[None] * lg[:, None, None]), 0.0)
    qd = np.exp((idx + 1.0)[None, :] * lg[:, None])
    kd = np.exp((c - 1.0 - idx)[None, :] * lg[:, None])
    gc = np.exp(c * lg)
    bc = lambda a: np.ascontiguousarray(np.broadcast_to(a[..., None], a.shape + (128,))).astype(np.float32)
    return decay.astype(np.float32), bc(qd), bc(kd), bc(gc[:, None])


def _retention(qkv, s0, ret_gn, c):
    nseq, seq, _ = qkv.shape
    decay, qd, kd, gc = _retention_consts(c)
    w = H_A * 128
    col = lambda idx: pl.BlockSpec((1, seq, w), lambda b: (b, 0, idx))
    const = lambda shape: pl.BlockSpec(shape, lambda b: (0,) * len(shape))
    state = pl.BlockSpec((1, H_A, DK_A, DV_A), lambda b: (b, 0, 0, 0))
    return pl.pallas_call(
        functools.partial(_retention_kernel, c=c, n_chunks=seq // c),
        grid=(nseq,),
        in_specs=[col(0), col(1), col(2), col(3), state, const((H_A, c, c)), const((H_A, c, 128)),
                  const((H_A, c, 128)), const((H_A, 1, 128)), const((1, w))],
        out_specs=[col(0), state],
        out_shape=[jax.ShapeDtypeStruct((nseq, seq, w), BF16),
                   jax.ShapeDtypeStruct((nseq, H_A, DK_A, DV_A), F32)],
        compiler_params=_params(("parallel",)),
        name="retention",
    )(qkv, qkv, qkv, qkv, s0, decay, qd, kd, gc, ret_gn.reshape(1, w))


def _split_halves(q):
    lane = lax.broadcasted_iota(jnp.int32, q.shape, 1)
    zero = jnp.zeros_like(q)
    return jnp.where(lane < 64, q, zero), jnp.where(lane >= 64, q, zero)


def _lambda(lq_ref, lk_ref, lam_init):
    e = jnp.exp(jnp.sum(lq_ref[...] * lk_ref[...], axis=1, keepdims=True))
    return e[0:1, :] - e[1:2, :] + lam_init


def _diff_epilogue(o, gn, lam_init):
    ms = jnp.mean(o * o, axis=-1, keepdims=True)
    return (o * lax.rsqrt(ms + EPS)) * gn * (1.0 - lam_init)


def _diff_attn_kernel(q_ref, k_ref, v_ref, b_ref, lq_ref, lk_ref, gn_ref, o_ref, s_ref, qs_ref, mx_ref, ls_ref,
                      acc_ref, *, tq, nd, lam_init):
    row = lax.broadcasted_iota(jnp.int32, (tq, tq), 0) // CHUNK
    col = lax.broadcasted_iota(jnp.int32, (tq, tq), 1) // CHUNK
    chunk_mask = col <= row
    lam = _lambda(lq_ref, lk_ref, lam_init)

    def prepare(t):
        halves = _split_halves(q_ref[0, t * tq:(t + 1) * tq, :])
        for i in range(2):
            r = 2 * (t % 2) + i
            qs_ref[r] = halves[i]
            mx_ref[r] = jnp.full((tq, 128), NEG, F32)
            ls_ref[r] = jnp.zeros((tq, 128), F32)
            acc_ref[r] = jnp.zeros((tq, DV_B), F32)

    def far_bias(t, kb):
        if (t - kb - 1) * tq + 1 < T5_MAX_DIST:
            return None
        return b_ref[0, nd - 1 - t + kb, 0:1, 0:1]

    def scores(t, kb):
        k = k_ref[0, kb * tq:(kb + 1) * tq, :]
        far = far_bias(t, kb)
        for i in range(2):
            r = 2 * (t % 2) + i
            s = _mm_nt(qs_ref[r], k)
            if far is None:
                s = s + b_ref[0, nd - 1 - t + kb]
            if kb == t:
                s = jnp.where(chunk_mask, s, NEG)
            s_ref[r * nd + kb] = s
            m = s[:, 0:128]
            for c0 in range(128, tq, 128):
                m = jnp.maximum(m, s[:, c0:c0 + 128])
            mx_ref[r] = jnp.maximum(mx_ref[r], m if far is None else m + far)

    def row_max(t):
        for i in range(2):
            r = 2 * (t % 2) + i
            mx_ref[r] = jnp.broadcast_to(jnp.max(mx_ref[r], axis=1, keepdims=True), (tq, 128))

    def weigh(t, kb):
        v = v_ref[0, kb * tq:(kb + 1) * tq, :]
        far = far_bias(t, kb)
        for i in range(2):
            r = 2 * (t % 2) + i
            m = mx_ref[r] if far is None else mx_ref[r] - far
            l = ls_ref[r]
            ps = []
            for c0 in range(0, tq, 128):
                p = jnp.exp2(s_ref[r * nd + kb, :, c0:c0 + 128] - m)
                l = l + p
                ps.append(p.astype(BF16))
            ls_ref[r] = l
            acc_ref[r] += _mm(jnp.concatenate(ps, axis=1), v)

    def finish(t):
        r = 2 * (t % 2)
        l0 = jnp.sum(ls_ref[r], axis=1, keepdims=True)
        l1 = jnp.sum(ls_ref[r + 1], axis=1, keepdims=True)
        o = acc_ref[r] / l0 - lam * (acc_ref[r + 1] / l1)
        o_ref[0, t * tq:(t + 1) * tq, :] = _diff_epilogue(o, gn_ref[...], lam_init).astype(BF16)

    last = nd - 1
    prepare(last)
    for kb in range(last + 1):
        scores(last, kb)
    row_max(last)
    for t in range(last - 1, -1, -1):
        prepare(t)
        for kb in range(t + 2):
            if kb <= t:
                scores(t, kb)
            weigh(t + 1, kb)
        finish(t + 1)
        row_max(t)
    weigh(0, 0)
    finish(0)


def _diff_attn(qkv, bias, lam_q, lam_k, diff_gn, lam_init):
    nseq, seq, _ = qkv.shape
    tq = DIFF_TQ
    nd = seq // tq
    full = lambda off: pl.BlockSpec((1, seq, 128), lambda b, h: (b, 0, off + h))
    small = lambda r, c: pl.BlockSpec((r, c), lambda b, h: (0, 0))
    return pl.pallas_call(
        functools.partial(_diff_attn_kernel, tq=tq, nd=nd, lam_init=lam_init),
        grid=(nseq, H_B),
        in_specs=[full(16), full(20), full(24),
                  pl.BlockSpec((1, nd, tq, tq), lambda b, h: (h, 0, 0, 0)),
                  small(2, DH_B), small(2, DH_B), small(1, DV_B)],
        out_specs=full(0),
        out_shape=jax.ShapeDtypeStruct((nseq, seq, H_B * DV_B), BF16),
        scratch_shapes=[pltpu.VMEM((4 * nd, tq, tq), F32), pltpu.VMEM((4, tq, 128), BF16),
                        pltpu.VMEM((4, tq, 128), F32), pltpu.VMEM((4, tq, 128), F32),
                        pltpu.VMEM((4, tq, DV_B), F32)],
        compiler_params=_params(("parallel", "parallel")),
        name="diff_attn",
    )(qkv, qkv, qkv, bias, lam_q, lam_k, diff_gn.reshape(1, DV_B))


def _diff_attn_sample_kernel(q_ref, kc_ref, vc_ref, kp_ref, vp_ref, b_ref, lq_ref, lk_ref, gn_ref, o_ref,
                             *, past, lam_init):
    lam = _lambda(lq_ref, lk_ref, lam_init)
    for h in range(H_B):
        cs = slice(h * 128, (h + 1) * 128)
        qs = _split_halves(q_ref[0, :, cs])
        kp = kp_ref[0, :, cs].astype(BF16)
        vp = vp_ref[0, :, cs].astype(BF16)
        kc = kc_ref[0, :, cs]
        vc = vc_ref[0, :, cs]
        b = b_ref[h]
        probs = []
        for i in range(2):
            sp = _mm_nt(qs[i], kp) + b[:, :past]
            sc = _mm_nt(qs[i], kc) + b[:, past:]
            m = jnp.maximum(jnp.max(sp, axis=1, keepdims=True), jnp.max(sc, axis=1, keepdims=True))
            pp = jnp.exp2(sp - m)
            pc = jnp.exp2(sc - m)
            l = jnp.sum(pp, axis=1, keepdims=True) + jnp.sum(pc, axis=1, keepdims=True)
            probs.append((pp / l, pc / l))
        ap = probs[0][0] - lam * probs[1][0]
        ac = probs[0][1] - lam * probs[1][1]
        o = _mm(ap.astype(BF16), vp) + _mm(ac.astype(BF16), vc)
        o_ref[0, :, cs] = _diff_epilogue(o, gn_ref[...], lam_init).astype(BF16)


def _diff_attn_sample(qkv, k_past, v_past, bias, lam_q, lam_k, diff_gn, lam_init):
    nseq, lq, _ = qkv.shape
    past = k_past.shape[1]
    w = H_B * DV_B
    cur = lambda idx: pl.BlockSpec((1, lq, w), lambda b: (b, 0, idx))
    old = pl.BlockSpec((1, past, w), lambda b: (b, 0, 0))
    small = lambda r, c: pl.BlockSpec((r, c), lambda b: (0, 0))
    return pl.pallas_call(
        functools.partial(_diff_attn_sample_kernel, past=past, lam_init=lam_init),
        grid=(nseq,),
        in_specs=[cur(4), cur(5), cur(6), old, old, _resident(bias.shape),
                  small(2, DH_B), small(2, DH_B), small(1, DV_B)],
        out_specs=cur(0),
        out_shape=jax.ShapeDtypeStruct((nseq, lq, w), BF16),
        compiler_params=_params(("parallel",)),
        name="diff_attn_sample",
    )(qkv, qkv, qkv, k_past, v_past, bias, lam_q, lam_k, diff_gn.reshape(1, DV_B))


def _inproj_cd_kernel(x_ref, g_ref, sc_ref, sh_ref, w_ref, main_ref, dt_ref, kc_ref, vc_ref, tail_ref,
                      *, tl, seg, keep):
    h = _norm_mod(x_ref[0], g_ref[...], sc_ref[0], sh_ref[0]).astype(BF16)
    for j in range(6):
        y = _mm(h, w_ref[:, j * 512:(j + 1) * 512])
        main_ref[0, :, j * 512:(j + 1) * 512] = (y * Q_C_SCALE if j == 0 else y).astype(BF16)
        if j == 1:
            kc_ref[0] = y[tl - keep:tl, :]
        if j == 2:
            vc_ref[0] = y[tl - keep:tl, :]
        if j >= 4:
            for s in range(tl // seg):
                tail_ref[s, :, (j - 4) * 512:(j - 3) * 512] = y[(s + 1) * seg - 8:(s + 1) * seg, :]
    dt_ref[0] = _mm(h, w_ref[:, N_CD_MAIN:N_CD_PAD])


def _inproj_cd(x, g, sc, sh, w, tl, keep, seg):
    nseq, seq, d = x.shape
    nseg = tl // seg
    assert keep <= tl and seq % tl == 0
    tok = lambda n: pl.BlockSpec((1, tl, n), lambda b, l: (b, l, 0))
    kept = pl.BlockSpec((1, keep, 512), lambda b, l: (b, 0, 0))
    return pl.pallas_call(
        functools.partial(_inproj_cd_kernel, tl=tl, seg=seg, keep=keep),
        grid=(nseq, seq // tl),
        in_specs=[tok(d), _resident((1, d)), _mod_spec(sc, tl), _mod_spec(sh, tl), _resident((d, N_CD_PAD))],
        out_specs=[tok(N_CD_MAIN), tok(128), kept, kept,
                   pl.BlockSpec((nseg, 8, CONV_DIM_D), lambda b, l: (b, 0, 0))],
        out_shape=[jax.ShapeDtypeStruct((nseq, seq, N_CD_MAIN), BF16),
                   jax.ShapeDtypeStruct((nseq, seq, 128), F32),
                   jax.ShapeDtypeStruct((nseq, keep, 512), F32),
                   jax.ShapeDtypeStruct((nseq, keep, 512), F32),
                   jax.ShapeDtypeStruct((nseq * nseg, 8, CONV_DIM_D), F32)],
        compiler_params=_params(("parallel", "arbitrary")),
        name="inproj_cd",
    )(x, g, sc, sh, w)


def _band_sample_kernel(q_ref, kc_ref, vc_ref, kp_ref, vp_ref, b_ref, o_ref, *, past):
    for pr in range(H_C // 2):
        cs = slice(pr * 128, (pr + 1) * 128)
        halves = _split_halves(q_ref[0, :, cs])
        kp = kp_ref[0, :, cs].astype(BF16)
        vp = vp_ref[0, :, cs].astype(BF16)
        kc = kc_ref[0, :, cs]
        vc = vc_ref[0, :, cs]
        outs = []
        for hi, qh in enumerate(halves):
            b = b_ref[2 * pr + hi]
            sp = _mm_nt(qh, kp) + b[:, :past]
            sc = _mm_nt(qh, kc) + b[:, past:]
            m = jnp.maximum(jnp.max(sp, axis=1, keepdims=True), jnp.max(sc, axis=1, keepdims=True))
            pp = jnp.exp2(sp - m)
            pc = jnp.exp2(sc - m)
            l = jnp.sum(pp, axis=1, keepdims=True) + jnp.sum(pc, axis=1, keepdims=True)
            outs.append(_mm((pp / l).astype(BF16), vp) + _mm((pc / l).astype(BF16), vc))
        lane = lax.broadcasted_iota(jnp.int32, outs[0].shape, 1)
        o_ref[0, :, cs] = jnp.where(lane < 64, outs[0], outs[1]).astype(BF16)


def _band_attn_sample(main, k_past, v_past, bias):
    nseq, lq, _ = main.shape
    past = k_past.shape[1]
    w = H_C * DH_C
    cur = lambda idx: pl.BlockSpec((1, lq, w), lambda b: (b, 0, idx))
    old = pl.BlockSpec((1, past, w), lambda b: (b, 0, 0))
    return pl.pallas_call(
        functools.partial(_band_sample_kernel, past=past),
        grid=(nseq,),
        in_specs=[cur(0), cur(1), cur(2), old, old, _resident(bias.shape)],
        out_specs=cur(0),
        out_shape=jax.ShapeDtypeStruct((nseq, lq, w), BF16),
        compiler_params=_params(("parallel",)),
        name="band_attn_sample",
    )(main, main, main, k_past, v_past, bias)


def _ssd_kernel(z_ref, xs_ref, bm_ref, cm_ref, dt_ref, cw_ref, cb_ref, dtb_ref, alog_ref, dskip_ref, ng_ref,
                past_ref, s0_ref, o_ref, sn_ref, buf_ref, st_ref, *, c):
    @pl.when(pl.program_id(1) == 0)
    def _():
        buf_ref[0:8, :] = past_ref[0]
        st_ref[...] = s0_ref[0]

    for _ in _ssd_steps(z_ref.at[0], xs_ref.at[0], bm_ref.at[0], cm_ref.at[0], dt_ref.at[0], cw_ref, cb_ref, dtb_ref,
                        alog_ref, dskip_ref, ng_ref, o_ref.at[0], buf_ref, st_ref, c=c):
        pass
    sn_ref[0] = st_ref[...]


def _ssd(main, dt, conv_w, conv_b, dt_bias, a_log, d_skip, norm_g, conv_past, s0, c):
    nseq, seq, _ = main.shape
    pad = lambda a: jnp.pad(a.reshape(1, H_D), ((0, 0), (0, 128 - H_D)))
    blk = lambda w, idx: pl.BlockSpec((1, c, w), lambda b, l: (b, l, idx))
    const = lambda r, w: pl.BlockSpec((r, w), lambda b, l: (0, 0))
    state = pl.BlockSpec((1, H_D * P_D, N_D), lambda b, l: (b, 0, 0))
    return pl.pallas_call(
        functools.partial(_ssd_kernel, c=c),
        grid=(nseq, seq // c),
        in_specs=[blk(512, 3), blk(512, 4), blk(256, 10), blk(256, 11), blk(128, 0),
                  const(CONV_D, CONV_DIM_D), const(1, CONV_DIM_D), const(1, 128), const(1, 128),
                  pl.BlockSpec(memory_space=pltpu.SMEM), const(1, D_INNER),
                  pl.BlockSpec((1, 8, CONV_DIM_D), lambda b, l: (b, 0, 0)), state],
        out_specs=[blk(512, 0), state],
        out_shape=[jax.ShapeDtypeStruct((nseq, seq, D_INNER), BF16),
                   jax.ShapeDtypeStruct((nseq, H_D * P_D, N_D), F32)],
        scratch_shapes=[pltpu.VMEM((c + 8, CONV_DIM_D), F32), pltpu.VMEM((H_D * P_D, N_D), F32)],
        compiler_params=_params(("parallel", "arbitrary")),
        name="ssd",
    )(main, main, main, main, dt, conv_w, conv_b.reshape(1, CONV_DIM_D), pad(dt_bias), pad(a_log), d_skip,
      norm_g.reshape(1, D_INNER), conv_past, s0)


def _ffn_kernel(x_ref, o1_ref, o2_ref, wo_ref, g1_ref, ng_ref, sc_ref, sh_ref, g2_ref, wup_ref, cw_ref, cb_ref,
                wdn_ref, past_ref, fg_ref, out_ref, tail_ref, act_ref, wb_ref, gt_ref, h_ref,
                *, tl, d_ff, final, seg):
    @pl.when(pl.program_id(1) == 0)
    def _():
        gt_ref[...] = past_ref[...]

    for _ in _ffn_steps(x_ref, o1_ref.at[0], o2_ref.at[0], wo_ref, g1_ref, ng_ref, sc_ref, sh_ref, g2_ref, wup_ref,
                        cw_ref, cb_ref, wdn_ref, fg_ref, out_ref, tail_ref, act_ref, wb_ref, gt_ref, h_ref,
                        tl=tl, d_ff=d_ff, final=final, seg=seg):
        pass


def _outproj_ffn(x, o1, o2, wo, g1, ng, sc, sh, g2, wup, cw, cb, wdn, past, fg, tl, seg, final):
    nseq, seq, d = x.shape
    d_ff = wdn.shape[0]
    nseg = tl // seg
    assert nseg == 1 or seq == tl
    tok = lambda n: pl.BlockSpec((1, tl, n), lambda b, l: (b, l, 0))
    tail = pl.BlockSpec((nseg, 8, d_ff), lambda b, l: (b, 0, 0))
    return pl.pallas_call(
        functools.partial(_ffn_kernel, tl=tl, d_ff=d_ff, final=final, seg=seg),
        grid=(nseq, seq // tl),
        in_specs=[tok(d), tok(o1.shape[-1]), tok(o2.shape[-1]), _resident(wo.shape), _mod_spec(g1, tl),
                  _resident((1, d)), _mod_spec(sc, tl), _mod_spec(sh, tl), _mod_spec(g2, tl), _resident(wup.shape),
                  _resident(cw.shape), _resident((1, d_ff)), _resident(wdn.shape), tail, _resident((1, d))],
        out_specs=[tok(d), tail],
        out_shape=[jax.ShapeDtypeStruct((nseq, seq, d), F32), jax.ShapeDtypeStruct((nseq * nseg, 8, d_ff), F32)],
        scratch_shapes=[pltpu.VMEM((tl, d_ff), BF16), pltpu.VMEM((nseg * (seg + 8), FF_CHUNK), F32),
                        pltpu.VMEM((nseg, 8, d_ff), F32), pltpu.VMEM((tl, d), BF16)],
        compiler_params=_params(("parallel", "arbitrary")),
        name="outproj_ffn",
    )(x, o1, o2, wo, g1, ng, sc, sh, g2, wup, cw, cb.reshape(1, d_ff), wdn, past, fg)


def _interleave(gens, strides):
    live = list(zip(gens, strides))
    while live:
        for item in list(live):
            for _ in range(item[1]):
                try:
                    next(item[0])
                except StopIteration:
                    live.remove(item)
                    break


def _chain(*gens):
    for g in gens:
        yield from g


def _ffn_steps(x_ref, o1, o2, wo_ref, g1_ref, ng_ref, sc_ref, sh_ref, g2_ref, wup_ref, cw_ref, cb_ref, wdn_ref,
               fg_ref, out_ref, tail_ref, act_ref, wb_ref, gt_ref, h_ref, *, tl, d_ff, final, seg):
    nseg = tl // seg
    stride = seg + 8
    half = wo_ref.shape[0] // 2
    mix = _mm(o1[...], wo_ref[0:half, :]) + _mm(o2[...], wo_ref[half:2 * half, :])
    x1 = x_ref[0] + g1_ref[0] * mix
    out_ref[0] = x1
    h_ref[...] = _norm_mod(x1, ng_ref[...], sc_ref[0], sh_ref[0]).astype(BF16)
    yield
    split = (d_ff // FF_CHUNK + 1) // 2 * FF_CHUNK
    for j in range(d_ff // FF_CHUNK):
        c0 = j * FF_CHUNK
        a = _mm(h_ref[...], wup_ref[:, c0:c0 + FF_CHUNK])
        g = _mm(h_ref[...], wup_ref[:, d_ff + c0:d_ff + c0 + FF_CHUNK])
        for s in range(nseg):
            wb_ref[s * stride:s * stride + 8, :] = gt_ref[s, :, c0:c0 + FF_CHUNK]
            wb_ref[s * stride + 8:(s + 1) * stride, :] = g[s * seg:(s + 1) * seg, :]
            gt_ref[s, :, c0:c0 + FF_CHUNK] = g[(s + 1) * seg - 8:(s + 1) * seg, :]
        back2 = [wb_ref[pl.ds(s * stride + 6, seg), :] for s in range(nseg)]
        back1 = [wb_ref[pl.ds(s * stride + 7, seg), :] for s in range(nseg)]
        if nseg > 1:
            back2, back1 = [jnp.concatenate(back2, axis=0)], [jnp.concatenate(back1, axis=0)]
        gc = (cb_ref[:, c0:c0 + FF_CHUNK] + back2[0] * cw_ref[0:1, c0:c0 + FF_CHUNK]
              + back1[0] * cw_ref[1:2, c0:c0 + FF_CHUNK] + g * cw_ref[2:3, c0:c0 + FF_CHUNK])
        act_ref[:, c0:c0 + FF_CHUNK] = (a * _gelu_tanh(gc)).astype(BF16)
        if c0 + FF_CHUNK == split:
            dn = _mm(act_ref[:, 0:split], wdn_ref[0:split, :])
        yield
    dn = dn + _mm(act_ref[:, split:d_ff], wdn_ref[split:d_ff, :])
    x2 = out_ref[0] + g2_ref[0] * dn
    if final:
        ms = jnp.mean(x2 * x2, axis=-1, keepdims=True)
        x2 = (x2 * lax.rsqrt(ms + EPS)) * fg_ref[...]
    out_ref[0] = x2
    tail_ref[...] = gt_ref[...]
    yield


def _band_steps(q, k_ref, v_ref, b_ref, o, t, *, tq):
    lane = lax.broadcasted_iota(jnp.int32, (tq, 128), 1)
    starts, offs = [], []
    for d in range(3):
        kt = t - 2 + d
        starts.append(pl.multiple_of(jnp.maximum(kt, 0) * tq, tq))
        offs.append(jnp.where(kt >= 0, 0.0, NEG))

    def scores(head):
        pr, hi = divmod(head, 2)
        cs = slice(pr * 128, (pr + 1) * 128)
        qh = _split_halves(q[:, cs])[hi]
        ss = []
        for d in range(3):
            s = _mm_nt(qh, k_ref[0, pl.ds(starts[d], tq), cs]) + b_ref[head, :, d * tq:(d + 1) * tq]
            ss.append(s + offs[d] if d < 2 else s)
        mm = ss[2][:, 0:128]
        for d in range(3):
            for c0 in range(0, tq, 128):
                mm = jnp.maximum(mm, ss[d][:, c0:c0 + 128])
        return ss, jnp.max(mm, axis=1, keepdims=True)

    def weigh(head, ss, m):
        cs = slice((head // 2) * 128, (head // 2 + 1) * 128)
        ls = jnp.zeros((tq, 128), F32)
        acc = jnp.zeros((tq, 128), F32)
        for d in range(3):
            p = jnp.exp2(ss[d] - m)
            for c0 in range(0, tq, 128):
                ls = ls + p[:, c0:c0 + 128]
            acc = acc + _mm(p.astype(BF16), v_ref[0, pl.ds(starts[d], tq), cs])
        return acc / jnp.sum(ls, axis=1, keepdims=True)

    outs = []
    pending = scores(0)
    yield
    for head in range(H_C):
        nxt = scores(head + 1) if head + 1 < H_C else None
        outs.append(weigh(head, *pending))
        pending = nxt
        if head % 2 == 1:
            cs = slice((head // 2) * 128, (head // 2 + 1) * 128)
            o[:, cs] = jnp.where(lane < 64, outs[head - 1], outs[head]).astype(BF16)
        yield


def _ssd_steps(z, xs_in, bm_in, cm_in, dt_in, cw_ref, cb_ref, dtb_ref, alog_ref, dskip_ref, ng_ref, o,
               buf_ref, st_ref, *, c):
    buf_ref[8:8 + c, 0:D_INNER] = xs_in[...].astype(F32)
    buf_ref[8:8 + c, D_INNER:D_INNER + 256] = bm_in[...].astype(F32)
    buf_ref[8:8 + c, D_INNER + 256:CONV_DIM_D] = cm_in[...].astype(F32)
    conv = cb_ref[...] + buf_ref[pl.ds(5, c), :] * cw_ref[0:1, :]
    for i in range(1, CONV_D):
        conv = conv + buf_ref[pl.ds(5 + i, c), :] * cw_ref[i:i + 1, :]
    buf_ref[0:8, :] = buf_ref[c:c + 8, :]
    act = _silu(conv)
    xs = act[:, 0:D_INNER]
    bm = act[:, D_INNER:D_INNER + 256].astype(BF16)
    cm = act[:, D_INNER + 256:CONV_DIM_D].astype(BF16)
    yield

    dt = _softplus(dt_in[...] + dtb_ref[...])
    da = dt * (-jnp.exp(alog_ref[...]))
    row = lax.broadcasted_iota(jnp.int32, (c, c), 0)
    col = lax.broadcasted_iota(jnp.int32, (c, c), 1)
    tri = row >= col
    cum = _mm_exact(tri.astype(F32), da)
    eye = (lax.broadcasted_iota(jnp.int32, (8, 128), 0) == lax.broadcasted_iota(jnp.int32, (8, 128), 1)).astype(F32)
    cum_t = _mm_exact(_mm_nt_exact(eye, da), (row <= col).astype(F32))
    dt_t = _mm_nt_exact(eye, dt)
    ecum = jnp.exp(cum)
    last = cum[c - 1:c, :]
    wgt = jnp.exp(last - cum) * dt
    elast = jnp.exp(last)
    lane = lax.broadcasted_iota(jnp.int32, (c, 128), 1)
    low = lane < 64
    rlow = lax.broadcasted_iota(jnp.int32, (128, 128), 0) < 64
    yield

    for g in range(G_D):
        bm_g = bm[:, g * 128:(g + 1) * 128]
        cm_g = cm[:, g * 128:(g + 1) * 128]
        cb = _mm_nt(cm_g, bm_g)
        ys = []
        for pp in range(2):
            p = 2 * g + pp
            h0, h1 = 2 * p, 2 * p + 1
            x_f = xs[:, p * 128:(p + 1) * 128]
            x_b = x_f.astype(BF16)
            y_in = []
            for hh in (h0, h1):
                seg = cum[:, hh:hh + 1] - cum_t[hh:hh + 1, :]
                dec = jnp.exp(jnp.where(tri, seg, NEG))
                y_in.append(_mm((cb * dec * dt_t[hh:hh + 1, :]).astype(BF16), x_b))
            st = st_ref[p * 128:(p + 1) * 128, :]
            y_x = _mm_nt(cm_g, st.astype(BF16)) * jnp.where(low, ecum[:, h0:h0 + 1], ecum[:, h1:h1 + 1])
            w2 = jnp.where(low, wgt[:, h0:h0 + 1], wgt[:, h1:h1 + 1])
            st_ref[p * 128:(p + 1) * 128, :] = (jnp.where(rlow, elast[:, h0:h0 + 1], elast[:, h1:h1 + 1]) * st
                                                + _mm_tn((x_f * w2).astype(BF16), bm_g))
            y = jnp.where(low, y_in[0], y_in[1]) + y_x + jnp.where(low, dskip_ref[h0], dskip_ref[h1]) * x_f
            ys.append(y * _silu(z[:, p * 128:(p + 1) * 128].astype(F32)))
            yield
        ms = (jnp.sum(ys[0] * ys[0], axis=-1, keepdims=True) + jnp.sum(ys[1] * ys[1], axis=-1, keepdims=True)) / 256.0
        inv = lax.rsqrt(ms + EPS)
        for pp in range(2):
            p = 2 * g + pp
            o[:, p * 128:(p + 1) * 128] = (ys[pp] * inv * ng_ref[:, p * 128:(p + 1) * 128]).astype(BF16)
    yield


def _cd_ffn_kernel(x_ref, wo_ref, g1_ref, ng_ref, sc_ref, sh_ref, g2_ref, wup_ref, cwf_ref, cbf_ref, wdn_ref, fg_ref,
                   q_ref, k_ref, v_ref, bb_ref,
                   z_ref, xs_ref, bm_ref, cm_ref, dt_ref, cws_ref, cbs_ref, dtb_ref, alog_ref, dskip_ref, ngs_ref,
                   out_ref, tail_ref, sn_ref,
                   obuf_ref, act_ref, wb_ref, gt_ref, buf_ref, st_ref, h_ref,
                   *, tl, d_ff, final, n_tiles, tiles_per_seq, c, tq):
    g = pl.program_id(0)
    gm = jnp.minimum(g, n_tiles - 1)
    gf = jnp.maximum(g - 1, 0)
    jm = gm % tiles_per_seq
    jf = gf % tiles_per_seq
    slot = g % 2
    prev = (g + 1) % 2

    @pl.when(g == 0)
    def _():
        obuf_ref[...] = jnp.zeros(obuf_ref.shape, BF16)

    @pl.when(jm == 0)
    def _():
        buf_ref[0:8, :] = jnp.zeros((8, CONV_DIM_D), F32)
        st_ref[...] = jnp.zeros(st_ref.shape, F32)

    @pl.when(jf == 0)
    def _():
        gt_ref[...] = jnp.zeros(gt_ref.shape, F32)

    ffn = _ffn_steps(x_ref, obuf_ref.at[prev, :, 0:512], obuf_ref.at[prev, :, 512:1024], wo_ref, g1_ref, ng_ref,
                     sc_ref, sh_ref, g2_ref, wup_ref, cwf_ref, cbf_ref, wdn_ref, fg_ref, out_ref, tail_ref, act_ref,
                     wb_ref, gt_ref, h_ref, tl=tl, d_ff=d_ff, final=final, seg=tl)
    band = _chain(*[_band_steps(q_ref.at[0, ti * tq:(ti + 1) * tq, :], k_ref, v_ref, bb_ref,
                                obuf_ref.at[slot, ti * tq:(ti + 1) * tq, 0:512],
                                jm * (tl // tq) + ti, tq=tq) for ti in range(tl // tq)])
    ssd = _chain(*[_ssd_steps(z_ref.at[0, ci * c:(ci + 1) * c, :], xs_ref.at[0, ci * c:(ci + 1) * c, :],
                              bm_ref.at[0, ci * c:(ci + 1) * c, :], cm_ref.at[0, ci * c:(ci + 1) * c, :],
                              dt_ref.at[0, ci * c:(ci + 1) * c, :], cws_ref, cbs_ref, dtb_ref, alog_ref, dskip_ref,
                              ngs_ref, obuf_ref.at[slot, ci * c:(ci + 1) * c, 512:1024], buf_ref, st_ref, c=c)
                   for ci in range(tl // c)])
    _interleave([ffn, _chain(band, ssd)], (1, 1))

    @pl.when(g < n_tiles)
    def _():
        sn_ref[0] = st_ref[...]


def _cd_mixers_ffn(x, main, dt, bias, wo, g1, ng, sc, sh, g2, wup, cwf, cbf, wdn, fg,
                   conv_w, conv_b, dt_bias, a_log, d_skip, norm_g, tl, final):
    nseq, seq, d = x.shape
    d_ff = wdn.shape[0]
    tps = seq // tl
    n_tiles = nseq * tps
    c = SCAN_CHUNK
    tq = BAND_TQ
    pad = lambda a: jnp.pad(a.reshape(1, H_D), ((0, 0), (0, 128 - H_D)))
    mix = lambda g: jnp.minimum(g, n_tiles - 1)
    ffn = lambda g: jnp.maximum(g - 1, 0)
    tok_f = lambda n: pl.BlockSpec((1, tl, n), lambda g: (ffn(g) // tps, ffn(g) % tps, 0))
    seq_f = lambda n: pl.BlockSpec((1, 1, n), lambda g: (ffn(g) // tps, 0, 0))
    tok_m = lambda w, idx: pl.BlockSpec((1, tl, w), lambda g: (mix(g) // tps, mix(g) % tps, idx))
    full_m = lambda idx: pl.BlockSpec((1, seq, 512), lambda g: (mix(g) // tps, 0, idx))
    const = lambda r, w: pl.BlockSpec((r, w), lambda g: (0, 0))
    return pl.pallas_call(
        functools.partial(_cd_ffn_kernel, tl=tl, d_ff=d_ff, final=final, n_tiles=n_tiles, tiles_per_seq=tps,
                          c=c, tq=tq),
        grid=(n_tiles + 1,),
        in_specs=[tok_f(d), _resident(wo.shape), seq_f(d), _resident((1, d)), seq_f(d), seq_f(d), seq_f(d),
                  _resident(wup.shape), _resident(cwf.shape), _resident((1, d_ff)), _resident(wdn.shape),
                  _resident((1, d)),
                  tok_m(512, 0), full_m(1), full_m(2), _resident(bias.shape),
                  tok_m(512, 3), tok_m(512, 4), tok_m(256, 10), tok_m(256, 11), tok_m(128, 0),
                  const(CONV_D, CONV_DIM_D), const(1, CONV_DIM_D), const(1, 128), const(1, 128),
                  pl.BlockSpec(memory_space=pltpu.SMEM), const(1, D_INNER)],
        out_specs=[tok_f(d),
                   pl.BlockSpec((1, 8, d_ff), lambda g: (ffn(g) // tps, 0, 0)),
                   pl.BlockSpec((1, H_D * P_D, N_D), lambda g: (mix(g) // tps, 0, 0))],
        out_shape=[jax.ShapeDtypeStruct((nseq, seq, d), F32), jax.ShapeDtypeStruct((nseq, 8, d_ff), F32),
                   jax.ShapeDtypeStruct((nseq, H_D * P_D, N_D), F32)],
        scratch_shapes=[pltpu.VMEM((2, tl, 1024), BF16), pltpu.VMEM((tl, d_ff), BF16),
                        pltpu.VMEM((tl + 8, FF_CHUNK), F32), pltpu.VMEM((1, 8, d_ff), F32),
                        pltpu.VMEM((c + 8, CONV_DIM_D), F32), pltpu.VMEM((H_D * P_D, N_D), F32),
                        pltpu.VMEM((tl, d), BF16)],
        compiler_params=_params(("arbitrary",)),
        name="cd_mixers_ffn",
    )(x, wo, g1, ng, sc, sh, g2, wup, cwf, cbf.reshape(1, d_ff), wdn, fg,
      main, main, main, bias,
      main, main, main, main, dt, conv_w, conv_b.reshape(1, CONV_DIM_D), pad(dt_bias), pad(a_log), d_skip,
      norm_g.reshape(1, D_INNER))


def _rope_tables(pos):
    half = DK_A // 2
    inv = jnp.power(ROPE_BASE, -jnp.arange(half, dtype=F32) / half)
    ang = pos.astype(F32)[:, None] * inv[None, :]
    cos = jnp.cos(ang)
    sin = jnp.sin(ang)
    return jnp.concatenate([cos, cos], axis=1), jnp.concatenate([-sin, sin], axis=1)


def _pad_rows(a, rows=8):
    return jnp.pad(a, ((0, 0), (rows - a.shape[1], 0), (0, 0)))


def _trunk(x, mods, pos, weights, caches, biases):
    nseq, seq, d = x.shape
    sample = caches is not None
    c = min(seq, SCAN_CHUNK)
    if sample:
        tl, seg = nseq * seq, seq
        pack = lambda a: a.reshape(1, nseq * seq, a.shape[-1])
        unpack = lambda a: a.reshape(nseq, seq, a.shape[-1])
        rows = lambda m: jnp.repeat(m, seq, axis=1).reshape(1, nseq * seq, d)
    else:
        tl = seg = min(seq, 512)
        pack = unpack = rows = lambda a: a
    tl_in = tl if sample else min(seq, INPROJ_TL)
    seg_in = seg if sample else tl_in
    depth = weights["w_up"].shape[0]
    outs = {k: [] for k in ("ret", "bk", "bv", "ck", "cv", "dconv", "dssm", "ffn")}
    cos, sin = _rope_tables(pos)
    if sample:
        cos, sin = jnp.tile(cos, (nseq, 1)), jnp.tile(sin, (nseq, 1))
    x = pack(x)
    for l in range(depth):
        i = l // 2
        sh1, sc1, g1, sh2, sc2, g2 = [rows(m) for m in mods[l]]
        ng1 = weights["norm_g"][l, 0].reshape(1, d)
        ng2 = weights["norm_g"][l, 1].reshape(1, d)
        fused = None
        if l % 2 == 0:
            qkv, kb, vb = _inproj_ab(x, ng1, sc1, sh1, weights["w_in_ab"][i], cos, sin, tl_in)
            qkv = unpack(qkv)
            s0 = caches["ret"][i] if sample else jnp.zeros((nseq, H_A, DK_A, DV_A), F32)
            o1, s_new = _retention(qkv, s0, weights["ret_gn"][i], c)
            lam_init = 0.8 - 0.6 * math.exp(-0.3 * l)
            if sample:
                o2 = _diff_attn_sample(qkv, caches["bk"][i], caches["bv"][i], biases["t5"],
                                       weights["lam_q"][i], weights["lam_k"][i], weights["diff_gn"][i], lam_init)
            else:
                o2 = _diff_attn(qkv, biases["t5"], weights["lam_q"][i], weights["lam_k"][i],
                                weights["diff_gn"][i], lam_init)
            wo = weights["w_out_ab"][i]
            outs["ret"].append(s_new)
            outs["bk"].append(kb.reshape(nseq, seq, H_B, 2 * DH_B))
            outs["bv"].append(vb.reshape(nseq, seq, H_B, DV_B))
        else:
            keep = seq if sample else min(C_WINDOW, seq)
            main, dt, kc, vc, tail = _inproj_cd(x, ng1, sc1, sh1, weights["w_in_cd"][i], tl_in,
                                                keep * (tl // seg), seg_in)
            main, dt = unpack(main), unpack(dt)
            if sample:
                o1 = _band_attn_sample(main, caches["ck"][i], caches["cv"][i], biases["band"][i])
                conv_past = _pad_rows(caches["dconv"][i])
                s0 = caches["dssm"][i].reshape(nseq, H_D * P_D, N_D)
                o2, ssm_new = _ssd(main, dt, weights["d_conv_w"][i], weights["d_conv_b"][i], weights["d_dt_bias"][i],
                                   weights["d_a_log"][i], weights["d_skip"][i], weights["d_norm_g"][i], conv_past, s0,
                                   c)
            else:
                fused = _cd_mixers_ffn(
                    x, main, dt, biases["band"][i], weights["w_out_cd"][i], g1, ng2, sc2, sh2, g2, weights["w_up"][l],
                    weights["ffn_conv_w"][l], weights["ffn_conv_b"][l], weights["w_down"][l],
                    weights["final_g"].reshape(1, d), weights["d_conv_w"][i], weights["d_conv_b"][i],
                    weights["d_dt_bias"][i], weights["d_a_log"][i], weights["d_skip"][i], weights["d_norm_g"][i],
                    FUSED_TL, final=(l == depth - 1))
                ssm_new = fused[2]
            wo = weights["w_out_cd"][i]
            outs["ck"].append(kc.reshape(nseq, keep, H_C, DH_C))
            outs["cv"].append(vc.reshape(nseq, keep, H_C, DH_C))
            outs["dconv"].append(tail[:, 8 - (CONV_D - 1):])
            outs["dssm"].append(ssm_new.reshape(nseq, H_D, P_D, N_D))
        d_ff = weights["w_down"].shape[1]
        if fused is not None:
            x, ftail = fused[0], fused[1]
        else:
            ffn_past = _pad_rows(caches["ffn"][l]) if sample else jnp.zeros((nseq, 8, d_ff), F32)
            x, ftail = _outproj_ffn(x, pack(o1), pack(o2), wo, g1, ng2, sc2, sh2, g2, weights["w_up"][l],
                                    weights["ffn_conv_w"][l], weights["ffn_conv_b"][l], weights["w_down"][l], ffn_past,
                                    weights["final_g"].reshape(1, d), tl, seg, final=(l == depth - 1))
        outs["ffn"].append(ftail[:, 8 - (CONV_F - 1):])
    stk = lambda t: jnp.stack(t).astype(F32)
    x = unpack(x)
    return (x,) + tuple(stk(outs[k]) for k in ("ret", "bk", "bv", "ck", "cv", "dconv", "dssm", "ffn"))


def kernel(x_prompt, x_sample, cache_ret_state, cache_b_k, cache_b_v, cache_c_k, cache_c_v, state_d_conv, state_d_ssm, state_ffn_conv, c_prompt, c_sample, w_mod, b_mod, norm_g, final_g, t5_table, w_in_ab, w_out_ab, ret_gn, lam_q, lam_k, diff_gn, w_in_cd, w_out_cd, rel_table, d_conv_w, d_conv_b, d_dt_bias, d_a_log, d_skip, d_norm_g, w_up, ffn_conv_w, ffn_conv_b, w_down):
    batch, seq, d = x_prompt.shape
    dec_batch, dec_seq, _ = x_sample.shape
    past = cache_b_k.shape[2]
    depth = w_mod.shape[0]
    assert dec_seq <= CHUNK and past % CHUNK == 0 and cache_c_k.shape[2] == C_WINDOW
    assert seq % BAND_TQ == 0 and seq % DIFF_TQ == 0

    w_in_cd_p = jnp.pad(w_in_cd, ((0, 0), (0, 0), (0, N_CD_PAD - w_in_cd.shape[-1]))).astype(BF16)
    weights = dict(
        norm_g=norm_g, final_g=final_g, w_in_ab=w_in_ab.astype(BF16), w_out_ab=w_out_ab.astype(BF16),
        ret_gn=ret_gn, lam_q=lam_q, lam_k=lam_k, diff_gn=diff_gn, w_in_cd=w_in_cd_p,
        w_out_cd=w_out_cd.astype(BF16), d_conv_w=d_conv_w, d_conv_b=d_conv_b, d_dt_bias=d_dt_bias, d_a_log=d_a_log,
        d_skip=d_skip, d_norm_g=d_norm_g, w_up=w_up.astype(BF16), ffn_conv_w=ffn_conv_w, ffn_conv_b=ffn_conv_b,
        w_down=w_down.astype(BF16))

    mod = _modulation(jnp.concatenate([c_prompt, c_sample], axis=0), w_mod, b_mod)

    def pieces(rows):
        return [[m[:, None, :] for m in jnp.split(mod[l, rows], 6, axis=-1)] for l in range(depth)]

    band = [_band_bias(rel_table[i]) for i in range(rel_table.shape[0])]
    biases_p = dict(t5=_t5_bias_prompt(t5_table, seq), band=band)
    biases_s = dict(t5=_t5_bias_sample(t5_table, past, dec_seq),
                    band=[b[:, :dec_seq, :C_WINDOW + dec_seq] for b in band])
    caches = dict(
        ret=cache_ret_state,
        bk=cache_b_k.reshape(cache_b_k.shape[0], dec_batch, past, H_B * 2 * DH_B),
        bv=cache_b_v.reshape(cache_b_v.shape[0], dec_batch, past, H_B * DV_B),
        ck=cache_c_k.reshape(cache_c_k.shape[0], dec_batch, C_WINDOW, H_C * DH_C),
        cv=cache_c_v.reshape(cache_c_v.shape[0], dec_batch, C_WINDOW, H_C * DH_C),
        dconv=state_d_conv, dssm=state_d_ssm, ffn=state_ffn_conv)

    pos_p = jnp.arange(seq, dtype=jnp.int32)
    pos_s = past + jnp.arange(dec_seq, dtype=jnp.int32)
    y_p, ret_p, bk_p, bv_p, ck_p, cv_p, dconv_p, dssm_p, ffn_p = _trunk(
        x_prompt, pieces(slice(0, batch)), pos_p, weights, None, biases_p)
    y_s, ret_s, bk_s, bv_s, ck_s, cv_s, dconv_s, dssm_s, ffn_s = _trunk(
        x_sample, pieces(slice(batch, batch + dec_batch)), pos_s, weights, caches, biases_s)
    return (y_p, y_s, ret_p, ret_s, bk_p, bk_s, bv_p, bv_s, ck_p, ck_s, cv_p, cv_s,
            dconv_p, dconv_s, dssm_p, dssm_s, ffn_p, ffn_s)
```
